```python
import jax, jax.numpy as jnp
from jax import lax
import numpy as np

D_MODEL = 1024
BATCH = 8
SEQ = 4096
DEPTH = 4

N_MIXERS = 3
EPS = 1e-6
A_CHUNK = 128
A_WIDTH = 2 * D_MODEL
A_GROUPS = 8
A_GROUP_DIM = A_WIDTH // A_GROUPS
B_HEADS = 4
B_KEY_DIM = D_MODEL // 2
B_VAL_DIM = D_MODEL
B_HK = B_KEY_DIM // B_HEADS
B_HV = B_VAL_DIM // B_HEADS
B_GATE_RANK = 16
B_TAU = 16.0
B_CHUNK = 64
C_HEADS = 8
C_HEAD_DIM = D_MODEL // C_HEADS
C_BLOCK = 128
FFN_DIM = 7 * D_MODEL // 2
N_EXPERTS = 8
TOP_K = 2
N_A = (DEPTH + 2) // 3
N_B = (DEPTH + 1) // 3
N_C = DEPTH // 3
N_DENSE = (DEPTH + 1) // 2
N_MOE = DEPTH // 2

kernel_name = "hybrid_gmlp_gla_fox_moe_trunk"


def rmsnorm(x, g):
    xf = x.astype(jnp.float32)
    y = xf * lax.rsqrt(jnp.mean(xf * xf, axis=-1, keepdims=True) + EPS)
    return (y * g.astype(jnp.float32)).astype(x.dtype)


def gmlp_mixer(h, w_in, v_g, w_s, b_s, w_out):
    bsz, s, _ = h.shape
    z = jax.nn.gelu(h @ w_in)
    u, v = jnp.split(z, 2, axis=-1)
    v = rmsnorm(v, v_g)
    nc = s // A_CHUNK
    v = v.reshape(bsz, nc, A_CHUNK, A_GROUPS, A_GROUP_DIM)
    mask = jnp.tril(jnp.ones((A_CHUNK, A_CHUNK), dtype=bool))
    w = jnp.where(mask[None], w_s, 0.0).astype(v.dtype)
    mixed = jnp.einsum('gts,bnsgd->bntgd', w, v) + b_s.T.astype(v.dtype)[None, None, :, :, None]
    y = u * mixed.reshape(bsz, s, A_WIDTH)
    return y @ w_out


def gla_mixer(h, w_in, w_gate2, gate_bias, o_g, w_out):
    bsz, s, _ = h.shape
    dt = h.dtype
    proj = h @ w_in
    q, k, v, r, g_low = jnp.split(
        proj, np.cumsum([B_KEY_DIM, B_KEY_DIM, B_VAL_DIM, B_VAL_DIM]).tolist(), axis=-1)
    log_a = jax.nn.log_sigmoid((g_low @ w_gate2 + gate_bias).astype(jnp.float32)) / B_TAU
    nc = s // B_CHUNK

    def to_chunks(t, hd):
        return t.astype(jnp.float32).reshape(bsz, nc, B_CHUNK, B_HEADS, hd).transpose(0, 3, 1, 2, 4)

    q = to_chunks(q, B_HK) * (B_HK ** -0.5)
    k = to_chunks(k, B_HK)
    v = to_chunks(v, B_HV)
    la = to_chunks(log_a, B_HK)
    bcum = jnp.cumsum(la, axis=3)
    b_last = bcum[:, :, :, -1:, :]
    q_d = q * jnp.exp(bcum)
    k_d = k * jnp.exp(-bcum)
    k_end = k * jnp.exp(b_last - bcum)
    dec = jnp.exp(b_last[:, :, :, 0, :])
    att = jnp.einsum('bhntk,bhnsk->bhnts', q_d, k_d)
    cmask = jnp.tril(jnp.ones((B_CHUNK, B_CHUNK), dtype=bool))
    att = jnp.where(cmask, att, 0.0)
    o_intra = jnp.einsum('bhnts,bhnsv->bhntv', att, v)

    def step(state, xs):
        qd_n, ke_n, v_n, dec_n = xs
        o_n = jnp.einsum('bhtk,bhkv->bhtv', qd_n, state)
        state = dec_n[..., None] * state + jnp.einsum('bhsk,bhsv->bhkv', ke_n, v_n)
        return state, o_n

    state0 = jnp.zeros((bsz, B_HEADS, B_HK, B_HV), jnp.float32)
    _, o_inter = lax.scan(step, state0, (jnp.moveaxis(q_d, 2, 0), jnp.moveaxis(k_end, 2, 0),
                                         jnp.moveaxis(v, 2, 0), jnp.moveaxis(dec, 2, 0)))
    o = o_intra + jnp.moveaxis(o_inter, 0, 2)
    o = o.transpose(0, 2, 3, 1, 4).reshape(bsz, s, B_HEADS, B_HV)
    o = rmsnorm(o, o_g).reshape(bsz, s, B_VAL_DIM).astype(dt)
    return (o * jax.nn.silu(r)) @ w_out


def fox_mixer(h, w_in, f_bias, w_out):
    bsz, s, _ = h.shape
    proj = h @ w_in
    q, k, v, f_logit = jnp.split(proj, [D_MODEL, 2 * D_MODEL, 3 * D_MODEL], axis=-1)

    def heads(t):
        return t.reshape(bsz, s, C_HEADS, C_HEAD_DIM).transpose(0, 2, 1, 3)

    q, k, v = heads(q), heads(k), heads(v)
    log_f = jax.nn.log_sigmoid(f_logit.astype(jnp.float32) + f_bias.astype(jnp.float32))
    c = jnp.cumsum(log_f, axis=1).transpose(0, 2, 1)
    nb = s // C_BLOCK
    q_blocks = q.reshape(bsz, C_HEADS, nb, C_BLOCK, C_HEAD_DIM).transpose(2, 0, 1, 3, 4)
    c_blocks = c.reshape(bsz, C_HEADS, nb, C_BLOCK).transpose(2, 0, 1, 3)
    key_pos = jnp.arange(s)
    scale = C_HEAD_DIM ** -0.5

    def block(args):
        qb, cb, i = args
        logits = jnp.einsum('bhtd,bhsd->bhts', qb, k, preferred_element_type=jnp.float32) * scale
        logits = logits + cb[..., None] - c[:, :, None, :]
        qpos = i * C_BLOCK + jnp.arange(C_BLOCK)
        mask = key_pos[None, :] <= qpos[:, None]
        logits = jnp.where(mask, logits, -jnp.inf)
        p = jax.nn.softmax(logits, axis=-1)
        return jnp.einsum('bhts,bhsd->bhtd', p.astype(v.dtype), v)

    o = lax.map(block, (q_blocks, c_blocks, jnp.arange(nb)))
    o = o.transpose(1, 0, 3, 2, 4).reshape(bsz, s, D_MODEL)
    return o @ w_out


def swiglu(h, w_in, w_out):
    g, u = jnp.split(h @ w_in, 2, axis=-1)
    return (jax.nn.silu(g) * u) @ w_out


def moe_ffn(h, router, w_in, w_out):
    bsz, s, d = h.shape
    t = h.reshape(-1, d)
    logits = (t @ router).astype(jnp.float32)
    top_v, top_i = lax.top_k(logits, TOP_K)
    gates = jax.nn.softmax(top_v, axis=-1)
    y = jnp.zeros_like(t)
    for e in range(N_EXPERTS):
        w_e = jnp.sum(jnp.where(top_i == e, gates, 0.0), axis=-1).astype(t.dtype)
        y = y + w_e[:, None] * swiglu(t, w_in[e], w_out[e])
    return y.reshape(bsz, s, d)


def setup_inputs(seed: int = 0) -> dict:
    key = jax.random.key(seed)
    ks = iter(jax.random.split(key, 32))

    def nrm(shape, fan_in):
        return jax.random.normal(next(ks), shape, jnp.float32) * (fan_in ** -0.5)

    def gain(shape):
        return 1.0 + 0.02 * jax.random.normal(next(ks), shape, jnp.float32)

    def small(shape):
        return 0.02 * jax.random.normal(next(ks), shape, jnp.float32)

    x = jax.random.normal(next(ks), (BATCH, SEQ, D_MODEL), jnp.float32)
    return {
        "x": x,
        "norm1_g": gain((DEPTH, D_MODEL)),
        "norm2_g": gain((DEPTH, D_MODEL)),
        "a_w_in": nrm((N_A, D_MODEL, 2 * A_WIDTH), D_MODEL),
        "a_vnorm_g": gain((N_A, A_WIDTH)),
        "a_w_s": nrm((N_A, A_GROUPS, A_CHUNK, A_CHUNK), A_CHUNK),
        "a_b_s": gain((N_A, A_GROUPS, A_CHUNK)),
        "a_w_out": nrm((N_A, A_WIDTH, D_MODEL), A_WIDTH),
        "b_w_in": nrm((N_B, D_MODEL, 2 * B_KEY_DIM + 2 * B_VAL_DIM + B_GATE_RANK), D_MODEL),
        "b_w_gate2": nrm((N_B, B_GATE_RANK, B_KEY_DIM), B_GATE_RANK),
        "b_gate_bias": small((N_B, B_KEY_DIM)),
        "b_onorm_g": gain((N_B, B_HV)),
        "b_w_out": nrm((N_B, B_VAL_DIM, D_MODEL), B_VAL_DIM),
        "c_w_in": nrm((N_C, D_MODEL, 3 * D_MODEL + C_HEADS), D_MODEL),
        "c_f_bias": 3.0 + 0.1 * jax.random.normal(next(ks), (N_C, C_HEADS), jnp.float32),
        "c_w_out": nrm((N_C, D_MODEL, D_MODEL), D_MODEL),
        "ffn_w_in": nrm((N_DENSE, D_MODEL, 2 * FFN_DIM), D_MODEL),
        "ffn_w_out": nrm((N_DENSE, FFN_DIM, D_MODEL), FFN_DIM),
        "moe_router": nrm((N_MOE, D_MODEL, N_EXPERTS), D_MODEL),
        "moe_w_in": nrm((N_MOE, N_EXPERTS, D_MODEL, 2 * FFN_DIM), D_MODEL),
        "moe_w_out": nrm((N_MOE, N_EXPERTS, FFN_DIM, D_MODEL), FFN_DIM),
        "final_g": gain((D_MODEL,)),
    }


def reference(x, norm1_g, norm2_g, a_w_in, a_vnorm_g, a_w_s, a_b_s, a_w_out,
              b_w_in, b_w_gate2, b_gate_bias, b_onorm_g, b_w_out,
              c_w_in, c_f_bias, c_w_out, ffn_w_in, ffn_w_out,
              moe_router, moe_w_in, moe_w_out, final_g):
    for i in range(DEPTH):
        h = rmsnorm(x, norm1_g[i])
        m, j = i % N_MIXERS, i // N_MIXERS
        if m == 0:
            x = x + gmlp_mixer(h, a_w_in[j], a_vnorm_g[j], a_w_s[j], a_b_s[j], a_w_out[j])
        elif m == 1:
            x = x + gla_mixer(h, b_w_in[j], b_w_gate2[j], b_gate_bias[j], b_onorm_g[j], b_w_out[j])
        else:
            x = x + fox_mixer(h, c_w_in[j], c_f_bias[j], c_w_out[j])
        h = rmsnorm(x, norm2_g[i])
        if i % 2 == 0:
            x = x + swiglu(h, ffn_w_in[i // 2], ffn_w_out[i // 2])
        else:
            x = x + moe_ffn(h, moe_router[i // 2], moe_w_in[i // 2], moe_w_out[i // 2])
    return rmsnorm(x, final_g)
```

```python
import functools

import jax
import jax.numpy as jnp
from jax import lax
from jax.experimental import pallas as pl
from jax.experimental.pallas import tpu as pltpu

F32 = jnp.float32
BF16 = jnp.bfloat16

EPS = 1e-6
N_MIXERS = 3
TOP_K = 2
A_CHUNK = 128
A_GROUPS = 8
B_HEADS = 4
B_GATE_RANK = 16
B_TAU = 16.0
B_CHUNK = 64
C_HEADS = 8

V7X_LANES = 128
V7X_VMEM_LIMIT_BYTES = 56 * 1024 * 1024


def _params(*sem):
    return pltpu.CompilerParams(dimension_semantics=sem, vmem_limit_bytes=V7X_VMEM_LIMIT_BYTES)


def _rms(x, g):
    ms = jnp.mean(x * x, axis=-1, keepdims=True)
    return x * lax.rsqrt(ms + EPS) * g


def _log_sigmoid(x):
    return jnp.minimum(x, 0.0) - jnp.log1p(jnp.exp(-jnp.abs(x)))


def _split3(x):
    hi = x.astype(BF16)
    r1 = x - hi.astype(F32)
    mid = r1.astype(BF16)
    lo = (r1 - mid.astype(F32)).astype(BF16)
    return hi, mid, lo


def _dot(a, b):
    return jnp.dot(a, b, preferred_element_type=F32)


def _dot_nt(a, b):
    return lax.dot_general(a, b, (((1,), (1,)), ((), ())), preferred_element_type=F32)


def _dot_tn(a, b):
    return lax.dot_general(a, b, (((0,), (0,)), ((), ())), preferred_element_type=F32)


def _tril_mask(n):
    row = lax.broadcasted_iota(jnp.int32, (n, n), 0)
    col = lax.broadcasted_iota(jnp.int32, (n, n), 1)
    return col <= row


def _norm_matmul_body(x_ref, g_ref, w_ref, o_ref, h_ref, *, act):
    @pl.when(pl.program_id(1) == 0)
    def _():
        h_ref[...] = _rms(x_ref[...], g_ref[...]).astype(BF16)

    y = _dot(h_ref[...], w_ref[...])
    if act == "gelu":
        y = jax.nn.gelu(y, approximate=True)
    o_ref[...] = y.astype(o_ref.dtype)


def norm_matmul(x, g, w, act=None, tm=1024, tn=512):
    n, d = x.shape
    nout = w.shape[1]
    tm, tn = min(tm, n), min(tn, nout)
    return pl.pallas_call(
        functools.partial(_norm_matmul_body, act=act),
        grid=(n // tm, nout // tn),
        in_specs=[
            pl.BlockSpec((tm, d), lambda i, j: (i, 0)),
            pl.BlockSpec((1, d), lambda i, j: (0, 0)),
            pl.BlockSpec((d, tn), lambda i, j: (0, j)),
        ],
        out_specs=pl.BlockSpec((tm, tn), lambda i, j: (i, j)),
        out_shape=jax.ShapeDtypeStruct((n, nout), BF16),
        scratch_shapes=[pltpu.VMEM((tm, d), BF16)],
        compiler_params=_params("parallel", "arbitrary"),
        name="norm_matmul",
    )(x, g, w)


def _matmul_residual_body(y_ref, w_ref, x_ref, o_ref):
    o_ref[...] = x_ref[...] + _dot(y_ref[...], w_ref[...])


def matmul_residual(y, w, x, tm=512):
    n, k = y.shape
    d = w.shape[1]
    tm = min(tm, n)
    return pl.pallas_call(
        _matmul_residual_body,
        grid=(n // tm,),
        in_specs=[
            pl.BlockSpec((tm, k), lambda i: (i, 0)),
            pl.BlockSpec((k, d), lambda i: (0, 0)),
            pl.BlockSpec((tm, d), lambda i: (i, 0)),
        ],
        out_specs=pl.BlockSpec((tm, d), lambda i: (i, 0)),
        out_shape=jax.ShapeDtypeStruct((n, d), F32),
        compiler_params=_params("parallel"),
        name="matmul_residual",
    )(y, w, x)


def _gmlp_body(z_ref, x_ref, vg_ref, ws_ref, bt_ref, wo_ref, o_ref, vn_ref, y_ref, *, tm, width):
    gd = width // A_GROUPS
    vn_ref[...] = _rms(z_ref[:, width:].astype(F32), vg_ref[...]).astype(BF16)
    tril = _tril_mask(A_CHUNK)
    for g in range(A_GROUPS):
        w = jnp.where(tril, ws_ref[g], 0.0).astype(BF16)
        bias = bt_ref[:, g : g + 1]
        cs = slice(g * gd, (g + 1) * gd)
        for c in range(tm // A_CHUNK):
            rs = slice(c * A_CHUNK, (c + 1) * A_CHUNK)
            mixed = _dot(w, vn_ref[rs, cs]) + bias
            y_ref[rs, cs] = (z_ref[rs, cs].astype(F32) * mixed).astype(BF16)
    o_ref[...] = x_ref[...] + _dot(y_ref[...], wo_ref[...])


def gmlp_spatial_out(z, x, vg, ws, bt, wo, tm=256):
    n, d = x.shape
    width = z.shape[1] // 2
    tm = min(tm, n)
    return pl.pallas_call(
        functools.partial(_gmlp_body, tm=tm, width=width),
        grid=(n // tm,),
        in_specs=[
            pl.BlockSpec((tm, 2 * width), lambda i: (i, 0)),
            pl.BlockSpec((tm, d), lambda i: (i, 0)),
            pl.BlockSpec((1, width), lambda i: (0, 0)),
            pl.BlockSpec((A_GROUPS, A_CHUNK, A_CHUNK), lambda i: (0, 0, 0)),
            pl.BlockSpec((A_CHUNK, A_GROUPS), lambda i: (0, 0)),
            pl.BlockSpec((width, d), lambda i: (0, 0)),
        ],
        out_specs=pl.BlockSpec((tm, d), lambda i: (i, 0)),
        out_shape=jax.ShapeDtypeStruct((n, d), F32),
        scratch_shapes=[pltpu.VMEM((tm, width), BF16), pltpu.VMEM((tm, width), BF16)],
        compiler_params=_params("parallel"),
        name="gmlp_spatial_out",
    )(z, x, vg, ws, bt, wo)


def _ffn_body(*refs, gated, final):
    x_ref, g_ref, wg_ref, wu_ref, wo_ref = refs[:5]
    rest = list(refs[5:])
    gate_ref = rest.pop(0) if gated else None
    fg_ref = rest.pop(0) if final else None
    o_ref, h_ref, acc_ref = rest
    e, j = pl.program_id(1), pl.program_id(2)

    @pl.when((e == 0) & (j == 0))
    def _():
        h_ref[...] = _rms(x_ref[...], g_ref[...]).astype(BF16)
        acc_ref[...] = jnp.zeros_like(acc_ref)

    h = h_ref[...]
    a = _dot(h, wg_ref[...])
    t = (a * jax.nn.sigmoid(a)) * _dot(h, wu_ref[...])
    if gated:
        lane = lax.broadcasted_iota(jnp.int32, gate_ref.shape, 1)
        t = t * jnp.sum(jnp.where(lane == e, gate_ref[...], 0.0), axis=-1, keepdims=True)
    acc_ref[...] += _dot(t.astype(BF16), wo_ref[...])

    @pl.when((e == pl.num_programs(1) - 1) & (j == pl.num_programs(2) - 1))
    def _():
        y = x_ref[...] + acc_ref[...]
        o_ref[...] = _rms(y, fg_ref[...]) if final else y


def ffn(x, g, w_in, w_out, gates=None, final_g=None, tm=1024, tf=512):
    n, d = x.shape
    ne, f = w_out.shape[0], w_out.shape[1]
    tm, tf = min(tm, n), min(tf, f)
    nf = f // tf
    gated, final = gates is not None, final_g is not None
    in_specs = [
        pl.BlockSpec((tm, d), lambda i, e, j: (i, 0)),
        pl.BlockSpec((1, d), lambda i, e, j: (0, 0)),
        pl.BlockSpec((None, d, tf), lambda i, e, j: (e, 0, j)),
        pl.BlockSpec((None, d, tf), lambda i, e, j: (e, 0, j + nf)),
        pl.BlockSpec((None, tf, d), lambda i, e, j: (e, j, 0)),
    ]
    args = [x, g, w_in, w_in, w_out]
    if gated:
        in_specs.append(pl.BlockSpec((tm, gates.shape[1]), lambda i, e, j: (i, 0)))
        args.append(gates)
    if final:
        in_specs.append(pl.BlockSpec((1, d), lambda i, e, j: (0, 0)))
        args.append(final_g)
    return pl.pallas_call(
        functools.partial(_ffn_body, gated=gated, final=final),
        grid=(n // tm, ne, nf),
        in_specs=in_specs,
        out_specs=pl.BlockSpec((tm, d), lambda i, e, j: (i, 0)),
        out_shape=jax.ShapeDtypeStruct((n, d), F32),
        scratch_shapes=[pltpu.VMEM((tm, d), BF16), pltpu.VMEM((tm, d), F32)],
        compiler_params=_params("parallel", "arbitrary", "arbitrary"),
        name="ffn",
    )(*args)


def _router_body(x_ref, g_ref, whi_ref, wlo_ref, o_ref, *, ne):
    h = _rms(x_ref[...], g_ref[...])
    h_hi = h.astype(BF16)
    h_lo = (h - h_hi.astype(F32)).astype(BF16)
    logits = _dot(h_hi, whi_ref[...]) + _dot(h_lo, whi_ref[...]) + _dot(h_hi, wlo_ref[...])
    lane = lax.broadcasted_iota(jnp.int32, logits.shape, 1).astype(F32)
    neg = -jnp.inf
    l1 = jnp.where(lane < ne, logits, neg)
    m1 = jnp.max(l1, axis=-1, keepdims=True)
    i1 = jnp.min(jnp.where(l1 == m1, lane, float(V7X_LANES)), axis=-1, keepdims=True)
    l2 = jnp.where(lane == i1, neg, l1)
    m2 = jnp.max(l2, axis=-1, keepdims=True)
    i2 = jnp.min(jnp.where(l2 == m2, lane, float(V7X_LANES)), axis=-1, keepdims=True)
    e2 = jnp.exp(m2 - m1)
    den = 1.0 + e2
    o_ref[...] = jnp.where(lane == i1, 1.0 / den, 0.0) + jnp.where(lane == i2, e2 / den, 0.0)


def router_gates(x, g, w_router, tm=1024):
    n, d = x.shape
    ne = w_router.shape[1]
    tm = min(tm, n)
    wpad = jnp.pad(w_router, ((0, 0), (0, V7X_LANES - ne)))
    w_hi = wpad.astype(BF16)
    w_lo = (wpad - w_hi.astype(F32)).astype(BF16)
    return pl.pallas_call(
        functools.partial(_router_body, ne=ne),
        grid=(n // tm,),
        in_specs=[
            pl.BlockSpec((tm, d), lambda i: (i, 0)),
            pl.BlockSpec((1, d), lambda i: (0, 0)),
            pl.BlockSpec((d, V7X_LANES), lambda i: (0, 0)),
            pl.BlockSpec((d, V7X_LANES), lambda i: (0, 0)),
        ],
        out_specs=pl.BlockSpec((tm, V7X_LANES), lambda i: (i, 0)),
        out_shape=jax.ShapeDtypeStruct((n, V7X_LANES), F32),
        compiler_params=_params("parallel"),
        name="router_gates",
    )(x, g, w_hi, w_lo)


def _gla_gate_body(x_ref, g_ref, wl_ref, w2_ref, b_ref, o_ref):
    h = _rms(x_ref[...], g_ref[...]).astype(BF16)
    g_low = _dot(h, wl_ref[...]).astype(BF16)
    o_ref[...] = _log_sigmoid(_dot(g_low, w2_ref[...]) + b_ref[...]) / B_TAU


def gla_gate(x, g, w_low, w_gate2, bias, tm=1024):
    n, d = x.shape
    dk = w_gate2.shape[1]
    tm = min(tm, n)
    wl = jnp.pad(w_low, ((0, 0), (0, V7X_LANES - B_GATE_RANK))).astype(BF16)
    w2 = jnp.pad(w_gate2, ((0, V7X_LANES - B_GATE_RANK), (0, 0))).astype(BF16)
    return pl.pallas_call(
        _gla_gate_body,
        grid=(n // tm,),
        in_specs=[
            pl.BlockSpec((tm, d), lambda i: (i, 0)),
            pl.BlockSpec((1, d), lambda i: (0, 0)),
            pl.BlockSpec((d, V7X_LANES), lambda i: (0, 0)),
            pl.BlockSpec((V7X_LANES, dk), lambda i: (0, 0)),
            pl.BlockSpec((1, dk), lambda i: (0, 0)),
        ],
        out_specs=pl.BlockSpec((tm, dk), lambda i: (i, 0)),
        out_shape=jax.ShapeDtypeStruct((n, dk), F32),
        compiler_params=_params("parallel"),
        name="gla_gate",
    )(x, g, wl, w2, bias)


def _gla_body(q_ref, k_ref, v_ref, r_ref, la_ref, og_ref, o_ref, st_ref, *, tc, hk, hv):
    @pl.when(pl.program_id(1) == 0)
    def _():
        st_ref[...] = jnp.zeros_like(st_ref)

    c = B_CHUNK
    tril = _tril_mask(c)
    ones_tril = tril.astype(BF16)
    scale = hk**-0.5
    for ci in range(tc // c):
        rs = slice(ci * c, (ci + 1) * c)
        hi, mid, lo = _split3(la_ref[rs, :])
        bcum = _dot(ones_tril, hi) + _dot(ones_tril, mid) + _dot(ones_tril, lo)
        b_last = bcum[c - 1 : c, :]
        e_pos = jnp.exp(bcum)
        e_neg = jnp.exp(-bcum)
        e_end = jnp.exp(b_last - bcum)
        dec = jnp.exp(b_last)
        for h in range(B_HEADS):
            ks = slice(h * hk, (h + 1) * hk)
            vs = slice(h * hv, (h + 1) * hv)
            q = q_ref[rs, ks].astype(F32) * scale
            k = k_ref[rs, ks].astype(F32)
            v = v_ref[rs, vs]
            q_d = (q * e_pos[:, ks]).astype(BF16)
            k_d = (k * e_neg[:, ks]).astype(BF16)
            k_end = (k * e_end[:, ks]).astype(BF16)
            att = jnp.where(tril, _dot_nt(q_d, k_d), 0.0)
            st = st_ref[h]
            o = _dot(att.astype(BF16), v) + _dot_nt(q_d, st.astype(BF16))
            st_ref[h] = dec[:, ks] * st + _dot_tn(v, k_end)
            y = _rms(o, og_ref[...]).astype(BF16).astype(F32)
            r = r_ref[rs, vs].astype(F32)
            o_ref[rs, vs] = (y * (r * jax.nn.sigmoid(r))).astype(BF16)


def gla_chunks(proj, la, o_g, bsz, tc=256):
    n = proj.shape[0]
    dk = la.shape[1]
    dv = (proj.shape[1] - 2 * dk) // 2
    s = n // bsz
    tc = min(tc, s)
    nt = s // tc
    hk, hv = dk // B_HEADS, dv // B_HEADS
    assert (2 * dk) % dv == 0
    v_blk = 2 * dk // dv
    row = lambda b, t: b * nt + t
    return pl.pallas_call(
        functools.partial(_gla_body, tc=tc, hk=hk, hv=hv),
        grid=(bsz, nt),
        in_specs=[
            pl.BlockSpec((tc, dk), lambda b, t: (row(b, t), 0)),
            pl.BlockSpec((tc, dk), lambda b, t: (row(b, t), 1)),
            pl.BlockSpec((tc, dv), lambda b, t: (row(b, t), v_blk)),
            pl.BlockSpec((tc, dv), lambda b, t: (row(b, t), v_blk + 1)),
            pl.BlockSpec((tc, dk), lambda b, t: (row(b, t), 0)),
            pl.BlockSpec((1, hv), lambda b, t: (0, 0)),
        ],
        out_specs=pl.BlockSpec((tc, dv), lambda b, t: (row(b, t), 0)),
        out_shape=jax.ShapeDtypeStruct((n, dv), BF16),
        scratch_shapes=[pltpu.VMEM((B_HEADS, hv, hk), F32)],
        compiler_params=_params("parallel", "arbitrary"),
        name="gla_chunks",
    )(proj, proj, proj, proj, la, o_g)


def _fox_gate_body(x_ref, g_ref, wf_ref, b_ref, ccol_ref, crow_ref, carry_ref, *, tb, nh):
    @pl.when(pl.program_id(1) == 0)
    def _():
        carry_ref[...] = jnp.zeros_like(carry_ref)

    h = _rms(x_ref[...], g_ref[...]).astype(BF16)
    lane = lax.broadcasted_iota(jnp.int32, (tb, V7X_LANES), 1)
    log_f = jnp.where(lane < nh, _log_sigmoid(_dot(h, wf_ref[...]) + b_ref[...]), 0.0)
    ones_tril = _tril_mask(tb).astype(BF16)
    hi, mid, lo = _split3(log_f)
    c = carry_ref[...] + (_dot(ones_tril, hi) + _dot(ones_tril, mid) + _dot(ones_tril, lo))
    carry_ref[...] = c[tb - 1 : tb, :]
    ccol_ref[...] = c
    sel = (lax.broadcasted_iota(jnp.int32, (nh, V7X_LANES), 0) == lax.broadcasted_iota(jnp.int32, (nh, V7X_LANES), 1)).astype(BF16)
    hi, mid, lo = _split3(c)
    crow_ref[...] = _dot_nt(sel, hi) + _dot_nt(sel, mid) + _dot_nt(sel, lo)


def fox_gate(x, g, w_f, f_bias, bsz, tb=512):
    n, d = x.shape
    nh = w_f.shape[1]
    s = n // bsz
    tb = min(tb, s)
    nt = s // tb
    wf = jnp.pad(w_f, ((0, 0), (0, V7X_LANES - nh))).astype(BF16)
    bias = jnp.pad(f_bias.reshape(1, nh), ((0, 0), (0, V7X_LANES - nh)))
    return pl.pallas_call(
        functools.partial(_fox_gate_body, tb=tb, nh=nh),
        grid=(bsz, nt),
        in_specs=[
            pl.BlockSpec((tb, d), lambda b, t: (b * nt + t, 0)),
            pl.BlockSpec((1, d), lambda b, t: (0, 0)),
            pl.BlockSpec((d, V7X_LANES), lambda b, t: (0, 0)),
            pl.BlockSpec((1, V7X_LANES), lambda b, t: (0, 0)),
        ],
        out_specs=[
            pl.BlockSpec((tb, V7X_LANES), lambda b, t: (b * nt + t, 0)),
            pl.BlockSpec((None, nh, tb), lambda b, t: (b, 0, t)),
        ],
        out_shape=[jax.ShapeDtypeStruct((n, V7X_LANES), F32), jax.ShapeDtypeStruct((bsz, nh, s), F32)],
        scratch_shapes=[pltpu.VMEM((1, V7X_LANES), F32)],
        compiler_params=_params("parallel", "arbitrary"),
        name="fox_gate",
    )(x, g, wf, bias)


def _fox_flash_body(q_ref, k_ref, v_ref, ccol_ref, crow_ref, o_ref, m_ref, l_ref, acc_ref, *, t, scale):
    h, i = pl.program_id(1), pl.program_id(2)
    q = q_ref[...]
    lane = lax.broadcasted_iota(jnp.int32, ccol_ref.shape, 1)
    cq = jnp.sum(jnp.where(lane == h, ccol_ref[...], 0.0), axis=-1, keepdims=True)
    m_ref[...] = jnp.full_like(m_ref, -jnp.inf)
    l_ref[...] = jnp.zeros_like(l_ref)
    acc_ref[...] = jnp.zeros_like(acc_ref)

    def block(kb, masked):
        off = pl.multiple_of(kb * t, t)
        k = k_ref[pl.ds(off, t), :]
        v = v_ref[pl.ds(off, t), :]
        s = _dot_nt(q, k) * scale + cq - crow_ref[pl.ds(kb, 1), :]
        if masked:
            s = jnp.where(_tril_mask(t), s, -jnp.inf)
        m_old = m_ref[...]
        m_new = jnp.maximum(m_old, jnp.max(s, axis=-1, keepdims=True))
        alpha = jnp.exp(m_old - m_new)
        p = jnp.exp(s - m_new)
        l_ref[...] = alpha * l_ref[...] + jnp.sum(p, axis=-1, keepdims=True)
        acc_ref[...] = alpha * acc_ref[...] + _dot(p.astype(BF16), v)
        m_ref[...] = m_new

    def off_diag(kb, carry):
        block(kb, False)
        return carry

    lax.fori_loop(0, i, off_diag, 0)
    block(i, True)
    o_ref[...] = (acc_ref[...] / l_ref[...]).astype(o_ref.dtype)


def fox_flash(proj, ccol, crow, bsz, nh, t=512):
    n = proj.shape[0]
    dh = proj.shape[1] // (3 * nh)
    s = n // bsz
    t = min(t, s)
    nq = s // t
    crow = crow.reshape(bsz, nh, nq, t)
    return pl.pallas_call(
        functools.partial(_fox_flash_body, t=t, scale=dh**-0.5),
        grid=(bsz, nh, nq),
        in_specs=[
            pl.BlockSpec((t, dh), lambda b, h, i: (b * nq + i, h)),
            pl.BlockSpec((s, dh), lambda b, h, i: (b, nh + h)),
            pl.BlockSpec((s, dh), lambda b, h, i: (b, 2 * nh + h)),
            pl.BlockSpec((t, V7X_LANES), lambda b, h, i: (b * nq + i, 0)),
            pl.BlockSpec((None, None, nq, t), lambda b, h, i: (b, h, 0, 0)),
        ],
        out_specs=pl.BlockSpec((t, dh), lambda b, h, i: (b * nq + i, h)),
        out_shape=jax.ShapeDtypeStruct((n, nh * dh), BF16),
        scratch_shapes=[pltpu.VMEM((t, 1), F32), pltpu.VMEM((t, 1), F32), pltpu.VMEM((t, dh), F32)],
        compiler_params=_params("parallel", "parallel", "arbitrary"),
        name="fox_flash",
    )(proj, proj, proj, ccol, crow)


def kernel(x, norm1_g, norm2_g, a_w_in, a_vnorm_g, a_w_s, a_b_s, a_w_out, b_w_in, b_w_gate2, b_gate_bias, b_onorm_g, b_w_out, c_w_in, c_f_bias, c_w_out, ffn_w_in, ffn_w_out, moe_router, moe_w_in, moe_w_out, final_g):
    bsz, s, d = x.shape
    depth = norm1_g.shape[0]
    xf = x.reshape(bsz * s, d)
    for i in range(depth):
        g1 = norm1_g[i].reshape(1, d)
        m, j = i % N_MIXERS, i // N_MIXERS
        if m == 0:
            z = norm_matmul(xf, g1, a_w_in[j].astype(BF16), act="gelu")
            xf = gmlp_spatial_out(z, xf, a_vnorm_g[j].reshape(1, -1), a_w_s[j], a_b_s[j].T, a_w_out[j].astype(BF16))
        elif m == 1:
            dk = b_w_gate2.shape[2]
            n_main = b_w_in.shape[2] - B_GATE_RANK
            proj = norm_matmul(xf, g1, b_w_in[j, :, :n_main].astype(BF16))
            la = gla_gate(xf, g1, b_w_in[j, :, n_main:], b_w_gate2[j], b_gate_bias[j].reshape(1, dk))
            o = gla_chunks(proj, la, b_onorm_g[j].reshape(1, -1), bsz)
            xf = matmul_residual(o, b_w_out[j].astype(BF16), xf)
        else:
            proj = norm_matmul(xf, g1, c_w_in[j, :, : 3 * d].astype(BF16))
            ccol, crow = fox_gate(xf, g1, c_w_in[j, :, 3 * d :], c_f_bias[j], bsz)
            o = fox_flash(proj, ccol, crow, bsz, C_HEADS)
            xf = matmul_residual(o, c_w_out[j].astype(BF16), xf)
        g2 = norm2_g[i].reshape(1, d)
        fg = final_g.reshape(1, d) if i == depth - 1 else None
        if i % 2 == 0:
            xf = ffn(xf, g2, ffn_w_in[i // 2][None].astype(BF16), ffn_w_out[i // 2][None].astype(BF16), final_g=fg)
        else:
            gates = router_gates(xf, g2, moe_router[i // 2])
            xf = ffn(xf, g2, moe_w_in[i // 2].astype(BF16), moe_w_out[i // 2].astype(BF16), gates=gates, final_g=fg)
    return xf.reshape(bsz, s, d)
```

```python
import functools

import jax
import jax.numpy as jnp
from jax import lax
from jax.experimental import pallas as pl
from jax.experimental.pallas import tpu as pltpu

F32 = jnp.float32
BF16 = jnp.bfloat16

EPS = 1e-6
N_MIXERS = 3
TOP_K = 2
A_CHUNK = 128
A_GROUPS = 8
B_HEADS = 4
B_GATE_RANK = 16
B_TAU = 16.0
B_CHUNK = 64
C_HEADS = 8

V7X_LANES = 128
V7X_VMEM_LIMIT_BYTES = 56 * 1024 * 1024


def _params(*sem):
    return pltpu.CompilerParams(dimension_semantics=sem, vmem_limit_bytes=V7X_VMEM_LIMIT_BYTES)


def _rms(x, g):
    ms = jnp.mean(x * x, axis=-1, keepdims=True)
    return x * lax.rsqrt(ms + EPS) * g


def _log_sigmoid(x):
    return jnp.minimum(x, 0.0) - jnp.log1p(jnp.exp(-jnp.abs(x)))


def _split3(x):
    hi = x.astype(BF16)
    r1 = x - hi.astype(F32)
    mid = r1.astype(BF16)
    lo = (r1 - mid.astype(F32)).astype(BF16)
    return hi, mid, lo


def _dot(a, b):
    return jnp.dot(a, b, preferred_element_type=F32)


def _dot_nt(a, b):
    return lax.dot_general(a, b, (((1,), (1,)), ((), ())), preferred_element_type=F32)


def _dot_tn(a, b):
    return lax.dot_general(a, b, (((0,), (0,)), ((), ())), preferred_element_type=F32)


def _tril_mask(n):
    row = lax.broadcasted_iota(jnp.int32, (n, n), 0)
    col = lax.broadcasted_iota(jnp.int32, (n, n), 1)
    return col <= row


def _norm_matmul_body(x_ref, g_ref, w_ref, o_ref, h_ref, *, act):
    @pl.when(pl.program_id(1) == 0)
    def _():
        h_ref[...] = _rms(x_ref[...], g_ref[...]).astype(BF16)

    y = _dot(h_ref[...], w_ref[...])
    if act == "gelu":
        y = jax.nn.gelu(y, approximate=True)
    o_ref[...] = y.astype(o_ref.dtype)


def norm_matmul(x, g, w, act=None, tm=1024, tn=512):
    n, d = x.shape
    nout = w.shape[1]
    tm, tn = min(tm, n), min(tn, nout)
    return pl.pallas_call(
        functools.partial(_norm_matmul_body, act=act),
        grid=(n // tm, nout // tn),
        in_specs=[
            pl.BlockSpec((tm, d), lambda i, j: (i, 0)),
            pl.BlockSpec((1, d), lambda i, j: (0, 0)),
            pl.BlockSpec((d, tn), lambda i, j: (0, j)),
        ],
        out_specs=pl.BlockSpec((tm, tn), lambda i, j: (i, j)),
        out_shape=jax.ShapeDtypeStruct((n, nout), BF16),
        scratch_shapes=[pltpu.VMEM((tm, d), BF16)],
        compiler_params=_params("parallel", "arbitrary"),
        name="norm_matmul",
    )(x, g, w)


def _matmul_residual_body(y_ref, w_ref, x_ref, o_ref):
    o_ref[...] = x_ref[...] + _dot(y_ref[...], w_ref[...])


def matmul_residual(y, w, x, tm=512):
    n, k = y.shape
    d = w.shape[1]
    tm = min(tm, n)
    return pl.pallas_call(
        _matmul_residual_body,
        grid=(n // tm,),
        in_specs=[
            pl.BlockSpec((tm, k), lambda i: (i, 0)),
            pl.BlockSpec((k, d), lambda i: (0, 0)),
            pl.BlockSpec((tm, d), lambda i: (i, 0)),
        ],
        out_specs=pl.BlockSpec((tm, d), lambda i: (i, 0)),
        out_shape=jax.ShapeDtypeStruct((n, d), F32),
        compiler_params=_params("parallel"),
        name="matmul_residual",
    )(y, w, x)


def _gmlp_body(z_ref, x_ref, vg_ref, ws_ref, bt_ref, wo_ref, o_ref, vn_ref, y_ref, *, tm, width):
    gd = width // A_GROUPS
    vn_ref[...] = _rms(z_ref[:, width:].astype(F32), vg_ref[...]).astype(BF16)
    tril = _tril_mask(A_CHUNK)
    for g in range(A_GROUPS):
        w = jnp.where(tril, ws_ref[g], 0.0).astype(BF16)
        bias = bt_ref[:, g : g + 1]
        cs = slice(g * gd, (g + 1) * gd)
        for c in range(tm // A_CHUNK):
            rs = slice(c * A_CHUNK, (c + 1) * A_CHUNK)
            mixed = _dot(w, vn_ref[rs, cs]) + bias
            y_ref[rs, cs] = (z_ref[rs, cs].astype(F32) * mixed).astype(BF16)
    o_ref[...] = x_ref[...] + _dot(y_ref[...], wo_ref[...])


def gmlp_spatial_out(z, x, vg, ws, bt, wo, tm=256):
    n, d = x.shape
    width = z.shape[1] // 2
    tm = min(tm, n)
    return pl.pallas_call(
        functools.partial(_gmlp_body, tm=tm, width=width),
        grid=(n // tm,),
        in_specs=[
            pl.BlockSpec((tm, 2 * width), lambda i: (i, 0)),
            pl.BlockSpec((tm, d), lambda i: (i, 0)),
            pl.BlockSpec((1, width), lambda i: (0, 0)),
            pl.BlockSpec((A_GROUPS, A_CHUNK, A_CHUNK), lambda i: (0, 0, 0)),
            pl.BlockSpec((A_CHUNK, A_GROUPS), lambda i: (0, 0)),
            pl.BlockSpec((width, d), lambda i: (0, 0)),
        ],
        out_specs=pl.BlockSpec((tm, d), lambda i: (i, 0)),
        out_shape=jax.ShapeDtypeStruct((n, d), F32),
        scratch_shapes=[pltpu.VMEM((tm, width), BF16), pltpu.VMEM((tm, width), BF16)],
        compiler_params=_params("parallel"),
        name="gmlp_spatial_out",
    )(z, x, vg, ws, bt, wo)


def _swiglu_step(h, wg_ref, wu_ref, wo_ref):
    a = _dot(h, wg_ref[...])
    t = (a * jax.nn.sigmoid(a)) * _dot(h, wu_ref[...])
    return _dot(t.astype(BF16), wo_ref[...])


def _ffn_body(*refs, final):
    x_ref, g_ref, wg_ref, wu_ref, wo_ref = refs[:5]
    rest = list(refs[5:])
    fg_ref = rest.pop(0) if final else None
    o_ref, h_ref, acc_ref = rest
    j = pl.program_id(1)

    @pl.when(j == 0)
    def _():
        h_ref[...] = _rms(x_ref[...], g_ref[...]).astype(BF16)
        acc_ref[...] = jnp.zeros_like(acc_ref)

    acc_ref[...] += _swiglu_step(h_ref[...], wg_ref, wu_ref, wo_ref)

    @pl.when(j == pl.num_programs(1) - 1)
    def _():
        y = x_ref[...] + acc_ref[...]
        o_ref[...] = _rms(y, fg_ref[...]) if final else y


def ffn(x, g, w_in, w_out, final_g=None, tm=1024, tf=512):
    n, d = x.shape
    f = w_out.shape[0]
    tm, tf = min(tm, n), min(tf, f)
    nf = f // tf
    final = final_g is not None
    in_specs = [
        pl.BlockSpec((tm, d), lambda i, j: (i, 0)),
        pl.BlockSpec((1, d), lambda i, j: (0, 0)),
        pl.BlockSpec((d, tf), lambda i, j: (0, j)),
        pl.BlockSpec((d, tf), lambda i, j: (0, j + nf)),
        pl.BlockSpec((tf, d), lambda i, j: (j, 0)),
    ]
    args = [x, g, w_in, w_in, w_out]
    if final:
        in_specs.append(pl.BlockSpec((1, d), lambda i, j: (0, 0)))
        args.append(final_g)
    return pl.pallas_call(
        functools.partial(_ffn_body, final=final),
        grid=(n // tm, nf),
        in_specs=in_specs,
        out_specs=pl.BlockSpec((tm, d), lambda i, j: (i, 0)),
        out_shape=jax.ShapeDtypeStruct((n, d), F32),
        scratch_shapes=[pltpu.VMEM((tm, d), BF16), pltpu.VMEM((tm, d), F32)],
        compiler_params=_params("parallel", "arbitrary"),
        name="ffn",
    )(*args)


TOK_ROWS = 8


def _router_body(x_ref, g_ref, whi_ref, wlo_ref, gate_ref, tok_ref, cnt_ref, carry_ref, *, ne, tm):
    @pl.when(pl.program_id(0) == 0)
    def _():
        carry_ref[...] = jnp.zeros_like(carry_ref)

    h = _rms(x_ref[...], g_ref[...])
    h_hi = h.astype(BF16)
    h_lo = (h - h_hi.astype(F32)).astype(BF16)
    logits = _dot(h_hi, whi_ref[...]) + _dot(h_lo, whi_ref[...]) + _dot(h_hi, wlo_ref[...])
    lane = lax.broadcasted_iota(jnp.int32, logits.shape, 1).astype(F32)
    neg = -jnp.inf
    l1 = jnp.where(lane < ne, logits, neg)
    m1 = jnp.max(l1, axis=-1, keepdims=True)
    i1 = jnp.min(jnp.where(l1 == m1, lane, float(V7X_LANES)), axis=-1, keepdims=True)
    l2 = jnp.where(lane == i1, neg, l1)
    m2 = jnp.max(l2, axis=-1, keepdims=True)
    i2 = jnp.min(jnp.where(l2 == m2, lane, float(V7X_LANES)), axis=-1, keepdims=True)
    e2 = jnp.exp(m2 - m1)
    den = 1.0 + e2
    gate_ref[...] = jnp.where(lane == 0.0, 1.0 / den, 0.0) + jnp.where(lane == 1.0, e2 / den, 0.0)

    sel1, sel2 = lane == i1, lane == i2
    onehot = jnp.where(sel1 | sel2, 1.0, 0.0)
    row = lax.broadcasted_iota(jnp.int32, (tm, tm), 0)
    col = lax.broadcasted_iota(jnp.int32, (tm, tm), 1)
    before = carry_ref[...] + _dot((col < row).astype(BF16), onehot.astype(BF16))
    carry_ref[...] = before[tm - 1 : tm, :] + onehot[tm - 1 : tm, :]
    cnt_ref[...] = carry_ref[...]
    r1 = jnp.sum(jnp.where(sel1, before, 0.0), axis=-1, keepdims=True)
    r2 = jnp.sum(jnp.where(sel2, before, 0.0), axis=-1, keepdims=True)
    table = (jnp.where(lane == 0.0, i1, 0.0) + jnp.where(lane == 1.0, i2, 0.0)
             + jnp.where(lane == 2.0, r1, 0.0) + jnp.where(lane == 3.0, r2, 0.0))
    pick = (lax.broadcasted_iota(jnp.int32, (TOK_ROWS, V7X_LANES), 0) == lax.broadcasted_iota(jnp.int32, (TOK_ROWS, V7X_LANES), 1)).astype(BF16)
    hi, mid, lo = _split3(table)
    tok_ref[...] = (_dot_nt(pick, hi) + _dot_nt(pick, mid) + _dot_nt(pick, lo)).astype(jnp.int32)


def moe_router(x, g, w_router, tm=1024):
    n, d = x.shape
    ne = w_router.shape[1]
    tm = min(tm, n)
    wpad = jnp.pad(w_router, ((0, 0), (0, V7X_LANES - ne)))
    w_hi = wpad.astype(BF16)
    w_lo = (wpad - w_hi.astype(F32)).astype(BF16)
    return pl.pallas_call(
        functools.partial(_router_body, ne=ne, tm=tm),
        grid=(n // tm,),
        in_specs=[
            pl.BlockSpec((tm, d), lambda i: (i, 0)),
            pl.BlockSpec((1, d), lambda i: (0, 0)),
            pl.BlockSpec((d, V7X_LANES), lambda i: (0, 0)),
            pl.BlockSpec((d, V7X_LANES), lambda i: (0, 0)),
        ],
        out_specs=[
            pl.BlockSpec((tm, V7X_LANES), lambda i: (i, 0)),
            pl.BlockSpec((TOK_ROWS, tm), lambda i: (0, i)),
            pl.BlockSpec((1, V7X_LANES), lambda i: (0, 0)),
        ],
        out_shape=[
            jax.ShapeDtypeStruct((n, V7X_LANES), F32),
            jax.ShapeDtypeStruct((TOK_ROWS, n), jnp.int32),
            jax.ShapeDtypeStruct((1, V7X_LANES), F32),
        ],
        scratch_shapes=[pltpu.VMEM((1, V7X_LANES), F32)],
        compiler_params=_params("arbitrary"),
        name="moe_router",
    )(x, g, w_hi, w_lo)


def _plan_body(cnt_ref, off_ref, te_ref, nv_ref, *, ne, tg, n_tiles):
    off = jnp.int32(0)
    tile = jnp.int32(0)
    for e in range(ne):
        nt = (cnt_ref[e] + (tg - 1)) // tg
        off_ref[e] = off

        def mark(ti, carry, e=e):
            te_ref[ti] = jnp.int32(e)
            return carry

        lax.fori_loop(tile, tile + nt, mark, 0)
        off = off + nt * tg
        tile = tile + nt
    nv_ref[0] = tile
    last = te_ref[jnp.maximum(tile - 1, 0)]

    def fill(ti, carry):
        te_ref[ti] = last
        return carry

    lax.fori_loop(tile, n_tiles, fill, 0)


def moe_plan(cnt, tg, n_tiles):
    ne = cnt.shape[0]
    smem = pl.BlockSpec(memory_space=pltpu.SMEM)
    return pl.pallas_call(
        functools.partial(_plan_body, ne=ne, tg=tg, n_tiles=n_tiles),
        in_specs=[smem],
        out_specs=[smem, smem, smem],
        out_shape=[
            jax.ShapeDtypeStruct((ne,), jnp.int32),
            jax.ShapeDtypeStruct((n_tiles,), jnp.int32),
            jax.ShapeDtypeStruct((1,), jnp.int32),
        ],
        name="moe_plan",
    )(cnt)


def _row_copy(src_hbm, s, dst, r, sem):
    return pltpu.make_async_copy(src_hbm.at[pl.ds(s, 1)], dst.at[pl.ds(r, 1)], sem)


def _dispatch_body(tok_hbm, off_ref, cnt_ref, zero_hbm, x_hbm, xs_hbm, tok_smem, sem_idx, sem_row, *, n, td, tg, ne):
    for e in range(ne):
        start = off_ref[e] + cnt_ref[e]
        stop = off_ref[e] + (cnt_ref[e] + (tg - 1)) // tg * tg

        def zero_start(r, carry):
            _row_copy(zero_hbm, 0, xs_hbm, r, sem_row).start()
            return carry

        def zero_wait(r, carry):
            _row_copy(zero_hbm, 0, xs_hbm, r, sem_row).wait()
            return carry

        lax.fori_loop(start, stop, zero_start, 0)
        lax.fori_loop(start, stop, zero_wait, 0)

    def chunk(ci, carry):
        base = pl.multiple_of(ci * td, td)
        idx = pltpu.make_async_copy(tok_hbm.at[:, pl.ds(base, td)], tok_smem, sem_idx)
        idx.start()
        idx.wait()

        def copies(t):
            return [_row_copy(x_hbm, base + t, xs_hbm, off_ref[tok_smem[k, t]] + tok_smem[TOP_K + k, t], sem_row) for k in range(TOP_K)]

        def start(t, c):
            for cp in copies(t):
                cp.start()
            return c

        def wait(t, c):
            for cp in copies(t):
                cp.wait()
            return c

        lax.fori_loop(0, td, start, 0)
        lax.fori_loop(0, td, wait, 0)
        return carry

    lax.fori_loop(0, n // td, chunk, 0)


def moe_dispatch(tok, off, cnt, zero_rows, x, tg, n_tiles, td=2048):
    n, d = x.shape
    td = min(td, n)
    smem = pl.BlockSpec(memory_space=pltpu.SMEM)
    hbm = pl.BlockSpec(memory_space=pl.ANY)
    return pl.pallas_call(
        functools.partial(_dispatch_body, n=n, td=td, tg=tg, ne=cnt.shape[0]),
        in_specs=[hbm, smem, smem, hbm, hbm],
        out_specs=hbm,
        out_shape=jax.ShapeDtypeStruct((n_tiles * tg, d), x.dtype),
        scratch_shapes=[pltpu.SMEM((TOK_ROWS, td), jnp.int32), pltpu.SemaphoreType.DMA(()), pltpu.SemaphoreType.DMA(())],
        compiler_params=pltpu.CompilerParams(has_side_effects=True),
        name="moe_dispatch",
    )(tok, off, cnt, zero_rows, x)


def _grouped_ffn_body(te_ref, nv_ref, x_ref, g_ref, wg_ref, wu_ref, wo_ref, o_ref, h_ref):
    i, j = pl.program_id(0), pl.program_id(1)

    @pl.when(i < nv_ref[0])
    def _():
        @pl.when(j == 0)
        def _():
            h_ref[...] = _rms(x_ref[...], g_ref[...]).astype(BF16)

        y = _swiglu_step(h_ref[...], wg_ref, wu_ref, wo_ref)

        @pl.when(j == 0)
        def _():
            o_ref[...] = y

        @pl.when(j > 0)
        def _():
            o_ref[...] += y


def moe_grouped_ffn(te, nv, xs, g, w_in, w_out, tg, tf=512):
    r, d = xs.shape
    f = w_out.shape[1]
    tf = min(tf, f)
    nf = f // tf
    n_tiles = r // tg

    def tile(i, nv):
        return jnp.minimum(i, nv[0] - 1)

    def fblk(i, j, nv):
        return jnp.where(i < nv[0], j, nf - 1)

    grid_spec = pltpu.PrefetchScalarGridSpec(
        num_scalar_prefetch=2,
        grid=(n_tiles, nf),
        in_specs=[
            pl.BlockSpec((tg, d), lambda i, j, te, nv: (tile(i, nv), 0)),
            pl.BlockSpec((1, d), lambda i, j, te, nv: (0, 0)),
            pl.BlockSpec((None, d, tf), lambda i, j, te, nv: (te[i], 0, fblk(i, j, nv))),
            pl.BlockSpec((None, d, tf), lambda i, j, te, nv: (te[i], 0, fblk(i, j, nv) + nf)),
            pl.BlockSpec((None, tf, d), lambda i, j, te, nv: (te[i], fblk(i, j, nv), 0)),
        ],
        out_specs=pl.BlockSpec((tg, d), lambda i, j, te, nv: (tile(i, nv), 0)),
        scratch_shapes=[pltpu.VMEM((tg, d), BF16)],
    )
    return pl.pallas_call(
        _grouped_ffn_body,
        grid_spec=grid_spec,
        out_shape=jax.ShapeDtypeStruct((r, d), F32),
        compiler_params=_params("arbitrary", "arbitrary"),
        name="moe_grouped_ffn",
    )(te, nv, xs, g, w_in, w_in, w_out)


def _combine_body(tok_hbm, off_ref, gate_ref, x_ref, ys_hbm, *rest, tc, final):
    rest = list(rest)
    fg_ref = rest.pop(0) if final else None
    o_ref, tok_smem, buf_ref, sem_idx, sem_row = rest
    base = pl.multiple_of(pl.program_id(0) * tc, tc)
    idx = pltpu.make_async_copy(tok_hbm.at[:, pl.ds(base, tc)], tok_smem, sem_idx)
    idx.start()
    idx.wait()

    def copies(t):
        return [_row_copy(ys_hbm, off_ref[tok_smem[k, t]] + tok_smem[TOP_K + k, t], buf_ref.at[k], t, sem_row) for k in range(TOP_K)]

    def start(t, c):
        for cp in copies(t):
            cp.start()
        return c

    def wait(t, c):
        for cp in copies(t):
            cp.wait()
        return c

    lax.fori_loop(0, tc, start, 0)
    lax.fori_loop(0, tc, wait, 0)
    y = x_ref[...] + gate_ref[:, 0:1] * buf_ref[0] + gate_ref[:, 1:2] * buf_ref[1]
    o_ref[...] = _rms(y, fg_ref[...]) if final else y


def moe_combine(tok, off, gates, x, ys, final_g=None, tc=512):
    n, d = x.shape
    tc = min(tc, n)
    final = final_g is not None
    smem = pl.BlockSpec(memory_space=pltpu.SMEM)
    hbm = pl.BlockSpec(memory_space=pl.ANY)
    in_specs = [hbm, smem, pl.BlockSpec((tc, V7X_LANES), lambda i: (i, 0)), pl.BlockSpec((tc, d), lambda i: (i, 0)), hbm]
    args = [tok, off, gates, x, ys]
    if final:
        in_specs.append(pl.BlockSpec((1, d), lambda i: (0, 0)))
        args.append(final_g)
    return pl.pallas_call(
        functools.partial(_combine_body, tc=tc, final=final),
        grid=(n // tc,),
        in_specs=in_specs,
        out_specs=pl.BlockSpec((tc, d), lambda i: (i, 0)),
        out_shape=jax.ShapeDtypeStruct((n, d), F32),
        scratch_shapes=[
            pltpu.SMEM((TOK_ROWS, tc), jnp.int32),
            pltpu.VMEM((TOP_K, tc, d), F32),
            pltpu.SemaphoreType.DMA(()),
            pltpu.SemaphoreType.DMA(()),
        ],
        compiler_params=_params("arbitrary"),
        name="moe_combine",
    )(*args)


def moe_ffn(x, g, w_router, w_in, w_out, final_g=None, tg=512):
    n, _ = x.shape
    ne = w_router.shape[1]
    tg = min(tg, n)
    n_tiles = TOP_K * n // tg + ne
    gates, tok, counts = moe_router(x, g, w_router)
    cnt = counts[0, :ne].astype(jnp.int32)
    off, te, nv = moe_plan(cnt, tg, n_tiles)
    xs = moe_dispatch(tok, off, cnt, jnp.zeros((8, x.shape[1]), x.dtype), x, tg, n_tiles)
    ys = moe_grouped_ffn(te, nv, xs, g, w_in, w_out, tg)
    return moe_combine(tok, off, gates, x, ys, final_g)


def _gla_gate_body(x_ref, g_ref, wl_ref, w2_ref, b_ref, o_ref):
    h = _rms(x_ref[...], g_ref[...]).astype(BF16)
    g_low = _dot(h, wl_ref[...]).astype(BF16)
    o_ref[...] = _log_sigmoid(_dot(g_low, w2_ref[...]) + b_ref[...]) / B_TAU


def gla_gate(x, g, w_low, w_gate2, bias, tm=1024):
    n, d = x.shape
    dk = w_gate2.shape[1]
    tm = min(tm, n)
    wl = jnp.pad(w_low, ((0, 0), (0, V7X_LANES - B_GATE_RANK))).astype(BF16)
    w2 = jnp.pad(w_gate2, ((0, V7X_LANES - B_GATE_RANK), (0, 0))).astype(BF16)
    return pl.pallas_call(
        _gla_gate_body,
        grid=(n // tm,),
        in_specs=[
            pl.BlockSpec((tm, d), lambda i: (i, 0)),
            pl.BlockSpec((1, d), lambda i: (0, 0)),
            pl.BlockSpec((d, V7X_LANES), lambda i: (0, 0)),
            pl.BlockSpec((V7X_LANES, dk), lambda i: (0, 0)),
            pl.BlockSpec((1, dk), lambda i: (0, 0)),
        ],
        out_specs=pl.BlockSpec((tm, dk), lambda i: (i, 0)),
        out_shape=jax.ShapeDtypeStruct((n, dk), F32),
        compiler_params=_params("parallel"),
        name="gla_gate",
    )(x, g, wl, w2, bias)


def _gla_body(q_ref, k_ref, v_ref, r_ref, la_ref, og_ref, o_ref, st_ref, *, tc, hk, hv):
    @pl.when(pl.program_id(1) == 0)
    def _():
        st_ref[...] = jnp.zeros_like(st_ref)

    c = B_CHUNK
    tril = _tril_mask(c)
    ones_tril = tril.astype(BF16)
    scale = hk**-0.5
    for ci in range(tc // c):
        rs = slice(ci * c, (ci + 1) * c)
        hi, mid, lo = _split3(la_ref[rs, :])
        bcum = _dot(ones_tril, hi) + _dot(ones_tril, mid) + _dot(ones_tril, lo)
        b_last = bcum[c - 1 : c, :]
        e_pos = jnp.exp(bcum)
        e_neg = jnp.exp(-bcum)
        e_end = jnp.exp(b_last - bcum)
        dec = jnp.exp(b_last)
        for h in range(B_HEADS):
            ks = slice(h * hk, (h + 1) * hk)
            vs = slice(h * hv, (h + 1) * hv)
            q = q_ref[rs, ks].astype(F32) * scale
            k = k_ref[rs, ks].astype(F32)
            v = v_ref[rs, vs]
            q_d = (q * e_pos[:, ks]).astype(BF16)
            k_d = (k * e_neg[:, ks]).astype(BF16)
            k_end = (k * e_end[:, ks]).astype(BF16)
            att = jnp.where(tril, _dot_nt(q_d, k_d), 0.0)
            st = st_ref[h]
            o = _dot(att.astype(BF16), v) + _dot_nt(q_d, st.astype(BF16))
            st_ref[h] = dec[:, ks] * st + _dot_tn(v, k_end)
            y = _rms(o, og_ref[...]).astype(BF16).astype(F32)
            r = r_ref[rs, vs].astype(F32)
            o_ref[rs, vs] = (y * (r * jax.nn.sigmoid(r))).astype(BF16)


def gla_chunks(proj, la, o_g, bsz, tc=256):
    n = proj.shape[0]
    dk = la.shape[1]
    dv = (proj.shape[1] - 2 * dk) // 2
    s = n // bsz
    tc = min(tc, s)
    nt = s // tc
    hk, hv = dk // B_HEADS, dv // B_HEADS
    assert (2 * dk) % dv == 0
    v_blk = 2 * dk // dv
    row = lambda b, t: b * nt + t
    return pl.pallas_call(
        functools.partial(_gla_body, tc=tc, hk=hk, hv=hv),
        grid=(bsz, nt),
        in_specs=[
            pl.BlockSpec((tc, dk), lambda b, t: (row(b, t), 0)),
            pl.BlockSpec((tc, dk), lambda b, t: (row(b, t), 1)),
            pl.BlockSpec((tc, dv), lambda b, t: (row(b, t), v_blk)),
            pl.BlockSpec((tc, dv), lambda b, t: (row(b, t), v_blk + 1)),
            pl.BlockSpec((tc, dk), lambda b, t: (row(b, t), 0)),
            pl.BlockSpec((1, hv), lambda b, t: (0, 0)),
        ],
        out_specs=pl.BlockSpec((tc, dv), lambda b, t: (row(b, t), 0)),
        out_shape=jax.ShapeDtypeStruct((n, dv), BF16),
        scratch_shapes=[pltpu.VMEM((B_HEADS, hv, hk), F32)],
        compiler_params=_params("parallel", "arbitrary"),
        name="gla_chunks",
    )(proj, proj, proj, proj, la, o_g)


def _fox_gate_body(x_ref, g_ref, wf_ref, b_ref, ccol_ref, crow_ref, carry_ref, *, tb, nh):
    @pl.when(pl.program_id(1) == 0)
    def _():
        carry_ref[...] = jnp.zeros_like(carry_ref)

    h = _rms(x_ref[...], g_ref[...]).astype(BF16)
    lane = lax.broadcasted_iota(jnp.int32, (tb, V7X_LANES), 1)
    log_f = jnp.where(lane < nh, _log_sigmoid(_dot(h, wf_ref[...]) + b_ref[...]), 0.0)
    ones_tril = _tril_mask(tb).astype(BF16)
    hi, mid, lo = _split3(log_f)
    c = carry_ref[...] + (_dot(ones_tril, hi) + _dot(ones_tril, mid) + _dot(ones_tril, lo))
    carry_ref[...] = c[tb - 1 : tb, :]
    ccol_ref[...] = c
    sel = (lax.broadcasted_iota(jnp.int32, (nh, V7X_LANES), 0) == lax.broadcasted_iota(jnp.int32, (nh, V7X_LANES), 1)).astype(BF16)
    hi, mid, lo = _split3(c)
    crow_ref[...] = _dot_nt(sel, hi) + _dot_nt(sel, mid) + _dot_nt(sel, lo)


def fox_gate(x, g, w_f, f_bias, bsz, tb=512):
    n, d = x.shape
    nh = w_f.shape[1]
    s = n // bsz
    tb = min(tb, s)
    nt = s // tb
    wf = jnp.pad(w_f, ((0, 0), (0, V7X_LANES - nh))).astype(BF16)
    bias = jnp.pad(f_bias.reshape(1, nh), ((0, 0), (0, V7X_LANES - nh)))
    return pl.pallas_call(
        functools.partial(_fox_gate_body, tb=tb, nh=nh),
        grid=(bsz, nt),
        in_specs=[
            pl.BlockSpec((tb, d), lambda b, t: (b * nt + t, 0)),
            pl.BlockSpec((1, d), lambda b, t: (0, 0)),
            pl.BlockSpec((d, V7X_LANES), lambda b, t: (0, 0)),
            pl.BlockSpec((1, V7X_LANES), lambda b, t: (0, 0)),
        ],
        out_specs=[
            pl.BlockSpec((tb, V7X_LANES), lambda b, t: (b * nt + t, 0)),
            pl.BlockSpec((None, nh, tb), lambda b, t: (b, 0, t)),
        ],
        out_shape=[jax.ShapeDtypeStruct((n, V7X_LANES), F32), jax.ShapeDtypeStruct((bsz, nh, s), F32)],
        scratch_shapes=[pltpu.VMEM((1, V7X_LANES), F32)],
        compiler_params=_params("parallel", "arbitrary"),
        name="fox_gate",
    )(x, g, wf, bias)


def _fox_flash_body(q_ref, k_ref, v_ref, ccol_ref, crow_ref, o_ref, m_ref, l_ref, acc_ref, *, t, scale):
    h, i = pl.program_id(1), pl.program_id(2)
    q = q_ref[...]
    lane = lax.broadcasted_iota(jnp.int32, ccol_ref.shape, 1)
    cq = jnp.sum(jnp.where(lane == h, ccol_ref[...], 0.0), axis=-1, keepdims=True)
    m_ref[...] = jnp.full_like(m_ref, -jnp.inf)
    l_ref[...] = jnp.zeros_like(l_ref)
    acc_ref[...] = jnp.zeros_like(acc_ref)

    def block(kb, masked):
        off = pl.multiple_of(kb * t, t)
        k = k_ref[pl.ds(off, t), :]
        v = v_ref[pl.ds(off, t), :]
        s = _dot_nt(q, k) * scale + cq - crow_ref[pl.ds(kb, 1), :]
        if masked:
            s = jnp.where(_tril_mask(t), s, -jnp.inf)
        m_old = m_ref[...]
        m_new = jnp.maximum(m_old, jnp.max(s, axis=-1, keepdims=True))
        alpha = jnp.exp(m_old - m_new)
        p = jnp.exp(s - m_new)
        l_ref[...] = alpha * l_ref[...] + jnp.sum(p, axis=-1, keepdims=True)
        acc_ref[...] = alpha * acc_ref[...] + _dot(p.astype(BF16), v)
        m_ref[...] = m_new

    def off_diag(kb, carry):
        block(kb, False)
        return carry

    lax.fori_loop(0, i, off_diag, 0)
    block(i, True)
    o_ref[...] = (acc_ref[...] / l_ref[...]).astype(o_ref.dtype)


def fox_flash(proj, ccol, crow, bsz, nh, t=512):
    n = proj.shape[0]
    dh = proj.shape[1] // (3 * nh)
    s = n // bsz
    t = min(t, s)
    nq = s // t
    crow = crow.reshape(bsz, nh, nq, t)
    return pl.pallas_call(
        functools.partial(_fox_flash_body, t=t, scale=dh**-0.5),
        grid=(bsz, nh, nq),
        in_specs=[
            pl.BlockSpec((t, dh), lambda b, h, i: (b * nq + i, h)),
            pl.BlockSpec((s, dh), lambda b, h, i: (b, nh + h)),
            pl.BlockSpec((s, dh), lambda b, h, i: (b, 2 * nh + h)),
            pl.BlockSpec((t, V7X_LANES), lambda b, h, i: (b * nq + i, 0)),
            pl.BlockSpec((None, None, nq, t), lambda b, h, i: (b, h, 0, 0)),
        ],
        out_specs=pl.BlockSpec((t, dh), lambda b, h, i: (b * nq + i, h)),
        out_shape=jax.ShapeDtypeStruct((n, nh * dh), BF16),
        scratch_shapes=[pltpu.VMEM((t, 1), F32), pltpu.VMEM((t, 1), F32), pltpu.VMEM((t, dh), F32)],
        compiler_params=_params("parallel", "parallel", "arbitrary"),
        name="fox_flash",
    )(proj, proj, proj, ccol, crow)


def kernel(x, norm1_g, norm2_g, a_w_in, a_vnorm_g, a_w_s, a_b_s, a_w_out, b_w_in, b_w_gate2, b_gate_bias, b_onorm_g, b_w_out, c_w_in, c_f_bias, c_w_out, ffn_w_in, ffn_w_out, moe_router, moe_w_in, moe_w_out, final_g):
    bsz, s, d = x.shape
    depth = norm1_g.shape[0]
    xf = x.reshape(bsz * s, d)
    for i in range(depth):
        g1 = norm1_g[i].reshape(1, d)
        m, j = i % N_MIXERS, i // N_MIXERS
        if m == 0:
            z = norm_matmul(xf, g1, a_w_in[j].astype(BF16), act="gelu")
            xf = gmlp_spatial_out(z, xf, a_vnorm_g[j].reshape(1, -1), a_w_s[j], a_b_s[j].T, a_w_out[j].astype(BF16))
        elif m == 1:
            dk = b_w_gate2.shape[2]
            n_main = b_w_in.shape[2] - B_GATE_RANK
            proj = norm_matmul(xf, g1, b_w_in[j, :, :n_main].astype(BF16))
            la = gla_gate(xf, g1, b_w_in[j, :, n_main:], b_w_gate2[j], b_gate_bias[j].reshape(1, dk))
            o = gla_chunks(proj, la, b_onorm_g[j].reshape(1, -1), bsz)
            xf = matmul_residual(o, b_w_out[j].astype(BF16), xf)
        else:
            proj = norm_matmul(xf, g1, c_w_in[j, :, : 3 * d].astype(BF16))
            ccol, crow = fox_gate(xf, g1, c_w_in[j, :, 3 * d :], c_f_bias[j], bsz)
            o = fox_flash(proj, ccol, crow, bsz, C_HEADS)
            xf = matmul_residual(o, c_w_out[j].astype(BF16), xf)
        g2 = norm2_g[i].reshape(1, d)
        fg = final_g.reshape(1, d) if i == depth - 1 else None
        if i % 2 == 0:
            xf = ffn(xf, g2, ffn_w_in[i // 2].astype(BF16), ffn_w_out[i // 2].astype(BF16), final_g=fg)
        else:
            xf = moe_ffn(xf, g2, moe_router[i // 2], moe_w_in[i // 2].astype(BF16), moe_w_out[i // 2].astype(BF16), final_g=fg)
    return xf.reshape(bsz, s, d)
```

```python
import functools

import jax
import jax.numpy as jnp
from jax import lax
from jax.experimental import pallas as pl
from jax.experimental.pallas import tpu as pltpu

F32 = jnp.float32
BF16 = jnp.bfloat16

EPS = 1e-6
N_MIXERS = 3
TOP_K = 2
A_CHUNK = 128
A_GROUPS = 8
B_HEADS = 4
B_GATE_RANK = 16
B_TAU = 16.0
B_CHUNK = 64
C_HEADS = 8

V7X_LANES = 128
V7X_VMEM_LIMIT_BYTES = 56 * 1024 * 1024


def _params(*sem):
    return pltpu.CompilerParams(dimension_semantics=sem, vmem_limit_bytes=V7X_VMEM_LIMIT_BYTES)


def _rms(x, g):
    ms = jnp.mean(x * x, axis=-1, keepdims=True)
    return x * lax.rsqrt(ms + EPS) * g


def _log_sigmoid(x):
    return jnp.minimum(x, 0.0) - jnp.log1p(jnp.exp(-jnp.abs(x)))


def _split3(x):
    hi = x.astype(BF16)
    r1 = x - hi.astype(F32)
    mid = r1.astype(BF16)
    lo = (r1 - mid.astype(F32)).astype(BF16)
    return hi, mid, lo


def _dot(a, b):
    return jnp.dot(a, b, preferred_element_type=F32)


def _dot_nt(a, b):
    return lax.dot_general(a, b, (((1,), (1,)), ((), ())), preferred_element_type=F32)


def _dot_tn(a, b):
    return lax.dot_general(a, b, (((0,), (0,)), ((), ())), preferred_element_type=F32)


def _tril_mask(n):
    row = lax.broadcasted_iota(jnp.int32, (n, n), 0)
    col = lax.broadcasted_iota(jnp.int32, (n, n), 1)
    return col <= row


def _norm_matmul_body(x_ref, g_ref, w_ref, o_ref, h_ref, *, act):
    @pl.when(pl.program_id(1) == 0)
    def _():
        h_ref[...] = _rms(x_ref[...], g_ref[...]).astype(BF16)

    y = _dot(h_ref[...], w_ref[...])
    if act == "gelu":
        y = jax.nn.gelu(y, approximate=True)
    o_ref[...] = y.astype(o_ref.dtype)


def norm_matmul(x, g, w, act=None, tm=1024, tn=512):
    n, d = x.shape
    nout = w.shape[1]
    tm, tn = min(tm, n), min(tn, nout)
    return pl.pallas_call(
        functools.partial(_norm_matmul_body, act=act),
        grid=(n // tm, nout // tn),
        in_specs=[
            pl.BlockSpec((tm, d), lambda i, j: (i, 0)),
            pl.BlockSpec((1, d), lambda i, j: (0, 0)),
            pl.BlockSpec((d, tn), lambda i, j: (0, j)),
        ],
        out_specs=pl.BlockSpec((tm, tn), lambda i, j: (i, j)),
        out_shape=jax.ShapeDtypeStruct((n, nout), BF16),
        scratch_shapes=[pltpu.VMEM((tm, d), BF16)],
        compiler_params=_params("parallel", "arbitrary"),
        name="norm_matmul",
    )(x, g, w)


def _matmul_residual_body(y_ref, w_ref, x_ref, o_ref):
    o_ref[...] = x_ref[...] + _dot(y_ref[...], w_ref[...])


def matmul_residual(y, w, x, tm=512):
    n, k = y.shape
    d = w.shape[1]
    tm = min(tm, n)
    return pl.pallas_call(
        _matmul_residual_body,
        grid=(n // tm,),
        in_specs=[
            pl.BlockSpec((tm, k), lambda i: (i, 0)),
            pl.BlockSpec((k, d), lambda i: (0, 0)),
            pl.BlockSpec((tm, d), lambda i: (i, 0)),
        ],
        out_specs=pl.BlockSpec((tm, d), lambda i: (i, 0)),
        out_shape=jax.ShapeDtypeStruct((n, d), F32),
        compiler_params=_params("parallel"),
        name="matmul_residual",
    )(y, w, x)


def _gmlp_body(z_ref, x_ref, vg_ref, ws_ref, bt_ref, wo_ref, o_ref, vn_ref, y_ref, *, tm, width):
    gd = width // A_GROUPS
    vn_ref[...] = _rms(z_ref[:, width:].astype(F32), vg_ref[...]).astype(BF16)
    tril = _tril_mask(A_CHUNK)
    for g in range(A_GROUPS):
        w = jnp.where(tril, ws_ref[g], 0.0).astype(BF16)
        bias = bt_ref[:, g : g + 1]
        cs = slice(g * gd, (g + 1) * gd)
        for c in range(tm // A_CHUNK):
            rs = slice(c * A_CHUNK, (c + 1) * A_CHUNK)
            mixed = _dot(w, vn_ref[rs, cs]) + bias
            y_ref[rs, cs] = (z_ref[rs, cs].astype(F32) * mixed).astype(BF16)
    o_ref[...] = x_ref[...] + _dot(y_ref[...], wo_ref[...])


def gmlp_spatial_out(z, x, vg, ws, bt, wo, tm=256):
    n, d = x.shape
    width = z.shape[1] // 2
    tm = min(tm, n)
    return pl.pallas_call(
        functools.partial(_gmlp_body, tm=tm, width=width),
        grid=(n // tm,),
        in_specs=[
            pl.BlockSpec((tm, 2 * width), lambda i: (i, 0)),
            pl.BlockSpec((tm, d), lambda i: (i, 0)),
            pl.BlockSpec((1, width), lambda i: (0, 0)),
            pl.BlockSpec((A_GROUPS, A_CHUNK, A_CHUNK), lambda i: (0, 0, 0)),
            pl.BlockSpec((A_CHUNK, A_GROUPS), lambda i: (0, 0)),
            pl.BlockSpec((width, d), lambda i: (0, 0)),
        ],
        out_specs=pl.BlockSpec((tm, d), lambda i: (i, 0)),
        out_shape=jax.ShapeDtypeStruct((n, d), F32),
        scratch_shapes=[pltpu.VMEM((tm, width), BF16), pltpu.VMEM((tm, width), BF16)],
        compiler_params=_params("parallel"),
        name="gmlp_spatial_out",
    )(z, x, vg, ws, bt, wo)


def _swiglu_step(h, wg_ref, wu_ref, wo_ref):
    a = _dot(h, wg_ref[...])
    t = (a * jax.nn.sigmoid(a)) * _dot(h, wu_ref[...])
    return _dot(t.astype(BF16), wo_ref[...])


def _ffn_body(*refs, final):
    x_ref, g_ref, wg_ref, wu_ref, wo_ref = refs[:5]
    rest = list(refs[5:])
    fg_ref = rest.pop(0) if final else None
    o_ref, h_ref, acc_ref = rest
    j = pl.program_id(1)

    @pl.when(j == 0)
    def _():
        h_ref[...] = _rms(x_ref[...], g_ref[...]).astype(BF16)
        acc_ref[...] = jnp.zeros_like(acc_ref)

    acc_ref[...] += _swiglu_step(h_ref[...], wg_ref, wu_ref, wo_ref)

    @pl.when(j == pl.num_programs(1) - 1)
    def _():
        y = x_ref[...] + acc_ref[...]
        o_ref[...] = _rms(y, fg_ref[...]) if final else y


def ffn(x, g, w_in, w_out, final_g=None, tm=1024, tf=512):
    n, d = x.shape
    f = w_out.shape[0]
    tm, tf = min(tm, n), min(tf, f)
    nf = f // tf
    final = final_g is not None
    in_specs = [
        pl.BlockSpec((tm, d), lambda i, j: (i, 0)),
        pl.BlockSpec((1, d), lambda i, j: (0, 0)),
        pl.BlockSpec((d, tf), lambda i, j: (0, j)),
        pl.BlockSpec((d, tf), lambda i, j: (0, j + nf)),
        pl.BlockSpec((tf, d), lambda i, j: (j, 0)),
    ]
    args = [x, g, w_in, w_in, w_out]
    if final:
        in_specs.append(pl.BlockSpec((1, d), lambda i, j: (0, 0)))
        args.append(final_g)
    return pl.pallas_call(
        functools.partial(_ffn_body, final=final),
        grid=(n // tm, nf),
        in_specs=in_specs,
        out_specs=pl.BlockSpec((tm, d), lambda i, j: (i, 0)),
        out_shape=jax.ShapeDtypeStruct((n, d), F32),
        scratch_shapes=[pltpu.VMEM((tm, d), BF16), pltpu.VMEM((tm, d), F32)],
        compiler_params=_params("parallel", "arbitrary"),
        name="ffn",
    )(*args)


TOK_ROWS = 8


def _router_body(x_ref, g_ref, whi_ref, wlo_ref, gate_ref, tok_ref, cnt_ref, carry_ref, *, ne, tm):
    @pl.when(pl.program_id(0) == 0)
    def _():
        carry_ref[...] = jnp.zeros_like(carry_ref)

    h = _rms(x_ref[...], g_ref[...])
    h_hi = h.astype(BF16)
    h_lo = (h - h_hi.astype(F32)).astype(BF16)
    logits = _dot(h_hi, whi_ref[...]) + _dot(h_lo, whi_ref[...]) + _dot(h_hi, wlo_ref[...])
    lane = lax.broadcasted_iota(jnp.int32, logits.shape, 1).astype(F32)
    neg = -jnp.inf
    l1 = jnp.where(lane < ne, logits, neg)
    m1 = jnp.max(l1, axis=-1, keepdims=True)
    i1 = jnp.min(jnp.where(l1 == m1, lane, float(V7X_LANES)), axis=-1, keepdims=True)
    l2 = jnp.where(lane == i1, neg, l1)
    m2 = jnp.max(l2, axis=-1, keepdims=True)
    i2 = jnp.min(jnp.where(l2 == m2, lane, float(V7X_LANES)), axis=-1, keepdims=True)
    e2 = jnp.exp(m2 - m1)
    den = 1.0 + e2
    gate_ref[...] = jnp.where(lane == 0.0, 1.0 / den, 0.0) + jnp.where(lane == 1.0, e2 / den, 0.0)

    sel1, sel2 = lane == i1, lane == i2
    onehot = jnp.where(sel1 | sel2, 1.0, 0.0)
    row = lax.broadcasted_iota(jnp.int32, (tm, tm), 0)
    col = lax.broadcasted_iota(jnp.int32, (tm, tm), 1)
    before = carry_ref[...] + _dot((col < row).astype(BF16), onehot.astype(BF16))
    carry_ref[...] = before[tm - 1 : tm, :] + onehot[tm - 1 : tm, :]
    cnt_ref[...] = carry_ref[...]
    r1 = jnp.sum(jnp.where(sel1, before, 0.0), axis=-1, keepdims=True)
    r2 = jnp.sum(jnp.where(sel2, before, 0.0), axis=-1, keepdims=True)
    table = (jnp.where(lane == 0.0, i1, 0.0) + jnp.where(lane == 1.0, i2, 0.0)
             + jnp.where(lane == 2.0, r1, 0.0) + jnp.where(lane == 3.0, r2, 0.0))
    pick = (lax.broadcasted_iota(jnp.int32, (TOK_ROWS, V7X_LANES), 0) == lax.broadcasted_iota(jnp.int32, (TOK_ROWS, V7X_LANES), 1)).astype(BF16)
    hi, mid, lo = _split3(table)
    tok_ref[...] = (_dot_nt(pick, hi) + _dot_nt(pick, mid) + _dot_nt(pick, lo)).astype(jnp.int32)


def moe_router(x, g, w_router, tm=1024):
    n, d = x.shape
    ne = w_router.shape[1]
    tm = min(tm, n)
    wpad = jnp.pad(w_router, ((0, 0), (0, V7X_LANES - ne)))
    w_hi = wpad.astype(BF16)
    w_lo = (wpad - w_hi.astype(F32)).astype(BF16)
    return pl.pallas_call(
        functools.partial(_router_body, ne=ne, tm=tm),
        grid=(n // tm,),
        in_specs=[
            pl.BlockSpec((tm, d), lambda i: (i, 0)),
            pl.BlockSpec((1, d), lambda i: (0, 0)),
            pl.BlockSpec((d, V7X_LANES), lambda i: (0, 0)),
            pl.BlockSpec((d, V7X_LANES), lambda i: (0, 0)),
        ],
        out_specs=[
            pl.BlockSpec((tm, V7X_LANES), lambda i: (i, 0)),
            pl.BlockSpec((TOK_ROWS, tm), lambda i: (0, i)),
            pl.BlockSpec((1, V7X_LANES), lambda i: (0, 0)),
        ],
        out_shape=[
            jax.ShapeDtypeStruct((n, V7X_LANES), F32),
            jax.ShapeDtypeStruct((TOK_ROWS, n), jnp.int32),
            jax.ShapeDtypeStruct((1, V7X_LANES), F32),
        ],
        scratch_shapes=[pltpu.VMEM((1, V7X_LANES), F32)],
        compiler_params=_params("arbitrary"),
        name="moe_router",
    )(x, g, w_hi, w_lo)


def _plan_body(cnt_ref, off_ref, te_ref, nv_ref, *, ne, tg, n_tiles):
    off = jnp.int32(0)
    tile = jnp.int32(0)
    for e in range(ne):
        nt = (cnt_ref[e] + (tg - 1)) // tg
        off_ref[e] = off

        def mark(ti, carry, e=e):
            te_ref[ti] = jnp.int32(e)
            return carry

        lax.fori_loop(tile, tile + nt, mark, 0)
        off = off + nt * tg
        tile = tile + nt
    nv_ref[0] = tile
    last = te_ref[jnp.maximum(tile - 1, 0)]

    def fill(ti, carry):
        te_ref[ti] = last
        return carry

    lax.fori_loop(tile, n_tiles, fill, 0)


def moe_plan(cnt, tg, n_tiles):
    ne = cnt.shape[0]
    smem = pl.BlockSpec(memory_space=pltpu.SMEM)
    return pl.pallas_call(
        functools.partial(_plan_body, ne=ne, tg=tg, n_tiles=n_tiles),
        in_specs=[smem],
        out_specs=[smem, smem, smem],
        out_shape=[
            jax.ShapeDtypeStruct((ne,), jnp.int32),
            jax.ShapeDtypeStruct((n_tiles,), jnp.int32),
            jax.ShapeDtypeStruct((1,), jnp.int32),
        ],
        name="moe_plan",
    )(cnt)


def _row_copy(src_hbm, s, dst, r, sem):
    return pltpu.make_async_copy(src_hbm.at[pl.ds(s, 1)], dst.at[pl.ds(r, 1)], sem)


def _dispatch_body(tok_hbm, off_ref, cnt_ref, x_ref, xs_hbm, tok_smem, zero_ref, sem_idx, sem_row, *, td, tg, ne):
    base = pl.multiple_of(pl.program_id(0) * td, td)
    idx = pltpu.make_async_copy(tok_hbm.at[:, pl.ds(base, td)], tok_smem, sem_idx)
    idx.start()

    @pl.when(pl.program_id(0) == 0)
    def _():
        zero_ref[...] = jnp.zeros_like(zero_ref)
        for e in range(ne):
            start = off_ref[e] + cnt_ref[e]
            stop = off_ref[e] + (cnt_ref[e] + (tg - 1)) // tg * tg

            def zero_start(r, carry):
                _row_copy(zero_ref, 0, xs_hbm, r, sem_row).start()
                return carry

            def zero_wait(r, carry):
                _row_copy(zero_ref, 0, xs_hbm, r, sem_row).wait()
                return carry

            lax.fori_loop(start, stop, zero_start, 0)
            lax.fori_loop(start, stop, zero_wait, 0)

    idx.wait()

    def copies(t):
        return [_row_copy(x_ref, t, xs_hbm, off_ref[tok_smem[k, t]] + tok_smem[TOP_K + k, t], sem_row) for k in range(TOP_K)]

    def start(t, c):
        for cp in copies(t):
            cp.start()
        return c

    def wait(t, c):
        for cp in copies(t):
            cp.wait()
        return c

    lax.fori_loop(0, td, start, 0)
    lax.fori_loop(0, td, wait, 0)


def moe_dispatch(tok, off, cnt, x, tg, n_tiles, td=1024):
    n, d = x.shape
    td = min(td, n)
    smem = pl.BlockSpec(memory_space=pltpu.SMEM)
    hbm = pl.BlockSpec(memory_space=pl.ANY)
    return pl.pallas_call(
        functools.partial(_dispatch_body, td=td, tg=tg, ne=cnt.shape[0]),
        grid=(n // td,),
        in_specs=[hbm, smem, smem, pl.BlockSpec((td, d), lambda i: (i, 0))],
        out_specs=hbm,
        out_shape=jax.ShapeDtypeStruct((n_tiles * tg, d), x.dtype),
        scratch_shapes=[
            pltpu.SMEM((TOK_ROWS, td), jnp.int32),
            pltpu.VMEM((8, d), x.dtype),
            pltpu.SemaphoreType.DMA(()),
            pltpu.SemaphoreType.DMA(()),
        ],
        compiler_params=pltpu.CompilerParams(dimension_semantics=("arbitrary",), has_side_effects=True, vmem_limit_bytes=V7X_VMEM_LIMIT_BYTES),
        name="moe_dispatch",
    )(tok, off, cnt, x)


def _grouped_ffn_body(te_ref, nv_ref, x_ref, g_ref, wg_ref, wu_ref, wo_ref, o_ref, h_ref):
    i, j = pl.program_id(0), pl.program_id(1)

    @pl.when(i < nv_ref[0])
    def _():
        @pl.when(j == 0)
        def _():
            h_ref[...] = _rms(x_ref[...], g_ref[...]).astype(BF16)

        y = _swiglu_step(h_ref[...], wg_ref, wu_ref, wo_ref)

        @pl.when(j == 0)
        def _():
            o_ref[...] = y

        @pl.when(j > 0)
        def _():
            o_ref[...] += y


def moe_grouped_ffn(te, nv, xs, g, w_in, w_out, tg, tf=512):
    r, d = xs.shape
    f = w_out.shape[1]
    tf = min(tf, f)
    nf = f // tf
    n_tiles = r // tg

    def tile(i, nv):
        return jnp.minimum(i, nv[0] - 1)

    def fblk(i, j, nv):
        return jnp.where(i < nv[0], j, nf - 1)

    grid_spec = pltpu.PrefetchScalarGridSpec(
        num_scalar_prefetch=2,
        grid=(n_tiles, nf),
        in_specs=[
            pl.BlockSpec((tg, d), lambda i, j, te, nv: (tile(i, nv), 0)),
            pl.BlockSpec((1, d), lambda i, j, te, nv: (0, 0)),
            pl.BlockSpec((None, d, tf), lambda i, j, te, nv: (te[i], 0, fblk(i, j, nv))),
            pl.BlockSpec((None, d, tf), lambda i, j, te, nv: (te[i], 0, fblk(i, j, nv) + nf)),
            pl.BlockSpec((None, tf, d), lambda i, j, te, nv: (te[i], fblk(i, j, nv), 0)),
        ],
        out_specs=pl.BlockSpec((tg, d), lambda i, j, te, nv: (tile(i, nv), 0)),
        scratch_shapes=[pltpu.VMEM((tg, d), BF16)],
    )
    return pl.pallas_call(
        _grouped_ffn_body,
        grid_spec=grid_spec,
        out_shape=jax.ShapeDtypeStruct((r, d), F32),
        compiler_params=_params("arbitrary", "arbitrary"),
        name="moe_grouped_ffn",
    )(te, nv, xs, g, w_in, w_in, w_out)


def _combine_body(tok_hbm, off_ref, gate_ref, x_ref, ys_hbm, *rest, tc, final):
    rest = list(rest)
    fg_ref = rest.pop(0) if final else None
    o_ref, tok_smem, buf_ref, sem_idx, sem_row = rest
    base = pl.multiple_of(pl.program_id(0) * tc, tc)
    idx = pltpu.make_async_copy(tok_hbm.at[:, pl.ds(base, tc)], tok_smem, sem_idx)
    idx.start()
    idx.wait()

    def copies(t):
        return [_row_copy(ys_hbm, off_ref[tok_smem[k, t]] + tok_smem[TOP_K + k, t], buf_ref.at[k], t, sem_row) for k in range(TOP_K)]

    def start(t, c):
        for cp in copies(t):
            cp.start()
        return c

    def wait(t, c):
        for cp in copies(t):
            cp.wait()
        return c

    lax.fori_loop(0, tc, start, 0)
    lax.fori_loop(0, tc, wait, 0)
    y = x_ref[...] + gate_ref[:, 0:1] * buf_ref[0] + gate_ref[:, 1:2] * buf_ref[1]
    o_ref[...] = _rms(y, fg_ref[...]) if final else y


def moe_combine(tok, off, gates, x, ys, final_g=None, tc=512):
    n, d = x.shape
    tc = min(tc, n)
    final = final_g is not None
    smem = pl.BlockSpec(memory_space=pltpu.SMEM)
    hbm = pl.BlockSpec(memory_space=pl.ANY)
    in_specs = [hbm, smem, pl.BlockSpec((tc, V7X_LANES), lambda i: (i, 0)), pl.BlockSpec((tc, d), lambda i: (i, 0)), hbm]
    args = [tok, off, gates, x, ys]
    if final:
        in_specs.append(pl.BlockSpec((1, d), lambda i: (0, 0)))
        args.append(final_g)
    return pl.pallas_call(
        functools.partial(_combine_body, tc=tc, final=final),
        grid=(n // tc,),
        in_specs=in_specs,
        out_specs=pl.BlockSpec((tc, d), lambda i: (i, 0)),
        out_shape=jax.ShapeDtypeStruct((n, d), F32),
        scratch_shapes=[
            pltpu.SMEM((TOK_ROWS, tc), jnp.int32),
            pltpu.VMEM((TOP_K, tc, d), F32),
            pltpu.SemaphoreType.DMA(()),
            pltpu.SemaphoreType.DMA(()),
        ],
        compiler_params=_params("arbitrary"),
        name="moe_combine",
    )(*args)


def moe_ffn(x, g, w_router, w_in, w_out, final_g=None, tg=512):
    n, _ = x.shape
    ne = w_router.shape[1]
    tg = min(tg, n)
    n_tiles = TOP_K * n // tg + ne
    gates, tok, counts = moe_router(x, g, w_router)
    cnt = counts[0, :ne].astype(jnp.int32)
    off, te, nv = moe_plan(cnt, tg, n_tiles)
    xs = moe_dispatch(tok, off, cnt, x, tg, n_tiles)
    ys = moe_grouped_ffn(te, nv, xs, g, w_in, w_out, tg)
    return moe_combine(tok, off, gates, x, ys, final_g)


def _gla_gate_body(x_ref, g_ref, wl_ref, w2_ref, b_ref, o_ref):
    h = _rms(x_ref[...], g_ref[...]).astype(BF16)
    g_low = _dot(h, wl_ref[...]).astype(BF16)
    o_ref[...] = _log_sigmoid(_dot(g_low, w2_ref[...]) + b_ref[...]) / B_TAU


def gla_gate(x, g, w_low, w_gate2, bias, tm=1024):
    n, d = x.shape
    dk = w_gate2.shape[1]
    tm = min(tm, n)
    wl = jnp.pad(w_low, ((0, 0), (0, V7X_LANES - B_GATE_RANK))).astype(BF16)
    w2 = jnp.pad(w_gate2, ((0, V7X_LANES - B_GATE_RANK), (0, 0))).astype(BF16)
    return pl.pallas_call(
        _gla_gate_body,
        grid=(n // tm,),
        in_specs=[
            pl.BlockSpec((tm, d), lambda i: (i, 0)),
            pl.BlockSpec((1, d), lambda i: (0, 0)),
            pl.BlockSpec((d, V7X_LANES), lambda i: (0, 0)),
            pl.BlockSpec((V7X_LANES, dk), lambda i: (0, 0)),
            pl.BlockSpec((1, dk), lambda i: (0, 0)),
        ],
        out_specs=pl.BlockSpec((tm, dk), lambda i: (i, 0)),
        out_shape=jax.ShapeDtypeStruct((n, dk), F32),
        compiler_params=_params("parallel"),
        name="gla_gate",
    )(x, g, wl, w2, bias)


def _gla_body(q_ref, k_ref, v_ref, r_ref, la_ref, og_ref, o_ref, st_ref, *, tc, hk, hv):
    @pl.when(pl.program_id(1) == 0)
    def _():
        st_ref[...] = jnp.zeros_like(st_ref)

    c = B_CHUNK
    tril = _tril_mask(c)
    ones_tril = tril.astype(BF16)
    scale = hk**-0.5
    for ci in range(tc // c):
        rs = slice(ci * c, (ci + 1) * c)
        hi, mid, lo = _split3(la_ref[rs, :])
        bcum = _dot(ones_tril, hi) + _dot(ones_tril, mid) + _dot(ones_tril, lo)
        b_last = bcum[c - 1 : c, :]
        e_pos = jnp.exp(bcum)
        e_neg = jnp.exp(-bcum)
        e_end = jnp.exp(b_last - bcum)
        dec = jnp.exp(b_last)
        for h in range(B_HEADS):
            ks = slice(h * hk, (h + 1) * hk)
            vs = slice(h * hv, (h + 1) * hv)
            q = q_ref[rs, ks].astype(F32) * scale
            k = k_ref[rs, ks].astype(F32)
            v = v_ref[rs, vs]
            q_d = (q * e_pos[:, ks]).astype(BF16)
            k_d = (k * e_neg[:, ks]).astype(BF16)
            k_end = (k * e_end[:, ks]).astype(BF16)
            att = jnp.where(tril, _dot_nt(q_d, k_d), 0.0)
            st = st_ref[h]
            o = _dot(att.astype(BF16), v) + _dot_nt(q_d, st.astype(BF16))
            st_ref[h] = dec[:, ks] * st + _dot_tn(v, k_end)
            y = _rms(o, og_ref[...]).astype(BF16).astype(F32)
            r = r_ref[rs, vs].astype(F32)
            o_ref[rs, vs] = (y * (r * jax.nn.sigmoid(r))).astype(BF16)


def gla_chunks(proj, la, o_g, bsz, tc=256):
    n = proj.shape[0]
    dk = la.shape[1]
    dv = (proj.shape[1] - 2 * dk) // 2
    s = n // bsz
    tc = min(tc, s)
    nt = s // tc
    hk, hv = dk // B_HEADS, dv // B_HEADS
    assert (2 * dk) % dv == 0
    v_blk = 2 * dk // dv
    row = lambda b, t: b * nt + t
    return pl.pallas_call(
        functools.partial(_gla_body, tc=tc, hk=hk, hv=hv),
        grid=(bsz, nt),
        in_specs=[
            pl.BlockSpec((tc, dk), lambda b, t: (row(b, t), 0)),
            pl.BlockSpec((tc, dk), lambda b, t: (row(b, t), 1)),
            pl.BlockSpec((tc, dv), lambda b, t: (row(b, t), v_blk)),
            pl.BlockSpec((tc, dv), lambda b, t: (row(b, t), v_blk + 1)),
            pl.BlockSpec((tc, dk), lambda b, t: (row(b, t), 0)),
            pl.BlockSpec((1, hv), lambda b, t: (0, 0)),
        ],
        out_specs=pl.BlockSpec((tc, dv), lambda b, t: (row(b, t), 0)),
        out_shape=jax.ShapeDtypeStruct((n, dv), BF16),
        scratch_shapes=[pltpu.VMEM((B_HEADS, hv, hk), F32)],
        compiler_params=_params("parallel", "arbitrary"),
        name="gla_chunks",
    )(proj, proj, proj, proj, la, o_g)


def _fox_gate_body(x_ref, g_ref, wf_ref, b_ref, ccol_ref, crow_ref, carry_ref, *, tb, nh):
    @pl.when(pl.program_id(1) == 0)
    def _():
        carry_ref[...] = jnp.zeros_like(carry_ref)

    h = _rms(x_ref[...], g_ref[...]).astype(BF16)
    lane = lax.broadcasted_iota(jnp.int32, (tb, V7X_LANES), 1)
    log_f = jnp.where(lane < nh, _log_sigmoid(_dot(h, wf_ref[...]) + b_ref[...]), 0.0)
    ones_tril = _tril_mask(tb).astype(BF16)
    hi, mid, lo = _split3(log_f)
    c = carry_ref[...] + (_dot(ones_tril, hi) + _dot(ones_tril, mid) + _dot(ones_tril, lo))
    carry_ref[...] = c[tb - 1 : tb, :]
    ccol_ref[...] = c
    sel = (lax.broadcasted_iota(jnp.int32, (nh, V7X_LANES), 0) == lax.broadcasted_iota(jnp.int32, (nh, V7X_LANES), 1)).astype(BF16)
    hi, mid, lo = _split3(c)
    crow_ref[...] = _dot_nt(sel, hi) + _dot_nt(sel, mid) + _dot_nt(sel, lo)


def fox_gate(x, g, w_f, f_bias, bsz, tb=512):
    n, d = x.shape
    nh = w_f.shape[1]
    s = n // bsz
    tb = min(tb, s)
    nt = s // tb
    wf = jnp.pad(w_f, ((0, 0), (0, V7X_LANES - nh))).astype(BF16)
    bias = jnp.pad(f_bias.reshape(1, nh), ((0, 0), (0, V7X_LANES - nh)))
    return pl.pallas_call(
        functools.partial(_fox_gate_body, tb=tb, nh=nh),
        grid=(bsz, nt),
        in_specs=[
            pl.BlockSpec((tb, d), lambda b, t: (b * nt + t, 0)),
            pl.BlockSpec((1, d), lambda b, t: (0, 0)),
            pl.BlockSpec((d, V7X_LANES), lambda b, t: (0, 0)),
            pl.BlockSpec((1, V7X_LANES), lambda b, t: (0, 0)),
        ],
        out_specs=[
            pl.BlockSpec((tb, V7X_LANES), lambda b, t: (b * nt + t, 0)),
            pl.BlockSpec((None, nh, tb), lambda b, t: (b, 0, t)),
        ],
        out_shape=[jax.ShapeDtypeStruct((n, V7X_LANES), F32), jax.ShapeDtypeStruct((bsz, nh, s), F32)],
        scratch_shapes=[pltpu.VMEM((1, V7X_LANES), F32)],
        compiler_params=_params("parallel", "arbitrary"),
        name="fox_gate",
    )(x, g, wf, bias)


def _fox_flash_body(q_ref, k_ref, v_ref, ccol_ref, crow_ref, o_ref, m_ref, l_ref, acc_ref, *, t, scale):
    h, i = pl.program_id(1), pl.program_id(2)
    q = q_ref[...]
    lane = lax.broadcasted_iota(jnp.int32, ccol_ref.shape, 1)
    cq = jnp.sum(jnp.where(lane == h, ccol_ref[...], 0.0), axis=-1, keepdims=True)
    m_ref[...] = jnp.full_like(m_ref, -jnp.inf)
    l_ref[...] = jnp.zeros_like(l_ref)
    acc_ref[...] = jnp.zeros_like(acc_ref)

    def block(kb, masked):
        off = pl.multiple_of(kb * t, t)
        k = k_ref[pl.ds(off, t), :]
        v = v_ref[pl.ds(off, t), :]
        s = _dot_nt(q, k) * scale + cq - crow_ref[pl.ds(kb, 1), :]
        if masked:
            s = jnp.where(_tril_mask(t), s, -jnp.inf)
        m_old = m_ref[...]
        m_new = jnp.maximum(m_old, jnp.max(s, axis=-1, keepdims=True))
        alpha = jnp.exp(m_old - m_new)
        p = jnp.exp(s - m_new)
        l_ref[...] = alpha * l_ref[...] + jnp.sum(p, axis=-1, keepdims=True)
        acc_ref[...] = alpha * acc_ref[...] + _dot(p.astype(BF16), v)
        m_ref[...] = m_new

    def off_diag(kb, carry):
        block(kb, False)
        return carry

    lax.fori_loop(0, i, off_diag, 0)
    block(i, True)
    o_ref[...] = (acc_ref[...] / l_ref[...]).astype(o_ref.dtype)


def fox_flash(proj, ccol, crow, bsz, nh, t=512):
    n = proj.shape[0]
    dh = proj.shape[1] // (3 * nh)
    s = n // bsz
    t = min(t, s)
    nq = s // t
    crow = crow.reshape(bsz, nh, nq, t)
    return pl.pallas_call(
        functools.partial(_fox_flash_body, t=t, scale=dh**-0.5),
        grid=(bsz, nh, nq),
        in_specs=[
            pl.BlockSpec((t, dh), lambda b, h, i: (b * nq + i, h)),
            pl.BlockSpec((s, dh), lambda b, h, i: (b, nh + h)),
            pl.BlockSpec((s, dh), lambda b, h, i: (b, 2 * nh + h)),
            pl.BlockSpec((t, V7X_LANES), lambda b, h, i: (b * nq + i, 0)),
            pl.BlockSpec((None, None, nq, t), lambda b, h, i: (b, h, 0, 0)),
        ],
        out_specs=pl.BlockSpec((t, dh), lambda b, h, i: (b * nq + i, h)),
        out_shape=jax.ShapeDtypeStruct((n, nh * dh), BF16),
        scratch_shapes=[pltpu.VMEM((t, 1), F32), pltpu.VMEM((t, 1), F32), pltpu.VMEM((t, dh), F32)],
        compiler_params=_params("parallel", "parallel", "arbitrary"),
        name="fox_flash",
    )(proj, proj, proj, ccol, crow)


def kernel(x, norm1_g, norm2_g, a_w_in, a_vnorm_g, a_w_s, a_b_s, a_w_out, b_w_in, b_w_gate2, b_gate_bias, b_onorm_g, b_w_out, c_w_in, c_f_bias, c_w_out, ffn_w_in, ffn_w_out, moe_router, moe_w_in, moe_w_out, final_g):
    bsz, s, d = x.shape
    depth = norm1_g.shape[0]
    xf = x.reshape(bsz * s, d)
    for i in range(depth):
        g1 = norm1_g[i].reshape(1, d)
        m, j = i % N_MIXERS, i // N_MIXERS
        if m == 0:
            z = norm_matmul(xf, g1, a_w_in[j].astype(BF16), act="gelu")
            xf = gmlp_spatial_out(z, xf, a_vnorm_g[j].reshape(1, -1), a_w_s[j], a_b_s[j].T, a_w_out[j].astype(BF16))
        elif m == 1:
            dk = b_w_gate2.shape[2]
            n_main = b_w_in.shape[2] - B_GATE_RANK
            proj = norm_matmul(xf, g1, b_w_in[j, :, :n_main].astype(BF16))
            la = gla_gate(xf, g1, b_w_in[j, :, n_main:], b_w_gate2[j], b_gate_bias[j].reshape(1, dk))
            o = gla_chunks(proj, la, b_onorm_g[j].reshape(1, -1), bsz)
            xf = matmul_residual(o, b_w_out[j].astype(BF16), xf)
        else:
            proj = norm_matmul(xf, g1, c_w_in[j, :, : 3 * d].astype(BF16))
            ccol, crow = fox_gate(xf, g1, c_w_in[j, :, 3 * d :], c_f_bias[j], bsz)
            o = fox_flash(proj, ccol, crow, bsz, C_HEADS)
            xf = matmul_residual(o, c_w_out[j].astype(BF16), xf)
        g2 = norm2_g[i].reshape(1, d)
        fg = final_g.reshape(1, d) if i == depth - 1 else None
        if i % 2 == 0:
            xf = ffn(xf, g2, ffn_w_in[i // 2].astype(BF16), ffn_w_out[i // 2].astype(BF16), final_g=fg)
        else:
            xf = moe_ffn(xf, g2, moe_router[i // 2], moe_w_in[i // 2].astype(BF16), moe_w_out[i // 2].astype(BF16), final_g=fg)
    return xf.reshape(bsz, s, d)
```

```python
import functools

import jax
import jax.numpy as jnp
from jax import lax
from jax.experimental import pallas as pl
from jax.experimental.pallas import tpu as pltpu

F32 = jnp.float32
BF16 = jnp.bfloat16

EPS = 1e-6
N_MIXERS = 3
TOP_K = 2
A_CHUNK = 128
A_GROUPS = 8
B_HEADS = 4
B_GATE_RANK = 16
B_TAU = 16.0
B_CHUNK = 64
C_HEADS = 8

V7X_LANES = 128
V7X_VMEM_LIMIT_BYTES = 56 * 1024 * 1024


def _params(*sem):
    return pltpu.CompilerParams(dimension_semantics=sem, vmem_limit_bytes=V7X_VMEM_LIMIT_BYTES)


def _rms(x, g):
    ms = jnp.mean(x * x, axis=-1, keepdims=True)
    return x * lax.rsqrt(ms + EPS) * g


def _log_sigmoid(x):
    return jnp.minimum(x, 0.0) - jnp.log1p(jnp.exp(-jnp.abs(x)))


def _split3(x):
    hi = x.astype(BF16)
    r1 = x - hi.astype(F32)
    mid = r1.astype(BF16)
    lo = (r1 - mid.astype(F32)).astype(BF16)
    return hi, mid, lo


def _dot(a, b):
    return jnp.dot(a, b, preferred_element_type=F32)


def _dot_nt(a, b):
    return lax.dot_general(a, b, (((1,), (1,)), ((), ())), preferred_element_type=F32)


def _dot_tn(a, b):
    return lax.dot_general(a, b, (((0,), (0,)), ((), ())), preferred_element_type=F32)


def _tril_mask(n):
    row = lax.broadcasted_iota(jnp.int32, (n, n), 0)
    col = lax.broadcasted_iota(jnp.int32, (n, n), 1)
    return col <= row


def _norm_matmul_body(x_ref, g_ref, w_ref, o_ref, h_ref, *, act):
    @pl.when(pl.program_id(1) == 0)
    def _():
        h_ref[...] = _rms(x_ref[...], g_ref[...]).astype(BF16)

    y = _dot(h_ref[...], w_ref[...])
    if act == "gelu":
        y = jax.nn.gelu(y, approximate=True)
    o_ref[...] = y.astype(o_ref.dtype)


def norm_matmul(x, g, w, act=None, tm=1024, tn=512):
    n, d = x.shape
    nout = w.shape[1]
    tm, tn = min(tm, n), min(tn, nout)
    return pl.pallas_call(
        functools.partial(_norm_matmul_body, act=act),
        grid=(n // tm, nout // tn),
        in_specs=[
            pl.BlockSpec((tm, d), lambda i, j: (i, 0)),
            pl.BlockSpec((1, d), lambda i, j: (0, 0)),
            pl.BlockSpec((d, tn), lambda i, j: (0, j)),
        ],
        out_specs=pl.BlockSpec((tm, tn), lambda i, j: (i, j)),
        out_shape=jax.ShapeDtypeStruct((n, nout), BF16),
        scratch_shapes=[pltpu.VMEM((tm, d), BF16)],
        compiler_params=_params("parallel", "arbitrary"),
        name="norm_matmul",
    )(x, g, w)


def _matmul_residual_body(y_ref, w_ref, x_ref, o_ref):
    o_ref[...] = x_ref[...] + _dot(y_ref[...], w_ref[...])


def matmul_residual(y, w, x, tm=512):
    n, k = y.shape
    d = w.shape[1]
    tm = min(tm, n)
    return pl.pallas_call(
        _matmul_residual_body,
        grid=(n // tm,),
        in_specs=[
            pl.BlockSpec((tm, k), lambda i: (i, 0)),
            pl.BlockSpec((k, d), lambda i: (0, 0)),
            pl.BlockSpec((tm, d), lambda i: (i, 0)),
        ],
        out_specs=pl.BlockSpec((tm, d), lambda i: (i, 0)),
        out_shape=jax.ShapeDtypeStruct((n, d), F32),
        compiler_params=_params("parallel"),
        name="matmul_residual",
    )(y, w, x)


def _gmlp_body(z_ref, x_ref, vg_ref, ws_ref, bt_ref, wo_ref, o_ref, vn_ref, y_ref, *, tm, width):
    gd = width // A_GROUPS
    vn_ref[...] = _rms(z_ref[:, width:].astype(F32), vg_ref[...]).astype(BF16)
    tril = _tril_mask(A_CHUNK)
    for g in range(A_GROUPS):
        w = jnp.where(tril, ws_ref[g], 0.0).astype(BF16)
        bias = bt_ref[:, g : g + 1]
        cs = slice(g * gd, (g + 1) * gd)
        for c in range(tm // A_CHUNK):
            rs = slice(c * A_CHUNK, (c + 1) * A_CHUNK)
            mixed = _dot(w, vn_ref[rs, cs]) + bias
            y_ref[rs, cs] = (z_ref[rs, cs].astype(F32) * mixed).astype(BF16)
    o_ref[...] = x_ref[...] + _dot(y_ref[...], wo_ref[...])


def gmlp_spatial_out(z, x, vg, ws, bt, wo, tm=256):
    n, d = x.shape
    width = z.shape[1] // 2
    tm = min(tm, n)
    return pl.pallas_call(
        functools.partial(_gmlp_body, tm=tm, width=width),
        grid=(n // tm,),
        in_specs=[
            pl.BlockSpec((tm, 2 * width), lambda i: (i, 0)),
            pl.BlockSpec((tm, d), lambda i: (i, 0)),
            pl.BlockSpec((1, width), lambda i: (0, 0)),
            pl.BlockSpec((A_GROUPS, A_CHUNK, A_CHUNK), lambda i: (0, 0, 0)),
            pl.BlockSpec((A_CHUNK, A_GROUPS), lambda i: (0, 0)),
            pl.BlockSpec((width, d), lambda i: (0, 0)),
        ],
        out_specs=pl.BlockSpec((tm, d), lambda i: (i, 0)),
        out_shape=jax.ShapeDtypeStruct((n, d), F32),
        scratch_shapes=[pltpu.VMEM((tm, width), BF16), pltpu.VMEM((tm, width), BF16)],
        compiler_params=_params("parallel"),
        name="gmlp_spatial_out",
    )(z, x, vg, ws, bt, wo)


def _swiglu_step(h, wg_ref, wu_ref, wo_ref):
    a = _dot(h, wg_ref[...])
    t = (a * jax.nn.sigmoid(a)) * _dot(h, wu_ref[...])
    return _dot(t.astype(BF16), wo_ref[...])


def _ffn_body(*refs, final):
    x_ref, g_ref, wg_ref, wu_ref, wo_ref = refs[:5]
    rest = list(refs[5:])
    fg_ref = rest.pop(0) if final else None
    o_ref, h_ref, acc_ref = rest
    j = pl.program_id(1)

    @pl.when(j == 0)
    def _():
        h_ref[...] = _rms(x_ref[...], g_ref[...]).astype(BF16)
        acc_ref[...] = jnp.zeros_like(acc_ref)

    acc_ref[...] += _swiglu_step(h_ref[...], wg_ref, wu_ref, wo_ref)

    @pl.when(j == pl.num_programs(1) - 1)
    def _():
        y = x_ref[...] + acc_ref[...]
        o_ref[...] = _rms(y, fg_ref[...]) if final else y


def ffn(x, g, w_in, w_out, final_g=None, tm=1024, tf=512):
    n, d = x.shape
    f = w_out.shape[0]
    tm, tf = min(tm, n), min(tf, f)
    nf = f // tf
    final = final_g is not None
    in_specs = [
        pl.BlockSpec((tm, d), lambda i, j: (i, 0)),
        pl.BlockSpec((1, d), lambda i, j: (0, 0)),
        pl.BlockSpec((d, tf), lambda i, j: (0, j)),
        pl.BlockSpec((d, tf), lambda i, j: (0, j + nf)),
        pl.BlockSpec((tf, d), lambda i, j: (j, 0)),
    ]
    args = [x, g, w_in, w_in, w_out]
    if final:
        in_specs.append(pl.BlockSpec((1, d), lambda i, j: (0, 0)))
        args.append(final_g)
    return pl.pallas_call(
        functools.partial(_ffn_body, final=final),
        grid=(n // tm, nf),
        in_specs=in_specs,
        out_specs=pl.BlockSpec((tm, d), lambda i, j: (i, 0)),
        out_shape=jax.ShapeDtypeStruct((n, d), F32),
        scratch_shapes=[pltpu.VMEM((tm, d), BF16), pltpu.VMEM((tm, d), F32)],
        compiler_params=_params("parallel", "arbitrary"),
        name="ffn",
    )(*args)


TOK_ROWS = 8


def _router_body(x_ref, g_ref, whi_ref, wlo_ref, gate_ref, tok_ref, cnt_ref, carry_ref, *, ne, tm):
    @pl.when(pl.program_id(0) == 0)
    def _():
        carry_ref[...] = jnp.zeros_like(carry_ref)

    h = _rms(x_ref[...], g_ref[...])
    h_hi = h.astype(BF16)
    h_lo = (h - h_hi.astype(F32)).astype(BF16)
    logits = _dot(h_hi, whi_ref[...]) + _dot(h_lo, whi_ref[...]) + _dot(h_hi, wlo_ref[...])
    lane = lax.broadcasted_iota(jnp.int32, logits.shape, 1).astype(F32)
    neg = -jnp.inf
    l1 = jnp.where(lane < ne, logits, neg)
    m1 = jnp.max(l1, axis=-1, keepdims=True)
    i1 = jnp.min(jnp.where(l1 == m1, lane, float(V7X_LANES)), axis=-1, keepdims=True)
    l2 = jnp.where(lane == i1, neg, l1)
    m2 = jnp.max(l2, axis=-1, keepdims=True)
    i2 = jnp.min(jnp.where(l2 == m2, lane, float(V7X_LANES)), axis=-1, keepdims=True)
    e2 = jnp.exp(m2 - m1)
    den = 1.0 + e2
    gate_ref[...] = jnp.where(lane == 0.0, 1.0 / den, 0.0) + jnp.where(lane == 1.0, e2 / den, 0.0)

    sel1, sel2 = lane == i1, lane == i2
    onehot = jnp.where(sel1 | sel2, 1.0, 0.0)
    row = lax.broadcasted_iota(jnp.int32, (tm, tm), 0)
    col = lax.broadcasted_iota(jnp.int32, (tm, tm), 1)
    before = carry_ref[...] + _dot((col < row).astype(BF16), onehot.astype(BF16))
    carry_ref[...] = before[tm - 1 : tm, :] + onehot[tm - 1 : tm, :]
    cnt_ref[...] = carry_ref[...]
    r1 = jnp.sum(jnp.where(sel1, before, 0.0), axis=-1, keepdims=True)
    r2 = jnp.sum(jnp.where(sel2, before, 0.0), axis=-1, keepdims=True)
    table = (jnp.where(lane == 0.0, i1, 0.0) + jnp.where(lane == 1.0, i2, 0.0)
             + jnp.where(lane == 2.0, r1, 0.0) + jnp.where(lane == 3.0, r2, 0.0))
    pick = (lax.broadcasted_iota(jnp.int32, (TOK_ROWS, V7X_LANES), 0) == lax.broadcasted_iota(jnp.int32, (TOK_ROWS, V7X_LANES), 1)).astype(BF16)
    hi, mid, lo = _split3(table)
    tok_ref[...] = (_dot_nt(pick, hi) + _dot_nt(pick, mid) + _dot_nt(pick, lo)).astype(jnp.int32)


def moe_router(x, g, w_router, tm=1024):
    n, d = x.shape
    ne = w_router.shape[1]
    tm = min(tm, n)
    wpad = jnp.pad(w_router, ((0, 0), (0, V7X_LANES - ne)))
    w_hi = wpad.astype(BF16)
    w_lo = (wpad - w_hi.astype(F32)).astype(BF16)
    return pl.pallas_call(
        functools.partial(_router_body, ne=ne, tm=tm),
        grid=(n // tm,),
        in_specs=[
            pl.BlockSpec((tm, d), lambda i: (i, 0)),
            pl.BlockSpec((1, d), lambda i: (0, 0)),
            pl.BlockSpec((d, V7X_LANES), lambda i: (0, 0)),
            pl.BlockSpec((d, V7X_LANES), lambda i: (0, 0)),
        ],
        out_specs=[
            pl.BlockSpec((tm, V7X_LANES), lambda i: (i, 0)),
            pl.BlockSpec((TOK_ROWS, tm), lambda i: (0, i)),
            pl.BlockSpec((1, V7X_LANES), lambda i: (0, 0)),
        ],
        out_shape=[
            jax.ShapeDtypeStruct((n, V7X_LANES), F32),
            jax.ShapeDtypeStruct((TOK_ROWS, n), jnp.int32),
            jax.ShapeDtypeStruct((1, V7X_LANES), F32),
        ],
        scratch_shapes=[pltpu.VMEM((1, V7X_LANES), F32)],
        compiler_params=_params("arbitrary"),
        name="moe_router",
    )(x, g, w_hi, w_lo)


def _plan_body(cnt_ref, off_ref, te_ref, nv_ref, *, ne, tg, n_tiles):
    off = jnp.int32(0)
    tile = jnp.int32(0)
    for e in range(ne):
        nt = (cnt_ref[e] + (tg - 1)) // tg
        off_ref[e] = off

        def mark(ti, carry, e=e):
            te_ref[ti] = jnp.int32(e)
            return carry

        lax.fori_loop(tile, tile + nt, mark, 0)
        off = off + nt * tg
        tile = tile + nt
    nv_ref[0] = tile
    last = te_ref[jnp.maximum(tile - 1, 0)]

    def fill(ti, carry):
        te_ref[ti] = last
        return carry

    lax.fori_loop(tile, n_tiles, fill, 0)


def moe_plan(cnt, tg, n_tiles):
    ne = cnt.shape[0]
    smem = pl.BlockSpec(memory_space=pltpu.SMEM)
    return pl.pallas_call(
        functools.partial(_plan_body, ne=ne, tg=tg, n_tiles=n_tiles),
        in_specs=[smem],
        out_specs=[smem, smem, smem],
        out_shape=[
            jax.ShapeDtypeStruct((ne,), jnp.int32),
            jax.ShapeDtypeStruct((n_tiles,), jnp.int32),
            jax.ShapeDtypeStruct((1,), jnp.int32),
        ],
        name="moe_plan",
    )(cnt)


def _row_copy(src_hbm, s, dst, r, sem):
    return pltpu.make_async_copy(src_hbm.at[pl.ds(s, 1)], dst.at[pl.ds(r, 1)], sem)


def _dispatch_body(tok_hbm, off_ref, cnt_ref, x_ref, xs_hbm, tok_smem, zero_ref, sem_idx, sem_row, *, td, tg, ne, n_tiles):
    base = pl.multiple_of(pl.program_id(0) * td, td)
    idx = pltpu.make_async_copy(tok_hbm.at[:, pl.ds(base, td)], tok_smem, sem_idx)
    idx.start()

    @pl.when(pl.program_id(0) == 0)
    def _():
        zero_ref[...] = jnp.zeros_like(zero_ref)
        for e in range(ne):
            start = off_ref[e] + cnt_ref[e]
            stop = off_ref[e] + (cnt_ref[e] + (tg - 1)) // tg * tg

            def zero_start(r, carry):
                _row_copy(zero_ref, 0, xs_hbm, r, sem_row).start()
                return carry

            def zero_wait(r, carry):
                _row_copy(zero_ref, 0, xs_hbm, r, sem_row).wait()
                return carry

            lax.fori_loop(start, stop, zero_start, 0)
            lax.fori_loop(start, stop, zero_wait, 0)

        def tile_copy(ti):
            return pltpu.make_async_copy(zero_ref, xs_hbm.at[pl.ds(pl.multiple_of(ti * tg, tg), tg)], sem_row)

        def tile_start(ti, carry):
            tile_copy(ti).start()
            return carry

        def tile_wait(ti, carry):
            tile_copy(ti).wait()
            return carry

        used = stop // tg
        lax.fori_loop(used, n_tiles, tile_start, 0)
        lax.fori_loop(used, n_tiles, tile_wait, 0)

    idx.wait()

    def copies(t):
        return [_row_copy(x_ref, t, xs_hbm, off_ref[tok_smem[k, t]] + tok_smem[TOP_K + k, t], sem_row) for k in range(TOP_K)]

    def start(t, c):
        for cp in copies(t):
            cp.start()
        return c

    lax.fori_loop(0, td, start, 0, unroll=8)
    for _ in range(TOP_K):
        pltpu.make_async_copy(x_ref, xs_hbm.at[pl.ds(0, td)], sem_row).wait()


def moe_dispatch(tok, off, cnt, x, tg, n_tiles, td=1024):
    n, d = x.shape
    td = min(td, n)
    smem = pl.BlockSpec(memory_space=pltpu.SMEM)
    hbm = pl.BlockSpec(memory_space=pl.ANY)
    return pl.pallas_call(
        functools.partial(_dispatch_body, td=td, tg=tg, ne=cnt.shape[0], n_tiles=n_tiles),
        grid=(n // td,),
        in_specs=[hbm, smem, smem, pl.BlockSpec((td, d), lambda i: (i, 0))],
        out_specs=hbm,
        out_shape=jax.ShapeDtypeStruct((n_tiles * tg, d), x.dtype),
        scratch_shapes=[
            pltpu.SMEM((TOK_ROWS, td), jnp.int32),
            pltpu.VMEM((tg, d), x.dtype),
            pltpu.SemaphoreType.DMA(()),
            pltpu.SemaphoreType.DMA(()),
        ],
        compiler_params=pltpu.CompilerParams(dimension_semantics=("arbitrary",), has_side_effects=True, vmem_limit_bytes=V7X_VMEM_LIMIT_BYTES),
        name="moe_dispatch",
    )(tok, off, cnt, x)


def _grouped_ffn_body(te_ref, nv_ref, x_ref, g_ref, wg_ref, wu_ref, wo_ref, o_ref, h_ref):
    i, j = pl.program_id(0), pl.program_id(1)

    @pl.when(i < nv_ref[0])
    def _():
        @pl.when(j == 0)
        def _():
            h_ref[...] = _rms(x_ref[...], g_ref[...]).astype(BF16)

        y = _swiglu_step(h_ref[...], wg_ref, wu_ref, wo_ref)

        @pl.when(j == 0)
        def _():
            o_ref[...] = y

        @pl.when(j > 0)
        def _():
            o_ref[...] += y

    @pl.when((i >= nv_ref[0]) & (j == 0))
    def _():
        o_ref[...] = jnp.zeros_like(o_ref)


def moe_grouped_ffn(te, nv, xs, g, w_in, w_out, tg, tf=512):
    r, d = xs.shape
    f = w_out.shape[1]
    tf = min(tf, f)
    nf = f // tf
    n_tiles = r // tg

    def tile(i, nv):
        return jnp.minimum(i, nv[0] - 1)

    def fblk(i, j, nv):
        return jnp.where(i < nv[0], j, nf - 1)

    grid_spec = pltpu.PrefetchScalarGridSpec(
        num_scalar_prefetch=2,
        grid=(n_tiles, nf),
        in_specs=[
            pl.BlockSpec((tg, d), lambda i, j, te, nv: (tile(i, nv), 0)),
            pl.BlockSpec((1, d), lambda i, j, te, nv: (0, 0)),
            pl.BlockSpec((None, d, tf), lambda i, j, te, nv: (te[i], 0, fblk(i, j, nv))),
            pl.BlockSpec((None, d, tf), lambda i, j, te, nv: (te[i], 0, fblk(i, j, nv) + nf)),
            pl.BlockSpec((None, tf, d), lambda i, j, te, nv: (te[i], fblk(i, j, nv), 0)),
        ],
        out_specs=pl.BlockSpec((tg, d), lambda i, j, te, nv: (i, 0)),
        scratch_shapes=[pltpu.VMEM((tg, d), BF16)],
    )
    return pl.pallas_call(
        _grouped_ffn_body,
        grid_spec=grid_spec,
        out_shape=jax.ShapeDtypeStruct((r, d), F32),
        compiler_params=_params("arbitrary", "arbitrary"),
        name="moe_grouped_ffn",
    )(te, nv, xs, g, w_in, w_in, w_out)


def _combine_body(tok_hbm, off_ref, gate_ref, x_ref, ys_hbm, *rest, tc, final):
    rest = list(rest)
    fg_ref = rest.pop(0) if final else None
    o_ref, tok_smem, buf_ref, sem_idx, sem_row = rest
    base = pl.multiple_of(pl.program_id(0) * tc, tc)
    idx = pltpu.make_async_copy(tok_hbm.at[:, pl.ds(base, tc)], tok_smem, sem_idx)
    idx.start()
    idx.wait()

    def copies(t):
        return [_row_copy(ys_hbm, off_ref[tok_smem[k, t]] + tok_smem[TOP_K + k, t], buf_ref.at[k], t, sem_row) for k in range(TOP_K)]

    def start(t, c):
        for cp in copies(t):
            cp.start()
        return c

    lax.fori_loop(0, tc, start, 0, unroll=8)
    for k in range(TOP_K):
        pltpu.make_async_copy(ys_hbm.at[pl.ds(0, tc)], buf_ref.at[k], sem_row).wait()
    y = x_ref[...] + gate_ref[:, 0:1] * buf_ref[0] + gate_ref[:, 1:2] * buf_ref[1]
    o_ref[...] = _rms(y, fg_ref[...]) if final else y


def moe_combine(tok, off, gates, x, ys, final_g=None, tc=512):
    n, d = x.shape
    tc = min(tc, n)
    final = final_g is not None
    smem = pl.BlockSpec(memory_space=pltpu.SMEM)
    hbm = pl.BlockSpec(memory_space=pl.ANY)
    in_specs = [hbm, smem, pl.BlockSpec((tc, V7X_LANES), lambda i: (i, 0)), pl.BlockSpec((tc, d), lambda i: (i, 0)), hbm]
    args = [tok, off, gates, x, ys]
    if final:
        in_specs.append(pl.BlockSpec((1, d), lambda i: (0, 0)))
        args.append(final_g)
    return pl.pallas_call(
        functools.partial(_combine_body, tc=tc, final=final),
        grid=(n // tc,),
        in_specs=in_specs,
        out_specs=pl.BlockSpec((tc, d), lambda i: (i, 0)),
        out_shape=jax.ShapeDtypeStruct((n, d), F32),
        scratch_shapes=[
            pltpu.SMEM((TOK_ROWS, tc), jnp.int32),
            pltpu.VMEM((TOP_K, tc, d), F32),
            pltpu.SemaphoreType.DMA(()),
            pltpu.SemaphoreType.DMA(()),
        ],
        compiler_params=_params("arbitrary"),
        name="moe_combine",
    )(*args)


def moe_ffn(x, g, w_router, w_in, w_out, final_g=None, tg=512):
    n, _ = x.shape
    ne = w_router.shape[1]
    tg = min(tg, n)
    n_tiles = TOP_K * n // tg + ne
    gates, tok, counts = moe_router(x, g, w_router)
    cnt = counts[0, :ne].astype(jnp.int32)
    off, te, nv = moe_plan(cnt, tg, n_tiles)
    xs = moe_dispatch(tok, off, cnt, x, tg, n_tiles)
    ys = moe_grouped_ffn(te, nv, xs, g, w_in, w_out, tg)
    return moe_combine(tok, off, gates, x, ys, final_g)


def _gla_gate_body(x_ref, g_ref, wl_ref, w2_ref, b_ref, o_ref):
    h = _rms(x_ref[...], g_ref[...]).astype(BF16)
    g_low = _dot(h, wl_ref[...]).astype(BF16)
    o_ref[...] = _log_sigmoid(_dot(g_low, w2_ref[...]) + b_ref[...]) / B_TAU


def gla_gate(x, g, w_low, w_gate2, bias, tm=1024):
    n, d = x.shape
    dk = w_gate2.shape[1]
    tm = min(tm, n)
    wl = jnp.pad(w_low, ((0, 0), (0, V7X_LANES - B_GATE_RANK))).astype(BF16)
    w2 = jnp.pad(w_gate2, ((0, V7X_LANES - B_GATE_RANK), (0, 0))).astype(BF16)
    return pl.pallas_call(
        _gla_gate_body,
        grid=(n // tm,),
        in_specs=[
            pl.BlockSpec((tm, d), lambda i: (i, 0)),
            pl.BlockSpec((1, d), lambda i: (0, 0)),
            pl.BlockSpec((d, V7X_LANES), lambda i: (0, 0)),
            pl.BlockSpec((V7X_LANES, dk), lambda i: (0, 0)),
            pl.BlockSpec((1, dk), lambda i: (0, 0)),
        ],
        out_specs=pl.BlockSpec((tm, dk), lambda i: (i, 0)),
        out_shape=jax.ShapeDtypeStruct((n, dk), F32),
        compiler_params=_params("parallel"),
        name="gla_gate",
    )(x, g, wl, w2, bias)


def _gla_body(q_ref, k_ref, v_ref, r_ref, la_ref, og_ref, o_ref, st_ref, *, tc, hk, hv):
    @pl.when(pl.program_id(1) == 0)
    def _():
        st_ref[...] = jnp.zeros_like(st_ref)

    c = B_CHUNK
    tril = _tril_mask(c)
    ones_tril = tril.astype(BF16)
    scale = hk**-0.5
    for ci in range(tc // c):
        rs = slice(ci * c, (ci + 1) * c)
        hi, mid, lo = _split3(la_ref[rs, :])
        bcum = _dot(ones_tril, hi) + _dot(ones_tril, mid) + _dot(ones_tril, lo)
        b_last = bcum[c - 1 : c, :]
        e_pos = jnp.exp(bcum)
        e_neg = jnp.exp(-bcum)
        e_end = jnp.exp(b_last - bcum)
        dec = jnp.exp(b_last)
        for h in range(B_HEADS):
            ks = slice(h * hk, (h + 1) * hk)
            vs = slice(h * hv, (h + 1) * hv)
            q = q_ref[rs, ks].astype(F32) * scale
            k = k_ref[rs, ks].astype(F32)
            v = v_ref[rs, vs]
            q_d = (q * e_pos[:, ks]).astype(BF16)
            k_d = (k * e_neg[:, ks]).astype(BF16)
            k_end = (k * e_end[:, ks]).astype(BF16)
            att = jnp.where(tril, _dot_nt(q_d, k_d), 0.0)
            st = st_ref[h]
            o = _dot(att.astype(BF16), v) + _dot_nt(q_d, st.astype(BF16))
            st_ref[h] = dec[:, ks] * st + _dot_tn(v, k_end)
            y = _rms(o, og_ref[...]).astype(BF16).astype(F32)
            r = r_ref[rs, vs].astype(F32)
            o_ref[rs, vs] = (y * (r * jax.nn.sigmoid(r))).astype(BF16)


def gla_chunks(proj, la, o_g, bsz, tc=256):
    n = proj.shape[0]
    dk = la.shape[1]
    dv = (proj.shape[1] - 2 * dk) // 2
    s = n // bsz
    tc = min(tc, s)
    nt = s // tc
    hk, hv = dk // B_HEADS, dv // B_HEADS
    assert (2 * dk) % dv == 0
    v_blk = 2 * dk // dv
    row = lambda b, t: b * nt + t
    return pl.pallas_call(
        functools.partial(_gla_body, tc=tc, hk=hk, hv=hv),
        grid=(bsz, nt),
        in_specs=[
            pl.BlockSpec((tc, dk), lambda b, t: (row(b, t), 0)),
            pl.BlockSpec((tc, dk), lambda b, t: (row(b, t), 1)),
            pl.BlockSpec((tc, dv), lambda b, t: (row(b, t), v_blk)),
            pl.BlockSpec((tc, dv), lambda b, t: (row(b, t), v_blk + 1)),
            pl.BlockSpec((tc, dk), lambda b, t: (row(b, t), 0)),
            pl.BlockSpec((1, hv), lambda b, t: (0, 0)),
        ],
        out_specs=pl.BlockSpec((tc, dv), lambda b, t: (row(b, t), 0)),
        out_shape=jax.ShapeDtypeStruct((n, dv), BF16),
        scratch_shapes=[pltpu.VMEM((B_HEADS, hv, hk), F32)],
        compiler_params=_params("parallel", "arbitrary"),
        name="gla_chunks",
    )(proj, proj, proj, proj, la, o_g)


def _fox_gate_body(x_ref, g_ref, wf_ref, b_ref, crow_ref, carry_ref, *, tb, nh):
    @pl.when(pl.program_id(1) == 0)
    def _():
        carry_ref[...] = jnp.zeros_like(carry_ref)

    h = _rms(x_ref[...], g_ref[...]).astype(BF16)
    lane = lax.broadcasted_iota(jnp.int32, (tb, V7X_LANES), 1)
    log_f = jnp.where(lane < nh, _log_sigmoid(_dot(h, wf_ref[...]) + b_ref[...]), 0.0)
    ones_tril = _tril_mask(tb).astype(BF16)
    hi, mid, lo = _split3(log_f)
    c = carry_ref[...] + (_dot(ones_tril, hi) + _dot(ones_tril, mid) + _dot(ones_tril, lo))
    carry_ref[...] = c[tb - 1 : tb, :]
    sel = (lax.broadcasted_iota(jnp.int32, (nh, V7X_LANES), 0) == lax.broadcasted_iota(jnp.int32, (nh, V7X_LANES), 1)).astype(BF16)
    hi, mid, lo = _split3(c)
    crow_ref[...] = _dot_nt(sel, hi) + _dot_nt(sel, mid) + _dot_nt(sel, lo)


def fox_gate(x, g, w_f, f_bias, bsz, tb=512):
    n, d = x.shape
    nh = w_f.shape[1]
    s = n // bsz
    tb = min(tb, s)
    nt = s // tb
    wf = jnp.pad(w_f, ((0, 0), (0, V7X_LANES - nh))).astype(BF16)
    bias = jnp.pad(f_bias.reshape(1, nh), ((0, 0), (0, V7X_LANES - nh)))
    return pl.pallas_call(
        functools.partial(_fox_gate_body, tb=tb, nh=nh),
        grid=(bsz, nt),
        in_specs=[
            pl.BlockSpec((tb, d), lambda b, t: (b * nt + t, 0)),
            pl.BlockSpec((1, d), lambda b, t: (0, 0)),
            pl.BlockSpec((d, V7X_LANES), lambda b, t: (0, 0)),
            pl.BlockSpec((1, V7X_LANES), lambda b, t: (0, 0)),
        ],
        out_specs=pl.BlockSpec((None, nh, tb), lambda b, t: (b, 0, t)),
        out_shape=jax.ShapeDtypeStruct((bsz, nh, s), F32),
        scratch_shapes=[pltpu.VMEM((1, V7X_LANES), F32)],
        compiler_params=_params("parallel", "arbitrary"),
        name="fox_gate",
    )(x, g, wf, bias)


FOX_SUBTILES = 2
FOX_ROW_CHUNK = 32


def _fox_flash_body(q_ref, k_ref, v_ref, crow_ref, o_ref, qs_ref, va_ref, s_ref, p_ref, al_ref, m_ref, acc_ref, *, t, scale):
    i = pl.program_id(2)
    dh = q_ref.shape[1]
    nl = t // V7X_LANES

    @pl.when(i == 0)
    def _():
        va_ref[:, :dh] = v_ref[...]
        va_ref[:, dh:] = jnp.ones((va_ref.shape[0], V7X_LANES), BF16)

    qs_ref[...] = (q_ref[...].astype(F32) * scale).astype(BF16)
    m_ref[...] = jnp.full_like(m_ref, -jnp.inf)
    acc_ref[...] = jnp.zeros_like(acc_ref)

    def block(sub, kb, masked):
        rows = slice(sub * t, (sub + 1) * t)
        off = pl.multiple_of(kb * t, t)
        s_ref[sub] = _dot_nt(qs_ref[rows, :], k_ref[pl.ds(off, t), :])
        bias = crow_ref[pl.ds(kb, 1), :]

        def chunk(c, carry):
            r = pl.multiple_of(c * FOX_ROW_CHUNK, FOX_ROW_CHUNK)
            sc = s_ref[sub, pl.ds(r, FOX_ROW_CHUNK), :] - bias
            if masked:
                row = r + lax.broadcasted_iota(jnp.int32, sc.shape, 0)
                col = lax.broadcasted_iota(jnp.int32, sc.shape, 1)
                sc = jnp.where(col <= row, sc, -jnp.inf)
            tiles = [sc[:, a * V7X_LANES : (a + 1) * V7X_LANES] for a in range(nl)]
            mx = functools.reduce(jnp.maximum, tiles)
            m_old = m_ref[pl.ds(sub * t + r, FOX_ROW_CHUNK), :]
            m_new = jnp.maximum(m_old, jnp.max(mx, axis=-1, keepdims=True))
            al_ref[sub, pl.ds(r, FOX_ROW_CHUNK), :] = jnp.exp(m_old - m_new)
            m_ref[pl.ds(sub * t + r, FOX_ROW_CHUNK), :] = m_new
            for a in range(nl):
                p_ref[sub, pl.ds(r, FOX_ROW_CHUNK), a * V7X_LANES : (a + 1) * V7X_LANES] = jnp.exp(tiles[a] - m_new).astype(BF16)
            return carry

        lax.fori_loop(0, t // FOX_ROW_CHUNK, chunk, 0, unroll=True)
        pv = _dot(p_ref[sub], va_ref[pl.ds(off, t), :])
        alpha = al_ref[sub]
        acc_ref[rows, :dh] = alpha * acc_ref[rows, :dh] + pv[:, :dh]
        acc_ref[rows, dh:] = alpha * acc_ref[rows, dh:] + pv[:, dh:]

    def below_diagonal(kb, carry):
        for sub in range(FOX_SUBTILES):
            block(sub, kb, False)
        return carry

    first = i * FOX_SUBTILES
    lax.fori_loop(0, first, below_diagonal, 0)
    for kb in range(FOX_SUBTILES):
        for sub in range(kb, FOX_SUBTILES):
            block(sub, first + kb, masked=(sub == kb))
    o_ref[...] = (acc_ref[:, :dh] / acc_ref[:, dh:]).astype(o_ref.dtype)


def fox_flash(proj, crow, bsz, nh, t=512):
    n = proj.shape[0]
    dh = proj.shape[1] // (3 * nh)
    assert dh == V7X_LANES
    s = n // bsz
    t = min(t, s // FOX_SUBTILES)
    tq = t * FOX_SUBTILES
    nq = s // tq
    crow = crow.reshape(bsz, nh, s // t, t)
    return pl.pallas_call(
        functools.partial(_fox_flash_body, t=t, scale=dh**-0.5),
        grid=(bsz, nh, nq),
        in_specs=[
            pl.BlockSpec((tq, dh), lambda b, h, i: (b * nq + i, h)),
            pl.BlockSpec((s, dh), lambda b, h, i: (b, nh + h)),
            pl.BlockSpec((s, dh), lambda b, h, i: (b, 2 * nh + h)),
            pl.BlockSpec((None, None, s // t, t), lambda b, h, i: (b, h, 0, 0)),
        ],
        out_specs=pl.BlockSpec((tq, dh), lambda b, h, i: (b * nq + i, h)),
        out_shape=jax.ShapeDtypeStruct((n, nh * dh), BF16),
        scratch_shapes=[
            pltpu.VMEM((tq, dh), BF16),
            pltpu.VMEM((s, dh + V7X_LANES), BF16),
            pltpu.VMEM((FOX_SUBTILES, t, t), F32),
            pltpu.VMEM((FOX_SUBTILES, t, t), BF16),
            pltpu.VMEM((FOX_SUBTILES, t, V7X_LANES), F32),
            pltpu.VMEM((tq, V7X_LANES), F32),
            pltpu.VMEM((tq, dh + V7X_LANES), F32),
        ],
        compiler_params=_params("parallel", "parallel", "arbitrary"),
        name="fox_flash",
    )(proj, proj, proj, crow)


def kernel(x, norm1_g, norm2_g, a_w_in, a_vnorm_g, a_w_s, a_b_s, a_w_out, b_w_in, b_w_gate2, b_gate_bias, b_onorm_g, b_w_out, c_w_in, c_f_bias, c_w_out, ffn_w_in, ffn_w_out, moe_router, moe_w_in, moe_w_out, final_g):
    bsz, s, d = x.shape
    depth = norm1_g.shape[0]
    xf = x.reshape(bsz * s, d)
    for i in range(depth):
        g1 = norm1_g[i].reshape(1, d)
        m, j = i % N_MIXERS, i // N_MIXERS
        if m == 0:
            z = norm_matmul(xf, g1, a_w_in[j].astype(BF16), act="gelu")
            xf = gmlp_spatial_out(z, xf, a_vnorm_g[j].reshape(1, -1), a_w_s[j], a_b_s[j].T, a_w_out[j].astype(BF16))
        elif m == 1:
            dk = b_w_gate2.shape[2]
            n_main = b_w_in.shape[2] - B_GATE_RANK
            proj = norm_matmul(xf, g1, b_w_in[j, :, :n_main].astype(BF16))
            la = gla_gate(xf, g1, b_w_in[j, :, n_main:], b_w_gate2[j], b_gate_bias[j].reshape(1, dk))
            o = gla_chunks(proj, la, b_onorm_g[j].reshape(1, -1), bsz)
            xf = matmul_residual(o, b_w_out[j].astype(BF16), xf)
        else:
            proj = norm_matmul(xf, g1, c_w_in[j, :, : 3 * d].astype(BF16))
            crow = fox_gate(xf, g1, c_w_in[j, :, 3 * d :], c_f_bias[j], bsz)
            o = fox_flash(proj, crow, bsz, C_HEADS)
            xf = matmul_residual(o, c_w_out[j].astype(BF16), xf)
        g2 = norm2_g[i].reshape(1, d)
        fg = final_g.reshape(1, d) if i == depth - 1 else None
        if i % 2 == 0:
            xf = ffn(xf, g2, ffn_w_in[i // 2].astype(BF16), ffn_w_out[i // 2].astype(BF16), final_g=fg)
        else:
            xf = moe_ffn(xf, g2, moe_router[i // 2], moe_w_in[i // 2].astype(BF16), moe_w_out[i // 2].astype(BF16), final_g=fg)
    return xf.reshape(bsz, s, d)
```

```python
import functools

import jax
import jax.numpy as jnp
from jax import lax
from jax.experimental import pallas as pl
from jax.experimental.pallas import tpu as pltpu

F32 = jnp.float32
BF16 = jnp.bfloat16

EPS = 1e-6
N_MIXERS = 3
TOP_K = 2
A_CHUNK = 128
A_GROUPS = 8
B_HEADS = 4
B_GATE_RANK = 16
B_TAU = 16.0
B_CHUNK = 64
C_HEADS = 8

V7X_LANES = 128
V7X_VMEM_LIMIT_BYTES = 56 * 1024 * 1024


def _params(*sem):
    return pltpu.CompilerParams(dimension_semantics=sem, vmem_limit_bytes=V7X_VMEM_LIMIT_BYTES)


def _block(total, target, align=V7X_LANES):
    if total <= target:
        return total
    return max(b for b in range(align, target + 1, align) if total % b == 0)


def _rms(x, g):
    ms = jnp.mean(x * x, axis=-1, keepdims=True)
    return x * lax.rsqrt(ms + EPS) * g


def _log_sigmoid(x):
    return jnp.minimum(x, 0.0) - jnp.log1p(jnp.exp(-jnp.abs(x)))


def _split3(x):
    hi = x.astype(BF16)
    r1 = x - hi.astype(F32)
    mid = r1.astype(BF16)
    lo = (r1 - mid.astype(F32)).astype(BF16)
    return hi, mid, lo


def _dot(a, b):
    return jnp.dot(a, b, preferred_element_type=F32)


def _dot_nt(a, b):
    return lax.dot_general(a, b, (((1,), (1,)), ((), ())), preferred_element_type=F32)


def _dot_tn(a, b):
    return lax.dot_general(a, b, (((0,), (0,)), ((), ())), preferred_element_type=F32)


def _tril_mask(n):
    row = lax.broadcasted_iota(jnp.int32, (n, n), 0)
    col = lax.broadcasted_iota(jnp.int32, (n, n), 1)
    return col <= row


def _norm_matmul_body(x_ref, g_ref, w_ref, o_ref, h_ref, *, act):
    @pl.when(pl.program_id(1) == 0)
    def _():
        h_ref[...] = _rms(x_ref[...], g_ref[...]).astype(BF16)

    y = _dot(h_ref[...], w_ref[...])
    if act == "gelu":
        y = jax.nn.gelu(y, approximate=True)
    o_ref[...] = y.astype(o_ref.dtype)


def norm_matmul(x, g, w, act=None, tm=1024, tn=1024):
    n, d = x.shape
    nout = w.shape[1]
    tm, tn = min(tm, n), _block(nout, tn)
    return pl.pallas_call(
        functools.partial(_norm_matmul_body, act=act),
        grid=(n // tm, nout // tn),
        in_specs=[
            pl.BlockSpec((tm, d), lambda i, j: (i, 0)),
            pl.BlockSpec((1, d), lambda i, j: (0, 0)),
            pl.BlockSpec((d, tn), lambda i, j: (0, j)),
        ],
        out_specs=pl.BlockSpec((tm, tn), lambda i, j: (i, j)),
        out_shape=jax.ShapeDtypeStruct((n, nout), BF16),
        scratch_shapes=[pltpu.VMEM((tm, d), BF16)],
        compiler_params=_params("parallel", "arbitrary"),
        name="norm_matmul",
    )(x, g, w)


def _matmul_residual_body(y_ref, w_ref, x_ref, o_ref):
    o_ref[...] = x_ref[...] + _dot(y_ref[...], w_ref[...])


def matmul_residual(y, w, x, tm=512):
    n, k = y.shape
    d = w.shape[1]
    tm = min(tm, n)
    return pl.pallas_call(
        _matmul_residual_body,
        grid=(n // tm,),
        in_specs=[
            pl.BlockSpec((tm, k), lambda i: (i, 0)),
            pl.BlockSpec((k, d), lambda i: (0, 0)),
            pl.BlockSpec((tm, d), lambda i: (i, 0)),
        ],
        out_specs=pl.BlockSpec((tm, d), lambda i: (i, 0)),
        out_shape=jax.ShapeDtypeStruct((n, d), F32),
        compiler_params=_params("parallel"),
        name="matmul_residual",
    )(y, w, x)


def _gmlp_body(z_ref, x_ref, vg_ref, ws_ref, bt_ref, wo_ref, o_ref, vn_ref, y_ref, *, tm, width):
    gd = width // A_GROUPS
    vn_ref[...] = _rms(z_ref[:, width:].astype(F32), vg_ref[...]).astype(BF16)
    tril = _tril_mask(A_CHUNK)
    for g in range(A_GROUPS):
        w = jnp.where(tril, ws_ref[g], 0.0).astype(BF16)
        bias = bt_ref[:, g : g + 1]
        cs = slice(g * gd, (g + 1) * gd)
        for c in range(tm // A_CHUNK):
            rs = slice(c * A_CHUNK, (c + 1) * A_CHUNK)
            mixed = _dot(w, vn_ref[rs, cs]) + bias
            y_ref[rs, cs] = (z_ref[rs, cs].astype(F32) * mixed).astype(BF16)
    o_ref[...] = x_ref[...] + _dot(y_ref[...], wo_ref[...])


def gmlp_spatial_out(z, x, vg, ws, bt, wo, tm=256):
    n, d = x.shape
    width = z.shape[1] // 2
    tm = min(tm, n)
    return pl.pallas_call(
        functools.partial(_gmlp_body, tm=tm, width=width),
        grid=(n // tm,),
        in_specs=[
            pl.BlockSpec((tm, 2 * width), lambda i: (i, 0)),
            pl.BlockSpec((tm, d), lambda i: (i, 0)),
            pl.BlockSpec((1, width), lambda i: (0, 0)),
            pl.BlockSpec((A_GROUPS, A_CHUNK, A_CHUNK), lambda i: (0, 0, 0)),
            pl.BlockSpec((A_CHUNK, A_GROUPS), lambda i: (0, 0)),
            pl.BlockSpec((width, d), lambda i: (0, 0)),
        ],
        out_specs=pl.BlockSpec((tm, d), lambda i: (i, 0)),
        out_shape=jax.ShapeDtypeStruct((n, d), F32),
        scratch_shapes=[pltpu.VMEM((tm, width), BF16), pltpu.VMEM((tm, width), BF16)],
        compiler_params=_params("parallel"),
        name="gmlp_spatial_out",
    )(z, x, vg, ws, bt, wo)


def _swiglu_step(h, wg_ref, wu_ref, wo_ref):
    a = _dot(h, wg_ref[...])
    t = (a * jax.nn.sigmoid(a)) * _dot(h, wu_ref[...])
    return _dot(t.astype(BF16), wo_ref[...])


def _ffn_body(*refs, final):
    x_ref, g_ref, wg_ref, wu_ref, wo_ref = refs[:5]
    rest = list(refs[5:])
    fg_ref = rest.pop(0) if final else None
    o_ref, h_ref, acc_ref = rest
    j = pl.program_id(1)

    @pl.when(j == 0)
    def _():
        h_ref[...] = _rms(x_ref[...], g_ref[...]).astype(BF16)
        acc_ref[...] = jnp.zeros_like(acc_ref)

    acc_ref[...] += _swiglu_step(h_ref[...], wg_ref, wu_ref, wo_ref)

    @pl.when(j == pl.num_programs(1) - 1)
    def _():
        y = x_ref[...] + acc_ref[...]
        o_ref[...] = _rms(y, fg_ref[...]) if final else y


def ffn(x, g, w_in, w_out, final_g=None, tm=1024, tf=512):
    n, d = x.shape
    f = w_out.shape[0]
    tm, tf = min(tm, n), min(tf, f)
    nf = f // tf
    final = final_g is not None
    in_specs = [
        pl.BlockSpec((tm, d), lambda i, j: (i, 0)),
        pl.BlockSpec((1, d), lambda i, j: (0, 0)),
        pl.BlockSpec((d, tf), lambda i, j: (0, j)),
        pl.BlockSpec((d, tf), lambda i, j: (0, j + nf)),
        pl.BlockSpec((tf, d), lambda i, j: (j, 0)),
    ]
    args = [x, g, w_in, w_in, w_out]
    if final:
        in_specs.append(pl.BlockSpec((1, d), lambda i, j: (0, 0)))
        args.append(final_g)
    return pl.pallas_call(
        functools.partial(_ffn_body, final=final),
        grid=(n // tm, nf),
        in_specs=in_specs,
        out_specs=pl.BlockSpec((tm, d), lambda i, j: (i, 0)),
        out_shape=jax.ShapeDtypeStruct((n, d), F32),
        scratch_shapes=[pltpu.VMEM((tm, d), BF16), pltpu.VMEM((tm, d), F32)],
        compiler_params=_params("parallel", "arbitrary"),
        name="ffn",
    )(*args)


TOK_ROWS = 8


def _router_body(x_ref, g_ref, whi_ref, wlo_ref, gate_ref, tok_ref, cnt_ref, carry_ref, *, ne, tm):
    @pl.when(pl.program_id(0) == 0)
    def _():
        carry_ref[...] = jnp.zeros_like(carry_ref)

    h = _rms(x_ref[...], g_ref[...])
    h_hi = h.astype(BF16)
    h_lo = (h - h_hi.astype(F32)).astype(BF16)
    logits = _dot(h_hi, whi_ref[...]) + _dot(h_lo, whi_ref[...]) + _dot(h_hi, wlo_ref[...])
    lane = lax.broadcasted_iota(jnp.int32, logits.shape, 1).astype(F32)
    neg = -jnp.inf
    l1 = jnp.where(lane < ne, logits, neg)
    m1 = jnp.max(l1, axis=-1, keepdims=True)
    i1 = jnp.min(jnp.where(l1 == m1, lane, float(V7X_LANES)), axis=-1, keepdims=True)
    l2 = jnp.where(lane == i1, neg, l1)
    m2 = jnp.max(l2, axis=-1, keepdims=True)
    i2 = jnp.min(jnp.where(l2 == m2, lane, float(V7X_LANES)), axis=-1, keepdims=True)
    e2 = jnp.exp(m2 - m1)
    den = 1.0 + e2
    gate_ref[...] = jnp.where(lane == 0.0, 1.0 / den, 0.0) + jnp.where(lane == 1.0, e2 / den, 0.0)

    sel1, sel2 = lane == i1, lane == i2
    onehot = jnp.where(sel1 | sel2, 1.0, 0.0)
    row = lax.broadcasted_iota(jnp.int32, (tm, tm), 0)
    col = lax.broadcasted_iota(jnp.int32, (tm, tm), 1)
    before = carry_ref[...] + _dot((col < row).astype(BF16), onehot.astype(BF16))
    carry_ref[...] = before[tm - 1 : tm, :] + onehot[tm - 1 : tm, :]
    cnt_ref[...] = carry_ref[...]
    r1 = jnp.sum(jnp.where(sel1, before, 0.0), axis=-1, keepdims=True)
    r2 = jnp.sum(jnp.where(sel2, before, 0.0), axis=-1, keepdims=True)
    table = (jnp.where(lane == 0.0, i1, 0.0) + jnp.where(lane == 1.0, i2, 0.0)
             + jnp.where(lane == 2.0, r1, 0.0) + jnp.where(lane == 3.0, r2, 0.0))
    pick = (lax.broadcasted_iota(jnp.int32, (TOK_ROWS, V7X_LANES), 0) == lax.broadcasted_iota(jnp.int32, (TOK_ROWS, V7X_LANES), 1)).astype(BF16)
    hi, mid, lo = _split3(table)
    tok_ref[...] = (_dot_nt(pick, hi) + _dot_nt(pick, mid) + _dot_nt(pick, lo)).astype(jnp.int32)


def moe_router(x, g, w_router, tm=1024):
    n, d = x.shape
    ne = w_router.shape[1]
    tm = min(tm, n)
    wpad = jnp.pad(w_router, ((0, 0), (0, V7X_LANES - ne)))
    w_hi = wpad.astype(BF16)
    w_lo = (wpad - w_hi.astype(F32)).astype(BF16)
    return pl.pallas_call(
        functools.partial(_router_body, ne=ne, tm=tm),
        grid=(n // tm,),
        in_specs=[
            pl.BlockSpec((tm, d), lambda i: (i, 0)),
            pl.BlockSpec((1, d), lambda i: (0, 0)),
            pl.BlockSpec((d, V7X_LANES), lambda i: (0, 0)),
            pl.BlockSpec((d, V7X_LANES), lambda i: (0, 0)),
        ],
        out_specs=[
            pl.BlockSpec((tm, V7X_LANES), lambda i: (i, 0)),
            pl.BlockSpec((TOK_ROWS, tm), lambda i: (0, i)),
            pl.BlockSpec((1, V7X_LANES), lambda i: (0, 0)),
        ],
        out_shape=[
            jax.ShapeDtypeStruct((n, V7X_LANES), F32),
            jax.ShapeDtypeStruct((TOK_ROWS, n), jnp.int32),
            jax.ShapeDtypeStruct((1, V7X_LANES), F32),
        ],
        scratch_shapes=[pltpu.VMEM((1, V7X_LANES), F32)],
        compiler_params=_params("arbitrary"),
        name="moe_router",
    )(x, g, w_hi, w_lo)


def _plan_body(cnt_ref, off_ref, te_ref, nv_ref, *, ne, tg, n_tiles):
    off = jnp.int32(0)
    tile = jnp.int32(0)
    for e in range(ne):
        nt = (cnt_ref[e] + (tg - 1)) // tg
        off_ref[e] = off

        def mark(ti, carry, e=e):
            te_ref[ti] = jnp.int32(e)
            return carry

        lax.fori_loop(tile, tile + nt, mark, 0)
        off = off + nt * tg
        tile = tile + nt
    nv_ref[0] = tile
    last = te_ref[jnp.maximum(tile - 1, 0)]

    def fill(ti, carry):
        te_ref[ti] = last
        return carry

    lax.fori_loop(tile, n_tiles, fill, 0)


def moe_plan(cnt, tg, n_tiles):
    ne = cnt.shape[0]
    smem = pl.BlockSpec(memory_space=pltpu.SMEM)
    return pl.pallas_call(
        functools.partial(_plan_body, ne=ne, tg=tg, n_tiles=n_tiles),
        in_specs=[smem],
        out_specs=[smem, smem, smem],
        out_shape=[
            jax.ShapeDtypeStruct((ne,), jnp.int32),
            jax.ShapeDtypeStruct((n_tiles,), jnp.int32),
            jax.ShapeDtypeStruct((1,), jnp.int32),
        ],
        name="moe_plan",
    )(cnt)


def _row_copy(src_hbm, s, dst, r, sem):
    return pltpu.make_async_copy(src_hbm.at[pl.ds(s, 1)], dst.at[pl.ds(r, 1)], sem)


def _dispatch_body(tok_hbm, off_ref, cnt_ref, x_ref, xs_hbm, tok_smem, zero_ref, sem_idx, sem_row, *, td, tg, ne, n_tiles):
    base = pl.multiple_of(pl.program_id(0) * td, td)
    idx = pltpu.make_async_copy(tok_hbm.at[:, pl.ds(base, td)], tok_smem, sem_idx)
    idx.start()

    @pl.when(pl.program_id(0) == 0)
    def _():
        zero_ref[...] = jnp.zeros_like(zero_ref)
        for e in range(ne):
            start = off_ref[e] + cnt_ref[e]
            stop = off_ref[e] + (cnt_ref[e] + (tg - 1)) // tg * tg

            def zero_start(r, carry):
                _row_copy(zero_ref, 0, xs_hbm, r, sem_row).start()
                return carry

            def zero_wait(r, carry):
                _row_copy(zero_ref, 0, xs_hbm, r, sem_row).wait()
                return carry

            lax.fori_loop(start, stop, zero_start, 0)
            lax.fori_loop(start, stop, zero_wait, 0)

        def tile_copy(ti):
            return pltpu.make_async_copy(zero_ref, xs_hbm.at[pl.ds(pl.multiple_of(ti * tg, tg), tg)], sem_row)

        def tile_start(ti, carry):
            tile_copy(ti).start()
            return carry

        def tile_wait(ti, carry):
            tile_copy(ti).wait()
            return carry

        used = stop // tg
        lax.fori_loop(used, n_tiles, tile_start, 0)
        lax.fori_loop(used, n_tiles, tile_wait, 0)

    idx.wait()

    def copies(t):
        return [_row_copy(x_ref, t, xs_hbm, off_ref[tok_smem[k, t]] + tok_smem[TOP_K + k, t], sem_row) for k in range(TOP_K)]

    def start(t, c):
        for cp in copies(t):
            cp.start()
        return c

    lax.fori_loop(0, td, start, 0, unroll=8)
    for _ in range(TOP_K):
        pltpu.make_async_copy(x_ref, xs_hbm.at[pl.ds(0, td)], sem_row).wait()


def moe_dispatch(tok, off, cnt, x, tg, n_tiles, td=1024):
    n, d = x.shape
    td = min(td, n)
    smem = pl.BlockSpec(memory_space=pltpu.SMEM)
    hbm = pl.BlockSpec(memory_space=pl.ANY)
    return pl.pallas_call(
        functools.partial(_dispatch_body, td=td, tg=tg, ne=cnt.shape[0], n_tiles=n_tiles),
        grid=(n // td,),
        in_specs=[hbm, smem, smem, pl.BlockSpec((td, d), lambda i: (i, 0))],
        out_specs=hbm,
        out_shape=jax.ShapeDtypeStruct((n_tiles * tg, d), x.dtype),
        scratch_shapes=[
            pltpu.SMEM((TOK_ROWS, td), jnp.int32),
            pltpu.VMEM((tg, d), x.dtype),
            pltpu.SemaphoreType.DMA(()),
            pltpu.SemaphoreType.DMA(()),
        ],
        compiler_params=pltpu.CompilerParams(dimension_semantics=("arbitrary",), has_side_effects=True, vmem_limit_bytes=V7X_VMEM_LIMIT_BYTES),
        name="moe_dispatch",
    )(tok, off, cnt, x)


def _grouped_ffn_body(te_ref, nv_ref, x_ref, g_ref, wg_ref, wu_ref, wo_ref, o_ref, h_ref):
    i, j = pl.program_id(0), pl.program_id(1)

    @pl.when(i < nv_ref[0])
    def _():
        @pl.when(j == 0)
        def _():
            h_ref[...] = _rms(x_ref[...], g_ref[...]).astype(BF16)

        y = _swiglu_step(h_ref[...], wg_ref, wu_ref, wo_ref)

        @pl.when(j == 0)
        def _():
            o_ref[...] = y

        @pl.when(j > 0)
        def _():
            o_ref[...] += y

    @pl.when((i >= nv_ref[0]) & (j == 0))
    def _():
        o_ref[...] = jnp.zeros_like(o_ref)


def moe_grouped_ffn(te, nv, xs, g, w_in, w_out, tg, tf=1024):
    r, d = xs.shape
    f = w_out.shape[1]
    tf = _block(f, tf)
    nf = f // tf
    n_tiles = r // tg

    def tile(i, nv):
        return jnp.minimum(i, nv[0] - 1)

    def fblk(i, j, nv):
        return jnp.where(i < nv[0], j, nf - 1)

    grid_spec = pltpu.PrefetchScalarGridSpec(
        num_scalar_prefetch=2,
        grid=(n_tiles, nf),
        in_specs=[
            pl.BlockSpec((tg, d), lambda i, j, te, nv: (tile(i, nv), 0)),
            pl.BlockSpec((1, d), lambda i, j, te, nv: (0, 0)),
            pl.BlockSpec((None, d, tf), lambda i, j, te, nv: (te[i], 0, fblk(i, j, nv))),
            pl.BlockSpec((None, d, tf), lambda i, j, te, nv: (te[i], 0, fblk(i, j, nv) + nf)),
            pl.BlockSpec((None, tf, d), lambda i, j, te, nv: (te[i], fblk(i, j, nv), 0)),
        ],
        out_specs=pl.BlockSpec((tg, d), lambda i, j, te, nv: (i, 0)),
        scratch_shapes=[pltpu.VMEM((tg, d), BF16)],
    )
    return pl.pallas_call(
        _grouped_ffn_body,
        grid_spec=grid_spec,
        out_shape=jax.ShapeDtypeStruct((r, d), F32),
        compiler_params=_params("arbitrary", "arbitrary"),
        name="moe_grouped_ffn",
    )(te, nv, xs, g, w_in, w_in, w_out)


def _combine_body(tok_hbm, off_ref, gate_ref, x_ref, ys_hbm, *rest, tc, final):
    rest = list(rest)
    fg_ref = rest.pop(0) if final else None
    o_ref, tok_smem, buf_ref, sem_idx, sem_row = rest
    base = pl.multiple_of(pl.program_id(0) * tc, tc)
    idx = pltpu.make_async_copy(tok_hbm.at[:, pl.ds(base, tc)], tok_smem, sem_idx)
    idx.start()
    idx.wait()

    def copies(t):
        return [_row_copy(ys_hbm, off_ref[tok_smem[k, t]] + tok_smem[TOP_K + k, t], buf_ref.at[k], t, sem_row) for k in range(TOP_K)]

    def start(t, c):
        for cp in copies(t):
            cp.start()
        return c

    lax.fori_loop(0, tc, start, 0, unroll=8)
    for k in range(TOP_K):
        pltpu.make_async_copy(ys_hbm.at[pl.ds(0, tc)], buf_ref.at[k], sem_row).wait()
    y = x_ref[...] + gate_ref[:, 0:1] * buf_ref[0] + gate_ref[:, 1:2] * buf_ref[1]
    o_ref[...] = _rms(y, fg_ref[...]) if final else y


def moe_combine(tok, off, gates, x, ys, final_g=None, tc=512):
    n, d = x.shape
    tc = min(tc, n)
    final = final_g is not None
    smem = pl.BlockSpec(memory_space=pltpu.SMEM)
    hbm = pl.BlockSpec(memory_space=pl.ANY)
    in_specs = [hbm, smem, pl.BlockSpec((tc, V7X_LANES), lambda i: (i, 0)), pl.BlockSpec((tc, d), lambda i: (i, 0)), hbm]
    args = [tok, off, gates, x, ys]
    if final:
        in_specs.append(pl.BlockSpec((1, d), lambda i: (0, 0)))
        args.append(final_g)
    return pl.pallas_call(
        functools.partial(_combine_body, tc=tc, final=final),
        grid=(n // tc,),
        in_specs=in_specs,
        out_specs=pl.BlockSpec((tc, d), lambda i: (i, 0)),
        out_shape=jax.ShapeDtypeStruct((n, d), F32),
        scratch_shapes=[
            pltpu.SMEM((TOK_ROWS, tc), jnp.int32),
            pltpu.VMEM((TOP_K, tc, d), F32),
            pltpu.SemaphoreType.DMA(()),
            pltpu.SemaphoreType.DMA(()),
        ],
        compiler_params=_params("arbitrary"),
        name="moe_combine",
    )(*args)


def moe_ffn(x, g, w_router, w_in, w_out, final_g=None, tg=512):
    n, _ = x.shape
    ne = w_router.shape[1]
    tg = min(tg, n)
    n_tiles = TOP_K * n // tg + ne
    gates, tok, counts = moe_router(x, g, w_router)
    cnt = counts[0, :ne].astype(jnp.int32)
    off, te, nv = moe_plan(cnt, tg, n_tiles)
    xs = moe_dispatch(tok, off, cnt, x, tg, n_tiles)
    ys = moe_grouped_ffn(te, nv, xs, g, w_in, w_out, tg)
    return moe_combine(tok, off, gates, x, ys, final_g)


def _gla_gate_body(x_ref, g_ref, wl_ref, w2_ref, b_ref, o_ref):
    h = _rms(x_ref[...], g_ref[...]).astype(BF16)
    g_low = _dot(h, wl_ref[...]).astype(BF16)
    o_ref[...] = _log_sigmoid(_dot(g_low, w2_ref[...]) + b_ref[...]) / B_TAU


def gla_gate(x, g, w_low, w_gate2, bias, tm=1024):
    n, d = x.shape
    dk = w_gate2.shape[1]
    tm = min(tm, n)
    wl = jnp.pad(w_low, ((0, 0), (0, V7X_LANES - B_GATE_RANK))).astype(BF16)
    w2 = jnp.pad(w_gate2, ((0, V7X_LANES - B_GATE_RANK), (0, 0))).astype(BF16)
    return pl.pallas_call(
        _gla_gate_body,
        grid=(n // tm,),
        in_specs=[
            pl.BlockSpec((tm, d), lambda i: (i, 0)),
            pl.BlockSpec((1, d), lambda i: (0, 0)),
            pl.BlockSpec((d, V7X_LANES), lambda i: (0, 0)),
            pl.BlockSpec((V7X_LANES, dk), lambda i: (0, 0)),
            pl.BlockSpec((1, dk), lambda i: (0, 0)),
        ],
        out_specs=pl.BlockSpec((tm, dk), lambda i: (i, 0)),
        out_shape=jax.ShapeDtypeStruct((n, dk), F32),
        compiler_params=_params("parallel"),
        name="gla_gate",
    )(x, g, wl, w2, bias)


def _gla_body(q_ref, k_ref, v_ref, r_ref, la_ref, og_ref, o_ref, st_ref, *, tc, hk, hv):
    @pl.when(pl.program_id(1) == 0)
    def _():
        st_ref[...] = jnp.zeros_like(st_ref)

    c = B_CHUNK
    tril = _tril_mask(c)
    ones_tril = tril.astype(BF16)
    scale = hk**-0.5
    for ci in range(tc // c):
        rs = slice(ci * c, (ci + 1) * c)
        hi, mid, lo = _split3(la_ref[rs, :])
        bcum = _dot(ones_tril, hi) + _dot(ones_tril, mid) + _dot(ones_tril, lo)
        b_last = bcum[c - 1 : c, :]
        e_pos = jnp.exp(bcum)
        e_neg = jnp.exp(-bcum)
        e_end = jnp.exp(b_last - bcum)
        dec = jnp.exp(b_last)
        for h in range(B_HEADS):
            ks = slice(h * hk, (h + 1) * hk)
            vs = slice(h * hv, (h + 1) * hv)
            q = q_ref[rs, ks].astype(F32) * scale
            k = k_ref[rs, ks].astype(F32)
            v = v_ref[rs, vs]
            q_d = (q * e_pos[:, ks]).astype(BF16)
            k_d = (k * e_neg[:, ks]).astype(BF16)
            k_end = (k * e_end[:, ks]).astype(BF16)
            att = jnp.where(tril, _dot_nt(q_d, k_d), 0.0)
            st = st_ref[h]
            o = _dot(att.astype(BF16), v) + _dot_nt(q_d, st.astype(BF16))
            st_ref[h] = dec[:, ks] * st + _dot_tn(v, k_end)
            y = _rms(o, og_ref[...]).astype(BF16).astype(F32)
            r = r_ref[rs, vs].astype(F32)
            o_ref[rs, vs] = (y * (r * jax.nn.sigmoid(r))).astype(BF16)


def gla_chunks(proj, la, o_g, bsz, tc=256):
    n = proj.shape[0]
    dk = la.shape[1]
    dv = (proj.shape[1] - 2 * dk) // 2
    s = n // bsz
    tc = min(tc, s)
    nt = s // tc
    hk, hv = dk // B_HEADS, dv // B_HEADS
    assert (2 * dk) % dv == 0
    v_blk = 2 * dk // dv
    row = lambda b, t: b * nt + t
    return pl.pallas_call(
        functools.partial(_gla_body, tc=tc, hk=hk, hv=hv),
        grid=(bsz, nt),
        in_specs=[
            pl.BlockSpec((tc, dk), lambda b, t: (row(b, t), 0)),
            pl.BlockSpec((tc, dk), lambda b, t: (row(b, t), 1)),
            pl.BlockSpec((tc, dv), lambda b, t: (row(b, t), v_blk)),
            pl.BlockSpec((tc, dv), lambda b, t: (row(b, t), v_blk + 1)),
            pl.BlockSpec((tc, dk), lambda b, t: (row(b, t), 0)),
            pl.BlockSpec((1, hv), lambda b, t: (0, 0)),
        ],
        out_specs=pl.BlockSpec((tc, dv), lambda b, t: (row(b, t), 0)),
        out_shape=jax.ShapeDtypeStruct((n, dv), BF16),
        scratch_shapes=[pltpu.VMEM((B_HEADS, hv, hk), F32)],
        compiler_params=_params("parallel", "arbitrary"),
        name="gla_chunks",
    )(proj, proj, proj, proj, la, o_g)


def _fox_gate_body(x_ref, g_ref, wf_ref, b_ref, crow_ref, carry_ref, *, tb, nh):
    @pl.when(pl.program_id(1) == 0)
    def _():
        carry_ref[...] = jnp.zeros_like(carry_ref)

    h = _rms(x_ref[...], g_ref[...]).astype(BF16)
    lane = lax.broadcasted_iota(jnp.int32, (tb, V7X_LANES), 1)
    log_f = jnp.where(lane < nh, _log_sigmoid(_dot(h, wf_ref[...]) + b_ref[...]), 0.0)
    ones_tril = _tril_mask(tb).astype(BF16)
    hi, mid, lo = _split3(log_f)
    c = carry_ref[...] + (_dot(ones_tril, hi) + _dot(ones_tril, mid) + _dot(ones_tril, lo))
    carry_ref[...] = c[tb - 1 : tb, :]
    sel = (lax.broadcasted_iota(jnp.int32, (nh, V7X_LANES), 0) == lax.broadcasted_iota(jnp.int32, (nh, V7X_LANES), 1)).astype(BF16)
    hi, mid, lo = _split3(c)
    crow_ref[...] = _dot_nt(sel, hi) + _dot_nt(sel, mid) + _dot_nt(sel, lo)


def fox_gate(x, g, w_f, f_bias, bsz, tb=512):
    n, d = x.shape
    nh = w_f.shape[1]
    s = n // bsz
    tb = min(tb, s)
    nt = s // tb
    wf = jnp.pad(w_f, ((0, 0), (0, V7X_LANES - nh))).astype(BF16)
    bias = jnp.pad(f_bias.reshape(1, nh), ((0, 0), (0, V7X_LANES - nh)))
    return pl.pallas_call(
        functools.partial(_fox_gate_body, tb=tb, nh=nh),
        grid=(bsz, nt),
        in_specs=[
            pl.BlockSpec((tb, d), lambda b, t: (b * nt + t, 0)),
            pl.BlockSpec((1, d), lambda b, t: (0, 0)),
            pl.BlockSpec((d, V7X_LANES), lambda b, t: (0, 0)),
            pl.BlockSpec((1, V7X_LANES), lambda b, t: (0, 0)),
        ],
        out_specs=pl.BlockSpec((None, nh, tb), lambda b, t: (b, 0, t)),
        out_shape=jax.ShapeDtypeStruct((bsz, nh, s), F32),
        scratch_shapes=[pltpu.VMEM((1, V7X_LANES), F32)],
        compiler_params=_params("parallel", "arbitrary"),
        name="fox_gate",
    )(x, g, wf, bias)


FOX_SUBTILES = 2
FOX_ROW_CHUNK = 32


def _fox_flash_body(q_ref, k_ref, v_ref, crow_ref, o_ref, qs_ref, va_ref, s_ref, p_ref, al_ref, m_ref, acc_ref, *, t, scale):
    i = pl.program_id(2)
    dh = q_ref.shape[1]
    nl = t // V7X_LANES

    @pl.when(i == 0)
    def _():
        va_ref[:, :dh] = v_ref[...]
        va_ref[:, dh:] = jnp.ones((va_ref.shape[0], V7X_LANES), BF16)

    qs_ref[...] = (q_ref[...].astype(F32) * scale).astype(BF16)
    m_ref[...] = jnp.full_like(m_ref, -jnp.inf)
    acc_ref[...] = jnp.zeros_like(acc_ref)

    def block(sub, kb, masked):
        rows = slice(sub * t, (sub + 1) * t)
        off = pl.multiple_of(kb * t, t)
        s_ref[sub] = _dot_nt(qs_ref[rows, :], k_ref[pl.ds(off, t), :])
        bias = crow_ref[pl.ds(kb, 1), :]

        def chunk(c, carry):
            r = pl.multiple_of(c * FOX_ROW_CHUNK, FOX_ROW_CHUNK)
            sc = s_ref[sub, pl.ds(r, FOX_ROW_CHUNK), :] - bias
            if masked:
                row = r + lax.broadcasted_iota(jnp.int32, sc.shape, 0)
                col = lax.broadcasted_iota(jnp.int32, sc.shape, 1)
                sc = jnp.where(col <= row, sc, -jnp.inf)
            tiles = [sc[:, a * V7X_LANES : (a + 1) * V7X_LANES] for a in range(nl)]
            mx = functools.reduce(jnp.maximum, tiles)
            m_old = m_ref[pl.ds(sub * t + r, FOX_ROW_CHUNK), :]
            m_new = jnp.maximum(m_old, jnp.max(mx, axis=-1, keepdims=True))
            al_ref[sub, pl.ds(r, FOX_ROW_CHUNK), :] = jnp.exp(m_old - m_new)
            m_ref[pl.ds(sub * t + r, FOX_ROW_CHUNK), :] = m_new
            for a in range(nl):
                p_ref[sub, pl.ds(r, FOX_ROW_CHUNK), a * V7X_LANES : (a + 1) * V7X_LANES] = jnp.exp(tiles[a] - m_new).astype(BF16)
            return carry

        lax.fori_loop(0, t // FOX_ROW_CHUNK, chunk, 0, unroll=True)
        pv = _dot(p_ref[sub], va_ref[pl.ds(off, t), :])
        alpha = al_ref[sub]
        acc_ref[rows, :dh] = alpha * acc_ref[rows, :dh] + pv[:, :dh]
        acc_ref[rows, dh:] = alpha * acc_ref[rows, dh:] + pv[:, dh:]

    def below_diagonal(kb, carry):
        for sub in range(FOX_SUBTILES):
            block(sub, kb, False)
        return carry

    first = i * FOX_SUBTILES
    lax.fori_loop(0, first, below_diagonal, 0)
    for kb in range(FOX_SUBTILES):
        for sub in range(kb, FOX_SUBTILES):
            block(sub, first + kb, masked=(sub == kb))
    o_ref[...] = (acc_ref[:, :dh] / acc_ref[:, dh:]).astype(o_ref.dtype)


def fox_flash(proj, crow, bsz, nh, t=512):
    n = proj.shape[0]
    dh = proj.shape[1] // (3 * nh)
    assert dh == V7X_LANES
    s = n // bsz
    t = min(t, s // FOX_SUBTILES)
    tq = t * FOX_SUBTILES
    nq = s // tq
    crow = crow.reshape(bsz, nh, s // t, t)
    return pl.pallas_call(
        functools.partial(_fox_flash_body, t=t, scale=dh**-0.5),
        grid=(bsz, nh, nq),
        in_specs=[
            pl.BlockSpec((tq, dh), lambda b, h, i: (b * nq + i, h)),
            pl.BlockSpec((s, dh), lambda b, h, i: (b, nh + h)),
            pl.BlockSpec((s, dh), lambda b, h, i: (b, 2 * nh + h)),
            pl.BlockSpec((None, None, s // t, t), lambda b, h, i: (b, h, 0, 0)),
        ],
        out_specs=pl.BlockSpec((tq, dh), lambda b, h, i: (b * nq + i, h)),
        out_shape=jax.ShapeDtypeStruct((n, nh * dh), BF16),
        scratch_shapes=[
            pltpu.VMEM((tq, dh), BF16),
            pltpu.VMEM((s, dh + V7X_LANES), BF16),
            pltpu.VMEM((FOX_SUBTILES, t, t), F32),
            pltpu.VMEM((FOX_SUBTILES, t, t), BF16),
            pltpu.VMEM((FOX_SUBTILES, t, V7X_LANES), F32),
            pltpu.VMEM((tq, V7X_LANES), F32),
            pltpu.VMEM((tq, dh + V7X_LANES), F32),
        ],
        compiler_params=_params("parallel", "parallel", "arbitrary"),
        name="fox_flash",
    )(proj, proj, proj, crow)


def kernel(x, norm1_g, norm2_g, a_w_in, a_vnorm_g, a_w_s, a_b_s, a_w_out, b_w_in, b_w_gate2, b_gate_bias, b_onorm_g, b_w_out, c_w_in, c_f_bias, c_w_out, ffn_w_in, ffn_w_out, moe_router, moe_w_in, moe_w_out, final_g):
    bsz, s, d = x.shape
    depth = norm1_g.shape[0]
    xf = x.reshape(bsz * s, d)
    for i in range(depth):
        g1 = norm1_g[i].reshape(1, d)
        m, j = i % N_MIXERS, i // N_MIXERS
        if m == 0:
            z = norm_matmul(xf, g1, a_w_in[j].astype(BF16), act="gelu")
            xf = gmlp_spatial_out(z, xf, a_vnorm_g[j].reshape(1, -1), a_w_s[j], a_b_s[j].T, a_w_out[j].astype(BF16))
        elif m == 1:
            dk = b_w_gate2.shape[2]
            n_main = b_w_in.shape[2] - B_GATE_RANK
            proj = norm_matmul(xf, g1, b_w_in[j, :, :n_main].astype(BF16))
            la = gla_gate(xf, g1, b_w_in[j, :, n_main:], b_w_gate2[j], b_gate_bias[j].reshape(1, dk))
            o = gla_chunks(proj, la, b_onorm_g[j].reshape(1, -1), bsz)
            xf = matmul_residual(o, b_w_out[j].astype(BF16), xf)
        else:
            proj = norm_matmul(xf, g1, c_w_in[j, :, : 3 * d].astype(BF16))
            crow = fox_gate(xf, g1, c_w_in[j, :, 3 * d :], c_f_bias[j], bsz)
            o = fox_flash(proj, crow, bsz, C_HEADS)
            xf = matmul_residual(o, c_w_out[j].astype(BF16), xf)
        g2 = norm2_g[i].reshape(1, d)
        fg = final_g.reshape(1, d) if i == depth - 1 else None
        if i % 2 == 0:
            xf = ffn(xf, g2, ffn_w_in[i // 2].astype(BF16), ffn_w_out[i // 2].astype(BF16), final_g=fg)
        else:
            xf = moe_ffn(xf, g2, moe_router[i // 2], moe_w_in[i // 2].astype(BF16), moe_w_out[i // 2].astype(BF16), final_g=fg)
    return xf.reshape(bsz, s, d)
```

```python
import functools

import jax
import jax.numpy as jnp
from jax import lax
from jax.experimental import pallas as pl
from jax.experimental.pallas import tpu as pltpu

F32 = jnp.float32
BF16 = jnp.bfloat16

EPS = 1e-6
N_MIXERS = 3
TOP_K = 2
A_CHUNK = 128
A_GROUPS = 8
B_HEADS = 4
B_GATE_RANK = 16
B_TAU = 16.0
B_CHUNK = 64
C_HEADS = 8

V7X_LANES = 128
V7X_VMEM_LIMIT_BYTES = 56 * 1024 * 1024


def _params(*sem):
    return pltpu.CompilerParams(dimension_semantics=sem, vmem_limit_bytes=V7X_VMEM_LIMIT_BYTES)


def _block(total, target, align=V7X_LANES):
    if total <= target:
        return total
    return max(b for b in range(align, target + 1, align) if total % b == 0)


def _rms(x, g):
    ms = jnp.mean(x * x, axis=-1, keepdims=True)
    return x * lax.rsqrt(ms + EPS) * g


def _log_sigmoid(x):
    return jnp.minimum(x, 0.0) - jnp.log1p(jnp.exp(-jnp.abs(x)))


def _split3(x):
    hi = x.astype(BF16)
    r1 = x - hi.astype(F32)
    mid = r1.astype(BF16)
    lo = (r1 - mid.astype(F32)).astype(BF16)
    return hi, mid, lo


def _dot(a, b):
    return jnp.dot(a, b, preferred_element_type=F32)


def _dot_nt(a, b):
    return lax.dot_general(a, b, (((1,), (1,)), ((), ())), preferred_element_type=F32)


def _dot_tn(a, b):
    return lax.dot_general(a, b, (((0,), (0,)), ((), ())), preferred_element_type=F32)


def _tril_mask(n):
    row = lax.broadcasted_iota(jnp.int32, (n, n), 0)
    col = lax.broadcasted_iota(jnp.int32, (n, n), 1)
    return col <= row


def _norm_matmul_body(x_ref, g_ref, w_ref, o_ref, h_ref, *, act):
    @pl.when(pl.program_id(1) == 0)
    def _():
        h_ref[...] = _rms(x_ref[...], g_ref[...]).astype(BF16)

    y = _dot(h_ref[...], w_ref[...])
    if act == "gelu":
        y = jax.nn.gelu(y, approximate=True)
    o_ref[...] = y.astype(o_ref.dtype)


def norm_matmul(x, g, w, act=None, tm=1024, tn=1024):
    n, d = x.shape
    nout = w.shape[1]
    tm, tn = min(tm, n), _block(nout, tn)
    return pl.pallas_call(
        functools.partial(_norm_matmul_body, act=act),
        grid=(n // tm, nout // tn),
        in_specs=[
            pl.BlockSpec((tm, d), lambda i, j: (i, 0)),
            pl.BlockSpec((1, d), lambda i, j: (0, 0)),
            pl.BlockSpec((d, tn), lambda i, j: (0, j)),
        ],
        out_specs=pl.BlockSpec((tm, tn), lambda i, j: (i, j)),
        out_shape=jax.ShapeDtypeStruct((n, nout), BF16),
        scratch_shapes=[pltpu.VMEM((tm, d), BF16)],
        compiler_params=_params("parallel", "arbitrary"),
        name="norm_matmul",
    )(x, g, w)


def _matmul_residual_body(y_ref, w_ref, x_ref, o_ref):
    o_ref[...] = x_ref[...] + _dot(y_ref[...], w_ref[...])


def matmul_residual(y, w, x, tm=512):
    n, k = y.shape
    d = w.shape[1]
    tm = min(tm, n)
    return pl.pallas_call(
        _matmul_residual_body,
        grid=(n // tm,),
        in_specs=[
            pl.BlockSpec((tm, k), lambda i: (i, 0)),
            pl.BlockSpec((k, d), lambda i: (0, 0)),
            pl.BlockSpec((tm, d), lambda i: (i, 0)),
        ],
        out_specs=pl.BlockSpec((tm, d), lambda i: (i, 0)),
        out_shape=jax.ShapeDtypeStruct((n, d), F32),
        compiler_params=_params("parallel"),
        name="matmul_residual",
    )(y, w, x)


def _gmlp_body(z_ref, x_ref, vg_ref, ws_ref, bt_ref, wo_ref, o_ref, vn_ref, y_ref, *, tm, width):
    gd = width // A_GROUPS
    vn_ref[...] = _rms(z_ref[:, width:].astype(F32), vg_ref[...]).astype(BF16)
    tril = _tril_mask(A_CHUNK)
    for g in range(A_GROUPS):
        w = jnp.where(tril, ws_ref[g], 0.0).astype(BF16)
        bias = bt_ref[:, g : g + 1]
        cs = slice(g * gd, (g + 1) * gd)
        for c in range(tm // A_CHUNK):
            rs = slice(c * A_CHUNK, (c + 1) * A_CHUNK)
            mixed = _dot(w, vn_ref[rs, cs]) + bias
            y_ref[rs, cs] = (z_ref[rs, cs].astype(F32) * mixed).astype(BF16)
    o_ref[...] = x_ref[...] + _dot(y_ref[...], wo_ref[...])


def gmlp_spatial_out(z, x, vg, ws, bt, wo, tm=256):
    n, d = x.shape
    width = z.shape[1] // 2
    tm = min(tm, n)
    return pl.pallas_call(
        functools.partial(_gmlp_body, tm=tm, width=width),
        grid=(n // tm,),
        in_specs=[
            pl.BlockSpec((tm, 2 * width), lambda i: (i, 0)),
            pl.BlockSpec((tm, d), lambda i: (i, 0)),
            pl.BlockSpec((1, width), lambda i: (0, 0)),
            pl.BlockSpec((A_GROUPS, A_CHUNK, A_CHUNK), lambda i: (0, 0, 0)),
            pl.BlockSpec((A_CHUNK, A_GROUPS), lambda i: (0, 0)),
            pl.BlockSpec((width, d), lambda i: (0, 0)),
        ],
        out_specs=pl.BlockSpec((tm, d), lambda i: (i, 0)),
        out_shape=jax.ShapeDtypeStruct((n, d), F32),
        scratch_shapes=[pltpu.VMEM((tm, width), BF16), pltpu.VMEM((tm, width), BF16)],
        compiler_params=_params("parallel"),
        name="gmlp_spatial_out",
    )(z, x, vg, ws, bt, wo)


def _swiglu_step(h, wg_ref, wu_ref, wo_ref):
    a = _dot(h, wg_ref[...])
    t = (a * jax.nn.sigmoid(a)) * _dot(h, wu_ref[...])
    return _dot(t.astype(BF16), wo_ref[...])


def _ffn_body(*refs, final):
    x_ref, g_ref, wg_ref, wu_ref, wo_ref = refs[:5]
    rest = list(refs[5:])
    fg_ref = rest.pop(0) if final else None
    o_ref, h_ref, acc_ref = rest
    j = pl.program_id(1)

    @pl.when(j == 0)
    def _():
        h_ref[...] = _rms(x_ref[...], g_ref[...]).astype(BF16)
        acc_ref[...] = jnp.zeros_like(acc_ref)

    acc_ref[...] += _swiglu_step(h_ref[...], wg_ref, wu_ref, wo_ref)

    @pl.when(j == pl.num_programs(1) - 1)
    def _():
        y = x_ref[...] + acc_ref[...]
        o_ref[...] = _rms(y, fg_ref[...]) if final else y


def ffn(x, g, w_in, w_out, final_g=None, tm=1024, tf=512):
    n, d = x.shape
    f = w_out.shape[0]
    tm, tf = min(tm, n), min(tf, f)
    nf = f // tf
    final = final_g is not None
    in_specs = [
        pl.BlockSpec((tm, d), lambda i, j: (i, 0)),
        pl.BlockSpec((1, d), lambda i, j: (0, 0)),
        pl.BlockSpec((d, tf), lambda i, j: (0, j)),
        pl.BlockSpec((d, tf), lambda i, j: (0, j + nf)),
        pl.BlockSpec((tf, d), lambda i, j: (j, 0)),
    ]
    args = [x, g, w_in, w_in, w_out]
    if final:
        in_specs.append(pl.BlockSpec((1, d), lambda i, j: (0, 0)))
        args.append(final_g)
    return pl.pallas_call(
        functools.partial(_ffn_body, final=final),
        grid=(n // tm, nf),
        in_specs=in_specs,
        out_specs=pl.BlockSpec((tm, d), lambda i, j: (i, 0)),
        out_shape=jax.ShapeDtypeStruct((n, d), F32),
        scratch_shapes=[pltpu.VMEM((tm, d), BF16), pltpu.VMEM((tm, d), F32)],
        compiler_params=_params("parallel", "arbitrary"),
        name="ffn",
    )(*args)


TOK_ROWS = 8


def _router_body(x_ref, g_ref, whi_ref, wlo_ref, gate_ref, tok_ref, cnt_ref, carry_ref, *, ne, tm):
    @pl.when(pl.program_id(0) == 0)
    def _():
        carry_ref[...] = jnp.zeros_like(carry_ref)

    h = _rms(x_ref[...], g_ref[...])
    h_hi = h.astype(BF16)
    h_lo = (h - h_hi.astype(F32)).astype(BF16)
    logits = _dot(h_hi, whi_ref[...]) + _dot(h_lo, whi_ref[...]) + _dot(h_hi, wlo_ref[...])
    lane = lax.broadcasted_iota(jnp.int32, logits.shape, 1).astype(F32)
    neg = -jnp.inf
    l1 = jnp.where(lane < ne, logits, neg)
    m1 = jnp.max(l1, axis=-1, keepdims=True)
    i1 = jnp.min(jnp.where(l1 == m1, lane, float(V7X_LANES)), axis=-1, keepdims=True)
    l2 = jnp.where(lane == i1, neg, l1)
    m2 = jnp.max(l2, axis=-1, keepdims=True)
    i2 = jnp.min(jnp.where(l2 == m2, lane, float(V7X_LANES)), axis=-1, keepdims=True)
    e2 = jnp.exp(m2 - m1)
    den = 1.0 + e2
    gate_ref[...] = jnp.where(lane == 0.0, 1.0 / den, 0.0) + jnp.where(lane == 1.0, e2 / den, 0.0)

    sel1, sel2 = lane == i1, lane == i2
    onehot = jnp.where(sel1 | sel2, 1.0, 0.0)
    row = lax.broadcasted_iota(jnp.int32, (tm, tm), 0)
    col = lax.broadcasted_iota(jnp.int32, (tm, tm), 1)
    before = carry_ref[...] + _dot((col < row).astype(BF16), onehot.astype(BF16))
    carry_ref[...] = before[tm - 1 : tm, :] + onehot[tm - 1 : tm, :]
    cnt_ref[...] = carry_ref[...]
    r1 = jnp.sum(jnp.where(sel1, before, 0.0), axis=-1, keepdims=True)
    r2 = jnp.sum(jnp.where(sel2, before, 0.0), axis=-1, keepdims=True)
    table = (jnp.where(lane == 0.0, i1, 0.0) + jnp.where(lane == 1.0, i2, 0.0)
             + jnp.where(lane == 2.0, r1, 0.0) + jnp.where(lane == 3.0, r2, 0.0))
    pick = (lax.broadcasted_iota(jnp.int32, (TOK_ROWS, V7X_LANES), 0) == lax.broadcasted_iota(jnp.int32, (TOK_ROWS, V7X_LANES), 1)).astype(BF16)
    hi, mid, lo = _split3(table)
    tok_ref[...] = (_dot_nt(pick, hi) + _dot_nt(pick, mid) + _dot_nt(pick, lo)).astype(jnp.int32)


def moe_router(x, g, w_router, tm=1024):
    n, d = x.shape
    ne = w_router.shape[1]
    tm = min(tm, n)
    wpad = jnp.pad(w_router, ((0, 0), (0, V7X_LANES - ne)))
    w_hi = wpad.astype(BF16)
    w_lo = (wpad - w_hi.astype(F32)).astype(BF16)
    return pl.pallas_call(
        functools.partial(_router_body, ne=ne, tm=tm),
        grid=(n // tm,),
        in_specs=[
            pl.BlockSpec((tm, d), lambda i: (i, 0)),
            pl.BlockSpec((1, d), lambda i: (0, 0)),
            pl.BlockSpec((d, V7X_LANES), lambda i: (0, 0)),
            pl.BlockSpec((d, V7X_LANES), lambda i: (0, 0)),
        ],
        out_specs=[
            pl.BlockSpec((tm, V7X_LANES), lambda i: (i, 0)),
            pl.BlockSpec((TOK_ROWS, tm), lambda i: (0, i)),
            pl.BlockSpec((1, V7X_LANES), lambda i: (0, 0)),
        ],
        out_shape=[
            jax.ShapeDtypeStruct((n, V7X_LANES), F32),
            jax.ShapeDtypeStruct((TOK_ROWS, n), jnp.int32),
            jax.ShapeDtypeStruct((1, V7X_LANES), F32),
        ],
        scratch_shapes=[pltpu.VMEM((1, V7X_LANES), F32)],
        compiler_params=_params("arbitrary"),
        name="moe_router",
    )(x, g, w_hi, w_lo)


def _plan_body(cnt_ref, off_ref, te_ref, nv_ref, *, ne, tg, n_tiles):
    off = jnp.int32(0)
    tile = jnp.int32(0)
    for e in range(ne):
        nt = (cnt_ref[e] + (tg - 1)) // tg
        off_ref[e] = off

        def mark(ti, carry, e=e):
            te_ref[ti] = jnp.int32(e)
            return carry

        lax.fori_loop(tile, tile + nt, mark, 0)
        off = off + nt * tg
        tile = tile + nt
    nv_ref[0] = tile
    last = te_ref[jnp.maximum(tile - 1, 0)]

    def fill(ti, carry):
        te_ref[ti] = last
        return carry

    lax.fori_loop(tile, n_tiles, fill, 0)


def moe_plan(cnt, tg, n_tiles):
    ne = cnt.shape[0]
    smem = pl.BlockSpec(memory_space=pltpu.SMEM)
    return pl.pallas_call(
        functools.partial(_plan_body, ne=ne, tg=tg, n_tiles=n_tiles),
        in_specs=[smem],
        out_specs=[smem, smem, smem],
        out_shape=[
            jax.ShapeDtypeStruct((ne,), jnp.int32),
            jax.ShapeDtypeStruct((n_tiles,), jnp.int32),
            jax.ShapeDtypeStruct((1,), jnp.int32),
        ],
        name="moe_plan",
    )(cnt)


def _row_copy(src_hbm, s, dst, r, sem):
    return pltpu.make_async_copy(src_hbm.at[pl.ds(s, 1)], dst.at[pl.ds(r, 1)], sem)


def _dispatch_body(tok_hbm, off_ref, cnt_ref, x_ref, xs_hbm, tok_smem, zero_ref, sem_idx, sem_row, *, td, tg, ne, n_tiles):
    base = pl.multiple_of(pl.program_id(0) * td, td)
    idx = pltpu.make_async_copy(tok_hbm.at[:, pl.ds(base, td)], tok_smem, sem_idx)
    idx.start()

    @pl.when(pl.program_id(0) == 0)
    def _():
        zero_ref[...] = jnp.zeros_like(zero_ref)
        for e in range(ne):
            start = off_ref[e] + cnt_ref[e]
            stop = off_ref[e] + (cnt_ref[e] + (tg - 1)) // tg * tg

            def zero_start(r, carry):
                _row_copy(zero_ref, 0, xs_hbm, r, sem_row).start()
                return carry

            def zero_wait(r, carry):
                _row_copy(zero_ref, 0, xs_hbm, r, sem_row).wait()
                return carry

            lax.fori_loop(start, stop, zero_start, 0)
            lax.fori_loop(start, stop, zero_wait, 0)

        def tile_copy(ti):
            return pltpu.make_async_copy(zero_ref, xs_hbm.at[pl.ds(pl.multiple_of(ti * tg, tg), tg)], sem_row)

        def tile_start(ti, carry):
            tile_copy(ti).start()
            return carry

        def tile_wait(ti, carry):
            tile_copy(ti).wait()
            return carry

        used = stop // tg
        lax.fori_loop(used, n_tiles, tile_start, 0)
        lax.fori_loop(used, n_tiles, tile_wait, 0)

    idx.wait()

    def copies(t):
        return [_row_copy(x_ref, t, xs_hbm, off_ref[tok_smem[k, t]] + tok_smem[TOP_K + k, t], sem_row) for k in range(TOP_K)]

    def start(t, c):
        for cp in copies(t):
            cp.start()
        return c

    lax.fori_loop(0, td, start, 0, unroll=8)
    for _ in range(TOP_K):
        pltpu.make_async_copy(x_ref, xs_hbm.at[pl.ds(0, td)], sem_row).wait()


def moe_dispatch(tok, off, cnt, x, tg, n_tiles, td=1024):
    n, d = x.shape
    td = min(td, n)
    smem = pl.BlockSpec(memory_space=pltpu.SMEM)
    hbm = pl.BlockSpec(memory_space=pl.ANY)
    return pl.pallas_call(
        functools.partial(_dispatch_body, td=td, tg=tg, ne=cnt.shape[0], n_tiles=n_tiles),
        grid=(n // td,),
        in_specs=[hbm, smem, smem, pl.BlockSpec((td, d), lambda i: (i, 0))],
        out_specs=hbm,
        out_shape=jax.ShapeDtypeStruct((n_tiles * tg, d), x.dtype),
        scratch_shapes=[
            pltpu.SMEM((TOK_ROWS, td), jnp.int32),
            pltpu.VMEM((tg, d), x.dtype),
            pltpu.SemaphoreType.DMA(()),
            pltpu.SemaphoreType.DMA(()),
        ],
        compiler_params=pltpu.CompilerParams(dimension_semantics=("arbitrary",), has_side_effects=True, vmem_limit_bytes=V7X_VMEM_LIMIT_BYTES),
        name="moe_dispatch",
    )(tok, off, cnt, x)


def _grouped_ffn_body(te_ref, nv_ref, x_ref, g_ref, wg_ref, wu_ref, wo_ref, o_ref, h_ref):
    i, j = pl.program_id(0), pl.program_id(1)

    @pl.when(i < nv_ref[0])
    def _():
        @pl.when(j == 0)
        def _():
            h_ref[...] = _rms(x_ref[...], g_ref[...]).astype(BF16)

        y = _swiglu_step(h_ref[...], wg_ref, wu_ref, wo_ref)

        @pl.when(j == 0)
        def _():
            o_ref[...] = y

        @pl.when(j > 0)
        def _():
            o_ref[...] += y

    @pl.when((i >= nv_ref[0]) & (j == 0))
    def _():
        o_ref[...] = jnp.zeros_like(o_ref)


def moe_grouped_ffn(te, nv, xs, g, w_in, w_out, tg, tf=1024):
    r, d = xs.shape
    f = w_out.shape[1]
    tf = _block(f, tf)
    nf = f // tf
    n_tiles = r // tg

    def tile(i, nv):
        return jnp.minimum(i, nv[0] - 1)

    def fblk(i, j, nv):
        return jnp.where(i < nv[0], j, nf - 1)

    grid_spec = pltpu.PrefetchScalarGridSpec(
        num_scalar_prefetch=2,
        grid=(n_tiles, nf),
        in_specs=[
            pl.BlockSpec((tg, d), lambda i, j, te, nv: (tile(i, nv), 0)),
            pl.BlockSpec((1, d), lambda i, j, te, nv: (0, 0)),
            pl.BlockSpec((None, d, tf), lambda i, j, te, nv: (te[i], 0, fblk(i, j, nv))),
            pl.BlockSpec((None, d, tf), lambda i, j, te, nv: (te[i], 0, fblk(i, j, nv) + nf)),
            pl.BlockSpec((None, tf, d), lambda i, j, te, nv: (te[i], fblk(i, j, nv), 0)),
        ],
        out_specs=pl.BlockSpec((tg, d), lambda i, j, te, nv: (i, 0)),
        scratch_shapes=[pltpu.VMEM((tg, d), BF16)],
    )
    return pl.pallas_call(
        _grouped_ffn_body,
        grid_spec=grid_spec,
        out_shape=jax.ShapeDtypeStruct((r, d), F32),
        compiler_params=_params("arbitrary", "arbitrary"),
        name="moe_grouped_ffn",
    )(te, nv, xs, g, w_in, w_in, w_out)


def _combine_body(tok_hbm, off_ref, gate_ref, x_ref, ys_hbm, *rest, tc, final):
    rest = list(rest)
    fg_ref = rest.pop(0) if final else None
    o_ref, tok_smem, buf_ref, sem_idx, sem_row = rest
    base = pl.multiple_of(pl.program_id(0) * tc, tc)
    idx = pltpu.make_async_copy(tok_hbm.at[:, pl.ds(base, tc)], tok_smem, sem_idx)
    idx.start()
    idx.wait()

    def copies(t):
        return [_row_copy(ys_hbm, off_ref[tok_smem[k, t]] + tok_smem[TOP_K + k, t], buf_ref.at[k], t, sem_row) for k in range(TOP_K)]

    def start(t, c):
        for cp in copies(t):
            cp.start()
        return c

    lax.fori_loop(0, tc, start, 0, unroll=8)
    for k in range(TOP_K):
        pltpu.make_async_copy(ys_hbm.at[pl.ds(0, tc)], buf_ref.at[k], sem_row).wait()
    y = x_ref[...] + gate_ref[:, 0:1] * buf_ref[0] + gate_ref[:, 1:2] * buf_ref[1]
    o_ref[...] = _rms(y, fg_ref[...]) if final else y


def moe_combine(tok, off, gates, x, ys, final_g=None, tc=512):
    n, d = x.shape
    tc = min(tc, n)
    final = final_g is not None
    smem = pl.BlockSpec(memory_space=pltpu.SMEM)
    hbm = pl.BlockSpec(memory_space=pl.ANY)
    in_specs = [hbm, smem, pl.BlockSpec((tc, V7X_LANES), lambda i: (i, 0)), pl.BlockSpec((tc, d), lambda i: (i, 0)), hbm]
    args = [tok, off, gates, x, ys]
    if final:
        in_specs.append(pl.BlockSpec((1, d), lambda i: (0, 0)))
        args.append(final_g)
    return pl.pallas_call(
        functools.partial(_combine_body, tc=tc, final=final),
        grid=(n // tc,),
        in_specs=in_specs,
        out_specs=pl.BlockSpec((tc, d), lambda i: (i, 0)),
        out_shape=jax.ShapeDtypeStruct((n, d), F32),
        scratch_shapes=[
            pltpu.SMEM((TOK_ROWS, tc), jnp.int32),
            pltpu.VMEM((TOP_K, tc, d), F32),
            pltpu.SemaphoreType.DMA(()),
            pltpu.SemaphoreType.DMA(()),
        ],
        compiler_params=_params("arbitrary"),
        name="moe_combine",
    )(*args)


def moe_ffn(x, g, w_router, w_in, w_out, final_g=None, tg=1024):
    n, _ = x.shape
    ne = w_router.shape[1]
    tg = min(tg, n)
    n_tiles = TOP_K * n // tg + ne
    gates, tok, counts = moe_router(x, g, w_router)
    cnt = counts[0, :ne].astype(jnp.int32)
    off, te, nv = moe_plan(cnt, tg, n_tiles)
    xs = moe_dispatch(tok, off, cnt, x, tg, n_tiles)
    ys = moe_grouped_ffn(te, nv, xs, g, w_in, w_out, tg)
    return moe_combine(tok, off, gates, x, ys, final_g)


def _gla_gate_body(x_ref, g_ref, wl_ref, w2_ref, b_ref, o_ref):
    h = _rms(x_ref[...], g_ref[...]).astype(BF16)
    g_low = _dot(h, wl_ref[...]).astype(BF16)
    o_ref[...] = _log_sigmoid(_dot(g_low, w2_ref[...]) + b_ref[...]) / B_TAU


def gla_gate(x, g, w_low, w_gate2, bias, tm=1024):
    n, d = x.shape
    dk = w_gate2.shape[1]
    tm = min(tm, n)
    wl = jnp.pad(w_low, ((0, 0), (0, V7X_LANES - B_GATE_RANK))).astype(BF16)
    w2 = jnp.pad(w_gate2, ((0, V7X_LANES - B_GATE_RANK), (0, 0))).astype(BF16)
    return pl.pallas_call(
        _gla_gate_body,
        grid=(n // tm,),
        in_specs=[
            pl.BlockSpec((tm, d), lambda i: (i, 0)),
            pl.BlockSpec((1, d), lambda i: (0, 0)),
            pl.BlockSpec((d, V7X_LANES), lambda i: (0, 0)),
            pl.BlockSpec((V7X_LANES, dk), lambda i: (0, 0)),
            pl.BlockSpec((1, dk), lambda i: (0, 0)),
        ],
        out_specs=pl.BlockSpec((tm, dk), lambda i: (i, 0)),
        out_shape=jax.ShapeDtypeStruct((n, dk), F32),
        compiler_params=_params("parallel"),
        name="gla_gate",
    )(x, g, wl, w2, bias)


def _gla_body(q_ref, k_ref, v_ref, r_ref, la_ref, og_ref, o_ref, st_ref, *, tc, hk, hv):
    @pl.when(pl.program_id(1) == 0)
    def _():
        st_ref[...] = jnp.zeros_like(st_ref)

    c = B_CHUNK
    tril = _tril_mask(c)
    ones_tril = tril.astype(BF16)
    scale = hk**-0.5
    for ci in range(tc // c):
        rs = slice(ci * c, (ci + 1) * c)
        hi, mid, lo = _split3(la_ref[rs, :])
        bcum = _dot(ones_tril, hi) + _dot(ones_tril, mid) + _dot(ones_tril, lo)
        b_last = bcum[c - 1 : c, :]
        e_pos = jnp.exp(bcum)
        e_neg = jnp.exp(-bcum)
        e_end = jnp.exp(b_last - bcum)
        dec = jnp.exp(b_last)
        for h in range(B_HEADS):
            ks = slice(h * hk, (h + 1) * hk)
            vs = slice(h * hv, (h + 1) * hv)
            q = q_ref[rs, ks].astype(F32) * scale
            k = k_ref[rs, ks].astype(F32)
            v = v_ref[rs, vs]
            q_d = (q * e_pos[:, ks]).astype(BF16)
            k_d = (k * e_neg[:, ks]).astype(BF16)
            k_end = (k * e_end[:, ks]).astype(BF16)
            att = jnp.where(tril, _dot_nt(q_d, k_d), 0.0)
            st = st_ref[h]
            o = _dot(att.astype(BF16), v) + _dot_nt(q_d, st.astype(BF16))
            st_ref[h] = dec[:, ks] * st + _dot_tn(v, k_end)
            y = _rms(o, og_ref[...]).astype(BF16).astype(F32)
            r = r_ref[rs, vs].astype(F32)
            o_ref[rs, vs] = (y * (r * jax.nn.sigmoid(r))).astype(BF16)


def gla_chunks(proj, la, o_g, bsz, tc=256):
    n = proj.shape[0]
    dk = la.shape[1]
    dv = (proj.shape[1] - 2 * dk) // 2
    s = n // bsz
    tc = min(tc, s)
    nt = s // tc
    hk, hv = dk // B_HEADS, dv // B_HEADS
    assert (2 * dk) % dv == 0
    v_blk = 2 * dk // dv
    row = lambda b, t: b * nt + t
    return pl.pallas_call(
        functools.partial(_gla_body, tc=tc, hk=hk, hv=hv),
        grid=(bsz, nt),
        in_specs=[
            pl.BlockSpec((tc, dk), lambda b, t: (row(b, t), 0)),
            pl.BlockSpec((tc, dk), lambda b, t: (row(b, t), 1)),
            pl.BlockSpec((tc, dv), lambda b, t: (row(b, t), v_blk)),
            pl.BlockSpec((tc, dv), lambda b, t: (row(b, t), v_blk + 1)),
            pl.BlockSpec((tc, dk), lambda b, t: (row(b, t), 0)),
            pl.BlockSpec((1, hv), lambda b, t: (0, 0)),
        ],
        out_specs=pl.BlockSpec((tc, dv), lambda b, t: (row(b, t), 0)),
        out_shape=jax.ShapeDtypeStruct((n, dv), BF16),
        scratch_shapes=[pltpu.VMEM((B_HEADS, hv, hk), F32)],
        compiler_params=_params("parallel", "arbitrary"),
        name="gla_chunks",
    )(proj, proj, proj, proj, la, o_g)


def _fox_gate_body(x_ref, g_ref, wf_ref, b_ref, crow_ref, carry_ref, *, tb, nh):
    @pl.when(pl.program_id(1) == 0)
    def _():
        carry_ref[...] = jnp.zeros_like(carry_ref)

    h = _rms(x_ref[...], g_ref[...]).astype(BF16)
    lane = lax.broadcasted_iota(jnp.int32, (tb, V7X_LANES), 1)
    log_f = jnp.where(lane < nh, _log_sigmoid(_dot(h, wf_ref[...]) + b_ref[...]), 0.0)
    ones_tril = _tril_mask(tb).astype(BF16)
    hi, mid, lo = _split3(log_f)
    c = carry_ref[...] + (_dot(ones_tril, hi) + _dot(ones_tril, mid) + _dot(ones_tril, lo))
    carry_ref[...] = c[tb - 1 : tb, :]
    sel = (lax.broadcasted_iota(jnp.int32, (nh, V7X_LANES), 0) == lax.broadcasted_iota(jnp.int32, (nh, V7X_LANES), 1)).astype(BF16)
    hi, mid, lo = _split3(c)
    crow_ref[...] = _dot_nt(sel, hi) + _dot_nt(sel, mid) + _dot_nt(sel, lo)


def fox_gate(x, g, w_f, f_bias, bsz, tb=512):
    n, d = x.shape
    nh = w_f.shape[1]
    s = n // bsz
    tb = min(tb, s)
    nt = s // tb
    wf = jnp.pad(w_f, ((0, 0), (0, V7X_LANES - nh))).astype(BF16)
    bias = jnp.pad(f_bias.reshape(1, nh), ((0, 0), (0, V7X_LANES - nh)))
    return pl.pallas_call(
        functools.partial(_fox_gate_body, tb=tb, nh=nh),
        grid=(bsz, nt),
        in_specs=[
            pl.BlockSpec((tb, d), lambda b, t: (b * nt + t, 0)),
            pl.BlockSpec((1, d), lambda b, t: (0, 0)),
            pl.BlockSpec((d, V7X_LANES), lambda b, t: (0, 0)),
            pl.BlockSpec((1, V7X_LANES), lambda b, t: (0, 0)),
        ],
        out_specs=pl.BlockSpec((None, nh, tb), lambda b, t: (b, 0, t)),
        out_shape=jax.ShapeDtypeStruct((bsz, nh, s), F32),
        scratch_shapes=[pltpu.VMEM((1, V7X_LANES), F32)],
        compiler_params=_params("parallel", "arbitrary"),
        name="fox_gate",
    )(x, g, wf, bias)


FOX_KEY_BLOCK = 512
FOX_SUBTILES = 8
FOX_ROW_CHUNK = 32


def _fox_flash_body(q_ref, k_ref, v_ref, crow_ref, o_ref, qs_ref, va_ref, s_ref, p_ref, al_ref, m_ref, acc_ref, *, t, nsub, scale):
    i = pl.program_id(2)
    dh = q_ref.shape[1]
    nl = t // V7X_LANES

    @pl.when(i == 0)
    def _():
        va_ref[:, :dh] = v_ref[...]
        va_ref[:, dh:] = jnp.ones((va_ref.shape[0], V7X_LANES), BF16)

    qs_ref[...] = (q_ref[...].astype(F32) * scale).astype(BF16)
    m_ref[...] = jnp.full_like(m_ref, -jnp.inf)
    acc_ref[...] = jnp.zeros_like(acc_ref)

    def block(sub, kb, masked):
        rows = slice(sub * t, (sub + 1) * t)
        off = pl.multiple_of(kb * t, t)
        s_ref[sub] = _dot_nt(qs_ref[rows, :], k_ref[pl.ds(off, t), :])
        bias = crow_ref[pl.ds(kb, 1), :]

        def chunk(c, carry):
            r = pl.multiple_of(c * FOX_ROW_CHUNK, FOX_ROW_CHUNK)
            sc = s_ref[sub, pl.ds(r, FOX_ROW_CHUNK), :] - bias
            if masked:
                row = r + lax.broadcasted_iota(jnp.int32, sc.shape, 0)
                col = lax.broadcasted_iota(jnp.int32, sc.shape, 1)
                sc = jnp.where(col <= row, sc, -jnp.inf)
            tiles = [sc[:, a * V7X_LANES : (a + 1) * V7X_LANES] for a in range(nl)]
            mx = functools.reduce(jnp.maximum, tiles)
            m_old = m_ref[pl.ds(sub * t + r, FOX_ROW_CHUNK), :]
            m_new = jnp.maximum(m_old, jnp.max(mx, axis=-1, keepdims=True))
            al_ref[sub, pl.ds(r, FOX_ROW_CHUNK), :] = jnp.exp(m_old - m_new)
            m_ref[pl.ds(sub * t + r, FOX_ROW_CHUNK), :] = m_new
            for a in range(nl):
                p_ref[sub, pl.ds(r, FOX_ROW_CHUNK), a * V7X_LANES : (a + 1) * V7X_LANES] = jnp.exp(tiles[a] - m_new).astype(BF16)
            return carry

        lax.fori_loop(0, t // FOX_ROW_CHUNK, chunk, 0, unroll=True)
        pv = _dot(p_ref[sub], va_ref[pl.ds(off, t), :])
        alpha = al_ref[sub]
        acc_ref[rows, :dh] = alpha * acc_ref[rows, :dh] + pv[:, :dh]
        acc_ref[rows, dh:] = alpha * acc_ref[rows, dh:] + pv[:, dh:]

    def below_diagonal(kb, carry):
        for sub in range(nsub):
            block(sub, kb, False)
        return carry

    first = i * nsub
    lax.fori_loop(0, first, below_diagonal, 0)
    for kb in range(nsub):
        for sub in range(kb, nsub):
            block(sub, first + kb, masked=(sub == kb))
    o_ref[...] = (acc_ref[:, :dh] / acc_ref[:, dh:]).astype(o_ref.dtype)


def fox_flash(proj, crow, bsz, nh):
    n = proj.shape[0]
    dh = proj.shape[1] // (3 * nh)
    assert dh == V7X_LANES
    s = n // bsz
    t = min(FOX_KEY_BLOCK, s)
    nsub = min(FOX_SUBTILES, s // t)
    tq = t * nsub
    nq = s // tq
    crow = crow.reshape(bsz, nh, s // t, t)
    return pl.pallas_call(
        functools.partial(_fox_flash_body, t=t, nsub=nsub, scale=dh**-0.5),
        grid=(bsz, nh, nq),
        in_specs=[
            pl.BlockSpec((tq, dh), lambda b, h, i: (b * nq + i, h)),
            pl.BlockSpec((s, dh), lambda b, h, i: (b, nh + h)),
            pl.BlockSpec((s, dh), lambda b, h, i: (b, 2 * nh + h)),
            pl.BlockSpec((None, None, s // t, t), lambda b, h, i: (b, h, 0, 0)),
        ],
        out_specs=pl.BlockSpec((tq, dh), lambda b, h, i: (b * nq + i, h)),
        out_shape=jax.ShapeDtypeStruct((n, nh * dh), BF16),
        scratch_shapes=[
            pltpu.VMEM((tq, dh), BF16),
            pltpu.VMEM((s, dh + V7X_LANES), BF16),
            pltpu.VMEM((nsub, t, t), F32),
            pltpu.VMEM((nsub, t, t), BF16),
            pltpu.VMEM((nsub, t, V7X_LANES), F32),
            pltpu.VMEM((tq, V7X_LANES), F32),
            pltpu.VMEM((tq, dh + V7X_LANES), F32),
        ],
        compiler_params=_params("parallel", "parallel", "arbitrary"),
        name="fox_flash",
    )(proj, proj, proj, crow)


def kernel(x, norm1_g, norm2_g, a_w_in, a_vnorm_g, a_w_s, a_b_s, a_w_out, b_w_in, b_w_gate2, b_gate_bias, b_onorm_g, b_w_out, c_w_in, c_f_bias, c_w_out, ffn_w_in, ffn_w_out, moe_router, moe_w_in, moe_w_out, final_g):
    bsz, s, d = x.shape
    depth = norm1_g.shape[0]
    xf = x.reshape(bsz * s, d)
    for i in range(depth):
        g1 = norm1_g[i].reshape(1, d)
        m, j = i % N_MIXERS, i // N_MIXERS
        if m == 0:
            z = norm_matmul(xf, g1, a_w_in[j].astype(BF16), act="gelu")
            xf = gmlp_spatial_out(z, xf, a_vnorm_g[j].reshape(1, -1), a_w_s[j], a_b_s[j].T, a_w_out[j].astype(BF16))
        elif m == 1:
            dk = b_w_gate2.shape[2]
            n_main = b_w_in.shape[2] - B_GATE_RANK
            proj = norm_matmul(xf, g1, b_w_in[j, :, :n_main].astype(BF16))
            la = gla_gate(xf, g1, b_w_in[j, :, n_main:], b_w_gate2[j], b_gate_bias[j].reshape(1, dk))
            o = gla_chunks(proj, la, b_onorm_g[j].reshape(1, -1), bsz)
            xf = matmul_residual(o, b_w_out[j].astype(BF16), xf)
        else:
            proj = norm_matmul(xf, g1, c_w_in[j, :, : 3 * d].astype(BF16))
            crow = fox_gate(xf, g1, c_w_in[j, :, 3 * d :], c_f_bias[j], bsz)
            o = fox_flash(proj, crow, bsz, C_HEADS)
            xf = matmul_residual(o, c_w_out[j].astype(BF16), xf)
        g2 = norm2_g[i].reshape(1, d)
        fg = final_g.reshape(1, d) if i == depth - 1 else None
        if i % 2 == 0:
            xf = ffn(xf, g2, ffn_w_in[i // 2].astype(BF16), ffn_w_out[i // 2].astype(BF16), final_g=fg)
        else:
            xf = moe_ffn(xf, g2, moe_router[i // 2], moe_w_in[i // 2].astype(BF16), moe_w_out[i // 2].astype(BF16), final_g=fg)
    return xf.reshape(bsz, s, d)
```

```python
import functools

import jax
import jax.numpy as jnp
from jax import lax
from jax.experimental import pallas as pl
from jax.experimental.pallas import tpu as pltpu

F32 = jnp.float32
BF16 = jnp.bfloat16

EPS = 1e-6
N_MIXERS = 3
TOP_K = 2
A_CHUNK = 128
A_GROUPS = 8
B_HEADS = 4
B_GATE_RANK = 16
B_TAU = 16.0
B_CHUNK = 64
C_HEADS = 8

V7X_LANES = 128
V7X_SUBLANES = 8
V7X_VMEM_LIMIT_BYTES = 56 * 1024 * 1024


def _params(*sem):
    return pltpu.CompilerParams(dimension_semantics=sem, vmem_limit_bytes=V7X_VMEM_LIMIT_BYTES)


def _block(total, target, align=V7X_LANES):
    if total <= target:
        return total
    return max(b for b in range(align, target + 1, align) if total % b == 0)


def _rms(x, g):
    ms = jnp.mean(x * x, axis=-1, keepdims=True)
    return x * lax.rsqrt(ms + EPS) * g


def _log_sigmoid(x):
    return jnp.minimum(x, 0.0) - jnp.log1p(jnp.exp(-jnp.abs(x)))


def _split3(x):
    hi = x.astype(BF16)
    r1 = x - hi.astype(F32)
    mid = r1.astype(BF16)
    lo = (r1 - mid.astype(F32)).astype(BF16)
    return hi, mid, lo


def _dot(a, b):
    return jnp.dot(a, b, preferred_element_type=F32)


def _dot_nt(a, b):
    return lax.dot_general(a, b, (((1,), (1,)), ((), ())), preferred_element_type=F32)


def _dot_tn(a, b):
    return lax.dot_general(a, b, (((0,), (0,)), ((), ())), preferred_element_type=F32)


def _tril_mask(n):
    row = lax.broadcasted_iota(jnp.int32, (n, n), 0)
    col = lax.broadcasted_iota(jnp.int32, (n, n), 1)
    return col <= row


def _norm_matmul_body(x_ref, g_ref, w_ref, o_ref, h_ref, *, act):
    @pl.when(pl.program_id(1) == 0)
    def _():
        h_ref[...] = _rms(x_ref[...], g_ref[...]).astype(BF16)

    y = _dot(h_ref[...], w_ref[...])
    if act == "gelu":
        y = jax.nn.gelu(y, approximate=True)
    o_ref[...] = y.astype(o_ref.dtype)


def norm_matmul(x, g, w, act=None, tm=1024, tn=1024):
    n, d = x.shape
    nout = w.shape[1]
    tm, tn = min(tm, n), _block(nout, tn)
    return pl.pallas_call(
        functools.partial(_norm_matmul_body, act=act),
        grid=(n // tm, nout // tn),
        in_specs=[
            pl.BlockSpec((tm, d), lambda i, j: (i, 0)),
            pl.BlockSpec((1, d), lambda i, j: (0, 0)),
            pl.BlockSpec((d, tn), lambda i, j: (0, j)),
        ],
        out_specs=pl.BlockSpec((tm, tn), lambda i, j: (i, j)),
        out_shape=jax.ShapeDtypeStruct((n, nout), BF16),
        scratch_shapes=[pltpu.VMEM((tm, d), BF16)],
        compiler_params=_params("parallel", "arbitrary"),
        name="norm_matmul",
    )(x, g, w)


def _matmul_residual_body(y_ref, w_ref, x_ref, o_ref):
    o_ref[...] = x_ref[...] + _dot(y_ref[...], w_ref[...])


def matmul_residual(y, w, x, tm=512):
    n, k = y.shape
    d = w.shape[1]
    tm = min(tm, n)
    return pl.pallas_call(
        _matmul_residual_body,
        grid=(n // tm,),
        in_specs=[
            pl.BlockSpec((tm, k), lambda i: (i, 0)),
            pl.BlockSpec((k, d), lambda i: (0, 0)),
            pl.BlockSpec((tm, d), lambda i: (i, 0)),
        ],
        out_specs=pl.BlockSpec((tm, d), lambda i: (i, 0)),
        out_shape=jax.ShapeDtypeStruct((n, d), F32),
        compiler_params=_params("parallel"),
        name="matmul_residual",
    )(y, w, x)


def _gmlp_body(z_ref, x_ref, vg_ref, ws_ref, bt_ref, wo_ref, o_ref, vn_ref, y_ref, *, tm, width):
    gd = width // A_GROUPS
    vn_ref[...] = _rms(z_ref[:, width:].astype(F32), vg_ref[...]).astype(BF16)
    tril = _tril_mask(A_CHUNK)
    for g in range(A_GROUPS):
        w = jnp.where(tril, ws_ref[g], 0.0).astype(BF16)
        bias = bt_ref[:, g : g + 1]
        cs = slice(g * gd, (g + 1) * gd)
        for c in range(tm // A_CHUNK):
            rs = slice(c * A_CHUNK, (c + 1) * A_CHUNK)
            mixed = _dot(w, vn_ref[rs, cs]) + bias
            y_ref[rs, cs] = (z_ref[rs, cs].astype(F32) * mixed).astype(BF16)
    o_ref[...] = x_ref[...] + _dot(y_ref[...], wo_ref[...])


def gmlp_spatial_out(z, x, vg, ws, bt, wo, tm=256):
    n, d = x.shape
    width = z.shape[1] // 2
    tm = min(tm, n)
    return pl.pallas_call(
        functools.partial(_gmlp_body, tm=tm, width=width),
        grid=(n // tm,),
        in_specs=[
            pl.BlockSpec((tm, 2 * width), lambda i: (i, 0)),
            pl.BlockSpec((tm, d), lambda i: (i, 0)),
            pl.BlockSpec((1, width), lambda i: (0, 0)),
            pl.BlockSpec((A_GROUPS, A_CHUNK, A_CHUNK), lambda i: (0, 0, 0)),
            pl.BlockSpec((A_CHUNK, A_GROUPS), lambda i: (0, 0)),
            pl.BlockSpec((width, d), lambda i: (0, 0)),
        ],
        out_specs=pl.BlockSpec((tm, d), lambda i: (i, 0)),
        out_shape=jax.ShapeDtypeStruct((n, d), F32),
        scratch_shapes=[pltpu.VMEM((tm, width), BF16), pltpu.VMEM((tm, width), BF16)],
        compiler_params=_params("parallel"),
        name="gmlp_spatial_out",
    )(z, x, vg, ws, bt, wo)


def _swiglu_step(h, wg_ref, wu_ref, wo_ref):
    a = _dot(h, wg_ref[...])
    t = (a * jax.nn.sigmoid(a)) * _dot(h, wu_ref[...])
    return _dot(t.astype(BF16), wo_ref[...])


def _ffn_body(*refs, final):
    x_ref, g_ref, wg_ref, wu_ref, wo_ref = refs[:5]
    rest = list(refs[5:])
    fg_ref = rest.pop(0) if final else None
    o_ref, h_ref, acc_ref = rest
    j = pl.program_id(1)

    @pl.when(j == 0)
    def _():
        h_ref[...] = _rms(x_ref[...], g_ref[...]).astype(BF16)
        acc_ref[...] = jnp.zeros_like(acc_ref)

    acc_ref[...] += _swiglu_step(h_ref[...], wg_ref, wu_ref, wo_ref)

    @pl.when(j == pl.num_programs(1) - 1)
    def _():
        y = x_ref[...] + acc_ref[...]
        o_ref[...] = _rms(y, fg_ref[...]) if final else y


def ffn(x, g, w_in, w_out, final_g=None, tm=1024, tf=512):
    n, d = x.shape
    f = w_out.shape[0]
    tm, tf = min(tm, n), min(tf, f)
    nf = f // tf
    final = final_g is not None
    in_specs = [
        pl.BlockSpec((tm, d), lambda i, j: (i, 0)),
        pl.BlockSpec((1, d), lambda i, j: (0, 0)),
        pl.BlockSpec((d, tf), lambda i, j: (0, j)),
        pl.BlockSpec((d, tf), lambda i, j: (0, j + nf)),
        pl.BlockSpec((tf, d), lambda i, j: (j, 0)),
    ]
    args = [x, g, w_in, w_in, w_out]
    if final:
        in_specs.append(pl.BlockSpec((1, d), lambda i, j: (0, 0)))
        args.append(final_g)
    return pl.pallas_call(
        functools.partial(_ffn_body, final=final),
        grid=(n // tm, nf),
        in_specs=in_specs,
        out_specs=pl.BlockSpec((tm, d), lambda i, j: (i, 0)),
        out_shape=jax.ShapeDtypeStruct((n, d), F32),
        scratch_shapes=[pltpu.VMEM((tm, d), BF16), pltpu.VMEM((tm, d), F32)],
        compiler_params=_params("parallel", "arbitrary"),
        name="ffn",
    )(*args)


TOK_ROWS = 8


def _router_body(x_ref, g_ref, whi_ref, wlo_ref, gate_ref, tok_ref, cnt_ref, carry_ref, *, ne, tm):
    @pl.when(pl.program_id(0) == 0)
    def _():
        carry_ref[...] = jnp.zeros_like(carry_ref)

    h = _rms(x_ref[...], g_ref[...])
    h_hi = h.astype(BF16)
    h_lo = (h - h_hi.astype(F32)).astype(BF16)
    logits = _dot(h_hi, whi_ref[...]) + _dot(h_lo, whi_ref[...]) + _dot(h_hi, wlo_ref[...])
    lane = lax.broadcasted_iota(jnp.int32, logits.shape, 1).astype(F32)
    neg = -jnp.inf
    l1 = jnp.where(lane < ne, logits, neg)
    m1 = jnp.max(l1, axis=-1, keepdims=True)
    i1 = jnp.min(jnp.where(l1 == m1, lane, float(V7X_LANES)), axis=-1, keepdims=True)
    l2 = jnp.where(lane == i1, neg, l1)
    m2 = jnp.max(l2, axis=-1, keepdims=True)
    i2 = jnp.min(jnp.where(l2 == m2, lane, float(V7X_LANES)), axis=-1, keepdims=True)
    e2 = jnp.exp(m2 - m1)
    den = 1.0 + e2
    gate_ref[...] = jnp.where(lane == 0.0, 1.0 / den, 0.0) + jnp.where(lane == 1.0, e2 / den, 0.0)

    sel1, sel2 = lane == i1, lane == i2
    onehot = jnp.where(sel1 | sel2, 1.0, 0.0)
    row = lax.broadcasted_iota(jnp.int32, (tm, tm), 0)
    col = lax.broadcasted_iota(jnp.int32, (tm, tm), 1)
    before = carry_ref[...] + _dot((col < row).astype(BF16), onehot.astype(BF16))
    carry_ref[...] = before[tm - 1 : tm, :] + onehot[tm - 1 : tm, :]
    cnt_ref[...] = carry_ref[...]
    r1 = jnp.sum(jnp.where(sel1, before, 0.0), axis=-1, keepdims=True)
    r2 = jnp.sum(jnp.where(sel2, before, 0.0), axis=-1, keepdims=True)
    table = (jnp.where(lane == 0.0, i1, 0.0) + jnp.where(lane == 1.0, i2, 0.0)
             + jnp.where(lane == 2.0, r1, 0.0) + jnp.where(lane == 3.0, r2, 0.0))
    pick = (lax.broadcasted_iota(jnp.int32, (TOK_ROWS, V7X_LANES), 0) == lax.broadcasted_iota(jnp.int32, (TOK_ROWS, V7X_LANES), 1)).astype(BF16)
    hi, mid, lo = _split3(table)
    tok_ref[...] = (_dot_nt(pick, hi) + _dot_nt(pick, mid) + _dot_nt(pick, lo)).astype(jnp.int32)


def moe_router(x, g, w_router, tm=1024):
    n, d = x.shape
    ne = w_router.shape[1]
    tm = min(tm, n)
    wpad = jnp.pad(w_router, ((0, 0), (0, V7X_LANES - ne)))
    w_hi = wpad.astype(BF16)
    w_lo = (wpad - w_hi.astype(F32)).astype(BF16)
    return pl.pallas_call(
        functools.partial(_router_body, ne=ne, tm=tm),
        grid=(n // tm,),
        in_specs=[
            pl.BlockSpec((tm, d), lambda i: (i, 0)),
            pl.BlockSpec((1, d), lambda i: (0, 0)),
            pl.BlockSpec((d, V7X_LANES), lambda i: (0, 0)),
            pl.BlockSpec((d, V7X_LANES), lambda i: (0, 0)),
        ],
        out_specs=[
            pl.BlockSpec((tm, V7X_LANES), lambda i: (i, 0)),
            pl.BlockSpec((TOK_ROWS, tm), lambda i: (0, i)),
            pl.BlockSpec((1, V7X_LANES), lambda i: (0, 0)),
        ],
        out_shape=[
            jax.ShapeDtypeStruct((n, V7X_LANES), F32),
            jax.ShapeDtypeStruct((TOK_ROWS, n), jnp.int32),
            jax.ShapeDtypeStruct((1, V7X_LANES), F32),
        ],
        scratch_shapes=[pltpu.VMEM((1, V7X_LANES), F32)],
        compiler_params=_params("arbitrary"),
        name="moe_router",
    )(x, g, w_hi, w_lo)


def _plan_body(cnt_ref, tok_ref, off_ref, te_ref, nv_ref, pos_ref, *, ne, tg, n_tiles):
    off = jnp.int32(0)
    tile = jnp.int32(0)
    expert = tok_ref[0:TOP_K, :]
    group_start = jnp.zeros_like(expert)
    for e in range(ne):
        nt = (cnt_ref[e] + (tg - 1)) // tg
        off_ref[e] = off
        group_start = jnp.where(expert == e, off, group_start)

        def mark(ti, carry, e=e):
            te_ref[ti] = jnp.int32(e)
            return carry

        lax.fori_loop(tile, tile + nt, mark, 0)
        off = off + nt * tg
        tile = tile + nt
    pos_ref[...] = jnp.zeros_like(pos_ref)
    pos_ref[0:TOP_K, :] = group_start + tok_ref[TOP_K : 2 * TOP_K, :]
    nv_ref[0] = tile
    last = te_ref[jnp.maximum(tile - 1, 0)]

    def fill(ti, carry):
        te_ref[ti] = last
        return carry

    lax.fori_loop(tile, n_tiles, fill, 0)


def moe_plan(cnt, tok, tg, n_tiles):
    ne = cnt.shape[0]
    smem = pl.BlockSpec(memory_space=pltpu.SMEM)
    vmem = pl.BlockSpec(memory_space=pltpu.VMEM)
    return pl.pallas_call(
        functools.partial(_plan_body, ne=ne, tg=tg, n_tiles=n_tiles),
        in_specs=[smem, vmem],
        out_specs=[smem, smem, smem, vmem],
        out_shape=[
            jax.ShapeDtypeStruct((ne,), jnp.int32),
            jax.ShapeDtypeStruct((n_tiles,), jnp.int32),
            jax.ShapeDtypeStruct((1,), jnp.int32),
            jax.ShapeDtypeStruct(tok.shape, jnp.int32),
        ],
        name="moe_plan",
    )(cnt, tok)


def _token_copy(src, s, dst, r, sem, ns):
    return pltpu.make_async_copy(src.at[pl.ds(pl.multiple_of(s * ns, ns), ns)], dst.at[pl.ds(pl.multiple_of(r * ns, ns), ns)], sem)


def _to_slabs(dst_ref, val, ns):
    rows = val.shape[0]
    for c in range(ns):
        dst_ref[pl.ds(c, rows, stride=ns), :] = val[:, c * V7X_LANES : (c + 1) * V7X_LANES]


def _slab(src_ref, c, rows, ns):
    return src_ref[pl.ds(c, rows, stride=ns), :]


def _dispatch_body(pos_hbm, off_ref, cnt_ref, x_ref, xs_hbm, pos_smem, x3_ref, zero_ref, sem_idx, sem_row, *, td, tg, ne, n_tiles, ns):
    base = pl.multiple_of(pl.program_id(0) * td, td)
    idx = pltpu.make_async_copy(pos_hbm.at[:, pl.ds(base, td)], pos_smem, sem_idx)
    idx.start()
    _to_slabs(x3_ref, x_ref[...], ns)

    @pl.when(pl.program_id(0) == 0)
    def _():
        zero_ref[...] = jnp.zeros_like(zero_ref)
        for e in range(ne):
            start = off_ref[e] + cnt_ref[e]
            stop = off_ref[e] + (cnt_ref[e] + (tg - 1)) // tg * tg

            def zero_start(r, carry):
                _token_copy(zero_ref, 0, xs_hbm, r, sem_row, ns).start()
                return carry

            def zero_wait(r, carry):
                _token_copy(zero_ref, 0, xs_hbm, r, sem_row, ns).wait()
                return carry

            lax.fori_loop(start, stop, zero_start, 0)
            lax.fori_loop(start, stop, zero_wait, 0)

        def tile_copy(ti):
            return pltpu.make_async_copy(zero_ref, xs_hbm.at[pl.ds(pl.multiple_of(ti * (tg * ns), tg * ns), tg * ns)], sem_row)

        def tile_start(ti, carry):
            tile_copy(ti).start()
            return carry

        def tile_wait(ti, carry):
            tile_copy(ti).wait()
            return carry

        used = stop // tg
        lax.fori_loop(used, n_tiles, tile_start, 0)
        lax.fori_loop(used, n_tiles, tile_wait, 0)

    idx.wait()

    def start(t, c):
        for k in range(TOP_K):
            _token_copy(x3_ref, t, xs_hbm, pos_smem[k, t], sem_row, ns).start()
        return c

    lax.fori_loop(0, td, start, 0, unroll=8)
    for _ in range(TOP_K):
        pltpu.make_async_copy(x3_ref, xs_hbm.at[pl.ds(0, td * ns)], sem_row).wait()


def moe_dispatch(pos, off, cnt, x, tg, n_tiles, td=1024):
    n, d = x.shape
    ns = d // V7X_LANES
    assert ns % V7X_SUBLANES == 0
    td = min(td, n)
    smem = pl.BlockSpec(memory_space=pltpu.SMEM)
    hbm = pl.BlockSpec(memory_space=pl.ANY)
    return pl.pallas_call(
        functools.partial(_dispatch_body, td=td, tg=tg, ne=cnt.shape[0], n_tiles=n_tiles, ns=ns),
        grid=(n // td,),
        in_specs=[hbm, smem, smem, pl.BlockSpec((td, d), lambda i: (i, 0))],
        out_specs=hbm,
        out_shape=jax.ShapeDtypeStruct((n_tiles * tg * ns, V7X_LANES), x.dtype),
        scratch_shapes=[
            pltpu.SMEM((TOK_ROWS, td), jnp.int32),
            pltpu.VMEM((td * ns, V7X_LANES), x.dtype),
            pltpu.VMEM((tg * ns, V7X_LANES), x.dtype),
            pltpu.SemaphoreType.DMA(()),
            pltpu.SemaphoreType.DMA(()),
        ],
        compiler_params=pltpu.CompilerParams(dimension_semantics=("arbitrary",), has_side_effects=True, vmem_limit_bytes=V7X_VMEM_LIMIT_BYTES),
        name="moe_dispatch",
    )(pos, off, cnt, x)


def _grouped_ffn_body(te_ref, nv_ref, x_ref, g_ref, wg_ref, wu_ref, wo_ref, o_ref, h_ref, acc_ref, *, tg, ns):
    i, j = pl.program_id(0), pl.program_id(1)
    d = ns * V7X_LANES

    @pl.when(i < nv_ref[0])
    def _():
        @pl.when(j == 0)
        def _():
            ss = jnp.zeros((tg, 1), F32)
            for c in range(ns):
                xc = _slab(x_ref, c, tg, ns)
                ss = ss + jnp.sum(xc * xc, axis=-1, keepdims=True)
            r = lax.rsqrt(ss / d + EPS)
            for c in range(ns):
                cols = slice(c * V7X_LANES, (c + 1) * V7X_LANES)
                h_ref[:, cols] = (_slab(x_ref, c, tg, ns) * r * g_ref[:, cols]).astype(BF16)

        y = _swiglu_step(h_ref[...], wg_ref, wu_ref, wo_ref)

        @pl.when(j == 0)
        def _():
            acc_ref[...] = y

        @pl.when(j > 0)
        def _():
            acc_ref[...] += y

        @pl.when(j == pl.num_programs(1) - 1)
        def _():
            _to_slabs(o_ref, acc_ref[...], ns)

    @pl.when((i >= nv_ref[0]) & (j == 0))
    def _():
        o_ref[...] = jnp.zeros_like(o_ref)


def moe_grouped_ffn(te, nv, xs, g, w_in, w_out, tg, tf=512):
    d = w_in.shape[1]
    ns = d // V7X_LANES
    r = xs.shape[0] // ns
    f = w_out.shape[1]
    tf = _block(f, tf)
    nf = f // tf
    n_tiles = r // tg

    def tile(i, nv):
        return jnp.minimum(i, nv[0] - 1)

    def fblk(i, j, nv):
        return jnp.where(i < nv[0], j, nf - 1)

    grid_spec = pltpu.PrefetchScalarGridSpec(
        num_scalar_prefetch=2,
        grid=(n_tiles, nf),
        in_specs=[
            pl.BlockSpec((tg * ns, V7X_LANES), lambda i, j, te, nv: (tile(i, nv), 0)),
            pl.BlockSpec((1, d), lambda i, j, te, nv: (0, 0)),
            pl.BlockSpec((None, d, tf), lambda i, j, te, nv: (te[i], 0, fblk(i, j, nv))),
            pl.BlockSpec((None, d, tf), lambda i, j, te, nv: (te[i], 0, fblk(i, j, nv) + nf)),
            pl.BlockSpec((None, tf, d), lambda i, j, te, nv: (te[i], fblk(i, j, nv), 0)),
        ],
        out_specs=pl.BlockSpec((tg * ns, V7X_LANES), lambda i, j, te, nv: (i, 0)),
        scratch_shapes=[pltpu.VMEM((tg, d), BF16), pltpu.VMEM((tg, d), F32)],
    )
    return pl.pallas_call(
        functools.partial(_grouped_ffn_body, tg=tg, ns=ns),
        grid_spec=grid_spec,
        out_shape=jax.ShapeDtypeStruct((r * ns, V7X_LANES), F32),
        compiler_params=_params("arbitrary", "arbitrary"),
        name="moe_grouped_ffn",
    )(te, nv, xs, g, w_in, w_in, w_out)


def _combine_body(pos_hbm, gate_ref, x_ref, ys_hbm, *rest, tc, ns, final):
    rest = list(rest)
    fg_ref = rest.pop(0) if final else None
    o_ref, pos_smem, buf_ref, sem_idx, sem_row = rest
    base = pl.multiple_of(pl.program_id(0) * tc, tc)
    idx = pltpu.make_async_copy(pos_hbm.at[:, pl.ds(base, tc)], pos_smem, sem_idx)
    idx.start()
    idx.wait()

    def start(t, c):
        for k in range(TOP_K):
            _token_copy(ys_hbm, pos_smem[k, t], buf_ref.at[k], t, sem_row, ns).start()
        return c

    lax.fori_loop(0, tc, start, 0, unroll=8)
    for k in range(TOP_K):
        pltpu.make_async_copy(ys_hbm.at[pl.ds(0, tc * ns)], buf_ref.at[k], sem_row).wait()
    g0, g1 = gate_ref[:, 0:1], gate_ref[:, 1:2]
    for c in range(ns):
        cols = slice(c * V7X_LANES, (c + 1) * V7X_LANES)
        o_ref[:, cols] = x_ref[:, cols] + g0 * _slab(buf_ref.at[0], c, tc, ns) + g1 * _slab(buf_ref.at[1], c, tc, ns)
    if final:
        o_ref[...] = _rms(o_ref[...], fg_ref[...])


def moe_combine(pos, gates, x, ys, final_g=None, tc=512):
    n, d = x.shape
    ns = d // V7X_LANES
    tc = min(tc, n)
    final = final_g is not None
    hbm = pl.BlockSpec(memory_space=pl.ANY)
    in_specs = [hbm, pl.BlockSpec((tc, V7X_LANES), lambda i: (i, 0)), pl.BlockSpec((tc, d), lambda i: (i, 0)), hbm]
    args = [pos, gates, x, ys]
    if final:
        in_specs.append(pl.BlockSpec((1, d), lambda i: (0, 0)))
        args.append(final_g)
    return pl.pallas_call(
        functools.partial(_combine_body, tc=tc, ns=ns, final=final),
        grid=(n // tc,),
        in_specs=in_specs,
        out_specs=pl.BlockSpec((tc, d), lambda i: (i, 0)),
        out_shape=jax.ShapeDtypeStruct((n, d), F32),
        scratch_shapes=[
            pltpu.SMEM((TOK_ROWS, tc), jnp.int32),
            pltpu.VMEM((TOP_K, tc * ns, V7X_LANES), F32),
            pltpu.SemaphoreType.DMA(()),
            pltpu.SemaphoreType.DMA(()),
        ],
        compiler_params=_params("arbitrary"),
        name="moe_combine",
    )(*args)


def moe_ffn(x, g, w_router, w_in, w_out, final_g=None, tg=1024):
    n, _ = x.shape
    ne = w_router.shape[1]
    tg = min(tg, n)
    n_tiles = TOP_K * n // tg + ne
    gates, tok, counts = moe_router(x, g, w_router)
    cnt = counts[0, :ne].astype(jnp.int32)
    off, te, nv, pos = moe_plan(cnt, tok, tg, n_tiles)
    xs = moe_dispatch(pos, off, cnt, x, tg, n_tiles)
    ys = moe_grouped_ffn(te, nv, xs, g, w_in, w_out, tg)
    return moe_combine(pos, gates, x, ys, final_g)


def _gla_gate_body(x_ref, g_ref, wl_ref, w2_ref, b_ref, o_ref):
    h = _rms(x_ref[...], g_ref[...]).astype(BF16)
    g_low = _dot(h, wl_ref[...]).astype(BF16)
    o_ref[...] = _log_sigmoid(_dot(g_low, w2_ref[...]) + b_ref[...]) / B_TAU


def gla_gate(x, g, w_low, w_gate2, bias, tm=1024):
    n, d = x.shape
    dk = w_gate2.shape[1]
    tm = min(tm, n)
    wl = jnp.pad(w_low, ((0, 0), (0, V7X_LANES - B_GATE_RANK))).astype(BF16)
    w2 = jnp.pad(w_gate2, ((0, V7X_LANES - B_GATE_RANK), (0, 0))).astype(BF16)
    return pl.pallas_call(
        _gla_gate_body,
        grid=(n // tm,),
        in_specs=[
            pl.BlockSpec((tm, d), lambda i: (i, 0)),
            pl.BlockSpec((1, d), lambda i: (0, 0)),
            pl.BlockSpec((d, V7X_LANES), lambda i: (0, 0)),
            pl.BlockSpec((V7X_LANES, dk), lambda i: (0, 0)),
            pl.BlockSpec((1, dk), lambda i: (0, 0)),
        ],
        out_specs=pl.BlockSpec((tm, dk), lambda i: (i, 0)),
        out_shape=jax.ShapeDtypeStruct((n, dk), F32),
        compiler_params=_params("parallel"),
        name="gla_gate",
    )(x, g, wl, w2, bias)


def _gla_body(q_ref, k_ref, v_ref, r_ref, la_ref, og_ref, o_ref, st_ref, *, tc, hk, hv):
    @pl.when(pl.program_id(1) == 0)
    def _():
        st_ref[...] = jnp.zeros_like(st_ref)

    c = B_CHUNK
    tril = _tril_mask(c)
    ones_tril = tril.astype(BF16)
    scale = hk**-0.5
    for ci in range(tc // c):
        rs = slice(ci * c, (ci + 1) * c)
        hi, mid, lo = _split3(la_ref[rs, :])
        bcum = _dot(ones_tril, hi) + _dot(ones_tril, mid) + _dot(ones_tril, lo)
        b_last = bcum[c - 1 : c, :]
        e_pos = jnp.exp(bcum)
        e_neg = jnp.exp(-bcum)
        e_end = jnp.exp(b_last - bcum)
        dec = jnp.exp(b_last)
        for h in range(B_HEADS):
            ks = slice(h * hk, (h + 1) * hk)
            vs = slice(h * hv, (h + 1) * hv)
            q = q_ref[rs, ks].astype(F32) * scale
            k = k_ref[rs, ks].astype(F32)
            v = v_ref[rs, vs]
            q_d = (q * e_pos[:, ks]).astype(BF16)
            k_d = (k * e_neg[:, ks]).astype(BF16)
            k_end = (k * e_end[:, ks]).astype(BF16)
            att = jnp.where(tril, _dot_nt(q_d, k_d), 0.0)
            st = st_ref[h]
            o = _dot(att.astype(BF16), v) + _dot_nt(q_d, st.astype(BF16))
            st_ref[h] = dec[:, ks] * st + _dot_tn(v, k_end)
            y = _rms(o, og_ref[...]).astype(BF16).astype(F32)
            r = r_ref[rs, vs].astype(F32)
            o_ref[rs, vs] = (y * (r * jax.nn.sigmoid(r))).astype(BF16)


def gla_chunks(proj, la, o_g, bsz, tc=256):
    n = proj.shape[0]
    dk = la.shape[1]
    dv = (proj.shape[1] - 2 * dk) // 2
    s = n // bsz
    tc = min(tc, s)
    nt = s // tc
    hk, hv = dk // B_HEADS, dv // B_HEADS
    assert (2 * dk) % dv == 0
    v_blk = 2 * dk // dv
    row = lambda b, t: b * nt + t
    return pl.pallas_call(
        functools.partial(_gla_body, tc=tc, hk=hk, hv=hv),
        grid=(bsz, nt),
        in_specs=[
            pl.BlockSpec((tc, dk), lambda b, t: (row(b, t), 0)),
            pl.BlockSpec((tc, dk), lambda b, t: (row(b, t), 1)),
            pl.BlockSpec((tc, dv), lambda b, t: (row(b, t), v_blk)),
            pl.BlockSpec((tc, dv), lambda b, t: (row(b, t), v_blk + 1)),
            pl.BlockSpec((tc, dk), lambda b, t: (row(b, t), 0)),
            pl.BlockSpec((1, hv), lambda b, t: (0, 0)),
        ],
        out_specs=pl.BlockSpec((tc, dv), lambda b, t: (row(b, t), 0)),
        out_shape=jax.ShapeDtypeStruct((n, dv), BF16),
        scratch_shapes=[pltpu.VMEM((B_HEADS, hv, hk), F32)],
        compiler_params=_params("parallel", "arbitrary"),
        name="gla_chunks",
    )(proj, proj, proj, proj, la, o_g)


def _fox_gate_body(x_ref, g_ref, wf_ref, b_ref, crow_ref, carry_ref, *, tb, nh):
    @pl.when(pl.program_id(1) == 0)
    def _():
        carry_ref[...] = jnp.zeros_like(carry_ref)

    h = _rms(x_ref[...], g_ref[...]).astype(BF16)
    lane = lax.broadcasted_iota(jnp.int32, (tb, V7X_LANES), 1)
    log_f = jnp.where(lane < nh, _log_sigmoid(_dot(h, wf_ref[...]) + b_ref[...]), 0.0)
    ones_tril = _tril_mask(tb).astype(BF16)
    hi, mid, lo = _split3(log_f)
    c = carry_ref[...] + (_dot(ones_tril, hi) + _dot(ones_tril, mid) + _dot(ones_tril, lo))
    carry_ref[...] = c[tb - 1 : tb, :]
    sel = (lax.broadcasted_iota(jnp.int32, (nh, V7X_LANES), 0) == lax.broadcasted_iota(jnp.int32, (nh, V7X_LANES), 1)).astype(BF16)
    hi, mid, lo = _split3(c)
    crow_ref[...] = _dot_nt(sel, hi) + _dot_nt(sel, mid) + _dot_nt(sel, lo)


def fox_gate(x, g, w_f, f_bias, bsz, tb=512):
    n, d = x.shape
    nh = w_f.shape[1]
    s = n // bsz
    tb = min(tb, s)
    nt = s // tb
    wf = jnp.pad(w_f, ((0, 0), (0, V7X_LANES - nh))).astype(BF16)
    bias = jnp.pad(f_bias.reshape(1, nh), ((0, 0), (0, V7X_LANES - nh)))
    return pl.pallas_call(
        functools.partial(_fox_gate_body, tb=tb, nh=nh),
        grid=(bsz, nt),
        in_specs=[
            pl.BlockSpec((tb, d), lambda b, t: (b * nt + t, 0)),
            pl.BlockSpec((1, d), lambda b, t: (0, 0)),
            pl.BlockSpec((d, V7X_LANES), lambda b, t: (0, 0)),
            pl.BlockSpec((1, V7X_LANES), lambda b, t: (0, 0)),
        ],
        out_specs=pl.BlockSpec((None, nh, tb), lambda b, t: (b, 0, t)),
        out_shape=jax.ShapeDtypeStruct((bsz, nh, s), F32),
        scratch_shapes=[pltpu.VMEM((1, V7X_LANES), F32)],
        compiler_params=_params("parallel", "arbitrary"),
        name="fox_gate",
    )(x, g, wf, bias)


FOX_KEY_BLOCK = 512
FOX_SUBTILES = 8
FOX_ROW_CHUNK = 32


def _fox_flash_body(q_ref, k_ref, v_ref, crow_ref, o_ref, qs_ref, va_ref, s_ref, p_ref, al_ref, m_ref, acc_ref, *, t, nsub, scale):
    i = pl.program_id(2)
    dh = q_ref.shape[1]
    nl = t // V7X_LANES

    @pl.when(i == 0)
    def _():
        va_ref[:, :dh] = v_ref[...]
        va_ref[:, dh:] = jnp.ones((va_ref.shape[0], V7X_LANES), BF16)

    qs_ref[...] = (q_ref[...].astype(F32) * scale).astype(BF16)
    m_ref[...] = jnp.full_like(m_ref, -jnp.inf)
    acc_ref[...] = jnp.zeros_like(acc_ref)

    def block(sub, kb, masked):
        rows = slice(sub * t, (sub + 1) * t)
        off = pl.multiple_of(kb * t, t)
        s_ref[sub] = _dot_nt(qs_ref[rows, :], k_ref[pl.ds(off, t), :])
        bias = crow_ref[pl.ds(kb, 1), :]

        def chunk(c, carry):
            r = pl.multiple_of(c * FOX_ROW_CHUNK, FOX_ROW_CHUNK)
            sc = s_ref[sub, pl.ds(r, FOX_ROW_CHUNK), :] - bias
            if masked:
                row = r + lax.broadcasted_iota(jnp.int32, sc.shape, 0)
                col = lax.broadcasted_iota(jnp.int32, sc.shape, 1)
                sc = jnp.where(col <= row, sc, -jnp.inf)
            tiles = [sc[:, a * V7X_LANES : (a + 1) * V7X_LANES] for a in range(nl)]
            mx = functools.reduce(jnp.maximum, tiles)
            m_old = m_ref[pl.ds(sub * t + r, FOX_ROW_CHUNK), :]
            m_new = jnp.maximum(m_old, jnp.max(mx, axis=-1, keepdims=True))
            al_ref[sub, pl.ds(r, FOX_ROW_CHUNK), :] = jnp.exp(m_old - m_new)
            m_ref[pl.ds(sub * t + r, FOX_ROW_CHUNK), :] = m_new
            for a in range(nl):
                p_ref[sub, pl.ds(r, FOX_ROW_CHUNK), a * V7X_LANES : (a + 1) * V7X_LANES] = jnp.exp(tiles[a] - m_new).astype(BF16)
            return carry

        lax.fori_loop(0, t // FOX_ROW_CHUNK, chunk, 0, unroll=True)
        pv = _dot(p_ref[sub], va_ref[pl.ds(off, t), :])
        alpha = al_ref[sub]
        acc_ref[rows, :dh] = alpha * acc_ref[rows, :dh] + pv[:, :dh]
        acc_ref[rows, dh:] = alpha * acc_ref[rows, dh:] + pv[:, dh:]

    def below_diagonal(kb, carry):
        for sub in range(nsub):
            block(sub, kb, False)
        return carry

    first = i * nsub
    lax.fori_loop(0, first, below_diagonal, 0)
    for kb in range(nsub):
        for sub in range(kb, nsub):
            block(sub, first + kb, masked=(sub == kb))
    o_ref[...] = (acc_ref[:, :dh] / acc_ref[:, dh:]).astype(o_ref.dtype)


def fox_flash(proj, crow, bsz, nh):
    n = proj.shape[0]
    dh = proj.shape[1] // (3 * nh)
    assert dh == V7X_LANES
    s = n // bsz
    t = min(FOX_KEY_BLOCK, s)
    nsub = min(FOX_SUBTILES, s // t)
    tq = t * nsub
    nq = s // tq
    crow = crow.reshape(bsz, nh, s // t, t)
    return pl.pallas_call(
        functools.partial(_fox_flash_body, t=t, nsub=nsub, scale=dh**-0.5),
        grid=(bsz, nh, nq),
        in_specs=[
            pl.BlockSpec((tq, dh), lambda b, h, i: (b * nq + i, h)),
            pl.BlockSpec((s, dh), lambda b, h, i: (b, nh + h)),
            pl.BlockSpec((s, dh), lambda b, h, i: (b, 2 * nh + h)),
            pl.BlockSpec((None, None, s // t, t), lambda b, h, i: (b, h, 0, 0)),
        ],
        out_specs=pl.BlockSpec((tq, dh), lambda b, h, i: (b * nq + i, h)),
        out_shape=jax.ShapeDtypeStruct((n, nh * dh), BF16),
        scratch_shapes=[
            pltpu.VMEM((tq, dh), BF16),
            pltpu.VMEM((s, dh + V7X_LANES), BF16),
            pltpu.VMEM((nsub, t, t), F32),
            pltpu.VMEM((nsub, t, t), BF16),
            pltpu.VMEM((nsub, t, V7X_LANES), F32),
            pltpu.VMEM((tq, V7X_LANES), F32),
            pltpu.VMEM((tq, dh + V7X_LANES), F32),
        ],
        compiler_params=_params("parallel", "parallel", "arbitrary"),
        name="fox_flash",
    )(proj, proj, proj, crow)


def kernel(x, norm1_g, norm2_g, a_w_in, a_vnorm_g, a_w_s, a_b_s, a_w_out, b_w_in, b_w_gate2, b_gate_bias, b_onorm_g, b_w_out, c_w_in, c_f_bias, c_w_out, ffn_w_in, ffn_w_out, moe_router, moe_w_in, moe_w_out, final_g):
    bsz, s, d = x.shape
    depth = norm1_g.shape[0]
    xf = x.reshape(bsz * s, d)
    for i in range(depth):
        g1 = norm1_g[i].reshape(1, d)
        m, j = i % N_MIXERS, i // N_MIXERS
        if m == 0:
            z = norm_matmul(xf, g1, a_w_in[j].astype(BF16), act="gelu")
            xf = gmlp_spatial_out(z, xf, a_vnorm_g[j].reshape(1, -1), a_w_s[j], a_b_s[j].T, a_w_out[j].astype(BF16))
        elif m == 1:
            dk = b_w_gate2.shape[2]
            n_main = b_w_in.shape[2] - B_GATE_RANK
            proj = norm_matmul(xf, g1, b_w_in[j, :, :n_main].astype(BF16))
            la = gla_gate(xf, g1, b_w_in[j, :, n_main:], b_w_gate2[j], b_gate_bias[j].reshape(1, dk))
            o = gla_chunks(proj, la, b_onorm_g[j].reshape(1, -1), bsz)
            xf = matmul_residual(o, b_w_out[j].astype(BF16), xf)
        else:
            proj = norm_matmul(xf, g1, c_w_in[j, :, : 3 * d].astype(BF16))
            crow = fox_gate(xf, g1, c_w_in[j, :, 3 * d :], c_f_bias[j], bsz)
            o = fox_flash(proj, crow, bsz, C_HEADS)
            xf = matmul_residual(o, c_w_out[j].astype(BF16), xf)
        g2 = norm2_g[i].reshape(1, d)
        fg = final_g.reshape(1, d) if i == depth - 1 else None
        if i % 2 == 0:
            xf = ffn(xf, g2, ffn_w_in[i // 2].astype(BF16), ffn_w_out[i // 2].astype(BF16), final_g=fg)
        else:
            xf = moe_ffn(xf, g2, moe_router[i // 2], moe_w_in[i // 2].astype(BF16), moe_w_out[i // 2].astype(BF16), final_g=fg)
    return xf.reshape(bsz, s, d)
```

```python
import functools

import jax
import jax.numpy as jnp
from jax import lax
from jax.experimental import pallas as pl
from jax.experimental.pallas import tpu as pltpu

F32 = jnp.float32
BF16 = jnp.bfloat16

EPS = 1e-6
N_MIXERS = 3
TOP_K = 2
A_CHUNK = 128
A_GROUPS = 8
B_HEADS = 4
B_GATE_RANK = 16
B_TAU = 16.0
B_CHUNK = 64
C_HEADS = 8

V7X_LANES = 128
V7X_SUBLANES = 8
V7X_VMEM_LIMIT_BYTES = 56 * 1024 * 1024


def _params(*sem):
    return pltpu.CompilerParams(dimension_semantics=sem, vmem_limit_bytes=V7X_VMEM_LIMIT_BYTES)


def _block(total, target, align=V7X_LANES):
    if total <= target:
        return total
    return max(b for b in range(align, target + 1, align) if total % b == 0)


def _rms(x, g):
    ms = jnp.mean(x * x, axis=-1, keepdims=True)
    return x * lax.rsqrt(ms + EPS) * g


def _log_sigmoid(x):
    return jnp.minimum(x, 0.0) - jnp.log1p(jnp.exp(-jnp.abs(x)))


def _split3(x):
    hi = x.astype(BF16)
    r1 = x - hi.astype(F32)
    mid = r1.astype(BF16)
    lo = (r1 - mid.astype(F32)).astype(BF16)
    return hi, mid, lo


def _dot(a, b):
    return jnp.dot(a, b, preferred_element_type=F32)


def _dot_nt(a, b):
    return lax.dot_general(a, b, (((1,), (1,)), ((), ())), preferred_element_type=F32)


def _dot_tn(a, b):
    return lax.dot_general(a, b, (((0,), (0,)), ((), ())), preferred_element_type=F32)


def _tril_mask(n):
    row = lax.broadcasted_iota(jnp.int32, (n, n), 0)
    col = lax.broadcasted_iota(jnp.int32, (n, n), 1)
    return col <= row


def _norm_matmul_body(x_ref, g_ref, w_ref, o_ref, h_ref, *, act):
    @pl.when(pl.program_id(1) == 0)
    def _():
        h_ref[...] = _rms(x_ref[...], g_ref[...]).astype(BF16)

    y = _dot(h_ref[...], w_ref[...])
    if act == "gelu":
        y = jax.nn.gelu(y, approximate=True)
    o_ref[...] = y.astype(o_ref.dtype)


def norm_matmul(x, g, w, act=None, tm=1024, tn=1024):
    n, d = x.shape
    nout = w.shape[1]
    tm, tn = min(tm, n), _block(nout, tn)
    return pl.pallas_call(
        functools.partial(_norm_matmul_body, act=act),
        grid=(n // tm, nout // tn),
        in_specs=[
            pl.BlockSpec((tm, d), lambda i, j: (i, 0)),
            pl.BlockSpec((1, d), lambda i, j: (0, 0)),
            pl.BlockSpec((d, tn), lambda i, j: (0, j)),
        ],
        out_specs=pl.BlockSpec((tm, tn), lambda i, j: (i, j)),
        out_shape=jax.ShapeDtypeStruct((n, nout), BF16),
        scratch_shapes=[pltpu.VMEM((tm, d), BF16)],
        compiler_params=_params("parallel", "arbitrary"),
        name="norm_matmul",
    )(x, g, w)


def _matmul_residual_body(y_ref, w_ref, x_ref, o_ref):
    o_ref[...] = x_ref[...] + _dot(y_ref[...], w_ref[...])


def matmul_residual(y, w, x, tm=512):
    n, k = y.shape
    d = w.shape[1]
    tm = min(tm, n)
    return pl.pallas_call(
        _matmul_residual_body,
        grid=(n // tm,),
        in_specs=[
            pl.BlockSpec((tm, k), lambda i: (i, 0)),
            pl.BlockSpec((k, d), lambda i: (0, 0)),
            pl.BlockSpec((tm, d), lambda i: (i, 0)),
        ],
        out_specs=pl.BlockSpec((tm, d), lambda i: (i, 0)),
        out_shape=jax.ShapeDtypeStruct((n, d), F32),
        compiler_params=_params("parallel"),
        name="matmul_residual",
    )(y, w, x)


def _gmlp_body(z_ref, x_ref, vg_ref, ws_ref, bt_ref, wo_ref, o_ref, vn_ref, y_ref, *, tm, width):
    gd = width // A_GROUPS
    vn_ref[...] = _rms(z_ref[:, width:].astype(F32), vg_ref[...]).astype(BF16)
    tril = _tril_mask(A_CHUNK)
    for g in range(A_GROUPS):
        w = jnp.where(tril, ws_ref[g], 0.0).astype(BF16)
        bias = bt_ref[:, g : g + 1]
        cs = slice(g * gd, (g + 1) * gd)
        for c in range(tm // A_CHUNK):
            rs = slice(c * A_CHUNK, (c + 1) * A_CHUNK)
            mixed = _dot(w, vn_ref[rs, cs]) + bias
            y_ref[rs, cs] = (z_ref[rs, cs].astype(F32) * mixed).astype(BF16)
    o_ref[...] = x_ref[...] + _dot(y_ref[...], wo_ref[...])


def gmlp_spatial_out(z, x, vg, ws, bt, wo, tm=256):
    n, d = x.shape
    width = z.shape[1] // 2
    tm = min(tm, n)
    return pl.pallas_call(
        functools.partial(_gmlp_body, tm=tm, width=width),
        grid=(n // tm,),
        in_specs=[
            pl.BlockSpec((tm, 2 * width), lambda i: (i, 0)),
            pl.BlockSpec((tm, d), lambda i: (i, 0)),
            pl.BlockSpec((1, width), lambda i: (0, 0)),
            pl.BlockSpec((A_GROUPS, A_CHUNK, A_CHUNK), lambda i: (0, 0, 0)),
            pl.BlockSpec((A_CHUNK, A_GROUPS), lambda i: (0, 0)),
            pl.BlockSpec((width, d), lambda i: (0, 0)),
        ],
        out_specs=pl.BlockSpec((tm, d), lambda i: (i, 0)),
        out_shape=jax.ShapeDtypeStruct((n, d), F32),
        scratch_shapes=[pltpu.VMEM((tm, width), BF16), pltpu.VMEM((tm, width), BF16)],
        compiler_params=_params("parallel"),
        name="gmlp_spatial_out",
    )(z, x, vg, ws, bt, wo)


def _swiglu_step(h, wg_ref, wu_ref, wo_ref):
    a = _dot(h, wg_ref[...])
    t = (a * jax.nn.sigmoid(a)) * _dot(h, wu_ref[...])
    return _dot(t.astype(BF16), wo_ref[...])


def _ffn_body(*refs, final):
    x_ref, g_ref, wg_ref, wu_ref, wo_ref = refs[:5]
    rest = list(refs[5:])
    fg_ref = rest.pop(0) if final else None
    o_ref, h_ref, acc_ref = rest
    j = pl.program_id(1)

    @pl.when(j == 0)
    def _():
        h_ref[...] = _rms(x_ref[...], g_ref[...]).astype(BF16)
        acc_ref[...] = jnp.zeros_like(acc_ref)

    acc_ref[...] += _swiglu_step(h_ref[...], wg_ref, wu_ref, wo_ref)

    @pl.when(j == pl.num_programs(1) - 1)
    def _():
        y = x_ref[...] + acc_ref[...]
        o_ref[...] = _rms(y, fg_ref[...]) if final else y


def ffn(x, g, w_in, w_out, final_g=None, tm=1024, tf=512):
    n, d = x.shape
    f = w_out.shape[0]
    tm, tf = min(tm, n), min(tf, f)
    nf = f // tf
    final = final_g is not None
    in_specs = [
        pl.BlockSpec((tm, d), lambda i, j: (i, 0)),
        pl.BlockSpec((1, d), lambda i, j: (0, 0)),
        pl.BlockSpec((d, tf), lambda i, j: (0, j)),
        pl.BlockSpec((d, tf), lambda i, j: (0, j + nf)),
        pl.BlockSpec((tf, d), lambda i, j: (j, 0)),
    ]
    args = [x, g, w_in, w_in, w_out]
    if final:
        in_specs.append(pl.BlockSpec((1, d), lambda i, j: (0, 0)))
        args.append(final_g)
    return pl.pallas_call(
        functools.partial(_ffn_body, final=final),
        grid=(n // tm, nf),
        in_specs=in_specs,
        out_specs=pl.BlockSpec((tm, d), lambda i, j: (i, 0)),
        out_shape=jax.ShapeDtypeStruct((n, d), F32),
        scratch_shapes=[pltpu.VMEM((tm, d), BF16), pltpu.VMEM((tm, d), F32)],
        compiler_params=_params("parallel", "arbitrary"),
        name="ffn",
    )(*args)


TOK_ROWS = 8


def _router_body(x_ref, g_ref, whi_ref, wlo_ref, gate_ref, tok_ref, cnt_ref, carry_ref, *, ne, tm):
    @pl.when(pl.program_id(0) == 0)
    def _():
        carry_ref[...] = jnp.zeros_like(carry_ref)

    h = _rms(x_ref[...], g_ref[...])
    h_hi = h.astype(BF16)
    h_lo = (h - h_hi.astype(F32)).astype(BF16)
    logits = _dot(h_hi, whi_ref[...]) + _dot(h_lo, whi_ref[...]) + _dot(h_hi, wlo_ref[...])
    lane = lax.broadcasted_iota(jnp.int32, logits.shape, 1).astype(F32)
    neg = -jnp.inf
    l1 = jnp.where(lane < ne, logits, neg)
    m1 = jnp.max(l1, axis=-1, keepdims=True)
    i1 = jnp.min(jnp.where(l1 == m1, lane, float(V7X_LANES)), axis=-1, keepdims=True)
    l2 = jnp.where(lane == i1, neg, l1)
    m2 = jnp.max(l2, axis=-1, keepdims=True)
    i2 = jnp.min(jnp.where(l2 == m2, lane, float(V7X_LANES)), axis=-1, keepdims=True)
    e2 = jnp.exp(m2 - m1)
    den = 1.0 + e2
    gate_ref[...] = jnp.where(lane == 0.0, 1.0 / den, 0.0) + jnp.where(lane == 1.0, e2 / den, 0.0)

    sel1, sel2 = lane == i1, lane == i2
    onehot = jnp.where(sel1 | sel2, 1.0, 0.0)
    row = lax.broadcasted_iota(jnp.int32, (tm, tm), 0)
    col = lax.broadcasted_iota(jnp.int32, (tm, tm), 1)
    before = carry_ref[...] + _dot((col < row).astype(BF16), onehot.astype(BF16))
    carry_ref[...] = before[tm - 1 : tm, :] + onehot[tm - 1 : tm, :]
    cnt_ref[...] = carry_ref[...]
    r1 = jnp.sum(jnp.where(sel1, before, 0.0), axis=-1, keepdims=True)
    r2 = jnp.sum(jnp.where(sel2, before, 0.0), axis=-1, keepdims=True)
    table = (jnp.where(lane == 0.0, i1, 0.0) + jnp.where(lane == 1.0, i2, 0.0)
             + jnp.where(lane == 2.0, r1, 0.0) + jnp.where(lane == 3.0, r2, 0.0))
    pick = (lax.broadcasted_iota(jnp.int32, (TOK_ROWS, V7X_LANES), 0) == lax.broadcasted_iota(jnp.int32, (TOK_ROWS, V7X_LANES), 1)).astype(BF16)
    hi, mid, lo = _split3(table)
    tok_ref[...] = (_dot_nt(pick, hi) + _dot_nt(pick, mid) + _dot_nt(pick, lo)).astype(jnp.int32)


def moe_router(x, g, w_router, tm=1024):
    n, d = x.shape
    ne = w_router.shape[1]
    tm = min(tm, n)
    wpad = jnp.pad(w_router, ((0, 0), (0, V7X_LANES - ne)))
    w_hi = wpad.astype(BF16)
    w_lo = (wpad - w_hi.astype(F32)).astype(BF16)
    return pl.pallas_call(
        functools.partial(_router_body, ne=ne, tm=tm),
        grid=(n // tm,),
        in_specs=[
            pl.BlockSpec((tm, d), lambda i: (i, 0)),
            pl.BlockSpec((1, d), lambda i: (0, 0)),
            pl.BlockSpec((d, V7X_LANES), lambda i: (0, 0)),
            pl.BlockSpec((d, V7X_LANES), lambda i: (0, 0)),
        ],
        out_specs=[
            pl.BlockSpec((tm, V7X_LANES), lambda i: (i, 0)),
            pl.BlockSpec((TOK_ROWS, tm), lambda i: (0, i)),
            pl.BlockSpec((1, V7X_LANES), lambda i: (0, 0)),
        ],
        out_shape=[
            jax.ShapeDtypeStruct((n, V7X_LANES), F32),
            jax.ShapeDtypeStruct((TOK_ROWS, n), jnp.int32),
            jax.ShapeDtypeStruct((1, V7X_LANES), F32),
        ],
        scratch_shapes=[pltpu.VMEM((1, V7X_LANES), F32)],
        compiler_params=_params("arbitrary"),
        name="moe_router",
    )(x, g, w_hi, w_lo)


def _plan_body(cnt_ref, tok_ref, off_ref, te_ref, tv_ref, nv_ref, pos_ref, *, ne, tg, n_tiles):
    off = jnp.int32(0)
    tile = jnp.int32(0)
    expert = tok_ref[0:TOP_K, :]
    group_start = jnp.zeros_like(expert)
    for e in range(ne):
        nt = (cnt_ref[e] + (tg - 1)) // tg
        off_ref[e] = off
        group_start = jnp.where(expert == e, off, group_start)

        def mark(ti, carry, e=e, first=tile):
            te_ref[ti] = jnp.int32(e)
            tv_ref[ti] = jnp.minimum(cnt_ref[e] - (ti - first) * tg, tg)
            return carry

        lax.fori_loop(tile, tile + nt, mark, 0)
        off = off + nt * tg
        tile = tile + nt
    pos_ref[...] = jnp.zeros_like(pos_ref)
    pos_ref[0:TOP_K, :] = group_start + tok_ref[TOP_K : 2 * TOP_K, :]
    nv_ref[0] = tile
    last = te_ref[jnp.maximum(tile - 1, 0)]

    def fill(ti, carry):
        te_ref[ti] = last
        tv_ref[ti] = jnp.int32(0)
        return carry

    lax.fori_loop(tile, n_tiles, fill, 0)


def moe_plan(cnt, tok, tg, n_tiles):
    ne = cnt.shape[0]
    smem = pl.BlockSpec(memory_space=pltpu.SMEM)
    vmem = pl.BlockSpec(memory_space=pltpu.VMEM)
    return pl.pallas_call(
        functools.partial(_plan_body, ne=ne, tg=tg, n_tiles=n_tiles),
        in_specs=[smem, vmem],
        out_specs=[smem, smem, smem, smem, vmem],
        out_shape=[
            jax.ShapeDtypeStruct((ne,), jnp.int32),
            jax.ShapeDtypeStruct((n_tiles,), jnp.int32),
            jax.ShapeDtypeStruct((n_tiles,), jnp.int32),
            jax.ShapeDtypeStruct((1,), jnp.int32),
            jax.ShapeDtypeStruct(tok.shape, jnp.int32),
        ],
        name="moe_plan",
    )(cnt, tok)


def _token_copy(src, s, dst, r, sem, ns):
    return pltpu.make_async_copy(src.at[pl.ds(pl.multiple_of(s * ns, ns), ns)], dst.at[pl.ds(pl.multiple_of(r * ns, ns), ns)], sem)


def _to_slabs(dst_ref, val, ns):
    rows = val.shape[0]
    for c in range(ns):
        dst_ref[pl.ds(c, rows, stride=ns), :] = val[:, c * V7X_LANES : (c + 1) * V7X_LANES]


def _slab(src_ref, c, rows, ns):
    return src_ref[pl.ds(c, rows, stride=ns), :]


def _dispatch_body(pos_hbm, off_ref, cnt_ref, x_ref, xs_hbm, pos_smem, x3_ref, zero_ref, sem_idx, sem_row, *, td, tg, ne, n_tiles, ns):
    base = pl.multiple_of(pl.program_id(0) * td, td)
    idx = pltpu.make_async_copy(pos_hbm.at[:, pl.ds(base, td)], pos_smem, sem_idx)
    idx.start()
    _to_slabs(x3_ref, x_ref[...], ns)

    @pl.when(pl.program_id(0) == 0)
    def _():
        zero_ref[...] = jnp.zeros_like(zero_ref)
        for e in range(ne):
            start = off_ref[e] + cnt_ref[e]
            stop = off_ref[e] + (cnt_ref[e] + (tg - 1)) // tg * tg

            def zero_start(r, carry):
                _token_copy(zero_ref, 0, xs_hbm, r, sem_row, ns).start()
                return carry

            def zero_wait(r, carry):
                _token_copy(zero_ref, 0, xs_hbm, r, sem_row, ns).wait()
                return carry

            lax.fori_loop(start, stop, zero_start, 0)
            lax.fori_loop(start, stop, zero_wait, 0)

        def tile_copy(ti):
            return pltpu.make_async_copy(zero_ref, xs_hbm.at[pl.ds(pl.multiple_of(ti * (tg * ns), tg * ns), tg * ns)], sem_row)

        def tile_start(ti, carry):
            tile_copy(ti).start()
            return carry

        def tile_wait(ti, carry):
            tile_copy(ti).wait()
            return carry

        used = stop // tg
        lax.fori_loop(used, n_tiles, tile_start, 0)
        lax.fori_loop(used, n_tiles, tile_wait, 0)

    idx.wait()

    def start(t, c):
        for k in range(TOP_K):
            _token_copy(x3_ref, t, xs_hbm, pos_smem[k, t], sem_row, ns).start()
        return c

    lax.fori_loop(0, td, start, 0, unroll=8)
    for _ in range(TOP_K):
        pltpu.make_async_copy(x3_ref, xs_hbm.at[pl.ds(0, td * ns)], sem_row).wait()


def moe_dispatch(pos, off, cnt, x, tg, n_tiles, td=1024):
    n, d = x.shape
    ns = d // V7X_LANES
    assert ns % V7X_SUBLANES == 0
    td = min(td, n)
    smem = pl.BlockSpec(memory_space=pltpu.SMEM)
    hbm = pl.BlockSpec(memory_space=pl.ANY)
    return pl.pallas_call(
        functools.partial(_dispatch_body, td=td, tg=tg, ne=cnt.shape[0], n_tiles=n_tiles, ns=ns),
        grid=(n // td,),
        in_specs=[hbm, smem, smem, pl.BlockSpec((td, d), lambda i: (i, 0))],
        out_specs=hbm,
        out_shape=jax.ShapeDtypeStruct((n_tiles * tg * ns, V7X_LANES), x.dtype),
        scratch_shapes=[
            pltpu.SMEM((TOK_ROWS, td), jnp.int32),
            pltpu.VMEM((td * ns, V7X_LANES), x.dtype),
            pltpu.VMEM((tg * ns, V7X_LANES), x.dtype),
            pltpu.SemaphoreType.DMA(()),
            pltpu.SemaphoreType.DMA(()),
        ],
        compiler_params=pltpu.CompilerParams(dimension_semantics=("arbitrary",), has_side_effects=True, vmem_limit_bytes=V7X_VMEM_LIMIT_BYTES),
        name="moe_dispatch",
    )(pos, off, cnt, x)


GROUP_SHORT_TILE_DIV = 4


def _grouped_ffn_body(te_ref, tv_ref, nv_ref, x_ref, g_ref, wg_ref, wu_ref, wo_ref, o_ref, h_ref, acc_ref, *, tg, ns):
    i, j = pl.program_id(0), pl.program_id(1)
    d = ns * V7X_LANES

    @pl.when(i < nv_ref[0])
    def _():
        @pl.when(j == 0)
        def _():
            ss = jnp.zeros((tg, 1), F32)
            for c in range(ns):
                xc = _slab(x_ref, c, tg, ns)
                ss = ss + jnp.sum(xc * xc, axis=-1, keepdims=True)
            r = lax.rsqrt(ss / d + EPS)
            for c in range(ns):
                cols = slice(c * V7X_LANES, (c + 1) * V7X_LANES)
                h_ref[:, cols] = (_slab(x_ref, c, tg, ns) * r * g_ref[:, cols]).astype(BF16)

        def accumulate(m):
            y = _swiglu_step(h_ref[0:m, :], wg_ref, wu_ref, wo_ref)

            @pl.when(j == 0)
            def _():
                acc_ref[0:m, :] = y
                if m < tg:
                    acc_ref[m:, :] = jnp.zeros((tg - m, d), F32)

            @pl.when(j > 0)
            def _():
                acc_ref[0:m, :] += y

        short = tv_ref[i] <= tg // GROUP_SHORT_TILE_DIV
        pl.when(short)(lambda: accumulate(tg // GROUP_SHORT_TILE_DIV))
        pl.when(jnp.logical_not(short))(lambda: accumulate(tg))

        @pl.when(j == pl.num_programs(1) - 1)
        def _():
            _to_slabs(o_ref, acc_ref[...], ns)

    @pl.when((i >= nv_ref[0]) & (j == 0))
    def _():
        o_ref[...] = jnp.zeros_like(o_ref)


def moe_grouped_ffn(te, tv, nv, xs, g, w_in, w_out, tg, tf=1024):
    d = w_in.shape[1]
    ns = d // V7X_LANES
    r = xs.shape[0] // ns
    f = w_out.shape[1]
    tf = _block(f, tf)
    nf = f // tf
    n_tiles = r // tg

    def tile(i, nv):
        return jnp.minimum(i, nv[0] - 1)

    def fblk(i, j, nv):
        return jnp.where(i < nv[0], j, nf - 1)

    grid_spec = pltpu.PrefetchScalarGridSpec(
        num_scalar_prefetch=3,
        grid=(n_tiles, nf),
        in_specs=[
            pl.BlockSpec((tg * ns, V7X_LANES), lambda i, j, te, tv, nv: (tile(i, nv), 0)),
            pl.BlockSpec((1, d), lambda i, j, te, tv, nv: (0, 0)),
            pl.BlockSpec((None, d, tf), lambda i, j, te, tv, nv: (te[i], 0, fblk(i, j, nv))),
            pl.BlockSpec((None, d, tf), lambda i, j, te, tv, nv: (te[i], 0, fblk(i, j, nv) + nf)),
            pl.BlockSpec((None, tf, d), lambda i, j, te, tv, nv: (te[i], fblk(i, j, nv), 0)),
        ],
        out_specs=pl.BlockSpec((tg * ns, V7X_LANES), lambda i, j, te, tv, nv: (i, 0)),
        scratch_shapes=[pltpu.VMEM((tg, d), BF16), pltpu.VMEM((tg, d), F32)],
    )
    return pl.pallas_call(
        functools.partial(_grouped_ffn_body, tg=tg, ns=ns),
        grid_spec=grid_spec,
        out_shape=jax.ShapeDtypeStruct((r * ns, V7X_LANES), F32),
        compiler_params=_params("arbitrary", "arbitrary"),
        name="moe_grouped_ffn",
    )(te, tv, nv, xs, g, w_in, w_in, w_out)


def _combine_body(pos_hbm, gate_ref, x_ref, ys_hbm, *rest, tc, ns, final):
    rest = list(rest)
    fg_ref = rest.pop(0) if final else None
    o_ref, pos_smem, buf_ref, sem_idx, sem_row = rest
    base = pl.multiple_of(pl.program_id(0) * tc, tc)
    idx = pltpu.make_async_copy(pos_hbm.at[:, pl.ds(base, tc)], pos_smem, sem_idx)
    idx.start()
    idx.wait()

    def start(t, c):
        for k in range(TOP_K):
            _token_copy(ys_hbm, pos_smem[k, t], buf_ref.at[k], t, sem_row, ns).start()
        return c

    lax.fori_loop(0, tc, start, 0, unroll=8)
    for k in range(TOP_K):
        pltpu.make_async_copy(ys_hbm.at[pl.ds(0, tc * ns)], buf_ref.at[k], sem_row).wait()
    g0, g1 = gate_ref[:, 0:1], gate_ref[:, 1:2]
    for c in range(ns):
        cols = slice(c * V7X_LANES, (c + 1) * V7X_LANES)
        o_ref[:, cols] = x_ref[:, cols] + g0 * _slab(buf_ref.at[0], c, tc, ns) + g1 * _slab(buf_ref.at[1], c, tc, ns)
    if final:
        o_ref[...] = _rms(o_ref[...], fg_ref[...])


def moe_combine(pos, gates, x, ys, final_g=None, tc=512):
    n, d = x.shape
    ns = d // V7X_LANES
    tc = min(tc, n)
    final = final_g is not None
    hbm = pl.BlockSpec(memory_space=pl.ANY)
    in_specs = [hbm, pl.BlockSpec((tc, V7X_LANES), lambda i: (i, 0)), pl.BlockSpec((tc, d), lambda i: (i, 0)), hbm]
    args = [pos, gates, x, ys]
    if final:
        in_specs.append(pl.BlockSpec((1, d), lambda i: (0, 0)))
        args.append(final_g)
    return pl.pallas_call(
        functools.partial(_combine_body, tc=tc, ns=ns, final=final),
        grid=(n // tc,),
        in_specs=in_specs,
        out_specs=pl.BlockSpec((tc, d), lambda i: (i, 0)),
        out_shape=jax.ShapeDtypeStruct((n, d), F32),
        scratch_shapes=[
            pltpu.SMEM((TOK_ROWS, tc), jnp.int32),
            pltpu.VMEM((TOP_K, tc * ns, V7X_LANES), F32),
            pltpu.SemaphoreType.DMA(()),
            pltpu.SemaphoreType.DMA(()),
        ],
        compiler_params=_params("arbitrary"),
        name="moe_combine",
    )(*args)


def moe_ffn(x, g, w_router, w_in, w_out, final_g=None, tg=1024):
    n, _ = x.shape
    ne = w_router.shape[1]
    tg = min(tg, n)
    n_tiles = TOP_K * n // tg + ne
    gates, tok, counts = moe_router(x, g, w_router)
    cnt = counts[0, :ne].astype(jnp.int32)
    off, te, tv, nv, pos = moe_plan(cnt, tok, tg, n_tiles)
    xs = moe_dispatch(pos, off, cnt, x, tg, n_tiles)
    ys = moe_grouped_ffn(te, tv, nv, xs, g, w_in, w_out, tg)
    return moe_combine(pos, gates, x, ys, final_g)


def _gla_gate_body(x_ref, g_ref, wl_ref, w2_ref, b_ref, o_ref):
    h = _rms(x_ref[...], g_ref[...]).astype(BF16)
    g_low = _dot(h, wl_ref[...]).astype(BF16)
    o_ref[...] = _log_sigmoid(_dot(g_low, w2_ref[...]) + b_ref[...]) / B_TAU


def gla_gate(x, g, w_low, w_gate2, bias, tm=1024):
    n, d = x.shape
    dk = w_gate2.shape[1]
    tm = min(tm, n)
    wl = jnp.pad(w_low, ((0, 0), (0, V7X_LANES - B_GATE_RANK))).astype(BF16)
    w2 = jnp.pad(w_gate2, ((0, V7X_LANES - B_GATE_RANK), (0, 0))).astype(BF16)
    return pl.pallas_call(
        _gla_gate_body,
        grid=(n // tm,),
        in_specs=[
            pl.BlockSpec((tm, d), lambda i: (i, 0)),
            pl.BlockSpec((1, d), lambda i: (0, 0)),
            pl.BlockSpec((d, V7X_LANES), lambda i: (0, 0)),
            pl.BlockSpec((V7X_LANES, dk), lambda i: (0, 0)),
            pl.BlockSpec((1, dk), lambda i: (0, 0)),
        ],
        out_specs=pl.BlockSpec((tm, dk), lambda i: (i, 0)),
        out_shape=jax.ShapeDtypeStruct((n, dk), F32),
        compiler_params=_params("parallel"),
        name="gla_gate",
    )(x, g, wl, w2, bias)


def _gla_body(q_ref, k_ref, v_ref, r_ref, la_ref, og_ref, o_ref, st_ref, *, tc, hk, hv):
    @pl.when(pl.program_id(1) == 0)
    def _():
        st_ref[...] = jnp.zeros_like(st_ref)

    c = B_CHUNK
    tril = _tril_mask(c)
    ones_tril = tril.astype(BF16)
    scale = hk**-0.5
    for ci in range(tc // c):
        rs = slice(ci * c, (ci + 1) * c)
        hi, mid, lo = _split3(la_ref[rs, :])
        bcum = _dot(ones_tril, hi) + _dot(ones_tril, mid) + _dot(ones_tril, lo)
        b_last = bcum[c - 1 : c, :]
        e_pos = jnp.exp(bcum)
        e_neg = jnp.exp(-bcum)
        e_end = jnp.exp(b_last - bcum)
        dec = jnp.exp(b_last)
        for h in range(B_HEADS):
            ks = slice(h * hk, (h + 1) * hk)
            vs = slice(h * hv, (h + 1) * hv)
            q = q_ref[rs, ks].astype(F32) * scale
            k = k_ref[rs, ks].astype(F32)
            v = v_ref[rs, vs]
            q_d = (q * e_pos[:, ks]).astype(BF16)
            k_d = (k * e_neg[:, ks]).astype(BF16)
            k_end = (k * e_end[:, ks]).astype(BF16)
            att = jnp.where(tril, _dot_nt(q_d, k_d), 0.0)
            st = st_ref[h]
            o = _dot(att.astype(BF16), v) + _dot_nt(q_d, st.astype(BF16))
            st_ref[h] = dec[:, ks] * st + _dot_tn(v, k_end)
            y = _rms(o, og_ref[...]).astype(BF16).astype(F32)
            r = r_ref[rs, vs].astype(F32)
            o_ref[rs, vs] = (y * (r * jax.nn.sigmoid(r))).astype(BF16)


def gla_chunks(proj, la, o_g, bsz, tc=256):
    n = proj.shape[0]
    dk = la.shape[1]
    dv = (proj.shape[1] - 2 * dk) // 2
    s = n // bsz
    tc = min(tc, s)
    nt = s // tc
    hk, hv = dk // B_HEADS, dv // B_HEADS
    assert (2 * dk) % dv == 0
    v_blk = 2 * dk // dv
    row = lambda b, t: b * nt + t
    return pl.pallas_call(
        functools.partial(_gla_body, tc=tc, hk=hk, hv=hv),
        grid=(bsz, nt),
        in_specs=[
            pl.BlockSpec((tc, dk), lambda b, t: (row(b, t), 0)),
            pl.BlockSpec((tc, dk), lambda b, t: (row(b, t), 1)),
            pl.BlockSpec((tc, dv), lambda b, t: (row(b, t), v_blk)),
            pl.BlockSpec((tc, dv), lambda b, t: (row(b, t), v_blk + 1)),
            pl.BlockSpec((tc, dk), lambda b, t: (row(b, t), 0)),
            pl.BlockSpec((1, hv), lambda b, t: (0, 0)),
        ],
        out_specs=pl.BlockSpec((tc, dv), lambda b, t: (row(b, t), 0)),
        out_shape=jax.ShapeDtypeStruct((n, dv), BF16),
        scratch_shapes=[pltpu.VMEM((B_HEADS, hv, hk), F32)],
        compiler_params=_params("parallel", "arbitrary"),
        name="gla_chunks",
    )(proj, proj, proj, proj, la, o_g)


def _fox_gate_body(x_ref, g_ref, wf_ref, b_ref, crow_ref, carry_ref, *, tb, nh):
    @pl.when(pl.program_id(1) == 0)
    def _():
        carry_ref[...] = jnp.zeros_like(carry_ref)

    h = _rms(x_ref[...], g_ref[...]).astype(BF16)
    lane = lax.broadcasted_iota(jnp.int32, (tb, V7X_LANES), 1)
    log_f = jnp.where(lane < nh, _log_sigmoid(_dot(h, wf_ref[...]) + b_ref[...]), 0.0)
    ones_tril = _tril_mask(tb).astype(BF16)
    hi, mid, lo = _split3(log_f)
    c = carry_ref[...] + (_dot(ones_tril, hi) + _dot(ones_tril, mid) + _dot(ones_tril, lo))
    carry_ref[...] = c[tb - 1 : tb, :]
    sel = (lax.broadcasted_iota(jnp.int32, (nh, V7X_LANES), 0) == lax.broadcasted_iota(jnp.int32, (nh, V7X_LANES), 1)).astype(BF16)
    hi, mid, lo = _split3(c)
    crow_ref[...] = _dot_nt(sel, hi) + _dot_nt(sel, mid) + _dot_nt(sel, lo)


def fox_gate(x, g, w_f, f_bias, bsz, tb=512):
    n, d = x.shape
    nh = w_f.shape[1]
    s = n // bsz
    tb = min(tb, s)
    nt = s // tb
    wf = jnp.pad(w_f, ((0, 0), (0, V7X_LANES - nh))).astype(BF16)
    bias = jnp.pad(f_bias.reshape(1, nh), ((0, 0), (0, V7X_LANES - nh)))
    return pl.pallas_call(
        functools.partial(_fox_gate_body, tb=tb, nh=nh),
        grid=(bsz, nt),
        in_specs=[
            pl.BlockSpec((tb, d), lambda b, t: (b * nt + t, 0)),
            pl.BlockSpec((1, d), lambda b, t: (0, 0)),
            pl.BlockSpec((d, V7X_LANES), lambda b, t: (0, 0)),
            pl.BlockSpec((1, V7X_LANES), lambda b, t: (0, 0)),
        ],
        out_specs=pl.BlockSpec((None, nh, tb), lambda b, t: (b, 0, t)),
        out_shape=jax.ShapeDtypeStruct((bsz, nh, s), F32),
        scratch_shapes=[pltpu.VMEM((1, V7X_LANES), F32)],
        compiler_params=_params("parallel", "arbitrary"),
        name="fox_gate",
    )(x, g, wf, bias)


FOX_KEY_BLOCK = 512
FOX_SUBTILES = 8
FOX_ROW_CHUNK = 32


def _fox_flash_body(q_ref, k_ref, v_ref, crow_ref, o_ref, qs_ref, va_ref, s_ref, p_ref, al_ref, m_ref, acc_ref, *, t, nsub, scale):
    i = pl.program_id(2)
    dh = q_ref.shape[1]
    nl = t // V7X_LANES

    @pl.when(i == 0)
    def _():
        va_ref[:, :dh] = v_ref[...]
        va_ref[:, dh:] = jnp.ones((va_ref.shape[0], V7X_LANES), BF16)

    qs_ref[...] = (q_ref[...].astype(F32) * scale).astype(BF16)
    m_ref[...] = jnp.full_like(m_ref, -jnp.inf)
    acc_ref[...] = jnp.zeros_like(acc_ref)

    def block(sub, kb, masked):
        rows = slice(sub * t, (sub + 1) * t)
        off = pl.multiple_of(kb * t, t)
        s_ref[sub] = _dot_nt(qs_ref[rows, :], k_ref[pl.ds(off, t), :])
        bias = crow_ref[pl.ds(kb, 1), :]

        def chunk(c, carry):
            r = pl.multiple_of(c * FOX_ROW_CHUNK, FOX_ROW_CHUNK)
            sc = s_ref[sub, pl.ds(r, FOX_ROW_CHUNK), :] - bias
            if masked:
                row = r + lax.broadcasted_iota(jnp.int32, sc.shape, 0)
                col = lax.broadcasted_iota(jnp.int32, sc.shape, 1)
                sc = jnp.where(col <= row, sc, -jnp.inf)
            tiles = [sc[:, a * V7X_LANES : (a + 1) * V7X_LANES] for a in range(nl)]
            mx = functools.reduce(jnp.maximum, tiles)
            m_old = m_ref[pl.ds(sub * t + r, FOX_ROW_CHUNK), :]
            m_new = jnp.maximum(m_old, jnp.max(mx, axis=-1, keepdims=True))
            al_ref[sub, pl.ds(r, FOX_ROW_CHUNK), :] = jnp.exp(m_old - m_new)
            m_ref[pl.ds(sub * t + r, FOX_ROW_CHUNK), :] = m_new
            for a in range(nl):
                p_ref[sub, pl.ds(r, FOX_ROW_CHUNK), a * V7X_LANES : (a + 1) * V7X_LANES] = jnp.exp(tiles[a] - m_new).astype(BF16)
            return carry

        lax.fori_loop(0, t // FOX_ROW_CHUNK, chunk, 0, unroll=True)
        pv = _dot(p_ref[sub], va_ref[pl.ds(off, t), :])
        alpha = al_ref[sub]
        acc_ref[rows, :dh] = alpha * acc_ref[rows, :dh] + pv[:, :dh]
        acc_ref[rows, dh:] = alpha * acc_ref[rows, dh:] + pv[:, dh:]

    def below_diagonal(kb, carry):
        for sub in range(nsub):
            block(sub, kb, False)
        return carry

    first = i * nsub
    lax.fori_loop(0, first, below_diagonal, 0)
    for kb in range(nsub):
        for sub in range(kb, nsub):
            block(sub, first + kb, masked=(sub == kb))
    o_ref[...] = (acc_ref[:, :dh] / acc_ref[:, dh:]).astype(o_ref.dtype)


def fox_flash(proj, crow, bsz, nh):
    n = proj.shape[0]
    dh = proj.shape[1] // (3 * nh)
    assert dh == V7X_LANES
    s = n // bsz
    t = min(FOX_KEY_BLOCK, s)
    nsub = min(FOX_SUBTILES, s // t)
    tq = t * nsub
    nq = s // tq
    crow = crow.reshape(bsz, nh, s // t, t)
    return pl.pallas_call(
        functools.partial(_fox_flash_body, t=t, nsub=nsub, scale=dh**-0.5),
        grid=(bsz, nh, nq),
        in_specs=[
            pl.BlockSpec((tq, dh), lambda b, h, i: (b * nq + i, h)),
            pl.BlockSpec((s, dh), lambda b, h, i: (b, nh + h)),
            pl.BlockSpec((s, dh), lambda b, h, i: (b, 2 * nh + h)),
            pl.BlockSpec((None, None, s // t, t), lambda b, h, i: (b, h, 0, 0)),
        ],
        out_specs=pl.BlockSpec((tq, dh), lambda b, h, i: (b * nq + i, h)),
        out_shape=jax.ShapeDtypeStruct((n, nh * dh), BF16),
        scratch_shapes=[
            pltpu.VMEM((tq, dh), BF16),
            pltpu.VMEM((s, dh + V7X_LANES), BF16),
            pltpu.VMEM((nsub, t, t), F32),
            pltpu.VMEM((nsub, t, t), BF16),
            pltpu.VMEM((nsub, t, V7X_LANES), F32),
            pltpu.VMEM((tq, V7X_LANES), F32),
            pltpu.VMEM((tq, dh + V7X_LANES), F32),
        ],
        compiler_params=_params("parallel", "parallel", "arbitrary"),
        name="fox_flash",
    )(proj, proj, proj, crow)


def kernel(x, norm1_g, norm2_g, a_w_in, a_vnorm_g, a_w_s, a_b_s, a_w_out, b_w_in, b_w_gate2, b_gate_bias, b_onorm_g, b_w_out, c_w_in, c_f_bias, c_w_out, ffn_w_in, ffn_w_out, moe_router, moe_w_in, moe_w_out, final_g):
    bsz, s, d = x.shape
    depth = norm1_g.shape[0]
    xf = x.reshape(bsz * s, d)
    for i in range(depth):
        g1 = norm1_g[i].reshape(1, d)
        m, j = i % N_MIXERS, i // N_MIXERS
        if m == 0:
            z = norm_matmul(xf, g1, a_w_in[j].astype(BF16), act="gelu")
            xf = gmlp_spatial_out(z, xf, a_vnorm_g[j].reshape(1, -1), a_w_s[j], a_b_s[j].T, a_w_out[j].astype(BF16))
        elif m == 1:
            dk = b_w_gate2.shape[2]
            n_main = b_w_in.shape[2] - B_GATE_RANK
            proj = norm_matmul(xf, g1, b_w_in[j, :, :n_main].astype(BF16))
            la = gla_gate(xf, g1, b_w_in[j, :, n_main:], b_w_gate2[j], b_gate_bias[j].reshape(1, dk))
            o = gla_chunks(proj, la, b_onorm_g[j].reshape(1, -1), bsz)
            xf = matmul_residual(o, b_w_out[j].astype(BF16), xf)
        else:
            proj = norm_matmul(xf, g1, c_w_in[j, :, : 3 * d].astype(BF16))
            crow = fox_gate(xf, g1, c_w_in[j, :, 3 * d :], c_f_bias[j], bsz)
            o = fox_flash(proj, crow, bsz, C_HEADS)
            xf = matmul_residual(o, c_w_out[j].astype(BF16), xf)
        g2 = norm2_g[i].reshape(1, d)
        fg = final_g.reshape(1, d) if i == depth - 1 else None
        if i % 2 == 0:
            xf = ffn(xf, g2, ffn_w_in[i // 2].astype(BF16), ffn_w_out[i // 2].astype(BF16), final_g=fg)
        else:
            xf = moe_ffn(xf, g2, moe_router[i // 2], moe_w_in[i // 2].astype(BF16), moe_w_out[i // 2].astype(BF16), final_g=fg)
    return xf.reshape(bsz, s, d)
```

```python
import functools

import jax
import jax.numpy as jnp
from jax import lax
from jax.experimental import pallas as pl
from jax.experimental.pallas import tpu as pltpu

F32 = jnp.float32
BF16 = jnp.bfloat16

EPS = 1e-6
N_MIXERS = 3
TOP_K = 2
A_CHUNK = 128
A_GROUPS = 8
B_HEADS = 4
B_GATE_RANK = 16
B_TAU = 16.0
B_CHUNK = 64
C_HEADS = 8

V7X_LANES = 128
V7X_SUBLANES = 8
V7X_VMEM_LIMIT_BYTES = 56 * 1024 * 1024


def _params(*sem):
    return pltpu.CompilerParams(dimension_semantics=sem, vmem_limit_bytes=V7X_VMEM_LIMIT_BYTES)


def _block(total, target, align=V7X_LANES):
    if total <= target:
        return total
    return max(b for b in range(align, target + 1, align) if total % b == 0)


def _rms(x, g):
    ms = jnp.mean(x * x, axis=-1, keepdims=True)
    return x * lax.rsqrt(ms + EPS) * g


def _log_sigmoid(x):
    return jnp.minimum(x, 0.0) - jnp.log1p(jnp.exp(-jnp.abs(x)))


def _split3(x):
    hi = x.astype(BF16)
    r1 = x - hi.astype(F32)
    mid = r1.astype(BF16)
    lo = (r1 - mid.astype(F32)).astype(BF16)
    return hi, mid, lo


def _dot(a, b):
    return jnp.dot(a, b, preferred_element_type=F32)


def _dot_nt(a, b):
    return lax.dot_general(a, b, (((1,), (1,)), ((), ())), preferred_element_type=F32)


def _dot_tn(a, b):
    return lax.dot_general(a, b, (((0,), (0,)), ((), ())), preferred_element_type=F32)


def _tril_mask(n):
    row = lax.broadcasted_iota(jnp.int32, (n, n), 0)
    col = lax.broadcasted_iota(jnp.int32, (n, n), 1)
    return col <= row


def _norm_matmul_body(x_ref, g_ref, w_ref, o_ref, h_ref, *, act):
    @pl.when(pl.program_id(1) == 0)
    def _():
        h_ref[...] = _rms(x_ref[...], g_ref[...]).astype(BF16)

    y = _dot(h_ref[...], w_ref[...])
    if act == "gelu":
        y = jax.nn.gelu(y, approximate=True)
    o_ref[...] = y.astype(o_ref.dtype)


def norm_matmul(x, g, w, act=None, tm=1024, tn=1024):
    n, d = x.shape
    nout = w.shape[1]
    tm, tn = min(tm, n), _block(nout, tn)
    return pl.pallas_call(
        functools.partial(_norm_matmul_body, act=act),
        grid=(n // tm, nout // tn),
        in_specs=[
            pl.BlockSpec((tm, d), lambda i, j: (i, 0)),
            pl.BlockSpec((1, d), lambda i, j: (0, 0)),
            pl.BlockSpec((d, tn), lambda i, j: (0, j)),
        ],
        out_specs=pl.BlockSpec((tm, tn), lambda i, j: (i, j)),
        out_shape=jax.ShapeDtypeStruct((n, nout), BF16),
        scratch_shapes=[pltpu.VMEM((tm, d), BF16)],
        compiler_params=_params("parallel", "arbitrary"),
        name="norm_matmul",
    )(x, g, w)


def _matmul_residual_body(y_ref, w_ref, x_ref, o_ref):
    o_ref[...] = x_ref[...] + _dot(y_ref[...], w_ref[...])


def matmul_residual(y, w, x, tm=512):
    n, k = y.shape
    d = w.shape[1]
    tm = min(tm, n)
    return pl.pallas_call(
        _matmul_residual_body,
        grid=(n // tm,),
        in_specs=[
            pl.BlockSpec((tm, k), lambda i: (i, 0)),
            pl.BlockSpec((k, d), lambda i: (0, 0)),
            pl.BlockSpec((tm, d), lambda i: (i, 0)),
        ],
        out_specs=pl.BlockSpec((tm, d), lambda i: (i, 0)),
        out_shape=jax.ShapeDtypeStruct((n, d), F32),
        compiler_params=_params("parallel"),
        name="matmul_residual",
    )(y, w, x)


def _gmlp_body(z_ref, x_ref, vg_ref, ws_ref, bt_ref, wo_ref, o_ref, vn_ref, y_ref, *, tm, width):
    gd = width // A_GROUPS
    vn_ref[...] = _rms(z_ref[:, width:].astype(F32), vg_ref[...]).astype(BF16)
    tril = _tril_mask(A_CHUNK)
    for g in range(A_GROUPS):
        w = jnp.where(tril, ws_ref[g], 0.0).astype(BF16)
        bias = bt_ref[:, g : g + 1]
        cs = slice(g * gd, (g + 1) * gd)
        for c in range(tm // A_CHUNK):
            rs = slice(c * A_CHUNK, (c + 1) * A_CHUNK)
            mixed = _dot(w, vn_ref[rs, cs]) + bias
            y_ref[rs, cs] = (z_ref[rs, cs].astype(F32) * mixed).astype(BF16)
    o_ref[...] = x_ref[...] + _dot(y_ref[...], wo_ref[...])


def gmlp_spatial_out(z, x, vg, ws, bt, wo, tm=512):
    n, d = x.shape
    width = z.shape[1] // 2
    tm = min(tm, n)
    return pl.pallas_call(
        functools.partial(_gmlp_body, tm=tm, width=width),
        grid=(n // tm,),
        in_specs=[
            pl.BlockSpec((tm, 2 * width), lambda i: (i, 0)),
            pl.BlockSpec((tm, d), lambda i: (i, 0)),
            pl.BlockSpec((1, width), lambda i: (0, 0)),
            pl.BlockSpec((A_GROUPS, A_CHUNK, A_CHUNK), lambda i: (0, 0, 0)),
            pl.BlockSpec((A_CHUNK, A_GROUPS), lambda i: (0, 0)),
            pl.BlockSpec((width, d), lambda i: (0, 0)),
        ],
        out_specs=pl.BlockSpec((tm, d), lambda i: (i, 0)),
        out_shape=jax.ShapeDtypeStruct((n, d), F32),
        scratch_shapes=[pltpu.VMEM((tm, width), BF16), pltpu.VMEM((tm, width), BF16)],
        compiler_params=_params("parallel"),
        name="gmlp_spatial_out",
    )(z, x, vg, ws, bt, wo)


def _swiglu_step(h, wg_ref, wu_ref, wo_ref):
    a = _dot(h, wg_ref[...])
    t = (a * jax.nn.sigmoid(a)) * _dot(h, wu_ref[...])
    return _dot(t.astype(BF16), wo_ref[...])


def _ffn_body(*refs, final):
    x_ref, g_ref, wg_ref, wu_ref, wo_ref = refs[:5]
    rest = list(refs[5:])
    fg_ref = rest.pop(0) if final else None
    o_ref, h_ref, acc_ref = rest
    j = pl.program_id(1)

    @pl.when(j == 0)
    def _():
        h_ref[...] = _rms(x_ref[...], g_ref[...]).astype(BF16)
        acc_ref[...] = jnp.zeros_like(acc_ref)

    acc_ref[...] += _swiglu_step(h_ref[...], wg_ref, wu_ref, wo_ref)

    @pl.when(j == pl.num_programs(1) - 1)
    def _():
        y = x_ref[...] + acc_ref[...]
        o_ref[...] = _rms(y, fg_ref[...]) if final else y


def ffn(x, g, w_in, w_out, final_g=None, tm=1024, tf=512):
    n, d = x.shape
    f = w_out.shape[0]
    tm, tf = min(tm, n), min(tf, f)
    nf = f // tf
    final = final_g is not None
    in_specs = [
        pl.BlockSpec((tm, d), lambda i, j: (i, 0)),
        pl.BlockSpec((1, d), lambda i, j: (0, 0)),
        pl.BlockSpec((d, tf), lambda i, j: (0, j)),
        pl.BlockSpec((d, tf), lambda i, j: (0, j + nf)),
        pl.BlockSpec((tf, d), lambda i, j: (j, 0)),
    ]
    args = [x, g, w_in, w_in, w_out]
    if final:
        in_specs.append(pl.BlockSpec((1, d), lambda i, j: (0, 0)))
        args.append(final_g)
    return pl.pallas_call(
        functools.partial(_ffn_body, final=final),
        grid=(n // tm, nf),
        in_specs=in_specs,
        out_specs=pl.BlockSpec((tm, d), lambda i, j: (i, 0)),
        out_shape=jax.ShapeDtypeStruct((n, d), F32),
        scratch_shapes=[pltpu.VMEM((tm, d), BF16), pltpu.VMEM((tm, d), F32)],
        compiler_params=_params("parallel", "arbitrary"),
        name="ffn",
    )(*args)


TOK_ROWS = 8


def _router_body(x_ref, g_ref, whi_ref, wlo_ref, gate_ref, tok_ref, cnt_ref, carry_ref, *, ne, tm):
    @pl.when(pl.program_id(0) == 0)
    def _():
        carry_ref[...] = jnp.zeros_like(carry_ref)

    h = _rms(x_ref[...], g_ref[...])
    h_hi = h.astype(BF16)
    h_lo = (h - h_hi.astype(F32)).astype(BF16)
    logits = _dot(h_hi, whi_ref[...]) + _dot(h_lo, whi_ref[...]) + _dot(h_hi, wlo_ref[...])
    lane = lax.broadcasted_iota(jnp.int32, logits.shape, 1).astype(F32)
    neg = -jnp.inf
    l1 = jnp.where(lane < ne, logits, neg)
    m1 = jnp.max(l1, axis=-1, keepdims=True)
    i1 = jnp.min(jnp.where(l1 == m1, lane, float(V7X_LANES)), axis=-1, keepdims=True)
    l2 = jnp.where(lane == i1, neg, l1)
    m2 = jnp.max(l2, axis=-1, keepdims=True)
    i2 = jnp.min(jnp.where(l2 == m2, lane, float(V7X_LANES)), axis=-1, keepdims=True)
    e2 = jnp.exp(m2 - m1)
    den = 1.0 + e2
    gate_ref[...] = jnp.where(lane == 0.0, 1.0 / den, 0.0) + jnp.where(lane == 1.0, e2 / den, 0.0)

    sel1, sel2 = lane == i1, lane == i2
    onehot = jnp.where(sel1 | sel2, 1.0, 0.0)
    row = lax.broadcasted_iota(jnp.int32, (tm, tm), 0)
    col = lax.broadcasted_iota(jnp.int32, (tm, tm), 1)
    before = carry_ref[...] + _dot((col < row).astype(BF16), onehot.astype(BF16))
    carry_ref[...] = before[tm - 1 : tm, :] + onehot[tm - 1 : tm, :]
    cnt_ref[...] = carry_ref[...]
    r1 = jnp.sum(jnp.where(sel1, before, 0.0), axis=-1, keepdims=True)
    r2 = jnp.sum(jnp.where(sel2, before, 0.0), axis=-1, keepdims=True)
    table = (jnp.where(lane == 0.0, i1, 0.0) + jnp.where(lane == 1.0, i2, 0.0)
             + jnp.where(lane == 2.0, r1, 0.0) + jnp.where(lane == 3.0, r2, 0.0))
    pick = (lax.broadcasted_iota(jnp.int32, (TOK_ROWS, V7X_LANES), 0) == lax.broadcasted_iota(jnp.int32, (TOK_ROWS, V7X_LANES), 1)).astype(BF16)
    hi, mid, lo = _split3(table)
    tok_ref[...] = (_dot_nt(pick, hi) + _dot_nt(pick, mid) + _dot_nt(pick, lo)).astype(jnp.int32)


def moe_router(x, g, w_router, tm=1024):
    n, d = x.shape
    ne = w_router.shape[1]
    tm = min(tm, n)
    wpad = jnp.pad(w_router, ((0, 0), (0, V7X_LANES - ne)))
    w_hi = wpad.astype(BF16)
    w_lo = (wpad - w_hi.astype(F32)).astype(BF16)
    return pl.pallas_call(
        functools.partial(_router_body, ne=ne, tm=tm),
        grid=(n // tm,),
        in_specs=[
            pl.BlockSpec((tm, d), lambda i: (i, 0)),
            pl.BlockSpec((1, d), lambda i: (0, 0)),
            pl.BlockSpec((d, V7X_LANES), lambda i: (0, 0)),
            pl.BlockSpec((d, V7X_LANES), lambda i: (0, 0)),
        ],
        out_specs=[
            pl.BlockSpec((tm, V7X_LANES), lambda i: (i, 0)),
            pl.BlockSpec((TOK_ROWS, tm), lambda i: (0, i)),
            pl.BlockSpec((1, V7X_LANES), lambda i: (0, 0)),
        ],
        out_shape=[
            jax.ShapeDtypeStruct((n, V7X_LANES), F32),
            jax.ShapeDtypeStruct((TOK_ROWS, n), jnp.int32),
            jax.ShapeDtypeStruct((1, V7X_LANES), F32),
        ],
        scratch_shapes=[pltpu.VMEM((1, V7X_LANES), F32)],
        compiler_params=_params("arbitrary"),
        name="moe_router",
    )(x, g, w_hi, w_lo)


def _plan_body(cnt_ref, tok_ref, off_ref, te_ref, tv_ref, nv_ref, pos_ref, *, ne, tg, n_tiles):
    off = jnp.int32(0)
    tile = jnp.int32(0)
    expert = tok_ref[0:TOP_K, :]
    group_start = jnp.zeros_like(expert)
    for e in range(ne):
        nt = (cnt_ref[e] + (tg - 1)) // tg
        off_ref[e] = off
        group_start = jnp.where(expert == e, off, group_start)

        def mark(ti, carry, e=e, first=tile):
            te_ref[ti] = jnp.int32(e)
            tv_ref[ti] = jnp.minimum(cnt_ref[e] - (ti - first) * tg, tg)
            return carry

        lax.fori_loop(tile, tile + nt, mark, 0)
        off = off + nt * tg
        tile = tile + nt
    pos_ref[...] = jnp.zeros_like(pos_ref)
    pos_ref[0:TOP_K, :] = group_start + tok_ref[TOP_K : 2 * TOP_K, :]
    nv_ref[0] = tile
    last = te_ref[jnp.maximum(tile - 1, 0)]

    def fill(ti, carry):
        te_ref[ti] = last
        tv_ref[ti] = jnp.int32(0)
        return carry

    lax.fori_loop(tile, n_tiles, fill, 0)


def moe_plan(cnt, tok, tg, n_tiles):
    ne = cnt.shape[0]
    smem = pl.BlockSpec(memory_space=pltpu.SMEM)
    vmem = pl.BlockSpec(memory_space=pltpu.VMEM)
    return pl.pallas_call(
        functools.partial(_plan_body, ne=ne, tg=tg, n_tiles=n_tiles),
        in_specs=[smem, vmem],
        out_specs=[smem, smem, smem, smem, vmem],
        out_shape=[
            jax.ShapeDtypeStruct((ne,), jnp.int32),
            jax.ShapeDtypeStruct((n_tiles,), jnp.int32),
            jax.ShapeDtypeStruct((n_tiles,), jnp.int32),
            jax.ShapeDtypeStruct((1,), jnp.int32),
            jax.ShapeDtypeStruct(tok.shape, jnp.int32),
        ],
        name="moe_plan",
    )(cnt, tok)


def _token_copy(src, s, dst, r, sem, ns):
    return pltpu.make_async_copy(src.at[pl.ds(pl.multiple_of(s * ns, ns), ns)], dst.at[pl.ds(pl.multiple_of(r * ns, ns), ns)], sem)


def _to_slabs(dst_ref, val, ns):
    rows = val.shape[0]
    for c in range(ns):
        dst_ref[pl.ds(c, rows, stride=ns), :] = val[:, c * V7X_LANES : (c + 1) * V7X_LANES]


def _slab(src_ref, c, rows, ns):
    return src_ref[pl.ds(c, rows, stride=ns), :]


def _dispatch_body(pos_hbm, off_ref, cnt_ref, x_ref, g_ref, xs_hbm, pos_smem, x3_ref, zero_ref, sem_idx, sem_row, *, td, tg, ne, n_tiles, ns):
    base = pl.multiple_of(pl.program_id(0) * td, td)
    idx = pltpu.make_async_copy(pos_hbm.at[:, pl.ds(base, td)], pos_smem, sem_idx)
    idx.start()
    _to_slabs(x3_ref, _rms(x_ref[...], g_ref[...]), ns)

    @pl.when(pl.program_id(0) == 0)
    def _():
        zero_ref[...] = jnp.zeros_like(zero_ref)
        for e in range(ne):
            start = off_ref[e] + cnt_ref[e]
            stop = off_ref[e] + (cnt_ref[e] + (tg - 1)) // tg * tg

            def zero_start(r, carry):
                _token_copy(zero_ref, 0, xs_hbm, r, sem_row, ns).start()
                return carry

            def zero_wait(r, carry):
                _token_copy(zero_ref, 0, xs_hbm, r, sem_row, ns).wait()
                return carry

            lax.fori_loop(start, stop, zero_start, 0)
            lax.fori_loop(start, stop, zero_wait, 0)

        def tile_copy(ti):
            return pltpu.make_async_copy(zero_ref, xs_hbm.at[pl.ds(pl.multiple_of(ti * (tg * ns), tg * ns), tg * ns)], sem_row)

        def tile_start(ti, carry):
            tile_copy(ti).start()
            return carry

        def tile_wait(ti, carry):
            tile_copy(ti).wait()
            return carry

        used = stop // tg
        lax.fori_loop(used, n_tiles, tile_start, 0)
        lax.fori_loop(used, n_tiles, tile_wait, 0)

    idx.wait()

    def start(t, c):
        for k in range(TOP_K):
            _token_copy(x3_ref, t, xs_hbm, pos_smem[k, t], sem_row, ns).start()
        return c

    lax.fori_loop(0, td, start, 0, unroll=8)
    for _ in range(TOP_K):
        pltpu.make_async_copy(x3_ref, xs_hbm.at[pl.ds(0, td * ns)], sem_row).wait()


def moe_dispatch(pos, off, cnt, x, g, tg, n_tiles, td=1024):
    n, d = x.shape
    ns = d // V7X_LANES
    assert ns % V7X_SUBLANES == 0
    td = min(td, n)
    smem = pl.BlockSpec(memory_space=pltpu.SMEM)
    hbm = pl.BlockSpec(memory_space=pl.ANY)
    return pl.pallas_call(
        functools.partial(_dispatch_body, td=td, tg=tg, ne=cnt.shape[0], n_tiles=n_tiles, ns=ns),
        grid=(n // td,),
        in_specs=[hbm, smem, smem, pl.BlockSpec((td, d), lambda i: (i, 0)), pl.BlockSpec((1, d), lambda i: (0, 0))],
        out_specs=hbm,
        out_shape=jax.ShapeDtypeStruct((n_tiles * tg * ns, V7X_LANES), x.dtype),
        scratch_shapes=[
            pltpu.SMEM((TOK_ROWS, td), jnp.int32),
            pltpu.VMEM((td * ns, V7X_LANES), x.dtype),
            pltpu.VMEM((tg * ns, V7X_LANES), x.dtype),
            pltpu.SemaphoreType.DMA(()),
            pltpu.SemaphoreType.DMA(()),
        ],
        compiler_params=pltpu.CompilerParams(dimension_semantics=("arbitrary",), has_side_effects=True, vmem_limit_bytes=V7X_VMEM_LIMIT_BYTES),
        name="moe_dispatch",
    )(pos, off, cnt, x, g)


GROUP_SHORT_TILE_DIV = 4


def _grouped_ffn_body(te_ref, tv_ref, nv_ref, x_ref, wg_ref, wu_ref, wo_ref, o_ref, h_ref, acc_ref, *, tg, ns):
    i, j = pl.program_id(0), pl.program_id(1)
    d = ns * V7X_LANES

    @pl.when(i < nv_ref[0])
    def _():
        @pl.when(j == 0)
        def _():
            for c in range(ns):
                h_ref[:, c * V7X_LANES : (c + 1) * V7X_LANES] = _slab(x_ref, c, tg, ns).astype(BF16)

        def accumulate(m):
            y = _swiglu_step(h_ref[0:m, :], wg_ref, wu_ref, wo_ref)

            @pl.when(j == 0)
            def _():
                acc_ref[0:m, :] = y
                if m < tg:
                    acc_ref[m:, :] = jnp.zeros((tg - m, d), F32)

            @pl.when(j > 0)
            def _():
                acc_ref[0:m, :] += y

        short = tv_ref[i] <= tg // GROUP_SHORT_TILE_DIV
        pl.when(short)(lambda: accumulate(tg // GROUP_SHORT_TILE_DIV))
        pl.when(jnp.logical_not(short))(lambda: accumulate(tg))

        @pl.when(j == pl.num_programs(1) - 1)
        def _():
            _to_slabs(o_ref, acc_ref[...], ns)

    @pl.when((i >= nv_ref[0]) & (j == 0))
    def _():
        o_ref[...] = jnp.zeros_like(o_ref)


def moe_grouped_ffn(te, tv, nv, xs, w_in, w_out, tg, tf=1024):
    d = w_in.shape[1]
    ns = d // V7X_LANES
    r = xs.shape[0] // ns
    f = w_out.shape[1]
    tf = _block(f, tf)
    nf = f // tf
    n_tiles = r // tg

    def tile(i, nv):
        return jnp.minimum(i, nv[0] - 1)

    def fblk(i, j, nv):
        return jnp.where(i < nv[0], j, nf - 1)

    grid_spec = pltpu.PrefetchScalarGridSpec(
        num_scalar_prefetch=3,
        grid=(n_tiles, nf),
        in_specs=[
            pl.BlockSpec((tg * ns, V7X_LANES), lambda i, j, te, tv, nv: (tile(i, nv), 0)),
            pl.BlockSpec((None, d, tf), lambda i, j, te, tv, nv: (te[i], 0, fblk(i, j, nv))),
            pl.BlockSpec((None, d, tf), lambda i, j, te, tv, nv: (te[i], 0, fblk(i, j, nv) + nf)),
            pl.BlockSpec((None, tf, d), lambda i, j, te, tv, nv: (te[i], fblk(i, j, nv), 0)),
        ],
        out_specs=pl.BlockSpec((tg * ns, V7X_LANES), lambda i, j, te, tv, nv: (i, 0)),
        scratch_shapes=[pltpu.VMEM((tg, d), BF16), pltpu.VMEM((tg, d), F32)],
    )
    return pl.pallas_call(
        functools.partial(_grouped_ffn_body, tg=tg, ns=ns),
        grid_spec=grid_spec,
        out_shape=jax.ShapeDtypeStruct((r * ns, V7X_LANES), F32),
        compiler_params=_params("parallel", "arbitrary"),
        name="moe_grouped_ffn",
    )(te, tv, nv, xs, w_in, w_in, w_out)


def _combine_body(pos_hbm, gate_ref, x_ref, ys_hbm, *rest, tc, ns, final):
    rest = list(rest)
    fg_ref = rest.pop(0) if final else None
    o_ref, pos_smem, buf_ref, sem_idx, sem_row = rest
    base = pl.multiple_of(pl.program_id(0) * tc, tc)
    idx = pltpu.make_async_copy(pos_hbm.at[:, pl.ds(base, tc)], pos_smem, sem_idx)
    idx.start()
    idx.wait()

    def start(t, c):
        for k in range(TOP_K):
            _token_copy(ys_hbm, pos_smem[k, t], buf_ref.at[k], t, sem_row, ns).start()
        return c

    lax.fori_loop(0, tc, start, 0, unroll=8)
    for k in range(TOP_K):
        pltpu.make_async_copy(ys_hbm.at[pl.ds(0, tc * ns)], buf_ref.at[k], sem_row).wait()
    g0, g1 = gate_ref[:, 0:1], gate_ref[:, 1:2]
    for c in range(ns):
        cols = slice(c * V7X_LANES, (c + 1) * V7X_LANES)
        o_ref[:, cols] = x_ref[:, cols] + g0 * _slab(buf_ref.at[0], c, tc, ns) + g1 * _slab(buf_ref.at[1], c, tc, ns)
    if final:
        o_ref[...] = _rms(o_ref[...], fg_ref[...])


def moe_combine(pos, gates, x, ys, final_g=None, tc=512):
    n, d = x.shape
    ns = d // V7X_LANES
    tc = min(tc, n)
    final = final_g is not None
    hbm = pl.BlockSpec(memory_space=pl.ANY)
    in_specs = [hbm, pl.BlockSpec((tc, V7X_LANES), lambda i: (i, 0)), pl.BlockSpec((tc, d), lambda i: (i, 0)), hbm]
    args = [pos, gates, x, ys]
    if final:
        in_specs.append(pl.BlockSpec((1, d), lambda i: (0, 0)))
        args.append(final_g)
    return pl.pallas_call(
        functools.partial(_combine_body, tc=tc, ns=ns, final=final),
        grid=(n // tc,),
        in_specs=in_specs,
        out_specs=pl.BlockSpec((tc, d), lambda i: (i, 0)),
        out_shape=jax.ShapeDtypeStruct((n, d), F32),
        scratch_shapes=[
            pltpu.SMEM((TOK_ROWS, tc), jnp.int32),
            pltpu.VMEM((TOP_K, tc * ns, V7X_LANES), F32),
            pltpu.SemaphoreType.DMA(()),
            pltpu.SemaphoreType.DMA(()),
        ],
        compiler_params=_params("arbitrary"),
        name="moe_combine",
    )(*args)


def moe_ffn(x, g, w_router, w_in, w_out, final_g=None, tg=1024):
    n, _ = x.shape
    ne = w_router.shape[1]
    tg = min(tg, n)
    n_tiles = TOP_K * n // tg + ne
    gates, tok, counts = moe_router(x, g, w_router)
    cnt = counts[0, :ne].astype(jnp.int32)
    off, te, tv, nv, pos = moe_plan(cnt, tok, tg, n_tiles)
    xs = moe_dispatch(pos, off, cnt, x, g, tg, n_tiles)
    ys = moe_grouped_ffn(te, tv, nv, xs, w_in, w_out, tg)
    return moe_combine(pos, gates, x, ys, final_g)


def _gla_gate_body(x_ref, g_ref, wl_ref, w2_ref, b_ref, o_ref):
    h = _rms(x_ref[...], g_ref[...]).astype(BF16)
    g_low = _dot(h, wl_ref[...]).astype(BF16)
    o_ref[...] = _log_sigmoid(_dot(g_low, w2_ref[...]) + b_ref[...]) / B_TAU


def gla_gate(x, g, w_low, w_gate2, bias, tm=1024):
    n, d = x.shape
    dk = w_gate2.shape[1]
    tm = min(tm, n)
    wl = jnp.pad(w_low, ((0, 0), (0, V7X_LANES - B_GATE_RANK))).astype(BF16)
    w2 = jnp.pad(w_gate2, ((0, V7X_LANES - B_GATE_RANK), (0, 0))).astype(BF16)
    return pl.pallas_call(
        _gla_gate_body,
        grid=(n // tm,),
        in_specs=[
            pl.BlockSpec((tm, d), lambda i: (i, 0)),
            pl.BlockSpec((1, d), lambda i: (0, 0)),
            pl.BlockSpec((d, V7X_LANES), lambda i: (0, 0)),
            pl.BlockSpec((V7X_LANES, dk), lambda i: (0, 0)),
            pl.BlockSpec((1, dk), lambda i: (0, 0)),
        ],
        out_specs=pl.BlockSpec((tm, dk), lambda i: (i, 0)),
        out_shape=jax.ShapeDtypeStruct((n, dk), F32),
        compiler_params=_params("parallel"),
        name="gla_gate",
    )(x, g, wl, w2, bias)


def _gla_body(q_ref, k_ref, v_ref, r_ref, la_ref, og_ref, o_ref, st_ref, *, tc, hk, hv):
    @pl.when(pl.program_id(1) == 0)
    def _():
        st_ref[...] = jnp.zeros_like(st_ref)

    c = B_CHUNK
    tril = _tril_mask(c)
    ones_tril = tril.astype(BF16)
    scale = hk**-0.5
    for ci in range(tc // c):
        rs = slice(ci * c, (ci + 1) * c)
        hi, mid, lo = _split3(la_ref[rs, :])
        bcum = _dot(ones_tril, hi) + _dot(ones_tril, mid) + _dot(ones_tril, lo)
        b_last = bcum[c - 1 : c, :]
        e_pos = jnp.exp(bcum)
        e_neg = jnp.exp(-bcum)
        e_end = jnp.exp(b_last - bcum)
        dec = jnp.exp(b_last)
        for h in range(B_HEADS):
            ks = slice(h * hk, (h + 1) * hk)
            vs = slice(h * hv, (h + 1) * hv)
            q = q_ref[rs, ks].astype(F32) * scale
            k = k_ref[rs, ks].astype(F32)
            v = v_ref[rs, vs]
            q_d = (q * e_pos[:, ks]).astype(BF16)
            k_d = (k * e_neg[:, ks]).astype(BF16)
            k_end = (k * e_end[:, ks]).astype(BF16)
            att = jnp.where(tril, _dot_nt(q_d, k_d), 0.0)
            st = st_ref[h]
            o = _dot(att.astype(BF16), v) + _dot_nt(q_d, st.astype(BF16))
            st_ref[h] = dec[:, ks] * st + _dot_tn(v, k_end)
            y = _rms(o, og_ref[...]).astype(BF16).astype(F32)
            r = r_ref[rs, vs].astype(F32)
            o_ref[rs, vs] = (y * (r * jax.nn.sigmoid(r))).astype(BF16)


def gla_chunks(proj, la, o_g, bsz, tc=256):
    n = proj.shape[0]
    dk = la.shape[1]
    dv = (proj.shape[1] - 2 * dk) // 2
    s = n // bsz
    tc = min(tc, s)
    nt = s // tc
    hk, hv = dk // B_HEADS, dv // B_HEADS
    assert (2 * dk) % dv == 0
    v_blk = 2 * dk // dv
    row = lambda b, t: b * nt + t
    return pl.pallas_call(
        functools.partial(_gla_body, tc=tc, hk=hk, hv=hv),
        grid=(bsz, nt),
        in_specs=[
            pl.BlockSpec((tc, dk), lambda b, t: (row(b, t), 0)),
            pl.BlockSpec((tc, dk), lambda b, t: (row(b, t), 1)),
            pl.BlockSpec((tc, dv), lambda b, t: (row(b, t), v_blk)),
            pl.BlockSpec((tc, dv), lambda b, t: (row(b, t), v_blk + 1)),
            pl.BlockSpec((tc, dk), lambda b, t: (row(b, t), 0)),
            pl.BlockSpec((1, hv), lambda b, t: (0, 0)),
        ],
        out_specs=pl.BlockSpec((tc, dv), lambda b, t: (row(b, t), 0)),
        out_shape=jax.ShapeDtypeStruct((n, dv), BF16),
        scratch_shapes=[pltpu.VMEM((B_HEADS, hv, hk), F32)],
        compiler_params=_params("parallel", "arbitrary"),
        name="gla_chunks",
    )(proj, proj, proj, proj, la, o_g)


def _fox_gate_body(x_ref, g_ref, wf_ref, b_ref, crow_ref, carry_ref, *, tb, nh):
    @pl.when(pl.program_id(1) == 0)
    def _():
        carry_ref[...] = jnp.zeros_like(carry_ref)

    h = _rms(x_ref[...], g_ref[...]).astype(BF16)
    lane = lax.broadcasted_iota(jnp.int32, (tb, V7X_LANES), 1)
    log_f = jnp.where(lane < nh, _log_sigmoid(_dot(h, wf_ref[...]) + b_ref[...]), 0.0)
    ones_tril = _tril_mask(tb).astype(BF16)
    hi, mid, lo = _split3(log_f)
    c = carry_ref[...] + (_dot(ones_tril, hi) + _dot(ones_tril, mid) + _dot(ones_tril, lo))
    carry_ref[...] = c[tb - 1 : tb, :]
    sel = (lax.broadcasted_iota(jnp.int32, (nh, V7X_LANES), 0) == lax.broadcasted_iota(jnp.int32, (nh, V7X_LANES), 1)).astype(BF16)
    hi, mid, lo = _split3(c)
    crow_ref[...] = _dot_nt(sel, hi) + _dot_nt(sel, mid) + _dot_nt(sel, lo)


def fox_gate(x, g, w_f, f_bias, bsz, tb=512):
    n, d = x.shape
    nh = w_f.shape[1]
    s = n // bsz
    tb = min(tb, s)
    nt = s // tb
    wf = jnp.pad(w_f, ((0, 0), (0, V7X_LANES - nh))).astype(BF16)
    bias = jnp.pad(f_bias.reshape(1, nh), ((0, 0), (0, V7X_LANES - nh)))
    return pl.pallas_call(
        functools.partial(_fox_gate_body, tb=tb, nh=nh),
        grid=(bsz, nt),
        in_specs=[
            pl.BlockSpec((tb, d), lambda b, t: (b * nt + t, 0)),
            pl.BlockSpec((1, d), lambda b, t: (0, 0)),
            pl.BlockSpec((d, V7X_LANES), lambda b, t: (0, 0)),
            pl.BlockSpec((1, V7X_LANES), lambda b, t: (0, 0)),
        ],
        out_specs=pl.BlockSpec((None, nh, tb), lambda b, t: (b, 0, t)),
        out_shape=jax.ShapeDtypeStruct((bsz, nh, s), F32),
        scratch_shapes=[pltpu.VMEM((1, V7X_LANES), F32)],
        compiler_params=_params("parallel", "arbitrary"),
        name="fox_gate",
    )(x, g, wf, bias)


FOX_KEY_BLOCK = 512
FOX_SUBTILES = 8
FOX_ROW_CHUNK = 32


def _fox_flash_body(q_ref, k_ref, v_ref, crow_ref, o_ref, qs_ref, va_ref, s_ref, p_ref, al_ref, m_ref, acc_ref, *, t, nsub, scale):
    i = pl.program_id(2)
    dh = q_ref.shape[1]
    nl = t // V7X_LANES

    @pl.when(i == 0)
    def _():
        va_ref[:, :dh] = v_ref[...]
        va_ref[:, dh:] = jnp.ones((va_ref.shape[0], V7X_LANES), BF16)

    qs_ref[...] = (q_ref[...].astype(F32) * scale).astype(BF16)
    m_ref[...] = jnp.full_like(m_ref, -jnp.inf)
    acc_ref[...] = jnp.zeros_like(acc_ref)

    def block(sub, kb, masked):
        rows = slice(sub * t, (sub + 1) * t)
        off = pl.multiple_of(kb * t, t)
        s_ref[sub] = _dot_nt(qs_ref[rows, :], k_ref[pl.ds(off, t), :])
        bias = crow_ref[pl.ds(kb, 1), :]

        def chunk(c, carry):
            r = pl.multiple_of(c * FOX_ROW_CHUNK, FOX_ROW_CHUNK)
            sc = s_ref[sub, pl.ds(r, FOX_ROW_CHUNK), :] - bias
            if masked:
                row = r + lax.broadcasted_iota(jnp.int32, sc.shape, 0)
                col = lax.broadcasted_iota(jnp.int32, sc.shape, 1)
                sc = jnp.where(col <= row, sc, -jnp.inf)
            tiles = [sc[:, a * V7X_LANES : (a + 1) * V7X_LANES] for a in range(nl)]
            mx = functools.reduce(jnp.maximum, tiles)
            m_old = m_ref[pl.ds(sub * t + r, FOX_ROW_CHUNK), :]
            m_new = jnp.maximum(m_old, jnp.max(mx, axis=-1, keepdims=True))
            al_ref[sub, pl.ds(r, FOX_ROW_CHUNK), :] = jnp.exp(m_old - m_new)
            m_ref[pl.ds(sub * t + r, FOX_ROW_CHUNK), :] = m_new
            for a in range(nl):
                p_ref[sub, pl.ds(r, FOX_ROW_CHUNK), a * V7X_LANES : (a + 1) * V7X_LANES] = jnp.exp(tiles[a] - m_new).astype(BF16)
            return carry

        lax.fori_loop(0, t // FOX_ROW_CHUNK, chunk, 0, unroll=True)
        pv = _dot(p_ref[sub], va_ref[pl.ds(off, t), :])
        alpha = al_ref[sub]
        acc_ref[rows, :dh] = alpha * acc_ref[rows, :dh] + pv[:, :dh]
        acc_ref[rows, dh:] = alpha * acc_ref[rows, dh:] + pv[:, dh:]

    def below_diagonal(kb, carry):
        for sub in range(nsub):
            block(sub, kb, False)
        return carry

    first = i * nsub
    lax.fori_loop(0, first, below_diagonal, 0)
    for kb in range(nsub):
        for sub in range(kb, nsub):
            block(sub, first + kb, masked=(sub == kb))
    o_ref[...] = (acc_ref[:, :dh] / acc_ref[:, dh:]).astype(o_ref.dtype)


def fox_flash(proj, crow, bsz, nh):
    n = proj.shape[0]
    dh = proj.shape[1] // (3 * nh)
    assert dh == V7X_LANES
    s = n // bsz
    t = min(FOX_KEY_BLOCK, s)
    nsub = min(FOX_SUBTILES, s // t)
    tq = t * nsub
    nq = s // tq
    crow = crow.reshape(bsz, nh, s // t, t)
    return pl.pallas_call(
        functools.partial(_fox_flash_body, t=t, nsub=nsub, scale=dh**-0.5),
        grid=(bsz, nh, nq),
        in_specs=[
            pl.BlockSpec((tq, dh), lambda b, h, i: (b * nq + i, h)),
            pl.BlockSpec((s, dh), lambda b, h, i: (b, nh + h)),
            pl.BlockSpec((s, dh), lambda b, h, i: (b, 2 * nh + h)),
            pl.BlockSpec((None, None, s // t, t), lambda b, h, i: (b, h, 0, 0)),
        ],
        out_specs=pl.BlockSpec((tq, dh), lambda b, h, i: (b * nq + i, h)),
        out_shape=jax.ShapeDtypeStruct((n, nh * dh), BF16),
        scratch_shapes=[
            pltpu.VMEM((tq, dh), BF16),
            pltpu.VMEM((s, dh + V7X_LANES), BF16),
            pltpu.VMEM((nsub, t, t), F32),
            pltpu.VMEM((nsub, t, t), BF16),
            pltpu.VMEM((nsub, t, V7X_LANES), F32),
            pltpu.VMEM((tq, V7X_LANES), F32),
            pltpu.VMEM((tq, dh + V7X_LANES), F32),
        ],
        compiler_params=_params("parallel", "parallel", "arbitrary"),
        name="fox_flash",
    )(proj, proj, proj, crow)


def kernel(x, norm1_g, norm2_g, a_w_in, a_vnorm_g, a_w_s, a_b_s, a_w_out, b_w_in, b_w_gate2, b_gate_bias, b_onorm_g, b_w_out, c_w_in, c_f_bias, c_w_out, ffn_w_in, ffn_w_out, moe_router, moe_w_in, moe_w_out, final_g):
    bsz, s, d = x.shape
    depth = norm1_g.shape[0]
    xf = x.reshape(bsz * s, d)
    for i in range(depth):
        g1 = norm1_g[i].reshape(1, d)
        m, j = i % N_MIXERS, i // N_MIXERS
        if m == 0:
            z = norm_matmul(xf, g1, a_w_in[j].astype(BF16), act="gelu")
            xf = gmlp_spatial_out(z, xf, a_vnorm_g[j].reshape(1, -1), a_w_s[j], a_b_s[j].T, a_w_out[j].astype(BF16))
        elif m == 1:
            dk = b_w_gate2.shape[2]
            n_main = b_w_in.shape[2] - B_GATE_RANK
            proj = norm_matmul(xf, g1, b_w_in[j, :, :n_main].astype(BF16))
            la = gla_gate(xf, g1, b_w_in[j, :, n_main:], b_w_gate2[j], b_gate_bias[j].reshape(1, dk))
            o = gla_chunks(proj, la, b_onorm_g[j].reshape(1, -1), bsz)
            xf = matmul_residual(o, b_w_out[j].astype(BF16), xf)
        else:
            proj = norm_matmul(xf, g1, c_w_in[j, :, : 3 * d].astype(BF16))
            crow = fox_gate(xf, g1, c_w_in[j, :, 3 * d :], c_f_bias[j], bsz)
            o = fox_flash(proj, crow, bsz, C_HEADS)
            xf = matmul_residual(o, c_w_out[j].astype(BF16), xf)
        g2 = norm2_g[i].reshape(1, d)
        fg = final_g.reshape(1, d) if i == depth - 1 else None
        if i % 2 == 0:
            xf = ffn(xf, g2, ffn_w_in[i // 2].astype(BF16), ffn_w_out[i // 2].astype(BF16), final_g=fg)
        else:
            xf = moe_ffn(xf, g2, moe_router[i // 2], moe_w_in[i // 2].astype(BF16), moe_w_out[i // 2].astype(BF16), final_g=fg)
    return xf.reshape(bsz, s, d)
```

```python
import functools

import jax
import jax.numpy as jnp
from jax import lax
from jax.experimental import pallas as pl
from jax.experimental.pallas import tpu as pltpu

F32 = jnp.float32
BF16 = jnp.bfloat16

EPS = 1e-6
N_MIXERS = 3
TOP_K = 2
A_CHUNK = 128
A_GROUPS = 8
B_HEADS = 4
B_GATE_RANK = 16
B_TAU = 16.0
B_CHUNK = 64
C_HEADS = 8

V7X_LANES = 128
V7X_SUBLANES = 8
V7X_VMEM_LIMIT_BYTES = 56 * 1024 * 1024


def _params(*sem):
    return pltpu.CompilerParams(dimension_semantics=sem, vmem_limit_bytes=V7X_VMEM_LIMIT_BYTES)


def _block(total, target, align=V7X_LANES):
    if total <= target:
        return total
    return max(b for b in range(align, target + 1, align) if total % b == 0)


def _rms(x, g):
    ms = jnp.mean(x * x, axis=-1, keepdims=True)
    return x * lax.rsqrt(ms + EPS) * g


def _log_sigmoid(x):
    return jnp.minimum(x, 0.0) - jnp.log1p(jnp.exp(-jnp.abs(x)))


def _split3(x):
    hi = x.astype(BF16)
    r1 = x - hi.astype(F32)
    mid = r1.astype(BF16)
    lo = (r1 - mid.astype(F32)).astype(BF16)
    return hi, mid, lo


def _dot(a, b):
    return jnp.dot(a, b, preferred_element_type=F32)


def _dot_nt(a, b):
    return lax.dot_general(a, b, (((1,), (1,)), ((), ())), preferred_element_type=F32)


def _dot_tn(a, b):
    return lax.dot_general(a, b, (((0,), (0,)), ((), ())), preferred_element_type=F32)


def _tril_mask(n):
    row = lax.broadcasted_iota(jnp.int32, (n, n), 0)
    col = lax.broadcasted_iota(jnp.int32, (n, n), 1)
    return col <= row


def _norm_matmul_body(x_ref, g_ref, w_ref, o_ref, h_ref, *, act):
    @pl.when(pl.program_id(1) == 0)
    def _():
        h_ref[...] = _rms(x_ref[...], g_ref[...]).astype(BF16)

    y = _dot(h_ref[...], w_ref[...])
    if act == "gelu":
        y = jax.nn.gelu(y, approximate=True)
    o_ref[...] = y.astype(o_ref.dtype)


def norm_matmul(x, g, w, act=None, tm=1024, tn=1024):
    n, d = x.shape
    nout = w.shape[1]
    tm, tn = min(tm, n), _block(nout, tn)
    return pl.pallas_call(
        functools.partial(_norm_matmul_body, act=act),
        grid=(n // tm, nout // tn),
        in_specs=[
            pl.BlockSpec((tm, d), lambda i, j: (i, 0)),
            pl.BlockSpec((1, d), lambda i, j: (0, 0)),
            pl.BlockSpec((d, tn), lambda i, j: (0, j)),
        ],
        out_specs=pl.BlockSpec((tm, tn), lambda i, j: (i, j)),
        out_shape=jax.ShapeDtypeStruct((n, nout), BF16),
        scratch_shapes=[pltpu.VMEM((tm, d), BF16)],
        compiler_params=_params("parallel", "arbitrary"),
        name="norm_matmul",
    )(x, g, w)


def _matmul_residual_body(y_ref, w_ref, x_ref, o_ref):
    o_ref[...] = x_ref[...] + _dot(y_ref[...], w_ref[...])


def matmul_residual(y, w, x, tm=512):
    n, k = y.shape
    d = w.shape[1]
    tm = min(tm, n)
    return pl.pallas_call(
        _matmul_residual_body,
        grid=(n // tm,),
        in_specs=[
            pl.BlockSpec((tm, k), lambda i: (i, 0)),
            pl.BlockSpec((k, d), lambda i: (0, 0)),
            pl.BlockSpec((tm, d), lambda i: (i, 0)),
        ],
        out_specs=pl.BlockSpec((tm, d), lambda i: (i, 0)),
        out_shape=jax.ShapeDtypeStruct((n, d), F32),
        compiler_params=_params("parallel"),
        name="matmul_residual",
    )(y, w, x)


def _gmlp_body(z_ref, x_ref, vg_ref, ws_ref, bt_ref, wo_ref, o_ref, vn_ref, y_ref, *, tm, width):
    gd = width // A_GROUPS
    vn_ref[...] = _rms(z_ref[:, width:].astype(F32), vg_ref[...]).astype(BF16)
    tril = _tril_mask(A_CHUNK)
    for g in range(A_GROUPS):
        w = jnp.where(tril, ws_ref[g], 0.0).astype(BF16)
        bias = bt_ref[:, g : g + 1]
        cs = slice(g * gd, (g + 1) * gd)
        for c in range(tm // A_CHUNK):
            rs = slice(c * A_CHUNK, (c + 1) * A_CHUNK)
            mixed = _dot(w, vn_ref[rs, cs]) + bias
            y_ref[rs, cs] = (z_ref[rs, cs].astype(F32) * mixed).astype(BF16)
    o_ref[...] = x_ref[...] + _dot(y_ref[...], wo_ref[...])


def gmlp_spatial_out(z, x, vg, ws, bt, wo, tm=512):
    n, d = x.shape
    width = z.shape[1] // 2
    tm = min(tm, n)
    return pl.pallas_call(
        functools.partial(_gmlp_body, tm=tm, width=width),
        grid=(n // tm,),
        in_specs=[
            pl.BlockSpec((tm, 2 * width), lambda i: (i, 0)),
            pl.BlockSpec((tm, d), lambda i: (i, 0)),
            pl.BlockSpec((1, width), lambda i: (0, 0)),
            pl.BlockSpec((A_GROUPS, A_CHUNK, A_CHUNK), lambda i: (0, 0, 0)),
            pl.BlockSpec((A_CHUNK, A_GROUPS), lambda i: (0, 0)),
            pl.BlockSpec((width, d), lambda i: (0, 0)),
        ],
        out_specs=pl.BlockSpec((tm, d), lambda i: (i, 0)),
        out_shape=jax.ShapeDtypeStruct((n, d), F32),
        scratch_shapes=[pltpu.VMEM((tm, width), BF16), pltpu.VMEM((tm, width), BF16)],
        compiler_params=_params("parallel"),
        name="gmlp_spatial_out",
    )(z, x, vg, ws, bt, wo)


def _swiglu_step(h, wg_ref, wu_ref, wo_ref):
    a = _dot(h, wg_ref[...])
    t = (a * jax.nn.sigmoid(a)) * _dot(h, wu_ref[...])
    return _dot(t.astype(BF16), wo_ref[...])


def _ffn_body(*refs, final):
    x_ref, g_ref, wg_ref, wu_ref, wo_ref = refs[:5]
    rest = list(refs[5:])
    fg_ref = rest.pop(0) if final else None
    o_ref, h_ref, acc_ref = rest
    j = pl.program_id(1)

    @pl.when(j == 0)
    def _():
        h_ref[...] = _rms(x_ref[...], g_ref[...]).astype(BF16)
        acc_ref[...] = jnp.zeros_like(acc_ref)

    acc_ref[...] += _swiglu_step(h_ref[...], wg_ref, wu_ref, wo_ref)

    @pl.when(j == pl.num_programs(1) - 1)
    def _():
        y = x_ref[...] + acc_ref[...]
        o_ref[...] = _rms(y, fg_ref[...]) if final else y


def ffn(x, g, w_in, w_out, final_g=None, tm=1024, tf=512):
    n, d = x.shape
    f = w_out.shape[0]
    tm, tf = min(tm, n), min(tf, f)
    nf = f // tf
    final = final_g is not None
    in_specs = [
        pl.BlockSpec((tm, d), lambda i, j: (i, 0)),
        pl.BlockSpec((1, d), lambda i, j: (0, 0)),
        pl.BlockSpec((d, tf), lambda i, j: (0, j)),
        pl.BlockSpec((d, tf), lambda i, j: (0, j + nf)),
        pl.BlockSpec((tf, d), lambda i, j: (j, 0)),
    ]
    args = [x, g, w_in, w_in, w_out]
    if final:
        in_specs.append(pl.BlockSpec((1, d), lambda i, j: (0, 0)))
        args.append(final_g)
    return pl.pallas_call(
        functools.partial(_ffn_body, final=final),
        grid=(n // tm, nf),
        in_specs=in_specs,
        out_specs=pl.BlockSpec((tm, d), lambda i, j: (i, 0)),
        out_shape=jax.ShapeDtypeStruct((n, d), F32),
        scratch_shapes=[pltpu.VMEM((tm, d), BF16), pltpu.VMEM((tm, d), F32)],
        compiler_params=_params("parallel", "arbitrary"),
        name="ffn",
    )(*args)


TOK_ROWS = 8


def _router_body(x_ref, g_ref, whi_ref, wlo_ref, gate_ref, tok_ref, cnt_ref, carry_ref, *, ne, tm):
    @pl.when(pl.program_id(0) == 0)
    def _():
        carry_ref[...] = jnp.zeros_like(carry_ref)

    h = _rms(x_ref[...], g_ref[...])
    h_hi = h.astype(BF16)
    h_lo = (h - h_hi.astype(F32)).astype(BF16)
    logits = _dot(h_hi, whi_ref[...]) + _dot(h_lo, whi_ref[...]) + _dot(h_hi, wlo_ref[...])
    lane = lax.broadcasted_iota(jnp.int32, logits.shape, 1).astype(F32)
    neg = -jnp.inf
    l1 = jnp.where(lane < ne, logits, neg)
    m1 = jnp.max(l1, axis=-1, keepdims=True)
    i1 = jnp.min(jnp.where(l1 == m1, lane, float(V7X_LANES)), axis=-1, keepdims=True)
    l2 = jnp.where(lane == i1, neg, l1)
    m2 = jnp.max(l2, axis=-1, keepdims=True)
    i2 = jnp.min(jnp.where(l2 == m2, lane, float(V7X_LANES)), axis=-1, keepdims=True)
    e2 = jnp.exp(m2 - m1)
    den = 1.0 + e2
    gate_ref[...] = jnp.where(lane == 0.0, 1.0 / den, 0.0) + jnp.where(lane == 1.0, e2 / den, 0.0)

    sel1, sel2 = lane == i1, lane == i2
    onehot = jnp.where(sel1 | sel2, 1.0, 0.0)
    row = lax.broadcasted_iota(jnp.int32, (tm, tm), 0)
    col = lax.broadcasted_iota(jnp.int32, (tm, tm), 1)
    before = carry_ref[...] + _dot((col < row).astype(BF16), onehot.astype(BF16))
    carry_ref[...] = before[tm - 1 : tm, :] + onehot[tm - 1 : tm, :]
    cnt_ref[...] = carry_ref[...]
    r1 = jnp.sum(jnp.where(sel1, before, 0.0), axis=-1, keepdims=True)
    r2 = jnp.sum(jnp.where(sel2, before, 0.0), axis=-1, keepdims=True)
    table = (jnp.where(lane == 0.0, i1, 0.0) + jnp.where(lane == 1.0, i2, 0.0)
             + jnp.where(lane == 2.0, r1, 0.0) + jnp.where(lane == 3.0, r2, 0.0))
    pick = (lax.broadcasted_iota(jnp.int32, (TOK_ROWS, V7X_LANES), 0) == lax.broadcasted_iota(jnp.int32, (TOK_ROWS, V7X_LANES), 1)).astype(BF16)
    hi, mid, lo = _split3(table)
    tok_ref[...] = (_dot_nt(pick, hi) + _dot_nt(pick, mid) + _dot_nt(pick, lo)).astype(jnp.int32)


def moe_router(x, g, w_router, tm=1024):
    n, d = x.shape
    ne = w_router.shape[1]
    tm = min(tm, n)
    wpad = jnp.pad(w_router, ((0, 0), (0, V7X_LANES - ne)))
    w_hi = wpad.astype(BF16)
    w_lo = (wpad - w_hi.astype(F32)).astype(BF16)
    return pl.pallas_call(
        functools.partial(_router_body, ne=ne, tm=tm),
        grid=(n // tm,),
        in_specs=[
            pl.BlockSpec((tm, d), lambda i: (i, 0)),
            pl.BlockSpec((1, d), lambda i: (0, 0)),
            pl.BlockSpec((d, V7X_LANES), lambda i: (0, 0)),
            pl.BlockSpec((d, V7X_LANES), lambda i: (0, 0)),
        ],
        out_specs=[
            pl.BlockSpec((tm, V7X_LANES), lambda i: (i, 0)),
            pl.BlockSpec((TOK_ROWS, tm), lambda i: (0, i)),
            pl.BlockSpec((1, V7X_LANES), lambda i: (0, 0)),
        ],
        out_shape=[
            jax.ShapeDtypeStruct((n, V7X_LANES), F32),
            jax.ShapeDtypeStruct((TOK_ROWS, n), jnp.int32),
            jax.ShapeDtypeStruct((1, V7X_LANES), F32),
        ],
        scratch_shapes=[pltpu.VMEM((1, V7X_LANES), F32)],
        compiler_params=_params("arbitrary"),
        name="moe_router",
    )(x, g, w_hi, w_lo)


def _plan_body(cnt_ref, tok_ref, off_ref, te_ref, tv_ref, nv_ref, pos_ref, *, ne, tg, n_tiles):
    off = jnp.int32(0)
    tile = jnp.int32(0)
    expert = tok_ref[0:TOP_K, :]
    group_start = jnp.zeros_like(expert)
    for e in range(ne):
        nt = (cnt_ref[e] + (tg - 1)) // tg
        off_ref[e] = off
        group_start = jnp.where(expert == e, off, group_start)

        def mark(ti, carry, e=e, first=tile):
            te_ref[ti] = jnp.int32(e)
            tv_ref[ti] = jnp.minimum(cnt_ref[e] - (ti - first) * tg, tg)
            return carry

        lax.fori_loop(tile, tile + nt, mark, 0)
        off = off + nt * tg
        tile = tile + nt
    pos_ref[...] = jnp.zeros_like(pos_ref)
    pos_ref[0:TOP_K, :] = group_start + tok_ref[TOP_K : 2 * TOP_K, :]
    nv_ref[0] = tile
    last = te_ref[jnp.maximum(tile - 1, 0)]

    def fill(ti, carry):
        te_ref[ti] = last
        tv_ref[ti] = jnp.int32(0)
        return carry

    lax.fori_loop(tile, n_tiles, fill, 0)


def moe_plan(cnt, tok, tg, n_tiles):
    ne = cnt.shape[0]
    smem = pl.BlockSpec(memory_space=pltpu.SMEM)
    vmem = pl.BlockSpec(memory_space=pltpu.VMEM)
    return pl.pallas_call(
        functools.partial(_plan_body, ne=ne, tg=tg, n_tiles=n_tiles),
        in_specs=[smem, vmem],
        out_specs=[smem, smem, smem, smem, vmem],
        out_shape=[
            jax.ShapeDtypeStruct((ne,), jnp.int32),
            jax.ShapeDtypeStruct((n_tiles,), jnp.int32),
            jax.ShapeDtypeStruct((n_tiles,), jnp.int32),
            jax.ShapeDtypeStruct((1,), jnp.int32),
            jax.ShapeDtypeStruct(tok.shape, jnp.int32),
        ],
        name="moe_plan",
    )(cnt, tok)


def _token_copy(src, s, dst, r, sem, ns):
    return pltpu.make_async_copy(src.at[pl.ds(pl.multiple_of(s * ns, ns), ns)], dst.at[pl.ds(pl.multiple_of(r * ns, ns), ns)], sem)


def _to_slabs(dst_ref, val, ns):
    rows = val.shape[0]
    for c in range(ns):
        dst_ref[pl.ds(c, rows, stride=ns), :] = val[:, c * V7X_LANES : (c + 1) * V7X_LANES]


def _slab(src_ref, c, rows, ns):
    return src_ref[pl.ds(c, rows, stride=ns), :]


def _dispatch_body(pos_hbm, off_ref, cnt_ref, x_ref, g_ref, xs_hbm, pos_smem, x3_ref, zero_ref, sem_idx, sem_row, *, td, tg, ne, n_tiles, ns):
    base = pl.multiple_of(pl.program_id(0) * td, td)
    idx = pltpu.make_async_copy(pos_hbm.at[:, pl.ds(base, td)], pos_smem, sem_idx)
    idx.start()
    _to_slabs(x3_ref, _rms(x_ref[...], g_ref[...]), ns)

    @pl.when(pl.program_id(0) == 0)
    def _():
        zero_ref[...] = jnp.zeros_like(zero_ref)
        for e in range(ne):
            start = off_ref[e] + cnt_ref[e]
            stop = off_ref[e] + (cnt_ref[e] + (tg - 1)) // tg * tg

            def zero_start(r, carry):
                _token_copy(zero_ref, 0, xs_hbm, r, sem_row, ns).start()
                return carry

            def zero_wait(r, carry):
                _token_copy(zero_ref, 0, xs_hbm, r, sem_row, ns).wait()
                return carry

            lax.fori_loop(start, stop, zero_start, 0)
            lax.fori_loop(start, stop, zero_wait, 0)

        def tile_copy(ti):
            return pltpu.make_async_copy(zero_ref, xs_hbm.at[pl.ds(pl.multiple_of(ti * (tg * ns), tg * ns), tg * ns)], sem_row)

        def tile_start(ti, carry):
            tile_copy(ti).start()
            return carry

        def tile_wait(ti, carry):
            tile_copy(ti).wait()
            return carry

        used = stop // tg
        lax.fori_loop(used, n_tiles, tile_start, 0)
        lax.fori_loop(used, n_tiles, tile_wait, 0)

    idx.wait()

    def start(t, c):
        for k in range(TOP_K):
            _token_copy(x3_ref, t, xs_hbm, pos_smem[k, t], sem_row, ns).start(priority=k % 2)
        return c

    lax.fori_loop(0, td, start, 0, unroll=8)
    for _ in range(TOP_K):
        pltpu.make_async_copy(x3_ref, xs_hbm.at[pl.ds(0, td * ns)], sem_row).wait()


def moe_dispatch(pos, off, cnt, x, g, tg, n_tiles, td=1024):
    n, d = x.shape
    ns = d // V7X_LANES
    assert ns % V7X_SUBLANES == 0
    td = min(td, n)
    smem = pl.BlockSpec(memory_space=pltpu.SMEM)
    hbm = pl.BlockSpec(memory_space=pl.ANY)
    return pl.pallas_call(
        functools.partial(_dispatch_body, td=td, tg=tg, ne=cnt.shape[0], n_tiles=n_tiles, ns=ns),
        grid=(n // td,),
        in_specs=[hbm, smem, smem, pl.BlockSpec((td, d), lambda i: (i, 0)), pl.BlockSpec((1, d), lambda i: (0, 0))],
        out_specs=hbm,
        out_shape=jax.ShapeDtypeStruct((n_tiles * tg * ns, V7X_LANES), x.dtype),
        scratch_shapes=[
            pltpu.SMEM((TOK_ROWS, td), jnp.int32),
            pltpu.VMEM((td * ns, V7X_LANES), x.dtype),
            pltpu.VMEM((tg * ns, V7X_LANES), x.dtype),
            pltpu.SemaphoreType.DMA(()),
            pltpu.SemaphoreType.DMA(()),
        ],
        compiler_params=pltpu.CompilerParams(dimension_semantics=("arbitrary",), has_side_effects=True, vmem_limit_bytes=V7X_VMEM_LIMIT_BYTES),
        name="moe_dispatch",
    )(pos, off, cnt, x, g)


GROUP_SHORT_TILE_DIV = 4


def _grouped_ffn_body(te_ref, tv_ref, nv_ref, x_ref, wg_ref, wu_ref, wo_ref, o_ref, h_ref, acc_ref, *, tg, ns):
    i, j = pl.program_id(0), pl.program_id(1)
    d = ns * V7X_LANES

    @pl.when(i < nv_ref[0])
    def _():
        @pl.when(j == 0)
        def _():
            for c in range(ns):
                h_ref[:, c * V7X_LANES : (c + 1) * V7X_LANES] = _slab(x_ref, c, tg, ns).astype(BF16)

        def accumulate(m):
            y = _swiglu_step(h_ref[0:m, :], wg_ref, wu_ref, wo_ref)

            @pl.when(j == 0)
            def _():
                acc_ref[0:m, :] = y
                if m < tg:
                    acc_ref[m:, :] = jnp.zeros((tg - m, d), F32)

            @pl.when(j > 0)
            def _():
                acc_ref[0:m, :] += y

        short = tv_ref[i] <= tg // GROUP_SHORT_TILE_DIV
        pl.when(short)(lambda: accumulate(tg // GROUP_SHORT_TILE_DIV))
        pl.when(jnp.logical_not(short))(lambda: accumulate(tg))

        @pl.when(j == pl.num_programs(1) - 1)
        def _():
            _to_slabs(o_ref, acc_ref[...], ns)

    @pl.when((i >= nv_ref[0]) & (j == 0))
    def _():
        o_ref[...] = jnp.zeros_like(o_ref)


def moe_grouped_ffn(te, tv, nv, xs, w_in, w_out, layer, tg, tf=1024):
    d = w_in.shape[2]
    ns = d // V7X_LANES
    r = xs.shape[0] // ns
    f = w_out.shape[2]
    tf = _block(f, tf)
    nf = f // tf
    n_tiles = r // tg

    def tile(i, nv):
        return jnp.minimum(i, nv[0] - 1)

    def fblk(i, j, nv):
        return jnp.where(i < nv[0], j, nf - 1)

    grid_spec = pltpu.PrefetchScalarGridSpec(
        num_scalar_prefetch=3,
        grid=(n_tiles, nf),
        in_specs=[
            pl.BlockSpec((tg * ns, V7X_LANES), lambda i, j, te, tv, nv: (tile(i, nv), 0)),
            pl.BlockSpec((None, None, d, tf), lambda i, j, te, tv, nv: (layer, te[i], 0, fblk(i, j, nv))),
            pl.BlockSpec((None, None, d, tf), lambda i, j, te, tv, nv: (layer, te[i], 0, fblk(i, j, nv) + nf)),
            pl.BlockSpec((None, None, tf, d), lambda i, j, te, tv, nv: (layer, te[i], fblk(i, j, nv), 0)),
        ],
        out_specs=pl.BlockSpec((tg * ns, V7X_LANES), lambda i, j, te, tv, nv: (i, 0)),
        scratch_shapes=[pltpu.VMEM((tg, d), BF16), pltpu.VMEM((tg, d), F32)],
    )
    return pl.pallas_call(
        functools.partial(_grouped_ffn_body, tg=tg, ns=ns),
        grid_spec=grid_spec,
        out_shape=jax.ShapeDtypeStruct((r * ns, V7X_LANES), F32),
        compiler_params=_params("parallel", "arbitrary"),
        name="moe_grouped_ffn",
    )(te, tv, nv, xs, w_in, w_in, w_out)


def _combine_body(pos_hbm, gate_ref, x_ref, ys_hbm, *rest, tc, ns, final):
    rest = list(rest)
    fg_ref = rest.pop(0) if final else None
    o_ref, pos_smem, buf_ref, sem_idx, sem_row = rest
    base = pl.multiple_of(pl.program_id(0) * tc, tc)
    idx = pltpu.make_async_copy(pos_hbm.at[:, pl.ds(base, tc)], pos_smem, sem_idx)
    idx.start()
    idx.wait()

    def start(t, c):
        for k in range(TOP_K):
            _token_copy(ys_hbm, pos_smem[k, t], buf_ref.at[k], t, sem_row, ns).start(priority=k % 2)
        return c

    lax.fori_loop(0, tc, start, 0, unroll=8)
    for k in range(TOP_K):
        pltpu.make_async_copy(ys_hbm.at[pl.ds(0, tc * ns)], buf_ref.at[k], sem_row).wait()
    g0, g1 = gate_ref[:, 0:1], gate_ref[:, 1:2]
    for c in range(ns):
        cols = slice(c * V7X_LANES, (c + 1) * V7X_LANES)
        o_ref[:, cols] = x_ref[:, cols] + g0 * _slab(buf_ref.at[0], c, tc, ns) + g1 * _slab(buf_ref.at[1], c, tc, ns)
    if final:
        o_ref[...] = _rms(o_ref[...], fg_ref[...])


def moe_combine(pos, gates, x, ys, final_g=None, tc=512):
    n, d = x.shape
    ns = d // V7X_LANES
    tc = min(tc, n)
    final = final_g is not None
    hbm = pl.BlockSpec(memory_space=pl.ANY)
    in_specs = [hbm, pl.BlockSpec((tc, V7X_LANES), lambda i: (i, 0)), pl.BlockSpec((tc, d), lambda i: (i, 0)), hbm]
    args = [pos, gates, x, ys]
    if final:
        in_specs.append(pl.BlockSpec((1, d), lambda i: (0, 0)))
        args.append(final_g)
    return pl.pallas_call(
        functools.partial(_combine_body, tc=tc, ns=ns, final=final),
        grid=(n // tc,),
        in_specs=in_specs,
        out_specs=pl.BlockSpec((tc, d), lambda i: (i, 0)),
        out_shape=jax.ShapeDtypeStruct((n, d), F32),
        scratch_shapes=[
            pltpu.SMEM((TOK_ROWS, tc), jnp.int32),
            pltpu.VMEM((TOP_K, tc * ns, V7X_LANES), F32),
            pltpu.SemaphoreType.DMA(()),
            pltpu.SemaphoreType.DMA(()),
        ],
        compiler_params=_params("arbitrary"),
        name="moe_combine",
    )(*args)


def moe_ffn(x, g, w_router, w_in, w_out, layer, final_g=None, tg=1024):
    n, _ = x.shape
    ne = w_router.shape[1]
    tg = min(tg, n)
    n_tiles = TOP_K * n // tg + ne
    gates, tok, counts = moe_router(x, g, w_router)
    cnt = counts[0, :ne].astype(jnp.int32)
    off, te, tv, nv, pos = moe_plan(cnt, tok, tg, n_tiles)
    xs = moe_dispatch(pos, off, cnt, x, g, tg, n_tiles)
    ys = moe_grouped_ffn(te, tv, nv, xs, w_in, w_out, layer, tg)
    return moe_combine(pos, gates, x, ys, final_g)


def _gla_gate_body(x_ref, g_ref, wl_ref, w2_ref, b_ref, o_ref):
    h = _rms(x_ref[...], g_ref[...]).astype(BF16)
    g_low = _dot(h, wl_ref[...]).astype(BF16)
    o_ref[...] = _log_sigmoid(_dot(g_low, w2_ref[...]) + b_ref[...]) / B_TAU


def gla_gate(x, g, w_low, w_gate2, bias, tm=1024):
    n, d = x.shape
    dk = w_gate2.shape[1]
    tm = min(tm, n)
    wl = jnp.pad(w_low, ((0, 0), (0, V7X_LANES - B_GATE_RANK))).astype(BF16)
    w2 = jnp.pad(w_gate2, ((0, V7X_LANES - B_GATE_RANK), (0, 0))).astype(BF16)
    return pl.pallas_call(
        _gla_gate_body,
        grid=(n // tm,),
        in_specs=[
            pl.BlockSpec((tm, d), lambda i: (i, 0)),
            pl.BlockSpec((1, d), lambda i: (0, 0)),
            pl.BlockSpec((d, V7X_LANES), lambda i: (0, 0)),
            pl.BlockSpec((V7X_LANES, dk), lambda i: (0, 0)),
            pl.BlockSpec((1, dk), lambda i: (0, 0)),
        ],
        out_specs=pl.BlockSpec((tm, dk), lambda i: (i, 0)),
        out_shape=jax.ShapeDtypeStruct((n, dk), F32),
        compiler_params=_params("parallel"),
        name="gla_gate",
    )(x, g, wl, w2, bias)


def _gla_body(q_ref, k_ref, v_ref, r_ref, la_ref, og_ref, o_ref, st_ref, *, tc, hk, hv):
    @pl.when(pl.program_id(1) == 0)
    def _():
        st_ref[...] = jnp.zeros_like(st_ref)

    c = B_CHUNK
    tril = _tril_mask(c)
    ones_tril = tril.astype(BF16)
    scale = hk**-0.5
    for ci in range(tc // c):
        rs = slice(ci * c, (ci + 1) * c)
        hi, mid, lo = _split3(la_ref[rs, :])
        bcum = _dot(ones_tril, hi) + _dot(ones_tril, mid) + _dot(ones_tril, lo)
        b_last = bcum[c - 1 : c, :]
        e_pos = jnp.exp(bcum)
        e_neg = jnp.exp(-bcum)
        e_end = jnp.exp(b_last - bcum)
        dec = jnp.exp(b_last)
        for h in range(B_HEADS):
            ks = slice(h * hk, (h + 1) * hk)
            vs = slice(h * hv, (h + 1) * hv)
            q = q_ref[rs, ks].astype(F32) * scale
            k = k_ref[rs, ks].astype(F32)
            v = v_ref[rs, vs]
            q_d = (q * e_pos[:, ks]).astype(BF16)
            k_d = (k * e_neg[:, ks]).astype(BF16)
            k_end = (k * e_end[:, ks]).astype(BF16)
            att = jnp.where(tril, _dot_nt(q_d, k_d), 0.0)
            st = st_ref[h]
            o = _dot(att.astype(BF16), v) + _dot_nt(q_d, st.astype(BF16))
            st_ref[h] = dec[:, ks] * st + _dot_tn(v, k_end)
            y = _rms(o, og_ref[...]).astype(BF16).astype(F32)
            r = r_ref[rs, vs].astype(F32)
            o_ref[rs, vs] = (y * (r * jax.nn.sigmoid(r))).astype(BF16)


def gla_chunks(proj, la, o_g, bsz, tc=256):
    n = proj.shape[0]
    dk = la.shape[1]
    dv = (proj.shape[1] - 2 * dk) // 2
    s = n // bsz
    tc = min(tc, s)
    nt = s // tc
    hk, hv = dk // B_HEADS, dv // B_HEADS
    assert (2 * dk) % dv == 0
    v_blk = 2 * dk // dv
    row = lambda b, t: b * nt + t
    return pl.pallas_call(
        functools.partial(_gla_body, tc=tc, hk=hk, hv=hv),
        grid=(bsz, nt),
        in_specs=[
            pl.BlockSpec((tc, dk), lambda b, t: (row(b, t), 0)),
            pl.BlockSpec((tc, dk), lambda b, t: (row(b, t), 1)),
            pl.BlockSpec((tc, dv), lambda b, t: (row(b, t), v_blk)),
            pl.BlockSpec((tc, dv), lambda b, t: (row(b, t), v_blk + 1)),
            pl.BlockSpec((tc, dk), lambda b, t: (row(b, t), 0)),
            pl.BlockSpec((1, hv), lambda b, t: (0, 0)),
        ],
        out_specs=pl.BlockSpec((tc, dv), lambda b, t: (row(b, t), 0)),
        out_shape=jax.ShapeDtypeStruct((n, dv), BF16),
        scratch_shapes=[pltpu.VMEM((B_HEADS, hv, hk), F32)],
        compiler_params=_params("parallel", "arbitrary"),
        name="gla_chunks",
    )(proj, proj, proj, proj, la, o_g)


def _fox_gate_body(x_ref, g_ref, wf_ref, b_ref, crow_ref, carry_ref, *, tb, nh):
    @pl.when(pl.program_id(1) == 0)
    def _():
        carry_ref[...] = jnp.zeros_like(carry_ref)

    h = _rms(x_ref[...], g_ref[...]).astype(BF16)
    lane = lax.broadcasted_iota(jnp.int32, (tb, V7X_LANES), 1)
    log_f = jnp.where(lane < nh, _log_sigmoid(_dot(h, wf_ref[...]) + b_ref[...]), 0.0)
    ones_tril = _tril_mask(tb).astype(BF16)
    hi, mid, lo = _split3(log_f)
    c = carry_ref[...] + (_dot(ones_tril, hi) + _dot(ones_tril, mid) + _dot(ones_tril, lo))
    carry_ref[...] = c[tb - 1 : tb, :]
    sel = (lax.broadcasted_iota(jnp.int32, (nh, V7X_LANES), 0) == lax.broadcasted_iota(jnp.int32, (nh, V7X_LANES), 1)).astype(BF16)
    hi, mid, lo = _split3(c)
    crow_ref[...] = _dot_nt(sel, hi) + _dot_nt(sel, mid) + _dot_nt(sel, lo)


def fox_gate(x, g, w_f, f_bias, bsz, tb=512):
    n, d = x.shape
    nh = w_f.shape[1]
    s = n // bsz
    tb = min(tb, s)
    nt = s // tb
    wf = jnp.pad(w_f, ((0, 0), (0, V7X_LANES - nh))).astype(BF16)
    bias = jnp.pad(f_bias.reshape(1, nh), ((0, 0), (0, V7X_LANES - nh)))
    return pl.pallas_call(
        functools.partial(_fox_gate_body, tb=tb, nh=nh),
        grid=(bsz, nt),
        in_specs=[
            pl.BlockSpec((tb, d), lambda b, t: (b * nt + t, 0)),
            pl.BlockSpec((1, d), lambda b, t: (0, 0)),
            pl.BlockSpec((d, V7X_LANES), lambda b, t: (0, 0)),
            pl.BlockSpec((1, V7X_LANES), lambda b, t: (0, 0)),
        ],
        out_specs=pl.BlockSpec((None, nh, tb), lambda b, t: (b, 0, t)),
        out_shape=jax.ShapeDtypeStruct((bsz, nh, s), F32),
        scratch_shapes=[pltpu.VMEM((1, V7X_LANES), F32)],
        compiler_params=_params("parallel", "arbitrary"),
        name="fox_gate",
    )(x, g, wf, bias)


FOX_KEY_BLOCK = 512
FOX_SUBTILES = 8
FOX_ROW_CHUNK = 32


def _fox_flash_body(q_ref, k_ref, v_ref, crow_ref, o_ref, qs_ref, va_ref, s_ref, p_ref, al_ref, m_ref, acc_ref, *, t, nsub, scale):
    i = pl.program_id(2)
    dh = q_ref.shape[1]
    nl = t // V7X_LANES

    @pl.when(i == 0)
    def _():
        va_ref[:, :dh] = v_ref[...]
        va_ref[:, dh:] = jnp.ones((va_ref.shape[0], V7X_LANES), BF16)

    qs_ref[...] = (q_ref[...].astype(F32) * scale).astype(BF16)
    m_ref[...] = jnp.full_like(m_ref, -jnp.inf)
    acc_ref[...] = jnp.zeros_like(acc_ref)

    def block(sub, kb, masked):
        rows = slice(sub * t, (sub + 1) * t)
        off = pl.multiple_of(kb * t, t)
        s_ref[sub] = _dot_nt(qs_ref[rows, :], k_ref[pl.ds(off, t), :])
        bias = crow_ref[pl.ds(kb, 1), :]

        def chunk(c, carry):
            r = pl.multiple_of(c * FOX_ROW_CHUNK, FOX_ROW_CHUNK)
            sc = s_ref[sub, pl.ds(r, FOX_ROW_CHUNK), :] - bias
            if masked:
                row = r + lax.broadcasted_iota(jnp.int32, sc.shape, 0)
                col = lax.broadcasted_iota(jnp.int32, sc.shape, 1)
                sc = jnp.where(col <= row, sc, -jnp.inf)
            tiles = [sc[:, a * V7X_LANES : (a + 1) * V7X_LANES] for a in range(nl)]
            mx = functools.reduce(jnp.maximum, tiles)
            m_old = m_ref[pl.ds(sub * t + r, FOX_ROW_CHUNK), :]
            m_new = jnp.maximum(m_old, jnp.max(mx, axis=-1, keepdims=True))
            al_ref[sub, pl.ds(r, FOX_ROW_CHUNK), :] = jnp.exp(m_old - m_new)
            m_ref[pl.ds(sub * t + r, FOX_ROW_CHUNK), :] = m_new
            for a in range(nl):
                p_ref[sub, pl.ds(r, FOX_ROW_CHUNK), a * V7X_LANES : (a + 1) * V7X_LANES] = jnp.exp(tiles[a] - m_new).astype(BF16)
            return carry

        lax.fori_loop(0, t // FOX_ROW_CHUNK, chunk, 0, unroll=True)
        pv = _dot(p_ref[sub], va_ref[pl.ds(off, t), :])
        alpha = al_ref[sub]
        acc_ref[rows, :dh] = alpha * acc_ref[rows, :dh] + pv[:, :dh]
        acc_ref[rows, dh:] = alpha * acc_ref[rows, dh:] + pv[:, dh:]

    def below_diagonal(kb, carry):
        for sub in range(nsub):
            block(sub, kb, False)
        return carry

    first = i * nsub
    lax.fori_loop(0, first, below_diagonal, 0)
    for kb in range(nsub):
        for sub in range(kb, nsub):
            block(sub, first + kb, masked=(sub == kb))
    o_ref[...] = (acc_ref[:, :dh] / acc_ref[:, dh:]).astype(o_ref.dtype)


def fox_flash(proj, crow, bsz, nh):
    n = proj.shape[0]
    dh = proj.shape[1] // (3 * nh)
    assert dh == V7X_LANES
    s = n // bsz
    t = min(FOX_KEY_BLOCK, s)
    nsub = min(FOX_SUBTILES, s // t)
    tq = t * nsub
    nq = s // tq
    crow = crow.reshape(bsz, nh, s // t, t)
    return pl.pallas_call(
        functools.partial(_fox_flash_body, t=t, nsub=nsub, scale=dh**-0.5),
        grid=(bsz, nh, nq),
        in_specs=[
            pl.BlockSpec((tq, dh), lambda b, h, i: (b * nq + i, h)),
            pl.BlockSpec((s, dh), lambda b, h, i: (b, nh + h)),
            pl.BlockSpec((s, dh), lambda b, h, i: (b, 2 * nh + h)),
            pl.BlockSpec((None, None, s // t, t), lambda b, h, i: (b, h, 0, 0)),
        ],
        out_specs=pl.BlockSpec((tq, dh), lambda b, h, i: (b * nq + i, h)),
        out_shape=jax.ShapeDtypeStruct((n, nh * dh), BF16),
        scratch_shapes=[
            pltpu.VMEM((tq, dh), BF16),
            pltpu.VMEM((s, dh + V7X_LANES), BF16),
            pltpu.VMEM((nsub, t, t), F32),
            pltpu.VMEM((nsub, t, t), BF16),
            pltpu.VMEM((nsub, t, V7X_LANES), F32),
            pltpu.VMEM((tq, V7X_LANES), F32),
            pltpu.VMEM((tq, dh + V7X_LANES), F32),
        ],
        compiler_params=_params("parallel", "parallel", "arbitrary"),
        name="fox_flash",
    )(proj, proj, proj, crow)


def kernel(x, norm1_g, norm2_g, a_w_in, a_vnorm_g, a_w_s, a_b_s, a_w_out, b_w_in, b_w_gate2, b_gate_bias, b_onorm_g, b_w_out, c_w_in, c_f_bias, c_w_out, ffn_w_in, ffn_w_out, moe_router, moe_w_in, moe_w_out, final_g):
    bsz, s, d = x.shape
    depth = norm1_g.shape[0]
    xf = x.reshape(bsz * s, d)
    moe_w_in_bf16, moe_w_out_bf16 = moe_w_in.astype(BF16), moe_w_out.astype(BF16)
    for i in range(depth):
        g1 = norm1_g[i].reshape(1, d)
        m, j = i % N_MIXERS, i // N_MIXERS
        if m == 0:
            z = norm_matmul(xf, g1, a_w_in[j].astype(BF16), act="gelu")
            xf = gmlp_spatial_out(z, xf, a_vnorm_g[j].reshape(1, -1), a_w_s[j], a_b_s[j].T, a_w_out[j].astype(BF16))
        elif m == 1:
            dk = b_w_gate2.shape[2]
            n_main = b_w_in.shape[2] - B_GATE_RANK
            proj = norm_matmul(xf, g1, b_w_in[j, :, :n_main].astype(BF16))
            la = gla_gate(xf, g1, b_w_in[j, :, n_main:], b_w_gate2[j], b_gate_bias[j].reshape(1, dk))
            o = gla_chunks(proj, la, b_onorm_g[j].reshape(1, -1), bsz)
            xf = matmul_residual(o, b_w_out[j].astype(BF16), xf)
        else:
            proj = norm_matmul(xf, g1, c_w_in[j, :, : 3 * d].astype(BF16))
            crow = fox_gate(xf, g1, c_w_in[j, :, 3 * d :], c_f_bias[j], bsz)
            o = fox_flash(proj, crow, bsz, C_HEADS)
            xf = matmul_residual(o, c_w_out[j].astype(BF16), xf)
        g2 = norm2_g[i].reshape(1, d)
        fg = final_g.reshape(1, d) if i == depth - 1 else None
        if i % 2 == 0:
            xf = ffn(xf, g2, ffn_w_in[i // 2].astype(BF16), ffn_w_out[i // 2].astype(BF16), final_g=fg)
        else:
            xf = moe_ffn(xf, g2, moe_router[i // 2], moe_w_in_bf16, moe_w_out_bf16, i // 2, final_g=fg)
    return xf.reshape(bsz, s, d)
```

```python
import functools

import jax
import jax.numpy as jnp
from jax import lax
from jax.experimental import pallas as pl
from jax.experimental.pallas import tpu as pltpu

F32 = jnp.float32
BF16 = jnp.bfloat16

EPS = 1e-6
N_MIXERS = 3
TOP_K = 2
A_CHUNK = 128
A_GROUPS = 8
B_HEADS = 4
B_GATE_RANK = 16
B_TAU = 16.0
B_CHUNK = 64
C_HEADS = 8

V7X_LANES = 128
V7X_SUBLANES = 8
V7X_VMEM_LIMIT_BYTES = 56 * 1024 * 1024


def _params(*sem):
    return pltpu.CompilerParams(dimension_semantics=sem, vmem_limit_bytes=V7X_VMEM_LIMIT_BYTES)


def _block(total, target, align=V7X_LANES):
    if total <= target:
        return total
    return max(b for b in range(align, target + 1, align) if total % b == 0)


def _rms(x, g):
    ms = jnp.mean(x * x, axis=-1, keepdims=True)
    return x * lax.rsqrt(ms + EPS) * g


def _log_sigmoid(x):
    return jnp.minimum(x, 0.0) - jnp.log1p(jnp.exp(-jnp.abs(x)))


def _split3(x):
    hi = x.astype(BF16)
    r1 = x - hi.astype(F32)
    mid = r1.astype(BF16)
    lo = (r1 - mid.astype(F32)).astype(BF16)
    return hi, mid, lo


def _dot(a, b):
    return jnp.dot(a, b, preferred_element_type=F32)


def _dot_nt(a, b):
    return lax.dot_general(a, b, (((1,), (1,)), ((), ())), preferred_element_type=F32)


def _dot_tn(a, b):
    return lax.dot_general(a, b, (((0,), (0,)), ((), ())), preferred_element_type=F32)


def _tril_mask(n):
    row = lax.broadcasted_iota(jnp.int32, (n, n), 0)
    col = lax.broadcasted_iota(jnp.int32, (n, n), 1)
    return col <= row


def _norm_matmul_body(x_ref, g_ref, w_ref, o_ref, h_ref, *, act):
    @pl.when(pl.program_id(1) == 0)
    def _():
        h_ref[...] = _rms(x_ref[...], g_ref[...]).astype(BF16)

    y = _dot(h_ref[...], w_ref[...])
    if act == "gelu":
        y = jax.nn.gelu(y, approximate=True)
    o_ref[...] = y.astype(o_ref.dtype)


def norm_matmul(x, g, w, act=None, tm=1024, tn=1024):
    n, d = x.shape
    nout = w.shape[1]
    tm, tn = min(tm, n), _block(nout, tn)
    return pl.pallas_call(
        functools.partial(_norm_matmul_body, act=act),
        grid=(n // tm, nout // tn),
        in_specs=[
            pl.BlockSpec((tm, d), lambda i, j: (i, 0)),
            pl.BlockSpec((1, d), lambda i, j: (0, 0)),
            pl.BlockSpec((d, tn), lambda i, j: (0, j)),
        ],
        out_specs=pl.BlockSpec((tm, tn), lambda i, j: (i, j)),
        out_shape=jax.ShapeDtypeStruct((n, nout), BF16),
        scratch_shapes=[pltpu.VMEM((tm, d), BF16)],
        compiler_params=_params("parallel", "arbitrary"),
        name="norm_matmul",
    )(x, g, w)


def _matmul_residual_body(y_ref, w_ref, x_ref, o_ref):
    o_ref[...] = x_ref[...] + _dot(y_ref[...], w_ref[...])


def matmul_residual(y, w, x, tm=512):
    n, k = y.shape
    d = w.shape[1]
    tm = min(tm, n)
    return pl.pallas_call(
        _matmul_residual_body,
        grid=(n // tm,),
        in_specs=[
            pl.BlockSpec((tm, k), lambda i: (i, 0)),
            pl.BlockSpec((k, d), lambda i: (0, 0)),
            pl.BlockSpec((tm, d), lambda i: (i, 0)),
        ],
        out_specs=pl.BlockSpec((tm, d), lambda i: (i, 0)),
        out_shape=jax.ShapeDtypeStruct((n, d), F32),
        compiler_params=_params("parallel"),
        name="matmul_residual",
    )(y, w, x)


def _gmlp_body(z_ref, x_ref, vg_ref, ws_ref, bt_ref, wo_ref, o_ref, vn_ref, y_ref, *, tm, width):
    gd = width // A_GROUPS
    vn_ref[...] = _rms(z_ref[:, width:].astype(F32), vg_ref[...]).astype(BF16)
    tril = _tril_mask(A_CHUNK)
    for g in range(A_GROUPS):
        w = jnp.where(tril, ws_ref[g], 0.0).astype(BF16)
        bias = bt_ref[:, g : g + 1]
        cs = slice(g * gd, (g + 1) * gd)
        for c in range(tm // A_CHUNK):
            rs = slice(c * A_CHUNK, (c + 1) * A_CHUNK)
            mixed = _dot(w, vn_ref[rs, cs]) + bias
            y_ref[rs, cs] = (z_ref[rs, cs].astype(F32) * mixed).astype(BF16)
    o_ref[...] = x_ref[...] + _dot(y_ref[...], wo_ref[...])


def gmlp_spatial_out(z, x, vg, ws, bt, wo, tm=512):
    n, d = x.shape
    width = z.shape[1] // 2
    tm = min(tm, n)
    return pl.pallas_call(
        functools.partial(_gmlp_body, tm=tm, width=width),
        grid=(n // tm,),
        in_specs=[
            pl.BlockSpec((tm, 2 * width), lambda i: (i, 0)),
            pl.BlockSpec((tm, d), lambda i: (i, 0)),
            pl.BlockSpec((1, width), lambda i: (0, 0)),
            pl.BlockSpec((A_GROUPS, A_CHUNK, A_CHUNK), lambda i: (0, 0, 0)),
            pl.BlockSpec((A_CHUNK, A_GROUPS), lambda i: (0, 0)),
            pl.BlockSpec((width, d), lambda i: (0, 0)),
        ],
        out_specs=pl.BlockSpec((tm, d), lambda i: (i, 0)),
        out_shape=jax.ShapeDtypeStruct((n, d), F32),
        scratch_shapes=[pltpu.VMEM((tm, width), BF16), pltpu.VMEM((tm, width), BF16)],
        compiler_params=_params("parallel"),
        name="gmlp_spatial_out",
    )(z, x, vg, ws, bt, wo)


def _swiglu_step(h, wg_ref, wu_ref, wo_ref):
    a = _dot(h, wg_ref[...])
    t = (a * jax.nn.sigmoid(a)) * _dot(h, wu_ref[...])
    return _dot(t.astype(BF16), wo_ref[...])


def _ffn_body(*refs, final):
    x_ref, g_ref, wg_ref, wu_ref, wo_ref = refs[:5]
    rest = list(refs[5:])
    fg_ref = rest.pop(0) if final else None
    o_ref, h_ref, acc_ref = rest
    j = pl.program_id(1)

    @pl.when(j == 0)
    def _():
        h_ref[...] = _rms(x_ref[...], g_ref[...]).astype(BF16)
        acc_ref[...] = jnp.zeros_like(acc_ref)

    acc_ref[...] += _swiglu_step(h_ref[...], wg_ref, wu_ref, wo_ref)

    @pl.when(j == pl.num_programs(1) - 1)
    def _():
        y = x_ref[...] + acc_ref[...]
        o_ref[...] = _rms(y, fg_ref[...]) if final else y


def ffn(x, g, w_in, w_out, final_g=None, tm=1024, tf=512):
    n, d = x.shape
    f = w_out.shape[0]
    tm, tf = min(tm, n), min(tf, f)
    nf = f // tf
    final = final_g is not None
    in_specs = [
        pl.BlockSpec((tm, d), lambda i, j: (i, 0)),
        pl.BlockSpec((1, d), lambda i, j: (0, 0)),
        pl.BlockSpec((d, tf), lambda i, j: (0, j)),
        pl.BlockSpec((d, tf), lambda i, j: (0, j + nf)),
        pl.BlockSpec((tf, d), lambda i, j: (j, 0)),
    ]
    args = [x, g, w_in, w_in, w_out]
    if final:
        in_specs.append(pl.BlockSpec((1, d), lambda i, j: (0, 0)))
        args.append(final_g)
    return pl.pallas_call(
        functools.partial(_ffn_body, final=final),
        grid=(n // tm, nf),
        in_specs=in_specs,
        out_specs=pl.BlockSpec((tm, d), lambda i, j: (i, 0)),
        out_shape=jax.ShapeDtypeStruct((n, d), F32),
        scratch_shapes=[pltpu.VMEM((tm, d), BF16), pltpu.VMEM((tm, d), F32)],
        compiler_params=_params("parallel", "arbitrary"),
        name="ffn",
    )(*args)


TOK_ROWS = 8


def _router_body(x_ref, g_ref, whi_ref, wlo_ref, gate_ref, tok_ref, cnt_ref, carry_ref, *, ne, tm):
    @pl.when(pl.program_id(0) == 0)
    def _():
        carry_ref[...] = jnp.zeros_like(carry_ref)

    h = _rms(x_ref[...], g_ref[...])
    h_hi = h.astype(BF16)
    h_lo = (h - h_hi.astype(F32)).astype(BF16)
    logits = _dot(h_hi, whi_ref[...]) + _dot(h_lo, whi_ref[...]) + _dot(h_hi, wlo_ref[...])
    lane = lax.broadcasted_iota(jnp.int32, logits.shape, 1).astype(F32)
    neg = -jnp.inf
    l1 = jnp.where(lane < ne, logits, neg)
    m1 = jnp.max(l1, axis=-1, keepdims=True)
    i1 = jnp.min(jnp.where(l1 == m1, lane, float(V7X_LANES)), axis=-1, keepdims=True)
    l2 = jnp.where(lane == i1, neg, l1)
    m2 = jnp.max(l2, axis=-1, keepdims=True)
    i2 = jnp.min(jnp.where(l2 == m2, lane, float(V7X_LANES)), axis=-1, keepdims=True)
    e2 = jnp.exp(m2 - m1)
    den = 1.0 + e2
    gate_ref[...] = jnp.where(lane == 0.0, 1.0 / den, 0.0) + jnp.where(lane == 1.0, e2 / den, 0.0)

    sel1, sel2 = lane == i1, lane == i2
    onehot = jnp.where(sel1 | sel2, 1.0, 0.0)
    row = lax.broadcasted_iota(jnp.int32, (tm, tm), 0)
    col = lax.broadcasted_iota(jnp.int32, (tm, tm), 1)
    before = carry_ref[...] + _dot((col < row).astype(BF16), onehot.astype(BF16))
    carry_ref[...] = before[tm - 1 : tm, :] + onehot[tm - 1 : tm, :]
    cnt_ref[...] = carry_ref[...]
    r1 = jnp.sum(jnp.where(sel1, before, 0.0), axis=-1, keepdims=True)
    r2 = jnp.sum(jnp.where(sel2, before, 0.0), axis=-1, keepdims=True)
    table = (jnp.where(lane == 0.0, i1, 0.0) + jnp.where(lane == 1.0, i2, 0.0)
             + jnp.where(lane == 2.0, r1, 0.0) + jnp.where(lane == 3.0, r2, 0.0))
    pick = (lax.broadcasted_iota(jnp.int32, (TOK_ROWS, V7X_LANES), 0) == lax.broadcasted_iota(jnp.int32, (TOK_ROWS, V7X_LANES), 1)).astype(BF16)
    hi, mid, lo = _split3(table)
    tok_ref[...] = (_dot_nt(pick, hi) + _dot_nt(pick, mid) + _dot_nt(pick, lo)).astype(jnp.int32)


def moe_router(x, g, w_router, tm=1024):
    n, d = x.shape
    ne = w_router.shape[1]
    tm = min(tm, n)
    wpad = jnp.pad(w_router, ((0, 0), (0, V7X_LANES - ne)))
    w_hi = wpad.astype(BF16)
    w_lo = (wpad - w_hi.astype(F32)).astype(BF16)
    return pl.pallas_call(
        functools.partial(_router_body, ne=ne, tm=tm),
        grid=(n // tm,),
        in_specs=[
            pl.BlockSpec((tm, d), lambda i: (i, 0)),
            pl.BlockSpec((1, d), lambda i: (0, 0)),
            pl.BlockSpec((d, V7X_LANES), lambda i: (0, 0)),
            pl.BlockSpec((d, V7X_LANES), lambda i: (0, 0)),
        ],
        out_specs=[
            pl.BlockSpec((tm, V7X_LANES), lambda i: (i, 0)),
            pl.BlockSpec((TOK_ROWS, tm), lambda i: (0, i)),
            pl.BlockSpec((1, V7X_LANES), lambda i: (0, 0)),
        ],
        out_shape=[
            jax.ShapeDtypeStruct((n, V7X_LANES), F32),
            jax.ShapeDtypeStruct((TOK_ROWS, n), jnp.int32),
            jax.ShapeDtypeStruct((1, V7X_LANES), F32),
        ],
        scratch_shapes=[pltpu.VMEM((1, V7X_LANES), F32)],
        compiler_params=_params("arbitrary"),
        name="moe_router",
    )(x, g, w_hi, w_lo)


def _plan_body(cnt_ref, tok_ref, off_ref, te_ref, tv_ref, nv_ref, pos_ref, *, ne, tg, n_tiles):
    off = jnp.int32(0)
    tile = jnp.int32(0)
    expert = tok_ref[0:TOP_K, :]
    group_start = jnp.zeros_like(expert)
    for e in range(ne):
        nt = (cnt_ref[e] + (tg - 1)) // tg
        off_ref[e] = off
        group_start = jnp.where(expert == e, off, group_start)

        def mark(ti, carry, e=e, first=tile):
            te_ref[ti] = jnp.int32(e)
            tv_ref[ti] = jnp.minimum(cnt_ref[e] - (ti - first) * tg, tg)
            return carry

        lax.fori_loop(tile, tile + nt, mark, 0)
        off = off + nt * tg
        tile = tile + nt
    pos_ref[...] = jnp.zeros_like(pos_ref)
    pos_ref[0:TOP_K, :] = group_start + tok_ref[TOP_K : 2 * TOP_K, :]
    nv_ref[0] = tile
    last = te_ref[jnp.maximum(tile - 1, 0)]

    def fill(ti, carry):
        te_ref[ti] = last
        tv_ref[ti] = jnp.int32(0)
        return carry

    lax.fori_loop(tile, n_tiles, fill, 0)


def moe_plan(cnt, tok, tg, n_tiles):
    ne = cnt.shape[0]
    smem = pl.BlockSpec(memory_space=pltpu.SMEM)
    vmem = pl.BlockSpec(memory_space=pltpu.VMEM)
    return pl.pallas_call(
        functools.partial(_plan_body, ne=ne, tg=tg, n_tiles=n_tiles),
        in_specs=[smem, vmem],
        out_specs=[smem, smem, smem, smem, vmem],
        out_shape=[
            jax.ShapeDtypeStruct((ne,), jnp.int32),
            jax.ShapeDtypeStruct((n_tiles,), jnp.int32),
            jax.ShapeDtypeStruct((n_tiles,), jnp.int32),
            jax.ShapeDtypeStruct((1,), jnp.int32),
            jax.ShapeDtypeStruct(tok.shape, jnp.int32),
        ],
        name="moe_plan",
    )(cnt, tok)


def _token_copy(src, s, dst, r, sem, ns):
    return pltpu.make_async_copy(src.at[pl.ds(pl.multiple_of(s * ns, ns), ns)], dst.at[pl.ds(pl.multiple_of(r * ns, ns), ns)], sem)


def _to_slabs(dst_ref, val, ns):
    rows = val.shape[0]
    for c in range(ns):
        dst_ref[pl.ds(c, rows, stride=ns), :] = val[:, c * V7X_LANES : (c + 1) * V7X_LANES]


def _slab(src_ref, c, rows, ns):
    return src_ref[pl.ds(c, rows, stride=ns), :]


def _dispatch_body(pos_hbm, off_ref, cnt_ref, x_ref, g_ref, xs_hbm, pos_smem, x3_ref, zero_ref, sem_idx, sem_row, *, td, tg, ne, n_tiles, ns):
    base = pl.multiple_of(pl.program_id(0) * td, td)
    idx = pltpu.make_async_copy(pos_hbm.at[:, pl.ds(base, td)], pos_smem, sem_idx)
    idx.start()
    _to_slabs(x3_ref, _rms(x_ref[...], g_ref[...]), ns)

    @pl.when(pl.program_id(0) == 0)
    def _():
        zero_ref[...] = jnp.zeros_like(zero_ref)
        for e in range(ne):
            start = off_ref[e] + cnt_ref[e]
            stop = off_ref[e] + (cnt_ref[e] + (tg - 1)) // tg * tg

            def zero_start(r, carry):
                _token_copy(zero_ref, 0, xs_hbm, r, sem_row, ns).start()
                return carry

            def zero_wait(r, carry):
                _token_copy(zero_ref, 0, xs_hbm, r, sem_row, ns).wait()
                return carry

            lax.fori_loop(start, stop, zero_start, 0)
            lax.fori_loop(start, stop, zero_wait, 0)

        def tile_copy(ti):
            return pltpu.make_async_copy(zero_ref, xs_hbm.at[pl.ds(pl.multiple_of(ti * (tg * ns), tg * ns), tg * ns)], sem_row)

        def tile_start(ti, carry):
            tile_copy(ti).start()
            return carry

        def tile_wait(ti, carry):
            tile_copy(ti).wait()
            return carry

        used = stop // tg
        lax.fori_loop(used, n_tiles, tile_start, 0)
        lax.fori_loop(used, n_tiles, tile_wait, 0)

    idx.wait()

    def start(t, c):
        for k in range(TOP_K):
            _token_copy(x3_ref, t, xs_hbm, pos_smem[k, t], sem_row, ns).start(priority=k % 2)
        return c

    lax.fori_loop(0, td, start, 0, unroll=8)
    for _ in range(TOP_K):
        pltpu.make_async_copy(x3_ref, xs_hbm.at[pl.ds(0, td * ns)], sem_row).wait()


def moe_dispatch(pos, off, cnt, x, g, tg, n_tiles, td=1024):
    n, d = x.shape
    ns = d // V7X_LANES
    assert ns % V7X_SUBLANES == 0
    td = min(td, n)
    smem = pl.BlockSpec(memory_space=pltpu.SMEM)
    hbm = pl.BlockSpec(memory_space=pl.ANY)
    return pl.pallas_call(
        functools.partial(_dispatch_body, td=td, tg=tg, ne=cnt.shape[0], n_tiles=n_tiles, ns=ns),
        grid=(n // td,),
        in_specs=[hbm, smem, smem, pl.BlockSpec((td, d), lambda i: (i, 0)), pl.BlockSpec((1, d), lambda i: (0, 0))],
        out_specs=hbm,
        out_shape=jax.ShapeDtypeStruct((n_tiles * tg * ns, V7X_LANES), x.dtype),
        scratch_shapes=[
            pltpu.SMEM((TOK_ROWS, td), jnp.int32),
            pltpu.VMEM((td * ns, V7X_LANES), x.dtype),
            pltpu.VMEM((tg * ns, V7X_LANES), x.dtype),
            pltpu.SemaphoreType.DMA(()),
            pltpu.SemaphoreType.DMA(()),
        ],
        compiler_params=pltpu.CompilerParams(dimension_semantics=("arbitrary",), has_side_effects=True, vmem_limit_bytes=V7X_VMEM_LIMIT_BYTES),
        name="moe_dispatch",
    )(pos, off, cnt, x, g)


GROUP_SHORT_TILE_DIV = 4


def _grouped_ffn_body(te_ref, tv_ref, nv_ref, x_ref, wg_ref, wu_ref, wo_ref, o_ref, h_ref, acc_ref, *, tg, ns):
    i, j = pl.program_id(0), pl.program_id(1)
    d = ns * V7X_LANES

    @pl.when(i < nv_ref[0])
    def _():
        @pl.when(j == 0)
        def _():
            for c in range(ns):
                h_ref[:, c * V7X_LANES : (c + 1) * V7X_LANES] = _slab(x_ref, c, tg, ns).astype(BF16)

        def accumulate(m):
            y = _swiglu_step(h_ref[0:m, :], wg_ref, wu_ref, wo_ref)

            @pl.when(j == 0)
            def _():
                acc_ref[0:m, :] = y
                if m < tg:
                    acc_ref[m:, :] = jnp.zeros((tg - m, d), F32)

            @pl.when(j > 0)
            def _():
                acc_ref[0:m, :] += y

        short = tv_ref[i] <= tg // GROUP_SHORT_TILE_DIV
        pl.when(short)(lambda: accumulate(tg // GROUP_SHORT_TILE_DIV))
        pl.when(jnp.logical_not(short))(lambda: accumulate(tg))

        @pl.when(j == pl.num_programs(1) - 1)
        def _():
            _to_slabs(o_ref, acc_ref[...], ns)

    @pl.when((i >= nv_ref[0]) & (j == 0))
    def _():
        o_ref[...] = jnp.zeros_like(o_ref)


def moe_grouped_ffn(te, tv, nv, xs, w_in, w_out, layer, tg, tf=512):
    d = w_in.shape[2]
    ns = d // V7X_LANES
    r = xs.shape[0] // ns
    f = w_out.shape[2]
    tf = _block(f, tf)
    nf = f // tf
    n_tiles = r // tg

    def tile(i, nv):
        return jnp.minimum(i, nv[0] - 1)

    def fblk(i, j, nv):
        return jnp.where(i < nv[0], j, nf - 1)

    grid_spec = pltpu.PrefetchScalarGridSpec(
        num_scalar_prefetch=3,
        grid=(n_tiles, nf),
        in_specs=[
            pl.BlockSpec((tg * ns, V7X_LANES), lambda i, j, te, tv, nv: (tile(i, nv), 0)),
            pl.BlockSpec((None, None, d, tf), lambda i, j, te, tv, nv: (layer, te[i], 0, fblk(i, j, nv))),
            pl.BlockSpec((None, None, d, tf), lambda i, j, te, tv, nv: (layer, te[i], 0, fblk(i, j, nv) + nf)),
            pl.BlockSpec((None, None, tf, d), lambda i, j, te, tv, nv: (layer, te[i], fblk(i, j, nv), 0)),
        ],
        out_specs=pl.BlockSpec((tg * ns, V7X_LANES), lambda i, j, te, tv, nv: (i, 0)),
        scratch_shapes=[pltpu.VMEM((tg, d), BF16), pltpu.VMEM((tg, d), F32)],
    )
    return pl.pallas_call(
        functools.partial(_grouped_ffn_body, tg=tg, ns=ns),
        grid_spec=grid_spec,
        out_shape=jax.ShapeDtypeStruct((r * ns, V7X_LANES), F32),
        compiler_params=_params("parallel", "arbitrary"),
        name="moe_grouped_ffn",
    )(te, tv, nv, xs, w_in, w_in, w_out)


def _combine_body(pos_hbm, gate_ref, x_ref, ys_hbm, *rest, tc, ns, final):
    rest = list(rest)
    fg_ref = rest.pop(0) if final else None
    o_ref, pos_smem, buf_ref, sem_idx, sem_row = rest
    s, nsteps = pl.program_id(0), pl.num_programs(0)

    def idx_copy(step, slot):
        return pltpu.make_async_copy(pos_hbm.at[:, pl.ds(pl.multiple_of(step * tc, tc), tc)], pos_smem.at[slot], sem_idx.at[slot])

    def gather(slot):
        def start(t, c):
            for k in range(TOP_K):
                _token_copy(ys_hbm, pos_smem[slot, k, t], buf_ref.at[slot, k], t, sem_row.at[slot], ns).start(priority=k % 2)
            return c

        lax.fori_loop(0, tc, start, 0, unroll=8)

    def step(slot):
        other = 1 - slot

        @pl.when(s == 0)
        def _():
            idx_copy(s, slot).start()
            idx_copy(s, slot).wait()
            gather(slot)

            @pl.when(nsteps > 1)
            def _():
                idx_copy(s + 1, other).start()

        @pl.when(s + 1 < nsteps)
        def _():
            idx_copy(s + 1, other).wait()
            gather(other)

        @pl.when(s + 2 < nsteps)
        def _():
            idx_copy(s + 2, slot).start()

        for k in range(TOP_K):
            pltpu.make_async_copy(ys_hbm.at[pl.ds(0, tc * ns)], buf_ref.at[slot, k], sem_row.at[slot]).wait()
        g0, g1 = gate_ref[:, 0:1], gate_ref[:, 1:2]
        for c in range(ns):
            cols = slice(c * V7X_LANES, (c + 1) * V7X_LANES)
            o_ref[:, cols] = x_ref[:, cols] + g0 * _slab(buf_ref.at[slot, 0], c, tc, ns) + g1 * _slab(buf_ref.at[slot, 1], c, tc, ns)
        if final:
            o_ref[...] = _rms(o_ref[...], fg_ref[...])

    for slot in range(2):
        pl.when(s % 2 == slot)(functools.partial(step, slot))


def moe_combine(pos, gates, x, ys, final_g=None, tc=512):
    n, d = x.shape
    ns = d // V7X_LANES
    tc = min(tc, n)
    final = final_g is not None
    hbm = pl.BlockSpec(memory_space=pl.ANY)
    in_specs = [hbm, pl.BlockSpec((tc, V7X_LANES), lambda i: (i, 0)), pl.BlockSpec((tc, d), lambda i: (i, 0)), hbm]
    args = [pos, gates, x, ys]
    if final:
        in_specs.append(pl.BlockSpec((1, d), lambda i: (0, 0)))
        args.append(final_g)
    return pl.pallas_call(
        functools.partial(_combine_body, tc=tc, ns=ns, final=final),
        grid=(n // tc,),
        in_specs=in_specs,
        out_specs=pl.BlockSpec((tc, d), lambda i: (i, 0)),
        out_shape=jax.ShapeDtypeStruct((n, d), F32),
        scratch_shapes=[
            pltpu.SMEM((2, TOK_ROWS, tc), jnp.int32),
            pltpu.VMEM((2, TOP_K, tc * ns, V7X_LANES), F32),
            pltpu.SemaphoreType.DMA((2,)),
            pltpu.SemaphoreType.DMA((2,)),
        ],
        compiler_params=_params("arbitrary"),
        name="moe_combine",
    )(*args)


def moe_ffn(x, g, w_router, w_in, w_out, layer, final_g=None, tg=1024):
    n, _ = x.shape
    ne = w_router.shape[1]
    tg = min(tg, n)
    n_tiles = TOP_K * n // tg + ne
    gates, tok, counts = moe_router(x, g, w_router)
    cnt = counts[0, :ne].astype(jnp.int32)
    off, te, tv, nv, pos = moe_plan(cnt, tok, tg, n_tiles)
    xs = moe_dispatch(pos, off, cnt, x, g, tg, n_tiles)
    ys = moe_grouped_ffn(te, tv, nv, xs, w_in, w_out, layer, tg)
    return moe_combine(pos, gates, x, ys, final_g)


def _gla_gate_body(x_ref, g_ref, wl_ref, w2_ref, b_ref, o_ref):
    h = _rms(x_ref[...], g_ref[...]).astype(BF16)
    g_low = _dot(h, wl_ref[...]).astype(BF16)
    o_ref[...] = _log_sigmoid(_dot(g_low, w2_ref[...]) + b_ref[...]) / B_TAU


def gla_gate(x, g, w_low, w_gate2, bias, tm=1024):
    n, d = x.shape
    dk = w_gate2.shape[1]
    tm = min(tm, n)
    wl = jnp.pad(w_low, ((0, 0), (0, V7X_LANES - B_GATE_RANK))).astype(BF16)
    w2 = jnp.pad(w_gate2, ((0, V7X_LANES - B_GATE_RANK), (0, 0))).astype(BF16)
    return pl.pallas_call(
        _gla_gate_body,
        grid=(n // tm,),
        in_specs=[
            pl.BlockSpec((tm, d), lambda i: (i, 0)),
            pl.BlockSpec((1, d), lambda i: (0, 0)),
            pl.BlockSpec((d, V7X_LANES), lambda i: (0, 0)),
            pl.BlockSpec((V7X_LANES, dk), lambda i: (0, 0)),
            pl.BlockSpec((1, dk), lambda i: (0, 0)),
        ],
        out_specs=pl.BlockSpec((tm, dk), lambda i: (i, 0)),
        out_shape=jax.ShapeDtypeStruct((n, dk), F32),
        compiler_params=_params("parallel"),
        name="gla_gate",
    )(x, g, wl, w2, bias)


def _gla_body(q_ref, k_ref, v_ref, r_ref, la_ref, og_ref, o_ref, st_ref, *, tc, hk, hv):
    @pl.when(pl.program_id(1) == 0)
    def _():
        st_ref[...] = jnp.zeros_like(st_ref)

    c = B_CHUNK
    tril = _tril_mask(c)
    ones_tril = tril.astype(BF16)
    scale = hk**-0.5
    for ci in range(tc // c):
        rs = slice(ci * c, (ci + 1) * c)
        hi, mid, lo = _split3(la_ref[rs, :])
        bcum = _dot(ones_tril, hi) + _dot(ones_tril, mid) + _dot(ones_tril, lo)
        b_last = bcum[c - 1 : c, :]
        e_pos = jnp.exp(bcum)
        e_neg = jnp.exp(-bcum)
        e_end = jnp.exp(b_last - bcum)
        dec = jnp.exp(b_last)
        for h in range(B_HEADS):
            ks = slice(h * hk, (h + 1) * hk)
            vs = slice(h * hv, (h + 1) * hv)
            q = q_ref[rs, ks].astype(F32) * scale
            k = k_ref[rs, ks].astype(F32)
            v = v_ref[rs, vs]
            q_d = (q * e_pos[:, ks]).astype(BF16)
            k_d = (k * e_neg[:, ks]).astype(BF16)
            k_end = (k * e_end[:, ks]).astype(BF16)
            att = jnp.where(tril, _dot_nt(q_d, k_d), 0.0)
            st = st_ref[h]
            o = _dot(att.astype(BF16), v) + _dot_nt(q_d, st.astype(BF16))
            st_ref[h] = dec[:, ks] * st + _dot_tn(v, k_end)
            y = _rms(o, og_ref[...]).astype(BF16).astype(F32)
            r = r_ref[rs, vs].astype(F32)
            o_ref[rs, vs] = (y * (r * jax.nn.sigmoid(r))).astype(BF16)


def gla_chunks(proj, la, o_g, bsz, tc=256):
    n = proj.shape[0]
    dk = la.shape[1]
    dv = (proj.shape[1] - 2 * dk) // 2
    s = n // bsz
    tc = min(tc, s)
    nt = s // tc
    hk, hv = dk // B_HEADS, dv // B_HEADS
    assert (2 * dk) % dv == 0
    v_blk = 2 * dk // dv
    row = lambda b, t: b * nt + t
    return pl.pallas_call(
        functools.partial(_gla_body, tc=tc, hk=hk, hv=hv),
        grid=(bsz, nt),
        in_specs=[
            pl.BlockSpec((tc, dk), lambda b, t: (row(b, t), 0)),
            pl.BlockSpec((tc, dk), lambda b, t: (row(b, t), 1)),
            pl.BlockSpec((tc, dv), lambda b, t: (row(b, t), v_blk)),
            pl.BlockSpec((tc, dv), lambda b, t: (row(b, t), v_blk + 1)),
            pl.BlockSpec((tc, dk), lambda b, t: (row(b, t), 0)),
            pl.BlockSpec((1, hv), lambda b, t: (0, 0)),
        ],
        out_specs=pl.BlockSpec((tc, dv), lambda b, t: (row(b, t), 0)),
        out_shape=jax.ShapeDtypeStruct((n, dv), BF16),
        scratch_shapes=[pltpu.VMEM((B_HEADS, hv, hk), F32)],
        compiler_params=_params("parallel", "arbitrary"),
        name="gla_chunks",
    )(proj, proj, proj, proj, la, o_g)


def _fox_gate_body(x_ref, g_ref, wf_ref, b_ref, crow_ref, carry_ref, *, tb, nh):
    @pl.when(pl.program_id(1) == 0)
    def _():
        carry_ref[...] = jnp.zeros_like(carry_ref)

    h = _rms(x_ref[...], g_ref[...]).astype(BF16)
    lane = lax.broadcasted_iota(jnp.int32, (tb, V7X_LANES), 1)
    log_f = jnp.where(lane < nh, _log_sigmoid(_dot(h, wf_ref[...]) + b_ref[...]), 0.0)
    ones_tril = _tril_mask(tb).astype(BF16)
    hi, mid, lo = _split3(log_f)
    c = carry_ref[...] + (_dot(ones_tril, hi) + _dot(ones_tril, mid) + _dot(ones_tril, lo))
    carry_ref[...] = c[tb - 1 : tb, :]
    sel = (lax.broadcasted_iota(jnp.int32, (nh, V7X_LANES), 0) == lax.broadcasted_iota(jnp.int32, (nh, V7X_LANES), 1)).astype(BF16)
    hi, mid, lo = _split3(c)
    crow_ref[...] = _dot_nt(sel, hi) + _dot_nt(sel, mid) + _dot_nt(sel, lo)


def fox_gate(x, g, w_f, f_bias, bsz, tb=512):
    n, d = x.shape
    nh = w_f.shape[1]
    s = n // bsz
    tb = min(tb, s)
    nt = s // tb
    wf = jnp.pad(w_f, ((0, 0), (0, V7X_LANES - nh))).astype(BF16)
    bias = jnp.pad(f_bias.reshape(1, nh), ((0, 0), (0, V7X_LANES - nh)))
    return pl.pallas_call(
        functools.partial(_fox_gate_body, tb=tb, nh=nh),
        grid=(bsz, nt),
        in_specs=[
            pl.BlockSpec((tb, d), lambda b, t: (b * nt + t, 0)),
            pl.BlockSpec((1, d), lambda b, t: (0, 0)),
            pl.BlockSpec((d, V7X_LANES), lambda b, t: (0, 0)),
            pl.BlockSpec((1, V7X_LANES), lambda b, t: (0, 0)),
        ],
        out_specs=pl.BlockSpec((None, nh, tb), lambda b, t: (b, 0, t)),
        out_shape=jax.ShapeDtypeStruct((bsz, nh, s), F32),
        scratch_shapes=[pltpu.VMEM((1, V7X_LANES), F32)],
        compiler_params=_params("parallel", "arbitrary"),
        name="fox_gate",
    )(x, g, wf, bias)


FOX_KEY_BLOCK = 512
FOX_SUBTILES = 8
FOX_ROW_CHUNK = 32


def _fox_flash_body(q_ref, k_ref, v_ref, crow_ref, o_ref, qs_ref, va_ref, s_ref, p_ref, al_ref, m_ref, acc_ref, *, t, nsub, scale):
    i = pl.program_id(2)
    dh = q_ref.shape[1]
    nl = t // V7X_LANES

    @pl.when(i == 0)
    def _():
        va_ref[:, :dh] = v_ref[...]
        va_ref[:, dh:] = jnp.ones((va_ref.shape[0], V7X_LANES), BF16)

    qs_ref[...] = (q_ref[...].astype(F32) * scale).astype(BF16)
    m_ref[...] = jnp.full_like(m_ref, -jnp.inf)
    acc_ref[...] = jnp.zeros_like(acc_ref)

    def block(sub, kb, masked):
        rows = slice(sub * t, (sub + 1) * t)
        off = pl.multiple_of(kb * t, t)
        s_ref[sub] = _dot_nt(qs_ref[rows, :], k_ref[pl.ds(off, t), :])
        bias = crow_ref[pl.ds(kb, 1), :]

        def chunk(c, carry):
            r = pl.multiple_of(c * FOX_ROW_CHUNK, FOX_ROW_CHUNK)
            sc = s_ref[sub, pl.ds(r, FOX_ROW_CHUNK), :] - bias
            if masked:
                row = r + lax.broadcasted_iota(jnp.int32, sc.shape, 0)
                col = lax.broadcasted_iota(jnp.int32, sc.shape, 1)
                sc = jnp.where(col <= row, sc, -jnp.inf)
            tiles = [sc[:, a * V7X_LANES : (a + 1) * V7X_LANES] for a in range(nl)]
            mx = functools.reduce(jnp.maximum, tiles)
            m_old = m_ref[pl.ds(sub * t + r, FOX_ROW_CHUNK), :]
            m_new = jnp.maximum(m_old, jnp.max(mx, axis=-1, keepdims=True))
            al_ref[sub, pl.ds(r, FOX_ROW_CHUNK), :] = jnp.exp(m_old - m_new)
            m_ref[pl.ds(sub * t + r, FOX_ROW_CHUNK), :] = m_new
            for a in range(nl):
                p_ref[sub, pl.ds(r, FOX_ROW_CHUNK), a * V7X_LANES : (a + 1) * V7X_LANES] = jnp.exp(tiles[a] - m_new).astype(BF16)
            return carry

        lax.fori_loop(0, t // FOX_ROW_CHUNK, chunk, 0, unroll=True)
        pv = _dot(p_ref[sub], va_ref[pl.ds(off, t), :])
        alpha = al_ref[sub]
        acc_ref[rows, :dh] = alpha * acc_ref[rows, :dh] + pv[:, :dh]
        acc_ref[rows, dh:] = alpha * acc_ref[rows, dh:] + pv[:, dh:]

    def below_diagonal(kb, carry):
        for sub in range(nsub):
            block(sub, kb, False)
        return carry

    first = i * nsub
    lax.fori_loop(0, first, below_diagonal, 0)
    for kb in range(nsub):
        for sub in range(kb, nsub):
            block(sub, first + kb, masked=(sub == kb))
    o_ref[...] = (acc_ref[:, :dh] / acc_ref[:, dh:]).astype(o_ref.dtype)


def fox_flash(proj, crow, bsz, nh):
    n = proj.shape[0]
    dh = proj.shape[1] // (3 * nh)
    assert dh == V7X_LANES
    s = n // bsz
    t = min(FOX_KEY_BLOCK, s)
    nsub = min(FOX_SUBTILES, s // t)
    tq = t * nsub
    nq = s // tq
    crow = crow.reshape(bsz, nh, s // t, t)
    return pl.pallas_call(
        functools.partial(_fox_flash_body, t=t, nsub=nsub, scale=dh**-0.5),
        grid=(bsz, nh, nq),
        in_specs=[
            pl.BlockSpec((tq, dh), lambda b, h, i: (b * nq + i, h)),
            pl.BlockSpec((s, dh), lambda b, h, i: (b, nh + h)),
            pl.BlockSpec((s, dh), lambda b, h, i: (b, 2 * nh + h)),
            pl.BlockSpec((None, None, s // t, t), lambda b, h, i: (b, h, 0, 0)),
        ],
        out_specs=pl.BlockSpec((tq, dh), lambda b, h, i: (b * nq + i, h)),
        out_shape=jax.ShapeDtypeStruct((n, nh * dh), BF16),
        scratch_shapes=[
            pltpu.VMEM((tq, dh), BF16),
            pltpu.VMEM((s, dh + V7X_LANES), BF16),
            pltpu.VMEM((nsub, t, t), F32),
            pltpu.VMEM((nsub, t, t), BF16),
            pltpu.VMEM((nsub, t, V7X_LANES), F32),
            pltpu.VMEM((tq, V7X_LANES), F32),
            pltpu.VMEM((tq, dh + V7X_LANES), F32),
        ],
        compiler_params=_params("parallel", "parallel", "arbitrary"),
        name="fox_flash",
    )(proj, proj, proj, crow)


def kernel(x, norm1_g, norm2_g, a_w_in, a_vnorm_g, a_w_s, a_b_s, a_w_out, b_w_in, b_w_gate2, b_gate_bias, b_onorm_g, b_w_out, c_w_in, c_f_bias, c_w_out, ffn_w_in, ffn_w_out, moe_router, moe_w_in, moe_w_out, final_g):
    bsz, s, d = x.shape
    depth = norm1_g.shape[0]
    xf = x.reshape(bsz * s, d)
    moe_w_in_bf16, moe_w_out_bf16 = moe_w_in.astype(BF16), moe_w_out.astype(BF16)
    for i in range(depth):
        g1 = norm1_g[i].reshape(1, d)
        m, j = i % N_MIXERS, i // N_MIXERS
        if m == 0:
            z = norm_matmul(xf, g1, a_w_in[j].astype(BF16), act="gelu")
            xf = gmlp_spatial_out(z, xf, a_vnorm_g[j].reshape(1, -1), a_w_s[j], a_b_s[j].T, a_w_out[j].astype(BF16))
        elif m == 1:
            dk = b_w_gate2.shape[2]
            n_main = b_w_in.shape[2] - B_GATE_RANK
            proj = norm_matmul(xf, g1, b_w_in[j, :, :n_main].astype(BF16))
            la = gla_gate(xf, g1, b_w_in[j, :, n_main:], b_w_gate2[j], b_gate_bias[j].reshape(1, dk))
            o = gla_chunks(proj, la, b_onorm_g[j].reshape(1, -1), bsz)
            xf = matmul_residual(o, b_w_out[j].astype(BF16), xf)
        else:
            proj = norm_matmul(xf, g1, c_w_in[j, :, : 3 * d].astype(BF16))
            crow = fox_gate(xf, g1, c_w_in[j, :, 3 * d :], c_f_bias[j], bsz)
            o = fox_flash(proj, crow, bsz, C_HEADS)
            xf = matmul_residual(o, c_w_out[j].astype(BF16), xf)
        g2 = norm2_g[i].reshape(1, d)
        fg = final_g.reshape(1, d) if i == depth - 1 else None
        if i % 2 == 0:
            xf = ffn(xf, g2, ffn_w_in[i // 2].astype(BF16), ffn_w_out[i // 2].astype(BF16), final_g=fg)
        else:
            xf = moe_ffn(xf, g2, moe_router[i // 2], moe_w_in_bf16, moe_w_out_bf16, i // 2, final_g=fg)
    return xf.reshape(bsz, s, d)
```

```python
import functools

import jax
import jax.numpy as jnp
from jax import lax
from jax.experimental import pallas as pl
from jax.experimental.pallas import tpu as pltpu

F32 = jnp.float32
BF16 = jnp.bfloat16

EPS = 1e-6
N_MIXERS = 3
TOP_K = 2
A_CHUNK = 128
A_GROUPS = 8
B_HEADS = 4
B_GATE_RANK = 16
B_TAU = 16.0
B_CHUNK = 64
C_HEADS = 8

V7X_LANES = 128
V7X_SUBLANES = 8
V7X_VMEM_LIMIT_BYTES = 56 * 1024 * 1024
V7X_VMEM_RESIDENT_LIMIT_BYTES = 62 * 1024 * 1024


def _params(*sem):
    return pltpu.CompilerParams(dimension_semantics=sem, vmem_limit_bytes=V7X_VMEM_LIMIT_BYTES)


def _block(total, target, align=V7X_LANES):
    if total <= target:
        return total
    return max(b for b in range(align, target + 1, align) if total % b == 0)


def _rms(x, g):
    ms = jnp.mean(x * x, axis=-1, keepdims=True)
    return x * lax.rsqrt(ms + EPS) * g


def _log_sigmoid(x):
    return jnp.minimum(x, 0.0) - jnp.log1p(jnp.exp(-jnp.abs(x)))


def _split3(x):
    hi = x.astype(BF16)
    r1 = x - hi.astype(F32)
    mid = r1.astype(BF16)
    lo = (r1 - mid.astype(F32)).astype(BF16)
    return hi, mid, lo


def _dot(a, b):
    return jnp.dot(a, b, preferred_element_type=F32)


def _dot_nt(a, b):
    return lax.dot_general(a, b, (((1,), (1,)), ((), ())), preferred_element_type=F32)


def _dot_tn(a, b):
    return lax.dot_general(a, b, (((0,), (0,)), ((), ())), preferred_element_type=F32)


def _tril_mask(n):
    row = lax.broadcasted_iota(jnp.int32, (n, n), 0)
    col = lax.broadcasted_iota(jnp.int32, (n, n), 1)
    return col <= row


def _norm_matmul_body(x_ref, g_ref, w_ref, o_ref, h_ref, *, act):
    @pl.when(pl.program_id(1) == 0)
    def _():
        h_ref[...] = _rms(x_ref[...], g_ref[...]).astype(BF16)

    y = _dot(h_ref[...], w_ref[...])
    if act == "gelu":
        y = jax.nn.gelu(y, approximate=True)
    o_ref[...] = y.astype(o_ref.dtype)


def norm_matmul(x, g, w, act=None, tm=1024, tn=1024):
    n, d = x.shape
    nout = w.shape[1]
    tm, tn = min(tm, n), _block(nout, tn)
    return pl.pallas_call(
        functools.partial(_norm_matmul_body, act=act),
        grid=(n // tm, nout // tn),
        in_specs=[
            pl.BlockSpec((tm, d), lambda i, j: (i, 0)),
            pl.BlockSpec((1, d), lambda i, j: (0, 0)),
            pl.BlockSpec((d, tn), lambda i, j: (0, j)),
        ],
        out_specs=pl.BlockSpec((tm, tn), lambda i, j: (i, j)),
        out_shape=jax.ShapeDtypeStruct((n, nout), BF16),
        scratch_shapes=[pltpu.VMEM((tm, d), BF16)],
        compiler_params=_params("parallel", "arbitrary"),
        name="norm_matmul",
    )(x, g, w)


def _matmul_residual_body(y_ref, w_ref, x_ref, o_ref):
    o_ref[...] = x_ref[...] + _dot(y_ref[...], w_ref[...])


def matmul_residual(y, w, x, tm=1024):
    n, k = y.shape
    d = w.shape[1]
    tm = min(tm, n)
    return pl.pallas_call(
        _matmul_residual_body,
        grid=(n // tm,),
        in_specs=[
            pl.BlockSpec((tm, k), lambda i: (i, 0)),
            pl.BlockSpec((k, d), lambda i: (0, 0)),
            pl.BlockSpec((tm, d), lambda i: (i, 0)),
        ],
        out_specs=pl.BlockSpec((tm, d), lambda i: (i, 0)),
        out_shape=jax.ShapeDtypeStruct((n, d), F32),
        compiler_params=_params("parallel"),
        name="matmul_residual",
    )(y, w, x)


def _gmlp_body(z_ref, x_ref, vg_ref, ws_ref, bt_ref, wo_ref, o_ref, vn_ref, y_ref, *, tm, width):
    gd = width // A_GROUPS
    vn_ref[...] = _rms(z_ref[:, width:].astype(F32), vg_ref[...]).astype(BF16)
    tril = _tril_mask(A_CHUNK)
    for g in range(A_GROUPS):
        w = jnp.where(tril, ws_ref[g], 0.0).astype(BF16)
        bias = bt_ref[:, g : g + 1]
        cs = slice(g * gd, (g + 1) * gd)
        for c in range(tm // A_CHUNK):
            rs = slice(c * A_CHUNK, (c + 1) * A_CHUNK)
            mixed = _dot(w, vn_ref[rs, cs]) + bias
            y_ref[rs, cs] = (z_ref[rs, cs].astype(F32) * mixed).astype(BF16)
    o_ref[...] = x_ref[...] + _dot(y_ref[...], wo_ref[...])


def gmlp_spatial_out(z, x, vg, ws, bt, wo, tm=512):
    n, d = x.shape
    width = z.shape[1] // 2
    tm = min(tm, n)
    return pl.pallas_call(
        functools.partial(_gmlp_body, tm=tm, width=width),
        grid=(n // tm,),
        in_specs=[
            pl.BlockSpec((tm, 2 * width), lambda i: (i, 0)),
            pl.BlockSpec((tm, d), lambda i: (i, 0)),
            pl.BlockSpec((1, width), lambda i: (0, 0)),
            pl.BlockSpec((A_GROUPS, A_CHUNK, A_CHUNK), lambda i: (0, 0, 0)),
            pl.BlockSpec((A_CHUNK, A_GROUPS), lambda i: (0, 0)),
            pl.BlockSpec((width, d), lambda i: (0, 0)),
        ],
        out_specs=pl.BlockSpec((tm, d), lambda i: (i, 0)),
        out_shape=jax.ShapeDtypeStruct((n, d), F32),
        scratch_shapes=[pltpu.VMEM((tm, width), BF16), pltpu.VMEM((tm, width), BF16)],
        compiler_params=_params("parallel"),
        name="gmlp_spatial_out",
    )(z, x, vg, ws, bt, wo)


def _swiglu_step(h, wg_ref, wu_ref, wo_ref):
    a = _dot(h, wg_ref[...])
    t = (a * jax.nn.sigmoid(a)) * _dot(h, wu_ref[...])
    return _dot(t.astype(BF16), wo_ref[...])


def _ffn_body(*refs, final):
    x_ref, g_ref, wg_ref, wu_ref, wo_ref = refs[:5]
    rest = list(refs[5:])
    fg_ref = rest.pop(0) if final else None
    o_ref, h_ref, acc_ref = rest
    j = pl.program_id(1)

    @pl.when(j == 0)
    def _():
        h_ref[...] = _rms(x_ref[...], g_ref[...]).astype(BF16)
        acc_ref[...] = jnp.zeros_like(acc_ref)

    acc_ref[...] += _swiglu_step(h_ref[...], wg_ref, wu_ref, wo_ref)

    @pl.when(j == pl.num_programs(1) - 1)
    def _():
        y = x_ref[...] + acc_ref[...]
        o_ref[...] = _rms(y, fg_ref[...]) if final else y


def _ffn_resident_body(*refs, f, tf, final):
    x_ref, g_ref, wi_ref, wo_ref = refs[:4]
    rest = list(refs[4:])
    fg_ref = rest.pop(0) if final else None
    o_ref, h_ref, t_ref = rest
    h_ref[...] = _rms(x_ref[...], g_ref[...]).astype(BF16)
    for c in range(f // tf):
        a = _dot(h_ref[...], wi_ref[:, c * tf : (c + 1) * tf])
        u = _dot(h_ref[...], wi_ref[:, f + c * tf : f + (c + 1) * tf])
        t_ref[:, c * tf : (c + 1) * tf] = ((a * jax.nn.sigmoid(a)) * u).astype(BF16)
    y = x_ref[...] + _dot(t_ref[...], wo_ref[...])
    o_ref[...] = _rms(y, fg_ref[...]) if final else y


def ffn_resident(x, g, w_in, w_out, final_g=None, tm=1024, tf=512):
    n, d = x.shape
    f = w_out.shape[0]
    tm, tf = min(tm, n), min(tf, f)
    final = final_g is not None
    once = pl.Buffered(1)
    in_specs = [
        pl.BlockSpec((tm, d), lambda i: (i, 0)),
        pl.BlockSpec((1, d), lambda i: (0, 0)),
        pl.BlockSpec((d, 2 * f), lambda i: (0, 0), pipeline_mode=once),
        pl.BlockSpec((f, d), lambda i: (0, 0), pipeline_mode=once),
    ]
    args = [x, g, w_in, w_out]
    if final:
        in_specs.append(pl.BlockSpec((1, d), lambda i: (0, 0)))
        args.append(final_g)
    return pl.pallas_call(
        functools.partial(_ffn_resident_body, f=f, tf=tf, final=final),
        grid=(n // tm,),
        in_specs=in_specs,
        out_specs=pl.BlockSpec((tm, d), lambda i: (i, 0)),
        out_shape=jax.ShapeDtypeStruct((n, d), F32),
        scratch_shapes=[pltpu.VMEM((tm, d), BF16), pltpu.VMEM((tm, f), BF16)],
        compiler_params=pltpu.CompilerParams(dimension_semantics=("parallel",), vmem_limit_bytes=V7X_VMEM_RESIDENT_LIMIT_BYTES),
        name="ffn_resident",
    )(*args)


def ffn(x, g, w_in, w_out, final_g=None, tm=1024, tf=512):
    n, d = x.shape
    f = w_out.shape[0]
    tm, tf = min(tm, n), min(tf, f)
    nf = f // tf
    final = final_g is not None
    in_specs = [
        pl.BlockSpec((tm, d), lambda i, j: (i, 0)),
        pl.BlockSpec((1, d), lambda i, j: (0, 0)),
        pl.BlockSpec((d, tf), lambda i, j: (0, j)),
        pl.BlockSpec((d, tf), lambda i, j: (0, j + nf)),
        pl.BlockSpec((tf, d), lambda i, j: (j, 0)),
    ]
    args = [x, g, w_in, w_in, w_out]
    if final:
        in_specs.append(pl.BlockSpec((1, d), lambda i, j: (0, 0)))
        args.append(final_g)
    return pl.pallas_call(
        functools.partial(_ffn_body, final=final),
        grid=(n // tm, nf),
        in_specs=in_specs,
        out_specs=pl.BlockSpec((tm, d), lambda i, j: (i, 0)),
        out_shape=jax.ShapeDtypeStruct((n, d), F32),
        scratch_shapes=[pltpu.VMEM((tm, d), BF16), pltpu.VMEM((tm, d), F32)],
        compiler_params=_params("parallel", "arbitrary"),
        name="ffn",
    )(*args)


TOK_ROWS = 8


def _router_body(x_ref, g_ref, whi_ref, wlo_ref, gate_ref, tok_ref, cnt_ref, carry_ref, *, ne, tm):
    @pl.when(pl.program_id(0) == 0)
    def _():
        carry_ref[...] = jnp.zeros_like(carry_ref)

    h = _rms(x_ref[...], g_ref[...])
    h_hi = h.astype(BF16)
    h_lo = (h - h_hi.astype(F32)).astype(BF16)
    logits = _dot(h_hi, whi_ref[...]) + _dot(h_lo, whi_ref[...]) + _dot(h_hi, wlo_ref[...])
    lane = lax.broadcasted_iota(jnp.int32, logits.shape, 1).astype(F32)
    neg = -jnp.inf
    l1 = jnp.where(lane < ne, logits, neg)
    m1 = jnp.max(l1, axis=-1, keepdims=True)
    i1 = jnp.min(jnp.where(l1 == m1, lane, float(V7X_LANES)), axis=-1, keepdims=True)
    l2 = jnp.where(lane == i1, neg, l1)
    m2 = jnp.max(l2, axis=-1, keepdims=True)
    i2 = jnp.min(jnp.where(l2 == m2, lane, float(V7X_LANES)), axis=-1, keepdims=True)
    e2 = jnp.exp(m2 - m1)
    den = 1.0 + e2
    gate_ref[...] = jnp.where(lane == 0.0, 1.0 / den, 0.0) + jnp.where(lane == 1.0, e2 / den, 0.0)

    sel1, sel2 = lane == i1, lane == i2
    onehot = jnp.where(sel1 | sel2, 1.0, 0.0)
    row = lax.broadcasted_iota(jnp.int32, (tm, tm), 0)
    col = lax.broadcasted_iota(jnp.int32, (tm, tm), 1)
    before = carry_ref[...] + _dot((col < row).astype(BF16), onehot.astype(BF16))
    carry_ref[...] = before[tm - 1 : tm, :] + onehot[tm - 1 : tm, :]
    cnt_ref[...] = carry_ref[...]
    r1 = jnp.sum(jnp.where(sel1, before, 0.0), axis=-1, keepdims=True)
    r2 = jnp.sum(jnp.where(sel2, before, 0.0), axis=-1, keepdims=True)
    table = (jnp.where(lane == 0.0, i1, 0.0) + jnp.where(lane == 1.0, i2, 0.0)
             + jnp.where(lane == 2.0, r1, 0.0) + jnp.where(lane == 3.0, r2, 0.0))
    pick = (lax.broadcasted_iota(jnp.int32, (TOK_ROWS, V7X_LANES), 0) == lax.broadcasted_iota(jnp.int32, (TOK_ROWS, V7X_LANES), 1)).astype(BF16)
    hi, mid, lo = _split3(table)
    tok_ref[...] = (_dot_nt(pick, hi) + _dot_nt(pick, mid) + _dot_nt(pick, lo)).astype(jnp.int32)


def moe_router(x, g, w_router, tm=512):
    n, d = x.shape
    ne = w_router.shape[1]
    tm = min(tm, n)
    wpad = jnp.pad(w_router, ((0, 0), (0, V7X_LANES - ne)))
    w_hi = wpad.astype(BF16)
    w_lo = (wpad - w_hi.astype(F32)).astype(BF16)
    return pl.pallas_call(
        functools.partial(_router_body, ne=ne, tm=tm),
        grid=(n // tm,),
        in_specs=[
            pl.BlockSpec((tm, d), lambda i: (i, 0)),
            pl.BlockSpec((1, d), lambda i: (0, 0)),
            pl.BlockSpec((d, V7X_LANES), lambda i: (0, 0)),
            pl.BlockSpec((d, V7X_LANES), lambda i: (0, 0)),
        ],
        out_specs=[
            pl.BlockSpec((tm, V7X_LANES), lambda i: (i, 0)),
            pl.BlockSpec((TOK_ROWS, tm), lambda i: (0, i)),
            pl.BlockSpec((1, V7X_LANES), lambda i: (0, 0)),
        ],
        out_shape=[
            jax.ShapeDtypeStruct((n, V7X_LANES), F32),
            jax.ShapeDtypeStruct((TOK_ROWS, n), jnp.int32),
            jax.ShapeDtypeStruct((1, V7X_LANES), F32),
        ],
        scratch_shapes=[pltpu.VMEM((1, V7X_LANES), F32)],
        compiler_params=_params("arbitrary"),
        name="moe_router",
    )(x, g, w_hi, w_lo)


def _plan_body(cnt_ref, tok_ref, off_ref, te_ref, tv_ref, nv_ref, pos_ref, *, ne, tg, n_tiles):
    off = jnp.int32(0)
    tile = jnp.int32(0)
    expert = tok_ref[0:TOP_K, :]
    group_start = jnp.zeros_like(expert)
    for e in range(ne):
        nt = (cnt_ref[e] + (tg - 1)) // tg
        off_ref[e] = off
        group_start = jnp.where(expert == e, off, group_start)

        def mark(ti, carry, e=e, first=tile):
            te_ref[ti] = jnp.int32(e)
            tv_ref[ti] = jnp.minimum(cnt_ref[e] - (ti - first) * tg, tg)
            return carry

        lax.fori_loop(tile, tile + nt, mark, 0)
        off = off + nt * tg
        tile = tile + nt
    pos_ref[...] = jnp.zeros_like(pos_ref)
    pos_ref[0:TOP_K, :] = group_start + tok_ref[TOP_K : 2 * TOP_K, :]
    nv_ref[0] = tile
    last = te_ref[jnp.maximum(tile - 1, 0)]

    def fill(ti, carry):
        te_ref[ti] = last
        tv_ref[ti] = jnp.int32(0)
        return carry

    lax.fori_loop(tile, n_tiles, fill, 0)


def moe_plan(cnt, tok, tg, n_tiles):
    ne = cnt.shape[0]
    smem = pl.BlockSpec(memory_space=pltpu.SMEM)
    vmem = pl.BlockSpec(memory_space=pltpu.VMEM)
    return pl.pallas_call(
        functools.partial(_plan_body, ne=ne, tg=tg, n_tiles=n_tiles),
        in_specs=[smem, vmem],
        out_specs=[smem, smem, smem, smem, vmem],
        out_shape=[
            jax.ShapeDtypeStruct((ne,), jnp.int32),
            jax.ShapeDtypeStruct((n_tiles,), jnp.int32),
            jax.ShapeDtypeStruct((n_tiles,), jnp.int32),
            jax.ShapeDtypeStruct((1,), jnp.int32),
            jax.ShapeDtypeStruct(tok.shape, jnp.int32),
        ],
        name="moe_plan",
    )(cnt, tok)


def _token_copy(src, s, dst, r, sem, ns):
    return pltpu.make_async_copy(src.at[pl.ds(pl.multiple_of(s * ns, ns), ns)], dst.at[pl.ds(pl.multiple_of(r * ns, ns), ns)], sem)


def _to_slabs(dst_ref, val, ns):
    rows = val.shape[0]
    for c in range(ns):
        dst_ref[pl.ds(c, rows, stride=ns), :] = val[:, c * V7X_LANES : (c + 1) * V7X_LANES]


def _slab(src_ref, c, rows, ns):
    return src_ref[pl.ds(c, rows, stride=ns), :]


def _dispatch_body(pos_hbm, off_ref, cnt_ref, x_ref, g_ref, xs_hbm, pos_smem, x3_ref, zero_ref, sem_idx, sem_row, *, td, tg, ne, n_tiles, ns):
    base = pl.multiple_of(pl.program_id(0) * td, td)
    idx = pltpu.make_async_copy(pos_hbm.at[:, pl.ds(base, td)], pos_smem, sem_idx)
    idx.start()
    _to_slabs(x3_ref, _rms(x_ref[...], g_ref[...]), ns)

    @pl.when(pl.program_id(0) == 0)
    def _():
        zero_ref[...] = jnp.zeros_like(zero_ref)
        for e in range(ne):
            start = off_ref[e] + cnt_ref[e]
            stop = off_ref[e] + (cnt_ref[e] + (tg - 1)) // tg * tg

            def zero_start(r, carry):
                _token_copy(zero_ref, 0, xs_hbm, r, sem_row, ns).start()
                return carry

            def zero_wait(r, carry):
                _token_copy(zero_ref, 0, xs_hbm, r, sem_row, ns).wait()
                return carry

            lax.fori_loop(start, stop, zero_start, 0)
            lax.fori_loop(start, stop, zero_wait, 0)

        def tile_copy(ti):
            return pltpu.make_async_copy(zero_ref, xs_hbm.at[pl.ds(pl.multiple_of(ti * (tg * ns), tg * ns), tg * ns)], sem_row)

        def tile_start(ti, carry):
            tile_copy(ti).start()
            return carry

        def tile_wait(ti, carry):
            tile_copy(ti).wait()
            return carry

        used = stop // tg
        lax.fori_loop(used, n_tiles, tile_start, 0)
        lax.fori_loop(used, n_tiles, tile_wait, 0)

    idx.wait()

    def start(t, c):
        for k in range(TOP_K):
            _token_copy(x3_ref, t, xs_hbm, pos_smem[k, t], sem_row, ns).start(priority=k % 2)
        return c

    lax.fori_loop(0, td, start, 0, unroll=8)
    for _ in range(TOP_K):
        pltpu.make_async_copy(x3_ref, xs_hbm.at[pl.ds(0, td * ns)], sem_row).wait()


def moe_dispatch(pos, off, cnt, x, g, tg, n_tiles, td=1024):
    n, d = x.shape
    ns = d // V7X_LANES
    assert ns % V7X_SUBLANES == 0
    td = min(td, n)
    smem = pl.BlockSpec(memory_space=pltpu.SMEM)
    hbm = pl.BlockSpec(memory_space=pl.ANY)
    return pl.pallas_call(
        functools.partial(_dispatch_body, td=td, tg=tg, ne=cnt.shape[0], n_tiles=n_tiles, ns=ns),
        grid=(n // td,),
        in_specs=[hbm, smem, smem, pl.BlockSpec((td, d), lambda i: (i, 0)), pl.BlockSpec((1, d), lambda i: (0, 0))],
        out_specs=hbm,
        out_shape=jax.ShapeDtypeStruct((n_tiles * tg * ns, V7X_LANES), x.dtype),
        scratch_shapes=[
            pltpu.SMEM((TOK_ROWS, td), jnp.int32),
            pltpu.VMEM((td * ns, V7X_LANES), x.dtype),
            pltpu.VMEM((tg * ns, V7X_LANES), x.dtype),
            pltpu.SemaphoreType.DMA(()),
            pltpu.SemaphoreType.DMA(()),
        ],
        compiler_params=pltpu.CompilerParams(dimension_semantics=("arbitrary",), has_side_effects=True, vmem_limit_bytes=V7X_VMEM_LIMIT_BYTES),
        name="moe_dispatch",
    )(pos, off, cnt, x, g)


GROUP_SHORT_TILE_DIV = 4


def _grouped_ffn_body(te_ref, tv_ref, nv_ref, x_ref, wi_ref, wo_ref, o_ref, h_ref, t_ref, *, tg, ns, f, tf):
    i = pl.program_id(0)

    @pl.when(i < nv_ref[0])
    def _():
        for c in range(ns):
            h_ref[:, c * V7X_LANES : (c + 1) * V7X_LANES] = _slab(x_ref, c, tg, ns).astype(BF16)

        def run(m):
            for c in range(f // tf):
                a = _dot(h_ref[0:m, :], wi_ref[:, c * tf : (c + 1) * tf])
                u = _dot(h_ref[0:m, :], wi_ref[:, f + c * tf : f + (c + 1) * tf])
                t_ref[0:m, c * tf : (c + 1) * tf] = ((a * jax.nn.sigmoid(a)) * u).astype(BF16)
            y = _dot(t_ref[0:m, :], wo_ref[...])
            for c in range(ns):
                o_ref[pl.ds(c, m, stride=ns), :] = y[:, c * V7X_LANES : (c + 1) * V7X_LANES]
            if m < tg:
                o_ref[m * ns :, :] = jnp.zeros(((tg - m) * ns, V7X_LANES), F32)

        short = tv_ref[i] <= tg // GROUP_SHORT_TILE_DIV
        pl.when(short)(lambda: run(tg // GROUP_SHORT_TILE_DIV))
        pl.when(jnp.logical_not(short))(lambda: run(tg))

    @pl.when(i >= nv_ref[0])
    def _():
        o_ref[...] = jnp.zeros_like(o_ref)


def moe_grouped_ffn(te, tv, nv, xs, w_in, w_out, layer, tg, tf=512):
    d = w_in.shape[2]
    ns = d // V7X_LANES
    r = xs.shape[0] // ns
    f = w_out.shape[2]
    tf = _block(f, tf)
    n_tiles = r // tg
    once = pl.Buffered(1)

    def tile(i, nv):
        return jnp.minimum(i, nv[0] - 1)

    grid_spec = pltpu.PrefetchScalarGridSpec(
        num_scalar_prefetch=3,
        grid=(n_tiles,),
        in_specs=[
            pl.BlockSpec((tg * ns, V7X_LANES), lambda i, te, tv, nv: (tile(i, nv), 0)),
            pl.BlockSpec((None, None, d, 2 * f), lambda i, te, tv, nv: (layer, te[i], 0, 0), pipeline_mode=once),
            pl.BlockSpec((None, None, f, d), lambda i, te, tv, nv: (layer, te[i], 0, 0), pipeline_mode=once),
        ],
        out_specs=pl.BlockSpec((tg * ns, V7X_LANES), lambda i, te, tv, nv: (i, 0)),
        scratch_shapes=[pltpu.VMEM((tg, d), BF16), pltpu.VMEM((tg, f), BF16)],
    )
    return pl.pallas_call(
        functools.partial(_grouped_ffn_body, tg=tg, ns=ns, f=f, tf=tf),
        grid_spec=grid_spec,
        out_shape=jax.ShapeDtypeStruct((r * ns, V7X_LANES), F32),
        compiler_params=pltpu.CompilerParams(dimension_semantics=("arbitrary",), vmem_limit_bytes=V7X_VMEM_RESIDENT_LIMIT_BYTES),
        name="moe_grouped_ffn",
    )(te, tv, nv, xs, w_in, w_out)


def _combine_body(pos_hbm, gate_ref, x_ref, ys_hbm, *rest, tc, ns, final):
    rest = list(rest)
    fg_ref = rest.pop(0) if final else None
    o_ref, pos_smem, buf_ref, sem_idx, sem_row = rest
    s, nsteps = pl.program_id(0), pl.num_programs(0)

    def idx_copy(step, slot):
        return pltpu.make_async_copy(pos_hbm.at[:, pl.ds(pl.multiple_of(step * tc, tc), tc)], pos_smem.at[slot], sem_idx.at[slot])

    def gather(slot):
        def start(t, c):
            for k in range(TOP_K):
                _token_copy(ys_hbm, pos_smem[slot, k, t], buf_ref.at[slot, k], t, sem_row.at[slot], ns).start(priority=k % 2)
            return c

        lax.fori_loop(0, tc, start, 0, unroll=8)

    def step(slot):
        other = 1 - slot

        @pl.when(s == 0)
        def _():
            idx_copy(s, slot).start()
            idx_copy(s, slot).wait()
            gather(slot)

            @pl.when(nsteps > 1)
            def _():
                idx_copy(s + 1, other).start()

        @pl.when(s + 1 < nsteps)
        def _():
            idx_copy(s + 1, other).wait()
            gather(other)

        @pl.when(s + 2 < nsteps)
        def _():
            idx_copy(s + 2, slot).start()

        for k in range(TOP_K):
            pltpu.make_async_copy(ys_hbm.at[pl.ds(0, tc * ns)], buf_ref.at[slot, k], sem_row.at[slot]).wait()
        g0, g1 = gate_ref[:, 0:1], gate_ref[:, 1:2]
        for c in range(ns):
            cols = slice(c * V7X_LANES, (c + 1) * V7X_LANES)
            o_ref[:, cols] = x_ref[:, cols] + g0 * _slab(buf_ref.at[slot, 0], c, tc, ns) + g1 * _slab(buf_ref.at[slot, 1], c, tc, ns)
        if final:
            o_ref[...] = _rms(o_ref[...], fg_ref[...])

    for slot in range(2):
        pl.when(s % 2 == slot)(functools.partial(step, slot))


def moe_combine(pos, gates, x, ys, final_g=None, tc=512):
    n, d = x.shape
    ns = d // V7X_LANES
    tc = min(tc, n)
    final = final_g is not None
    hbm = pl.BlockSpec(memory_space=pl.ANY)
    in_specs = [hbm, pl.BlockSpec((tc, V7X_LANES), lambda i: (i, 0)), pl.BlockSpec((tc, d), lambda i: (i, 0)), hbm]
    args = [pos, gates, x, ys]
    if final:
        in_specs.append(pl.BlockSpec((1, d), lambda i: (0, 0)))
        args.append(final_g)
    return pl.pallas_call(
        functools.partial(_combine_body, tc=tc, ns=ns, final=final),
        grid=(n // tc,),
        in_specs=in_specs,
        out_specs=pl.BlockSpec((tc, d), lambda i: (i, 0)),
        out_shape=jax.ShapeDtypeStruct((n, d), F32),
        scratch_shapes=[
            pltpu.SMEM((2, TOK_ROWS, tc), jnp.int32),
            pltpu.VMEM((2, TOP_K, tc * ns, V7X_LANES), F32),
            pltpu.SemaphoreType.DMA((2,)),
            pltpu.SemaphoreType.DMA((2,)),
        ],
        compiler_params=_params("arbitrary"),
        name="moe_combine",
    )(*args)


def moe_ffn(x, g, w_router, w_in, w_out, layer, final_g=None, tg=1024):
    n, _ = x.shape
    ne = w_router.shape[1]
    tg = min(tg, n)
    n_tiles = TOP_K * n // tg + ne
    gates, tok, counts = moe_router(x, g, w_router)
    cnt = counts[0, :ne].astype(jnp.int32)
    off, te, tv, nv, pos = moe_plan(cnt, tok, tg, n_tiles)
    xs = moe_dispatch(pos, off, cnt, x, g, tg, n_tiles)
    ys = moe_grouped_ffn(te, tv, nv, xs, w_in, w_out, layer, tg)
    return moe_combine(pos, gates, x, ys, final_g)


def _gla_gate_body(x_ref, g_ref, wl_ref, w2_ref, b_ref, o_ref):
    h = _rms(x_ref[...], g_ref[...]).astype(BF16)
    g_low = _dot(h, wl_ref[...]).astype(BF16)
    o_ref[...] = _log_sigmoid(_dot(g_low, w2_ref[...]) + b_ref[...]) / B_TAU


def gla_gate(x, g, w_low, w_gate2, bias, tm=1024):
    n, d = x.shape
    dk = w_gate2.shape[1]
    tm = min(tm, n)
    wl = jnp.pad(w_low, ((0, 0), (0, V7X_LANES - B_GATE_RANK))).astype(BF16)
    w2 = jnp.pad(w_gate2, ((0, V7X_LANES - B_GATE_RANK), (0, 0))).astype(BF16)
    return pl.pallas_call(
        _gla_gate_body,
        grid=(n // tm,),
        in_specs=[
            pl.BlockSpec((tm, d), lambda i: (i, 0)),
            pl.BlockSpec((1, d), lambda i: (0, 0)),
            pl.BlockSpec((d, V7X_LANES), lambda i: (0, 0)),
            pl.BlockSpec((V7X_LANES, dk), lambda i: (0, 0)),
            pl.BlockSpec((1, dk), lambda i: (0, 0)),
        ],
        out_specs=pl.BlockSpec((tm, dk), lambda i: (i, 0)),
        out_shape=jax.ShapeDtypeStruct((n, dk), F32),
        compiler_params=_params("parallel"),
        name="gla_gate",
    )(x, g, wl, w2, bias)


def _gla_body(q_ref, k_ref, v_ref, r_ref, la_ref, og_ref, o_ref, st_ref, *, tc, hk, hv):
    @pl.when(pl.program_id(1) == 0)
    def _():
        st_ref[...] = jnp.zeros_like(st_ref)

    c = B_CHUNK
    tril = _tril_mask(c)
    ones_tril = tril.astype(BF16)
    scale = hk**-0.5
    for ci in range(tc // c):
        rs = slice(ci * c, (ci + 1) * c)
        hi, mid, lo = _split3(la_ref[rs, :])
        bcum = _dot(ones_tril, hi) + _dot(ones_tril, mid) + _dot(ones_tril, lo)
        b_last = bcum[c - 1 : c, :]
        e_pos = jnp.exp(bcum)
        e_neg = jnp.exp(-bcum)
        e_end = jnp.exp(b_last - bcum)
        dec = jnp.exp(b_last)
        for h in range(B_HEADS):
            ks = slice(h * hk, (h + 1) * hk)
            vs = slice(h * hv, (h + 1) * hv)
            q = q_ref[rs, ks].astype(F32) * scale
            k = k_ref[rs, ks].astype(F32)
            v = v_ref[rs, vs]
            q_d = (q * e_pos[:, ks]).astype(BF16)
            k_d = (k * e_neg[:, ks]).astype(BF16)
            k_end = (k * e_end[:, ks]).astype(BF16)
            att = jnp.where(tril, _dot_nt(q_d, k_d), 0.0)
            st = st_ref[h]
            o = _dot(att.astype(BF16), v) + _dot_nt(q_d, st.astype(BF16))
            st_ref[h] = dec[:, ks] * st + _dot_tn(v, k_end)
            y = _rms(o, og_ref[...]).astype(BF16).astype(F32)
            r = r_ref[rs, vs].astype(F32)
            o_ref[rs, vs] = (y * (r * jax.nn.sigmoid(r))).astype(BF16)


def gla_chunks(proj, la, o_g, bsz, tc=256):
    n = proj.shape[0]
    dk = la.shape[1]
    dv = (proj.shape[1] - 2 * dk) // 2
    s = n // bsz
    tc = min(tc, s)
    nt = s // tc
    hk, hv = dk // B_HEADS, dv // B_HEADS
    assert (2 * dk) % dv == 0
    v_blk = 2 * dk // dv
    row = lambda b, t: b * nt + t
    return pl.pallas_call(
        functools.partial(_gla_body, tc=tc, hk=hk, hv=hv),
        grid=(bsz, nt),
        in_specs=[
            pl.BlockSpec((tc, dk), lambda b, t: (row(b, t), 0)),
            pl.BlockSpec((tc, dk), lambda b, t: (row(b, t), 1)),
            pl.BlockSpec((tc, dv), lambda b, t: (row(b, t), v_blk)),
            pl.BlockSpec((tc, dv), lambda b, t: (row(b, t), v_blk + 1)),
            pl.BlockSpec((tc, dk), lambda b, t: (row(b, t), 0)),
            pl.BlockSpec((1, hv), lambda b, t: (0, 0)),
        ],
        out_specs=pl.BlockSpec((tc, dv), lambda b, t: (row(b, t), 0)),
        out_shape=jax.ShapeDtypeStruct((n, dv), BF16),
        scratch_shapes=[pltpu.VMEM((B_HEADS, hv, hk), F32)],
        compiler_params=_params("parallel", "arbitrary"),
        name="gla_chunks",
    )(proj, proj, proj, proj, la, o_g)


def _fox_gate_body(x_ref, g_ref, wf_ref, b_ref, crow_ref, carry_ref, *, tb, nh):
    @pl.when(pl.program_id(1) == 0)
    def _():
        carry_ref[...] = jnp.zeros_like(carry_ref)

    h = _rms(x_ref[...], g_ref[...]).astype(BF16)
    lane = lax.broadcasted_iota(jnp.int32, (tb, V7X_LANES), 1)
    log_f = jnp.where(lane < nh, _log_sigmoid(_dot(h, wf_ref[...]) + b_ref[...]), 0.0)
    ones_tril = _tril_mask(tb).astype(BF16)
    hi, mid, lo = _split3(log_f)
    c = carry_ref[...] + (_dot(ones_tril, hi) + _dot(ones_tril, mid) + _dot(ones_tril, lo))
    carry_ref[...] = c[tb - 1 : tb, :]
    sel = (lax.broadcasted_iota(jnp.int32, (nh, V7X_LANES), 0) == lax.broadcasted_iota(jnp.int32, (nh, V7X_LANES), 1)).astype(BF16)
    hi, mid, lo = _split3(c)
    crow_ref[...] = _dot_nt(sel, hi) + _dot_nt(sel, mid) + _dot_nt(sel, lo)


def fox_gate(x, g, w_f, f_bias, bsz, tb=512):
    n, d = x.shape
    nh = w_f.shape[1]
    s = n // bsz
    tb = min(tb, s)
    nt = s // tb
    wf = jnp.pad(w_f, ((0, 0), (0, V7X_LANES - nh))).astype(BF16)
    bias = jnp.pad(f_bias.reshape(1, nh), ((0, 0), (0, V7X_LANES - nh)))
    return pl.pallas_call(
        functools.partial(_fox_gate_body, tb=tb, nh=nh),
        grid=(bsz, nt),
        in_specs=[
            pl.BlockSpec((tb, d), lambda b, t: (b * nt + t, 0)),
            pl.BlockSpec((1, d), lambda b, t: (0, 0)),
            pl.BlockSpec((d, V7X_LANES), lambda b, t: (0, 0)),
            pl.BlockSpec((1, V7X_LANES), lambda b, t: (0, 0)),
        ],
        out_specs=pl.BlockSpec((None, nh, tb), lambda b, t: (b, 0, t)),
        out_shape=jax.ShapeDtypeStruct((bsz, nh, s), F32),
        scratch_shapes=[pltpu.VMEM((1, V7X_LANES), F32)],
        compiler_params=_params("parallel", "arbitrary"),
        name="fox_gate",
    )(x, g, wf, bias)


FOX_KEY_BLOCK = 512
FOX_SUBTILES = 8
FOX_ROW_CHUNK = 32


def _fox_flash_body(q_ref, k_ref, v_ref, crow_ref, o_ref, qs_ref, va_ref, s_ref, p_ref, al_ref, m_ref, acc_ref, *, t, nsub, scale):
    i = pl.program_id(2)
    dh = q_ref.shape[1]
    nl = t // V7X_LANES

    @pl.when(i == 0)
    def _():
        va_ref[:, :dh] = v_ref[...]
        va_ref[:, dh:] = jnp.ones((va_ref.shape[0], V7X_LANES), BF16)

    qs_ref[...] = (q_ref[...].astype(F32) * scale).astype(BF16)
    m_ref[...] = jnp.full_like(m_ref, -jnp.inf)
    acc_ref[...] = jnp.zeros_like(acc_ref)

    def block(sub, kb, masked):
        rows = slice(sub * t, (sub + 1) * t)
        off = pl.multiple_of(kb * t, t)
        s_ref[sub] = _dot_nt(qs_ref[rows, :], k_ref[pl.ds(off, t), :])
        bias = crow_ref[pl.ds(kb, 1), :]

        def chunk(c, carry):
            r = pl.multiple_of(c * FOX_ROW_CHUNK, FOX_ROW_CHUNK)
            sc = s_ref[sub, pl.ds(r, FOX_ROW_CHUNK), :] - bias
            if masked:
                row = r + lax.broadcasted_iota(jnp.int32, sc.shape, 0)
                col = lax.broadcasted_iota(jnp.int32, sc.shape, 1)
                sc = jnp.where(col <= row, sc, -jnp.inf)
            tiles = [sc[:, a * V7X_LANES : (a + 1) * V7X_LANES] for a in range(nl)]
            mx = functools.reduce(jnp.maximum, tiles)
            m_old = m_ref[pl.ds(sub * t + r, FOX_ROW_CHUNK), :]
            m_new = jnp.maximum(m_old, jnp.max(mx, axis=-1, keepdims=True))
            al_ref[sub, pl.ds(r, FOX_ROW_CHUNK), :] = jnp.exp(m_old - m_new)
            m_ref[pl.ds(sub * t + r, FOX_ROW_CHUNK), :] = m_new
            for a in range(nl):
                p_ref[sub, pl.ds(r, FOX_ROW_CHUNK), a * V7X_LANES : (a + 1) * V7X_LANES] = jnp.exp(tiles[a] - m_new).astype(BF16)
            return carry

        lax.fori_loop(0, t // FOX_ROW_CHUNK, chunk, 0, unroll=True)
        pv = _dot(p_ref[sub], va_ref[pl.ds(off, t), :])
        alpha = al_ref[sub]
        acc_ref[rows, :dh] = alpha * acc_ref[rows, :dh] + pv[:, :dh]
        acc_ref[rows, dh:] = alpha * acc_ref[rows, dh:] + pv[:, dh:]

    def below_diagonal(kb, carry):
        for sub in range(nsub):
            block(sub, kb, False)
        return carry

    first = i * nsub
    lax.fori_loop(0, first, below_diagonal, 0)
    for kb in range(nsub):
        for sub in range(kb, nsub):
            block(sub, first + kb, masked=(sub == kb))
    o_ref[...] = (acc_ref[:, :dh] / acc_ref[:, dh:]).astype(o_ref.dtype)


def fox_flash(proj, crow, bsz, nh):
    n = proj.shape[0]
    dh = proj.shape[1] // (3 * nh)
    assert dh == V7X_LANES
    s = n // bsz
    t = min(FOX_KEY_BLOCK, s)
    nsub = min(FOX_SUBTILES, s // t)
    tq = t * nsub
    nq = s // tq
    crow = crow.reshape(bsz, nh, s // t, t)
    return pl.pallas_call(
        functools.partial(_fox_flash_body, t=t, nsub=nsub, scale=dh**-0.5),
        grid=(bsz, nh, nq),
        in_specs=[
            pl.BlockSpec((tq, dh), lambda b, h, i: (b * nq + i, h)),
            pl.BlockSpec((s, dh), lambda b, h, i: (b, nh + h)),
            pl.BlockSpec((s, dh), lambda b, h, i: (b, 2 * nh + h)),
            pl.BlockSpec((None, None, s // t, t), lambda b, h, i: (b, h, 0, 0)),
        ],
        out_specs=pl.BlockSpec((tq, dh), lambda b, h, i: (b * nq + i, h)),
        out_shape=jax.ShapeDtypeStruct((n, nh * dh), BF16),
        scratch_shapes=[
            pltpu.VMEM((tq, dh), BF16),
            pltpu.VMEM((s, dh + V7X_LANES), BF16),
            pltpu.VMEM((nsub, t, t), F32),
            pltpu.VMEM((nsub, t, t), BF16),
            pltpu.VMEM((nsub, t, V7X_LANES), F32),
            pltpu.VMEM((tq, V7X_LANES), F32),
            pltpu.VMEM((tq, dh + V7X_LANES), F32),
        ],
        compiler_params=_params("parallel", "parallel", "arbitrary"),
        name="fox_flash",
    )(proj, proj, proj, crow)


def kernel(x, norm1_g, norm2_g, a_w_in, a_vnorm_g, a_w_s, a_b_s, a_w_out, b_w_in, b_w_gate2, b_gate_bias, b_onorm_g, b_w_out, c_w_in, c_f_bias, c_w_out, ffn_w_in, ffn_w_out, moe_router, moe_w_in, moe_w_out, final_g):
    bsz, s, d = x.shape
    depth = norm1_g.shape[0]
    xf = x.reshape(bsz * s, d)
    moe_w_in_bf16, moe_w_out_bf16 = moe_w_in.astype(BF16), moe_w_out.astype(BF16)
    for i in range(depth):
        g1 = norm1_g[i].reshape(1, d)
        m, j = i % N_MIXERS, i // N_MIXERS
        if m == 0:
            z = norm_matmul(xf, g1, a_w_in[j].astype(BF16), act="gelu")
            xf = gmlp_spatial_out(z, xf, a_vnorm_g[j].reshape(1, -1), a_w_s[j], a_b_s[j].T, a_w_out[j].astype(BF16))
        elif m == 1:
            dk = b_w_gate2.shape[2]
            n_main = b_w_in.shape[2] - B_GATE_RANK
            proj = norm_matmul(xf, g1, b_w_in[j, :, :n_main].astype(BF16))
            la = gla_gate(xf, g1, b_w_in[j, :, n_main:], b_w_gate2[j], b_gate_bias[j].reshape(1, dk))
            o = gla_chunks(proj, la, b_onorm_g[j].reshape(1, -1), bsz)
            xf = matmul_residual(o, b_w_out[j].astype(BF16), xf)
        else:
            proj = norm_matmul(xf, g1, c_w_in[j, :, : 3 * d].astype(BF16))
            crow = fox_gate(xf, g1, c_w_in[j, :, 3 * d :], c_f_bias[j], bsz)
            o = fox_flash(proj, crow, bsz, C_HEADS)
            xf = matmul_residual(o, c_w_out[j].astype(BF16), xf)
        g2 = norm2_g[i].reshape(1, d)
        fg = final_g.reshape(1, d) if i == depth - 1 else None
        if i % 2 == 0:
            xf = ffn_resident(xf, g2, ffn_w_in[i // 2].astype(BF16), ffn_w_out[i // 2].astype(BF16), final_g=fg)
        else:
            xf = moe_ffn(xf, g2, moe_router[i // 2], moe_w_in_bf16, moe_w_out_bf16, i // 2, final_g=fg)
    return xf.reshape(bsz, s, d)
```

```python
import functools

import jax
import jax.numpy as jnp
from jax import lax
from jax.experimental import pallas as pl
from jax.experimental.pallas import tpu as pltpu

F32 = jnp.float32
BF16 = jnp.bfloat16

EPS = 1e-6
N_MIXERS = 3
TOP_K = 2
A_CHUNK = 128
A_GROUPS = 8
B_HEADS = 4
B_GATE_RANK = 16
B_TAU = 16.0
B_CHUNK = 64
C_HEADS = 8

V7X_LANES = 128
V7X_SUBLANES = 8
V7X_VMEM_LIMIT_BYTES = 56 * 1024 * 1024
V7X_VMEM_RESIDENT_LIMIT_BYTES = 62 * 1024 * 1024


def _params(*sem):
    return pltpu.CompilerParams(dimension_semantics=sem, vmem_limit_bytes=V7X_VMEM_LIMIT_BYTES)


def _block(total, target, align=V7X_LANES):
    if total <= target:
        return total
    return max(b for b in range(align, target + 1, align) if total % b == 0)


def _rms(x, g):
    ms = jnp.mean(x * x, axis=-1, keepdims=True)
    return x * lax.rsqrt(ms + EPS) * g


def _log_sigmoid(x):
    return jnp.minimum(x, 0.0) - jnp.log1p(jnp.exp(-jnp.abs(x)))


def _split3(x):
    hi = x.astype(BF16)
    r1 = x - hi.astype(F32)
    mid = r1.astype(BF16)
    lo = (r1 - mid.astype(F32)).astype(BF16)
    return hi, mid, lo


def _dot(a, b):
    return jnp.dot(a, b, preferred_element_type=F32)


def _dot_nt(a, b):
    return lax.dot_general(a, b, (((1,), (1,)), ((), ())), preferred_element_type=F32)


def _dot_tn(a, b):
    return lax.dot_general(a, b, (((0,), (0,)), ((), ())), preferred_element_type=F32)


def _tril_mask(n):
    row = lax.broadcasted_iota(jnp.int32, (n, n), 0)
    col = lax.broadcasted_iota(jnp.int32, (n, n), 1)
    return col <= row


def _norm_matmul_body(x_ref, g_ref, w_ref, o_ref, h_ref, *, tn):
    h_ref[...] = _rms(x_ref[...], g_ref[...]).astype(BF16)
    for c in range(w_ref.shape[1] // tn):
        cols = slice(c * tn, (c + 1) * tn)
        o_ref[:, cols] = _dot(h_ref[...], w_ref[:, cols]).astype(o_ref.dtype)


def norm_matmul(x, g, w, tm=1024, tn=1024):
    n, d = x.shape
    nout = w.shape[1]
    tm, tn = min(tm, n), _block(nout, tn)
    return pl.pallas_call(
        functools.partial(_norm_matmul_body, tn=tn),
        grid=(n // tm,),
        in_specs=[
            pl.BlockSpec((tm, d), lambda i: (i, 0)),
            pl.BlockSpec((1, d), lambda i: (0, 0)),
            pl.BlockSpec((d, nout), lambda i: (0, 0), pipeline_mode=pl.Buffered(1)),
        ],
        out_specs=pl.BlockSpec((tm, nout), lambda i: (i, 0)),
        out_shape=jax.ShapeDtypeStruct((n, nout), BF16),
        scratch_shapes=[pltpu.VMEM((tm, d), BF16)],
        compiler_params=_params("parallel"),
        name="norm_matmul",
    )(x, g, w)


def _matmul_residual_body(y_ref, w_ref, x_ref, o_ref):
    o_ref[...] = x_ref[...] + _dot(y_ref[...], w_ref[...])


def matmul_residual(y, w, x, tm=1024):
    n, k = y.shape
    d = w.shape[1]
    tm = min(tm, n)
    return pl.pallas_call(
        _matmul_residual_body,
        grid=(n // tm,),
        in_specs=[
            pl.BlockSpec((tm, k), lambda i: (i, 0)),
            pl.BlockSpec((k, d), lambda i: (0, 0)),
            pl.BlockSpec((tm, d), lambda i: (i, 0)),
        ],
        out_specs=pl.BlockSpec((tm, d), lambda i: (i, 0)),
        out_shape=jax.ShapeDtypeStruct((n, d), F32),
        compiler_params=_params("parallel"),
        name="matmul_residual",
    )(y, w, x)


def _gmlp_body(x_ref, g_ref, wi_ref, vg_ref, ws_ref, bt_ref, wo_ref, o_ref, h_ref, z_ref, vn_ref, y_ref, *, tm, width, tn):
    gd = width // A_GROUPS
    h_ref[...] = _rms(x_ref[...], g_ref[...]).astype(BF16)
    for c in range(2 * width // tn):
        cols = slice(c * tn, (c + 1) * tn)
        z_ref[:, cols] = jax.nn.gelu(_dot(h_ref[...], wi_ref[:, cols]), approximate=True).astype(BF16)
    vn_ref[...] = _rms(z_ref[:, width:].astype(F32), vg_ref[...]).astype(BF16)
    tril = _tril_mask(A_CHUNK)
    for g in range(A_GROUPS):
        w = jnp.where(tril, ws_ref[g], 0.0).astype(BF16)
        bias = bt_ref[:, g : g + 1]
        cs = slice(g * gd, (g + 1) * gd)
        for c in range(tm // A_CHUNK):
            rs = slice(c * A_CHUNK, (c + 1) * A_CHUNK)
            mixed = _dot(w, vn_ref[rs, cs]) + bias
            y_ref[rs, cs] = (z_ref[rs, cs].astype(F32) * mixed).astype(BF16)
    o_ref[...] = x_ref[...] + _dot(y_ref[...], wo_ref[...])


def gmlp_mixer(x, g, wi, vg, ws, bt, wo, tm=1024, tn=1024):
    n, d = x.shape
    width = wi.shape[1] // 2
    tm, tn = min(tm, n), _block(2 * width, tn)
    once = pl.Buffered(1)
    return pl.pallas_call(
        functools.partial(_gmlp_body, tm=tm, width=width, tn=tn),
        grid=(n // tm,),
        in_specs=[
            pl.BlockSpec((tm, d), lambda i: (i, 0)),
            pl.BlockSpec((1, d), lambda i: (0, 0)),
            pl.BlockSpec((d, 2 * width), lambda i: (0, 0), pipeline_mode=once),
            pl.BlockSpec((1, width), lambda i: (0, 0)),
            pl.BlockSpec((A_GROUPS, A_CHUNK, A_CHUNK), lambda i: (0, 0, 0)),
            pl.BlockSpec((A_CHUNK, A_GROUPS), lambda i: (0, 0)),
            pl.BlockSpec((width, d), lambda i: (0, 0), pipeline_mode=once),
        ],
        out_specs=pl.BlockSpec((tm, d), lambda i: (i, 0)),
        out_shape=jax.ShapeDtypeStruct((n, d), F32),
        scratch_shapes=[
            pltpu.VMEM((tm, d), BF16),
            pltpu.VMEM((tm, 2 * width), BF16),
            pltpu.VMEM((tm, width), BF16),
            pltpu.VMEM((tm, width), BF16),
        ],
        compiler_params=pltpu.CompilerParams(dimension_semantics=("parallel",), vmem_limit_bytes=V7X_VMEM_RESIDENT_LIMIT_BYTES),
        name="gmlp_mixer",
    )(x, g, wi, vg, ws, bt, wo)


def _ffn_resident_body(*refs, f, tf, final):
    x_ref, g_ref, wi_ref, wo_ref = refs[:4]
    rest = list(refs[4:])
    fg_ref = rest.pop(0) if final else None
    o_ref, h_ref, t_ref = rest
    h_ref[...] = _rms(x_ref[...], g_ref[...]).astype(BF16)
    for c in range(f // tf):
        a = _dot(h_ref[...], wi_ref[:, c * tf : (c + 1) * tf])
        u = _dot(h_ref[...], wi_ref[:, f + c * tf : f + (c + 1) * tf])
        t_ref[:, c * tf : (c + 1) * tf] = ((a * jax.nn.sigmoid(a)) * u).astype(BF16)
    y = x_ref[...] + _dot(t_ref[...], wo_ref[...])
    o_ref[...] = _rms(y, fg_ref[...]) if final else y


def ffn_resident(x, g, w_in, w_out, final_g=None, tm=1024, tf=512):
    n, d = x.shape
    f = w_out.shape[0]
    tm, tf = min(tm, n), min(tf, f)
    final = final_g is not None
    once = pl.Buffered(1)
    in_specs = [
        pl.BlockSpec((tm, d), lambda i: (i, 0)),
        pl.BlockSpec((1, d), lambda i: (0, 0)),
        pl.BlockSpec((d, 2 * f), lambda i: (0, 0), pipeline_mode=once),
        pl.BlockSpec((f, d), lambda i: (0, 0), pipeline_mode=once),
    ]
    args = [x, g, w_in, w_out]
    if final:
        in_specs.append(pl.BlockSpec((1, d), lambda i: (0, 0)))
        args.append(final_g)
    return pl.pallas_call(
        functools.partial(_ffn_resident_body, f=f, tf=tf, final=final),
        grid=(n // tm,),
        in_specs=in_specs,
        out_specs=pl.BlockSpec((tm, d), lambda i: (i, 0)),
        out_shape=jax.ShapeDtypeStruct((n, d), F32),
        scratch_shapes=[pltpu.VMEM((tm, d), BF16), pltpu.VMEM((tm, f), BF16)],
        compiler_params=pltpu.CompilerParams(dimension_semantics=("parallel",), vmem_limit_bytes=V7X_VMEM_RESIDENT_LIMIT_BYTES),
        name="ffn_resident",
    )(*args)


TOK_ROWS = 8


def _router_body(x_ref, g_ref, whi_ref, wlo_ref, gate_ref, tok_ref, cnt_ref, carry_ref, *, ne, tm):
    @pl.when(pl.program_id(0) == 0)
    def _():
        carry_ref[...] = jnp.zeros_like(carry_ref)

    h = _rms(x_ref[...], g_ref[...])
    h_hi = h.astype(BF16)
    h_lo = (h - h_hi.astype(F32)).astype(BF16)
    logits = _dot(h_hi, whi_ref[...]) + _dot(h_lo, whi_ref[...]) + _dot(h_hi, wlo_ref[...])
    lane = lax.broadcasted_iota(jnp.int32, logits.shape, 1).astype(F32)
    neg = -jnp.inf
    l1 = jnp.where(lane < ne, logits, neg)
    m1 = jnp.max(l1, axis=-1, keepdims=True)
    i1 = jnp.min(jnp.where(l1 == m1, lane, float(V7X_LANES)), axis=-1, keepdims=True)
    l2 = jnp.where(lane == i1, neg, l1)
    m2 = jnp.max(l2, axis=-1, keepdims=True)
    i2 = jnp.min(jnp.where(l2 == m2, lane, float(V7X_LANES)), axis=-1, keepdims=True)
    e2 = jnp.exp(m2 - m1)
    den = 1.0 + e2
    gate_ref[...] = jnp.where(lane == 0.0, 1.0 / den, 0.0) + jnp.where(lane == 1.0, e2 / den, 0.0)

    sel1, sel2 = lane == i1, lane == i2
    onehot = jnp.where(sel1 | sel2, 1.0, 0.0)
    row = lax.broadcasted_iota(jnp.int32, (tm, tm), 0)
    col = lax.broadcasted_iota(jnp.int32, (tm, tm), 1)
    before = carry_ref[...] + _dot((col < row).astype(BF16), onehot.astype(BF16))
    carry_ref[...] = before[tm - 1 : tm, :] + onehot[tm - 1 : tm, :]
    cnt_ref[...] = carry_ref[...]
    r1 = jnp.sum(jnp.where(sel1, before, 0.0), axis=-1, keepdims=True)
    r2 = jnp.sum(jnp.where(sel2, before, 0.0), axis=-1, keepdims=True)
    table = (jnp.where(lane == 0.0, i1, 0.0) + jnp.where(lane == 1.0, i2, 0.0)
             + jnp.where(lane == 2.0, r1, 0.0) + jnp.where(lane == 3.0, r2, 0.0))
    pick = (lax.broadcasted_iota(jnp.int32, (TOK_ROWS, V7X_LANES), 0) == lax.broadcasted_iota(jnp.int32, (TOK_ROWS, V7X_LANES), 1)).astype(BF16)
    hi, mid, lo = _split3(table)
    tok_ref[...] = (_dot_nt(pick, hi) + _dot_nt(pick, mid) + _dot_nt(pick, lo)).astype(jnp.int32)


def moe_router(x, g, w_router, tm=512):
    n, d = x.shape
    ne = w_router.shape[1]
    tm = min(tm, n)
    wpad = jnp.pad(w_router, ((0, 0), (0, V7X_LANES - ne)))
    w_hi = wpad.astype(BF16)
    w_lo = (wpad - w_hi.astype(F32)).astype(BF16)
    return pl.pallas_call(
        functools.partial(_router_body, ne=ne, tm=tm),
        grid=(n // tm,),
        in_specs=[
            pl.BlockSpec((tm, d), lambda i: (i, 0)),
            pl.BlockSpec((1, d), lambda i: (0, 0)),
            pl.BlockSpec((d, V7X_LANES), lambda i: (0, 0)),
            pl.BlockSpec((d, V7X_LANES), lambda i: (0, 0)),
        ],
        out_specs=[
            pl.BlockSpec((tm, V7X_LANES), lambda i: (i, 0)),
            pl.BlockSpec((TOK_ROWS, tm), lambda i: (0, i)),
            pl.BlockSpec((1, V7X_LANES), lambda i: (0, 0)),
        ],
        out_shape=[
            jax.ShapeDtypeStruct((n, V7X_LANES), F32),
            jax.ShapeDtypeStruct((TOK_ROWS, n), jnp.int32),
            jax.ShapeDtypeStruct((1, V7X_LANES), F32),
        ],
        scratch_shapes=[pltpu.VMEM((1, V7X_LANES), F32)],
        compiler_params=_params("arbitrary"),
        name="moe_router",
    )(x, g, w_hi, w_lo)


def _plan_body(cnt_ref, tok_ref, off_ref, te_ref, tv_ref, nv_ref, pos_ref, *, ne, tg, n_tiles):
    off = jnp.int32(0)
    tile = jnp.int32(0)
    expert = tok_ref[0:TOP_K, :]
    group_start = jnp.zeros_like(expert)
    for e in range(ne):
        nt = (cnt_ref[e] + (tg - 1)) // tg
        off_ref[e] = off
        group_start = jnp.where(expert == e, off, group_start)

        def mark(ti, carry, e=e, first=tile):
            te_ref[ti] = jnp.int32(e)
            tv_ref[ti] = jnp.minimum(cnt_ref[e] - (ti - first) * tg, tg)
            return carry

        lax.fori_loop(tile, tile + nt, mark, 0)
        off = off + nt * tg
        tile = tile + nt
    pos_ref[...] = jnp.zeros_like(pos_ref)
    pos_ref[0:TOP_K, :] = group_start + tok_ref[TOP_K : 2 * TOP_K, :]
    nv_ref[0] = tile
    last = te_ref[jnp.maximum(tile - 1, 0)]

    def fill(ti, carry):
        te_ref[ti] = last
        tv_ref[ti] = jnp.int32(0)
        return carry

    lax.fori_loop(tile, n_tiles, fill, 0)


def moe_plan(cnt, tok, tg, n_tiles):
    ne = cnt.shape[0]
    smem = pl.BlockSpec(memory_space=pltpu.SMEM)
    vmem = pl.BlockSpec(memory_space=pltpu.VMEM)
    return pl.pallas_call(
        functools.partial(_plan_body, ne=ne, tg=tg, n_tiles=n_tiles),
        in_specs=[smem, vmem],
        out_specs=[smem, smem, smem, smem, vmem],
        out_shape=[
            jax.ShapeDtypeStruct((ne,), jnp.int32),
            jax.ShapeDtypeStruct((n_tiles,), jnp.int32),
            jax.ShapeDtypeStruct((n_tiles,), jnp.int32),
            jax.ShapeDtypeStruct((1,), jnp.int32),
            jax.ShapeDtypeStruct(tok.shape, jnp.int32),
        ],
        name="moe_plan",
    )(cnt, tok)


def _token_copy(src, s, dst, r, sem, ns):
    return pltpu.make_async_copy(src.at[pl.ds(pl.multiple_of(s * ns, ns), ns)], dst.at[pl.ds(pl.multiple_of(r * ns, ns), ns)], sem)


def _to_slabs(dst_ref, val, ns):
    rows = val.shape[0]
    for c in range(ns):
        dst_ref[pl.ds(c, rows, stride=ns), :] = val[:, c * V7X_LANES : (c + 1) * V7X_LANES]


def _slab(src_ref, c, rows, ns):
    return src_ref[pl.ds(c, rows, stride=ns), :]


def _dispatch_body(pos_hbm, off_ref, cnt_ref, x_ref, g_ref, xs_hbm, pos_smem, x3_ref, zero_ref, sem_idx, sem_row, *, td, tg, ne, n_tiles, ns):
    base = pl.multiple_of(pl.program_id(0) * td, td)
    idx = pltpu.make_async_copy(pos_hbm.at[:, pl.ds(base, td)], pos_smem, sem_idx)
    idx.start()
    _to_slabs(x3_ref, _rms(x_ref[...], g_ref[...]), ns)

    @pl.when(pl.program_id(0) == 0)
    def _():
        zero_ref[...] = jnp.zeros_like(zero_ref)
        for e in range(ne):
            start = off_ref[e] + cnt_ref[e]
            stop = off_ref[e] + (cnt_ref[e] + (tg - 1)) // tg * tg

            def zero_start(r, carry):
                _token_copy(zero_ref, 0, xs_hbm, r, sem_row, ns).start()
                return carry

            def zero_wait(r, carry):
                _token_copy(zero_ref, 0, xs_hbm, r, sem_row, ns).wait()
                return carry

            lax.fori_loop(start, stop, zero_start, 0)
            lax.fori_loop(start, stop, zero_wait, 0)

        def tile_copy(ti):
            return pltpu.make_async_copy(zero_ref, xs_hbm.at[pl.ds(pl.multiple_of(ti * (tg * ns), tg * ns), tg * ns)], sem_row)

        def tile_start(ti, carry):
            tile_copy(ti).start()
            return carry

        def tile_wait(ti, carry):
            tile_copy(ti).wait()
            return carry

        used = stop // tg
        lax.fori_loop(used, n_tiles, tile_start, 0)
        lax.fori_loop(used, n_tiles, tile_wait, 0)

    idx.wait()

    def start(t, c):
        for k in range(TOP_K):
            _token_copy(x3_ref, t, xs_hbm, pos_smem[k, t], sem_row, ns).start(priority=k % 2)
        return c

    lax.fori_loop(0, td, start, 0, unroll=8)
    for _ in range(TOP_K):
        pltpu.make_async_copy(x3_ref, xs_hbm.at[pl.ds(0, td * ns)], sem_row).wait()


def moe_dispatch(pos, off, cnt, x, g, tg, n_tiles, td=1024):
    n, d = x.shape
    ns = d // V7X_LANES
    assert ns % V7X_SUBLANES == 0
    td = min(td, n)
    smem = pl.BlockSpec(memory_space=pltpu.SMEM)
    hbm = pl.BlockSpec(memory_space=pl.ANY)
    return pl.pallas_call(
        functools.partial(_dispatch_body, td=td, tg=tg, ne=cnt.shape[0], n_tiles=n_tiles, ns=ns),
        grid=(n // td,),
        in_specs=[hbm, smem, smem, pl.BlockSpec((td, d), lambda i: (i, 0)), pl.BlockSpec((1, d), lambda i: (0, 0))],
        out_specs=hbm,
        out_shape=jax.ShapeDtypeStruct((n_tiles * tg * ns, V7X_LANES), x.dtype),
        scratch_shapes=[
            pltpu.SMEM((TOK_ROWS, td), jnp.int32),
            pltpu.VMEM((td * ns, V7X_LANES), x.dtype),
            pltpu.VMEM((tg * ns, V7X_LANES), x.dtype),
            pltpu.SemaphoreType.DMA(()),
            pltpu.SemaphoreType.DMA(()),
        ],
        compiler_params=pltpu.CompilerParams(dimension_semantics=("arbitrary",), has_side_effects=True, vmem_limit_bytes=V7X_VMEM_LIMIT_BYTES),
        name="moe_dispatch",
    )(pos, off, cnt, x, g)


GROUP_SHORT_TILE_DIV = 4


def _grouped_ffn_body(te_ref, tv_ref, nv_ref, x_ref, wi_ref, wo_ref, o_ref, h_ref, t_ref, *, tg, ns, f, tf):
    i = pl.program_id(0)

    @pl.when(i < nv_ref[0])
    def _():
        for c in range(ns):
            h_ref[:, c * V7X_LANES : (c + 1) * V7X_LANES] = _slab(x_ref, c, tg, ns).astype(BF16)

        def run(m):
            for c in range(f // tf):
                a = _dot(h_ref[0:m, :], wi_ref[:, c * tf : (c + 1) * tf])
                u = _dot(h_ref[0:m, :], wi_ref[:, f + c * tf : f + (c + 1) * tf])
                t_ref[0:m, c * tf : (c + 1) * tf] = ((a * jax.nn.sigmoid(a)) * u).astype(BF16)
            y = _dot(t_ref[0:m, :], wo_ref[...])
            for c in range(ns):
                o_ref[pl.ds(c, m, stride=ns), :] = y[:, c * V7X_LANES : (c + 1) * V7X_LANES]
            if m < tg:
                o_ref[m * ns :, :] = jnp.zeros(((tg - m) * ns, V7X_LANES), F32)

        short = tv_ref[i] <= tg // GROUP_SHORT_TILE_DIV
        pl.when(short)(lambda: run(tg // GROUP_SHORT_TILE_DIV))
        pl.when(jnp.logical_not(short))(lambda: run(tg))

    @pl.when(i >= nv_ref[0])
    def _():
        o_ref[...] = jnp.zeros_like(o_ref)


def moe_grouped_ffn(te, tv, nv, xs, w_in, w_out, layer, tg, tf=512):
    d = w_in.shape[2]
    ns = d // V7X_LANES
    r = xs.shape[0] // ns
    f = w_out.shape[2]
    tf = _block(f, tf)
    n_tiles = r // tg
    once = pl.Buffered(1)

    def tile(i, nv):
        return jnp.minimum(i, nv[0] - 1)

    grid_spec = pltpu.PrefetchScalarGridSpec(
        num_scalar_prefetch=3,
        grid=(n_tiles,),
        in_specs=[
            pl.BlockSpec((tg * ns, V7X_LANES), lambda i, te, tv, nv: (tile(i, nv), 0)),
            pl.BlockSpec((None, None, d, 2 * f), lambda i, te, tv, nv: (layer, te[i], 0, 0), pipeline_mode=once),
            pl.BlockSpec((None, None, f, d), lambda i, te, tv, nv: (layer, te[i], 0, 0), pipeline_mode=once),
        ],
        out_specs=pl.BlockSpec((tg * ns, V7X_LANES), lambda i, te, tv, nv: (i, 0)),
        scratch_shapes=[pltpu.VMEM((tg, d), BF16), pltpu.VMEM((tg, f), BF16)],
    )
    return pl.pallas_call(
        functools.partial(_grouped_ffn_body, tg=tg, ns=ns, f=f, tf=tf),
        grid_spec=grid_spec,
        out_shape=jax.ShapeDtypeStruct((r * ns, V7X_LANES), F32),
        compiler_params=pltpu.CompilerParams(dimension_semantics=("arbitrary",), vmem_limit_bytes=V7X_VMEM_RESIDENT_LIMIT_BYTES),
        name="moe_grouped_ffn",
    )(te, tv, nv, xs, w_in, w_out)


def _combine_body(pos_hbm, gate_ref, x_ref, ys_hbm, *rest, tc, ns, final):
    rest = list(rest)
    fg_ref = rest.pop(0) if final else None
    o_ref, pos_smem, buf_ref, sem_idx, sem_row = rest
    s, nsteps = pl.program_id(0), pl.num_programs(0)

    def idx_copy(step, slot):
        return pltpu.make_async_copy(pos_hbm.at[:, pl.ds(pl.multiple_of(step * tc, tc), tc)], pos_smem.at[slot], sem_idx.at[slot])

    def gather(slot):
        def start(t, c):
            for k in range(TOP_K):
                _token_copy(ys_hbm, pos_smem[slot, k, t], buf_ref.at[slot, k], t, sem_row.at[slot], ns).start(priority=k % 2)
            return c

        lax.fori_loop(0, tc, start, 0, unroll=8)

    def step(slot):
        other = 1 - slot

        @pl.when(s == 0)
        def _():
            idx_copy(s, slot).start()
            idx_copy(s, slot).wait()
            gather(slot)

            @pl.when(nsteps > 1)
            def _():
                idx_copy(s + 1, other).start()

        @pl.when(s + 1 < nsteps)
        def _():
            idx_copy(s + 1, other).wait()
            gather(other)

        @pl.when(s + 2 < nsteps)
        def _():
            idx_copy(s + 2, slot).start()

        for k in range(TOP_K):
            pltpu.make_async_copy(ys_hbm.at[pl.ds(0, tc * ns)], buf_ref.at[slot, k], sem_row.at[slot]).wait()
        g0, g1 = gate_ref[:, 0:1], gate_ref[:, 1:2]
        for c in range(ns):
            cols = slice(c * V7X_LANES, (c + 1) * V7X_LANES)
            o_ref[:, cols] = x_ref[:, cols] + g0 * _slab(buf_ref.at[slot, 0], c, tc, ns) + g1 * _slab(buf_ref.at[slot, 1], c, tc, ns)
        if final:
            o_ref[...] = _rms(o_ref[...], fg_ref[...])

    for slot in range(2):
        pl.when(s % 2 == slot)(functools.partial(step, slot))


def moe_combine(pos, gates, x, ys, final_g=None, tc=512):
    n, d = x.shape
    ns = d // V7X_LANES
    tc = min(tc, n)
    final = final_g is not None
    hbm = pl.BlockSpec(memory_space=pl.ANY)
    in_specs = [hbm, pl.BlockSpec((tc, V7X_LANES), lambda i: (i, 0)), pl.BlockSpec((tc, d), lambda i: (i, 0)), hbm]
    args = [pos, gates, x, ys]
    if final:
        in_specs.append(pl.BlockSpec((1, d), lambda i: (0, 0)))
        args.append(final_g)
    return pl.pallas_call(
        functools.partial(_combine_body, tc=tc, ns=ns, final=final),
        grid=(n // tc,),
        in_specs=in_specs,
        out_specs=pl.BlockSpec((tc, d), lambda i: (i, 0)),
        out_shape=jax.ShapeDtypeStruct((n, d), F32),
        scratch_shapes=[
            pltpu.SMEM((2, TOK_ROWS, tc), jnp.int32),
            pltpu.VMEM((2, TOP_K, tc * ns, V7X_LANES), F32),
            pltpu.SemaphoreType.DMA((2,)),
            pltpu.SemaphoreType.DMA((2,)),
        ],
        compiler_params=_params("arbitrary"),
        name="moe_combine",
    )(*args)


def moe_ffn(x, g, w_router, w_in, w_out, layer, final_g=None, tg=1024):
    n, _ = x.shape
    ne = w_router.shape[1]
    tg = min(tg, n)
    n_tiles = TOP_K * n // tg + ne
    gates, tok, counts = moe_router(x, g, w_router)
    cnt = counts[0, :ne].astype(jnp.int32)
    off, te, tv, nv, pos = moe_plan(cnt, tok, tg, n_tiles)
    xs = moe_dispatch(pos, off, cnt, x, g, tg, n_tiles)
    ys = moe_grouped_ffn(te, tv, nv, xs, w_in, w_out, layer, tg)
    return moe_combine(pos, gates, x, ys, final_g)


def _gla_gate_body(x_ref, g_ref, wl_ref, w2_ref, b_ref, o_ref):
    h = _rms(x_ref[...], g_ref[...]).astype(BF16)
    g_low = _dot(h, wl_ref[...]).astype(BF16)
    o_ref[...] = _log_sigmoid(_dot(g_low, w2_ref[...]) + b_ref[...]) / B_TAU


def gla_gate(x, g, w_low, w_gate2, bias, tm=1024):
    n, d = x.shape
    dk = w_gate2.shape[1]
    tm = min(tm, n)
    wl = jnp.pad(w_low, ((0, 0), (0, V7X_LANES - B_GATE_RANK))).astype(BF16)
    w2 = jnp.pad(w_gate2, ((0, V7X_LANES - B_GATE_RANK), (0, 0))).astype(BF16)
    return pl.pallas_call(
        _gla_gate_body,
        grid=(n // tm,),
        in_specs=[
            pl.BlockSpec((tm, d), lambda i: (i, 0)),
            pl.BlockSpec((1, d), lambda i: (0, 0)),
            pl.BlockSpec((d, V7X_LANES), lambda i: (0, 0)),
            pl.BlockSpec((V7X_LANES, dk), lambda i: (0, 0)),
            pl.BlockSpec((1, dk), lambda i: (0, 0)),
        ],
        out_specs=pl.BlockSpec((tm, dk), lambda i: (i, 0)),
        out_shape=jax.ShapeDtypeStruct((n, dk), F32),
        compiler_params=_params("parallel"),
        name="gla_gate",
    )(x, g, wl, w2, bias)


def _gla_body(q_ref, k_ref, v_ref, r_ref, la_ref, og_ref, o_ref, st_ref, *, tc, hk, hv):
    @pl.when(pl.program_id(1) == 0)
    def _():
        st_ref[...] = jnp.zeros_like(st_ref)

    c = B_CHUNK
    tril = _tril_mask(c)
    ones_tril = tril.astype(BF16)
    scale = hk**-0.5
    for ci in range(tc // c):
        rs = slice(ci * c, (ci + 1) * c)
        hi, mid, lo = _split3(la_ref[rs, :])
        bcum = _dot(ones_tril, hi) + _dot(ones_tril, mid) + _dot(ones_tril, lo)
        b_last = bcum[c - 1 : c, :]
        e_pos = jnp.exp(bcum)
        e_neg = jnp.exp(-bcum)
        e_end = jnp.exp(b_last - bcum)
        dec = jnp.exp(b_last)
        for h in range(B_HEADS):
            ks = slice(h * hk, (h + 1) * hk)
            vs = slice(h * hv, (h + 1) * hv)
            q = q_ref[rs, ks].astype(F32) * scale
            k = k_ref[rs, ks].astype(F32)
            v = v_ref[rs, vs]
            q_d = (q * e_pos[:, ks]).astype(BF16)
            k_d = (k * e_neg[:, ks]).astype(BF16)
            k_end = (k * e_end[:, ks]).astype(BF16)
            att = jnp.where(tril, _dot_nt(q_d, k_d), 0.0)
            st = st_ref[h]
            o = _dot(att.astype(BF16), v) + _dot_nt(q_d, st.astype(BF16))
            st_ref[h] = dec[:, ks] * st + _dot_tn(v, k_end)
            y = _rms(o, og_ref[...]).astype(BF16).astype(F32)
            r = r_ref[rs, vs].astype(F32)
            o_ref[rs, vs] = (y * (r * jax.nn.sigmoid(r))).astype(BF16)


def gla_chunks(proj, la, o_g, bsz, tc=256):
    n = proj.shape[0]
    dk = la.shape[1]
    dv = (proj.shape[1] - 2 * dk) // 2
    s = n // bsz
    tc = min(tc, s)
    nt = s // tc
    hk, hv = dk // B_HEADS, dv // B_HEADS
    assert (2 * dk) % dv == 0
    v_blk = 2 * dk // dv
    row = lambda b, t: b * nt + t
    return pl.pallas_call(
        functools.partial(_gla_body, tc=tc, hk=hk, hv=hv),
        grid=(bsz, nt),
        in_specs=[
            pl.BlockSpec((tc, dk), lambda b, t: (row(b, t), 0)),
            pl.BlockSpec((tc, dk), lambda b, t: (row(b, t), 1)),
            pl.BlockSpec((tc, dv), lambda b, t: (row(b, t), v_blk)),
            pl.BlockSpec((tc, dv), lambda b, t: (row(b, t), v_blk + 1)),
            pl.BlockSpec((tc, dk), lambda b, t: (row(b, t), 0)),
            pl.BlockSpec((1, hv), lambda b, t: (0, 0)),
        ],
        out_specs=pl.BlockSpec((tc, dv), lambda b, t: (row(b, t), 0)),
        out_shape=jax.ShapeDtypeStruct((n, dv), BF16),
        scratch_shapes=[pltpu.VMEM((B_HEADS, hv, hk), F32)],
        compiler_params=_params("parallel", "arbitrary"),
        name="gla_chunks",
    )(proj, proj, proj, proj, la, o_g)


def _fox_gate_body(x_ref, g_ref, wf_ref, b_ref, crow_ref, carry_ref, *, tb, nh):
    @pl.when(pl.program_id(1) == 0)
    def _():
        carry_ref[...] = jnp.zeros_like(carry_ref)

    h = _rms(x_ref[...], g_ref[...]).astype(BF16)
    lane = lax.broadcasted_iota(jnp.int32, (tb, V7X_LANES), 1)
    log_f = jnp.where(lane < nh, _log_sigmoid(_dot(h, wf_ref[...]) + b_ref[...]), 0.0)
    ones_tril = _tril_mask(tb).astype(BF16)
    hi, mid, lo = _split3(log_f)
    c = carry_ref[...] + (_dot(ones_tril, hi) + _dot(ones_tril, mid) + _dot(ones_tril, lo))
    carry_ref[...] = c[tb - 1 : tb, :]
    sel = (lax.broadcasted_iota(jnp.int32, (nh, V7X_LANES), 0) == lax.broadcasted_iota(jnp.int32, (nh, V7X_LANES), 1)).astype(BF16)
    hi, mid, lo = _split3(c)
    crow_ref[...] = _dot_nt(sel, hi) + _dot_nt(sel, mid) + _dot_nt(sel, lo)


def fox_gate(x, g, w_f, f_bias, bsz, tb=512):
    n, d = x.shape
    nh = w_f.shape[1]
    s = n // bsz
    tb = min(tb, s)
    nt = s // tb
    wf = jnp.pad(w_f, ((0, 0), (0, V7X_LANES - nh))).astype(BF16)
    bias = jnp.pad(f_bias.reshape(1, nh), ((0, 0), (0, V7X_LANES - nh)))
    return pl.pallas_call(
        functools.partial(_fox_gate_body, tb=tb, nh=nh),
        grid=(bsz, nt),
        in_specs=[
            pl.BlockSpec((tb, d), lambda b, t: (b * nt + t, 0)),
            pl.BlockSpec((1, d), lambda b, t: (0, 0)),
            pl.BlockSpec((d, V7X_LANES), lambda b, t: (0, 0)),
            pl.BlockSpec((1, V7X_LANES), lambda b, t: (0, 0)),
        ],
        out_specs=pl.BlockSpec((None, nh, tb), lambda b, t: (b, 0, t)),
        out_shape=jax.ShapeDtypeStruct((bsz, nh, s), F32),
        scratch_shapes=[pltpu.VMEM((1, V7X_LANES), F32)],
        compiler_params=_params("parallel", "arbitrary"),
        name="fox_gate",
    )(x, g, wf, bias)


FOX_KEY_BLOCK = 512
FOX_SUBTILES = 8
FOX_ROW_CHUNK = 32


def _fox_flash_body(q_ref, k_ref, v_ref, crow_ref, o_ref, qs_ref, va_ref, s_ref, p_ref, al_ref, m_ref, acc_ref, *, t, nsub, scale):
    i = pl.program_id(2)
    dh = q_ref.shape[1]
    nl = t // V7X_LANES

    @pl.when(i == 0)
    def _():
        va_ref[:, :dh] = v_ref[...]
        va_ref[:, dh:] = jnp.ones((va_ref.shape[0], V7X_LANES), BF16)

    qs_ref[...] = (q_ref[...].astype(F32) * scale).astype(BF16)
    m_ref[...] = jnp.full_like(m_ref, -jnp.inf)
    acc_ref[...] = jnp.zeros_like(acc_ref)

    def block(sub, kb, masked):
        rows = slice(sub * t, (sub + 1) * t)
        off = pl.multiple_of(kb * t, t)
        s_ref[sub] = _dot_nt(qs_ref[rows, :], k_ref[pl.ds(off, t), :])
        bias = crow_ref[pl.ds(kb, 1), :]

        def chunk(c, carry):
            r = pl.multiple_of(c * FOX_ROW_CHUNK, FOX_ROW_CHUNK)
            sc = s_ref[sub, pl.ds(r, FOX_ROW_CHUNK), :] - bias
            if masked:
                row = r + lax.broadcasted_iota(jnp.int32, sc.shape, 0)
                col = lax.broadcasted_iota(jnp.int32, sc.shape, 1)
                sc = jnp.where(col <= row, sc, -jnp.inf)
            tiles = [sc[:, a * V7X_LANES : (a + 1) * V7X_LANES] for a in range(nl)]
            mx = functools.reduce(jnp.maximum, tiles)
            m_old = m_ref[pl.ds(sub * t + r, FOX_ROW_CHUNK), :]
            m_new = jnp.maximum(m_old, jnp.max(mx, axis=-1, keepdims=True))
            al_ref[sub, pl.ds(r, FOX_ROW_CHUNK), :] = jnp.exp(m_old - m_new)
            m_ref[pl.ds(sub * t + r, FOX_ROW_CHUNK), :] = m_new
            for a in range(nl):
                p_ref[sub, pl.ds(r, FOX_ROW_CHUNK), a * V7X_LANES : (a + 1) * V7X_LANES] = jnp.exp(tiles[a] - m_new).astype(BF16)
            return carry

        lax.fori_loop(0, t // FOX_ROW_CHUNK, chunk, 0, unroll=True)
        pv = _dot(p_ref[sub], va_ref[pl.ds(off, t), :])
        alpha = al_ref[sub]
        acc_ref[rows, :dh] = alpha * acc_ref[rows, :dh] + pv[:, :dh]
        acc_ref[rows, dh:] = alpha * acc_ref[rows, dh:] + pv[:, dh:]

    def below_diagonal(kb, carry):
        for sub in range(nsub):
            block(sub, kb, False)
        return carry

    first = i * nsub
    lax.fori_loop(0, first, below_diagonal, 0)
    for kb in range(nsub):
        for sub in range(kb, nsub):
            block(sub, first + kb, masked=(sub == kb))
    o_ref[...] = (acc_ref[:, :dh] / acc_ref[:, dh:]).astype(o_ref.dtype)


def fox_flash(proj, crow, bsz, nh):
    n = proj.shape[0]
    dh = proj.shape[1] // (3 * nh)
    assert dh == V7X_LANES
    s = n // bsz
    t = min(FOX_KEY_BLOCK, s)
    nsub = min(FOX_SUBTILES, s // t)
    tq = t * nsub
    nq = s // tq
    crow = crow.reshape(bsz, nh, s // t, t)
    return pl.pallas_call(
        functools.partial(_fox_flash_body, t=t, nsub=nsub, scale=dh**-0.5),
        grid=(bsz, nh, nq),
        in_specs=[
            pl.BlockSpec((tq, dh), lambda b, h, i: (b * nq + i, h)),
            pl.BlockSpec((s, dh), lambda b, h, i: (b, nh + h)),
            pl.BlockSpec((s, dh), lambda b, h, i: (b, 2 * nh + h)),
            pl.BlockSpec((None, None, s // t, t), lambda b, h, i: (b, h, 0, 0)),
        ],
        out_specs=pl.BlockSpec((tq, dh), lambda b, h, i: (b * nq + i, h)),
        out_shape=jax.ShapeDtypeStruct((n, nh * dh), BF16),
        scratch_shapes=[
            pltpu.VMEM((tq, dh), BF16),
            pltpu.VMEM((s, dh + V7X_LANES), BF16),
            pltpu.VMEM((nsub, t, t), F32),
            pltpu.VMEM((nsub, t, t), BF16),
            pltpu.VMEM((nsub, t, V7X_LANES), F32),
            pltpu.VMEM((tq, V7X_LANES), F32),
            pltpu.VMEM((tq, dh + V7X_LANES), F32),
        ],
        compiler_params=_params("parallel", "parallel", "arbitrary"),
        name="fox_flash",
    )(proj, proj, proj, crow)


def kernel(x, norm1_g, norm2_g, a_w_in, a_vnorm_g, a_w_s, a_b_s, a_w_out, b_w_in, b_w_gate2, b_gate_bias, b_onorm_g, b_w_out, c_w_in, c_f_bias, c_w_out, ffn_w_in, ffn_w_out, moe_router, moe_w_in, moe_w_out, final_g):
    bsz, s, d = x.shape
    depth = norm1_g.shape[0]
    xf = x.reshape(bsz * s, d)
    moe_w_in_bf16, moe_w_out_bf16 = moe_w_in.astype(BF16), moe_w_out.astype(BF16)
    for i in range(depth):
        g1 = norm1_g[i].reshape(1, d)
        m, j = i % N_MIXERS, i // N_MIXERS
        if m == 0:
            xf = gmlp_mixer(xf, g1, a_w_in[j].astype(BF16), a_vnorm_g[j].reshape(1, -1), a_w_s[j], a_b_s[j].T, a_w_out[j].astype(BF16))
        elif m == 1:
            dk = b_w_gate2.shape[2]
            n_main = b_w_in.shape[2] - B_GATE_RANK
            proj = norm_matmul(xf, g1, b_w_in[j, :, :n_main].astype(BF16))
            la = gla_gate(xf, g1, b_w_in[j, :, n_main:], b_w_gate2[j], b_gate_bias[j].reshape(1, dk))
            o = gla_chunks(proj, la, b_onorm_g[j].reshape(1, -1), bsz)
            xf = matmul_residual(o, b_w_out[j].astype(BF16), xf)
        else:
            proj = norm_matmul(xf, g1, c_w_in[j, :, : 3 * d].astype(BF16))
            crow = fox_gate(xf, g1, c_w_in[j, :, 3 * d :], c_f_bias[j], bsz)
            o = fox_flash(proj, crow, bsz, C_HEADS)
            xf = matmul_residual(o, c_w_out[j].astype(BF16), xf)
        g2 = norm2_g[i].reshape(1, d)
        fg = final_g.reshape(1, d) if i == depth - 1 else None
        if i % 2 == 0:
            xf = ffn_resident(xf, g2, ffn_w_in[i // 2].astype(BF16), ffn_w_out[i // 2].astype(BF16), final_g=fg)
        else:
            xf = moe_ffn(xf, g2, moe_router[i // 2], moe_w_in_bf16, moe_w_out_bf16, i // 2, final_g=fg)
    return xf.reshape(bsz, s, d)
```

```python
import functools

import jax
import jax.numpy as jnp
from jax import lax
from jax.experimental import pallas as pl
from jax.experimental.pallas import tpu as pltpu

F32 = jnp.float32
BF16 = jnp.bfloat16

EPS = 1e-6
N_MIXERS = 3
TOP_K = 2
A_CHUNK = 128
A_GROUPS = 8
B_HEADS = 4
B_GATE_RANK = 16
B_TAU = 16.0
B_CHUNK = 64
C_HEADS = 8

V7X_LANES = 128
V7X_SUBLANES = 8
V7X_VMEM_LIMIT_BYTES = 56 * 1024 * 1024
V7X_VMEM_RESIDENT_LIMIT_BYTES = 62 * 1024 * 1024


def _params(*sem):
    return pltpu.CompilerParams(dimension_semantics=sem, vmem_limit_bytes=V7X_VMEM_LIMIT_BYTES)


def _block(total, target, align=V7X_LANES):
    if total <= target:
        return total
    return max(b for b in range(align, target + 1, align) if total % b == 0)


def _rms(x, g):
    ms = jnp.mean(x * x, axis=-1, keepdims=True)
    return x * lax.rsqrt(ms + EPS) * g


def _log_sigmoid(x):
    return jnp.minimum(x, 0.0) - jnp.log1p(jnp.exp(-jnp.abs(x)))


def _split3(x):
    hi = x.astype(BF16)
    r1 = x - hi.astype(F32)
    mid = r1.astype(BF16)
    lo = (r1 - mid.astype(F32)).astype(BF16)
    return hi, mid, lo


def _dot(a, b):
    return jnp.dot(a, b, preferred_element_type=F32)


def _dot_nt(a, b):
    return lax.dot_general(a, b, (((1,), (1,)), ((), ())), preferred_element_type=F32)


def _dot_tn(a, b):
    return lax.dot_general(a, b, (((0,), (0,)), ((), ())), preferred_element_type=F32)


def _tril_mask(n):
    row = lax.broadcasted_iota(jnp.int32, (n, n), 0)
    col = lax.broadcasted_iota(jnp.int32, (n, n), 1)
    return col <= row


def _norm_matmul_body(x_ref, g_ref, w_ref, o_ref, h_ref, *, tn):
    h_ref[...] = _rms(x_ref[...], g_ref[...]).astype(BF16)
    for c in range(w_ref.shape[1] // tn):
        cols = slice(c * tn, (c + 1) * tn)
        o_ref[:, cols] = _dot(h_ref[...], w_ref[:, cols]).astype(o_ref.dtype)


def norm_matmul(x, g, w, tm=1024, tn=1024):
    n, d = x.shape
    nout = w.shape[1]
    tm, tn = min(tm, n), _block(nout, tn)
    return pl.pallas_call(
        functools.partial(_norm_matmul_body, tn=tn),
        grid=(n // tm,),
        in_specs=[
            pl.BlockSpec((tm, d), lambda i: (i, 0)),
            pl.BlockSpec((1, d), lambda i: (0, 0)),
            pl.BlockSpec((d, nout), lambda i: (0, 0), pipeline_mode=pl.Buffered(1)),
        ],
        out_specs=pl.BlockSpec((tm, nout), lambda i: (i, 0)),
        out_shape=jax.ShapeDtypeStruct((n, nout), BF16),
        scratch_shapes=[pltpu.VMEM((tm, d), BF16)],
        compiler_params=_params("parallel"),
        name="norm_matmul",
    )(x, g, w)


def _matmul_residual_body(y_ref, w_ref, x_ref, o_ref):
    o_ref[...] = x_ref[...] + _dot(y_ref[...], w_ref[...])


def matmul_residual(y, w, x, tm=1024):
    n, k = y.shape
    d = w.shape[1]
    tm = min(tm, n)
    return pl.pallas_call(
        _matmul_residual_body,
        grid=(n // tm,),
        in_specs=[
            pl.BlockSpec((tm, k), lambda i: (i, 0)),
            pl.BlockSpec((k, d), lambda i: (0, 0)),
            pl.BlockSpec((tm, d), lambda i: (i, 0)),
        ],
        out_specs=pl.BlockSpec((tm, d), lambda i: (i, 0)),
        out_shape=jax.ShapeDtypeStruct((n, d), F32),
        compiler_params=_params("parallel"),
        name="matmul_residual",
    )(y, w, x)


def _gmlp_body(x_ref, g_ref, wi_ref, vg_ref, ws_ref, bt_ref, wo_ref, o_ref, h_ref, z_ref, vn_ref, y_ref, *, tm, width, tn):
    gd = width // A_GROUPS
    h_ref[...] = _rms(x_ref[...], g_ref[...]).astype(BF16)
    for c in range(2 * width // tn):
        cols = slice(c * tn, (c + 1) * tn)
        z_ref[:, cols] = jax.nn.gelu(_dot(h_ref[...], wi_ref[:, cols]), approximate=True).astype(BF16)
    vn_ref[...] = _rms(z_ref[:, width:].astype(F32), vg_ref[...]).astype(BF16)
    tril = _tril_mask(A_CHUNK)
    for g in range(A_GROUPS):
        w = jnp.where(tril, ws_ref[g], 0.0).astype(BF16)
        bias = bt_ref[:, g : g + 1]
        cs = slice(g * gd, (g + 1) * gd)
        for c in range(tm // A_CHUNK):
            rs = slice(c * A_CHUNK, (c + 1) * A_CHUNK)
            mixed = _dot(w, vn_ref[rs, cs]) + bias
            y_ref[rs, cs] = (z_ref[rs, cs].astype(F32) * mixed).astype(BF16)
    o_ref[...] = x_ref[...] + _dot(y_ref[...], wo_ref[...])


def gmlp_mixer(x, g, wi, vg, ws, bt, wo, tm=1024, tn=1024):
    n, d = x.shape
    width = wi.shape[1] // 2
    tm, tn = min(tm, n), _block(2 * width, tn)
    once = pl.Buffered(1)
    return pl.pallas_call(
        functools.partial(_gmlp_body, tm=tm, width=width, tn=tn),
        grid=(n // tm,),
        in_specs=[
            pl.BlockSpec((tm, d), lambda i: (i, 0)),
            pl.BlockSpec((1, d), lambda i: (0, 0)),
            pl.BlockSpec((d, 2 * width), lambda i: (0, 0), pipeline_mode=once),
            pl.BlockSpec((1, width), lambda i: (0, 0)),
            pl.BlockSpec((A_GROUPS, A_CHUNK, A_CHUNK), lambda i: (0, 0, 0)),
            pl.BlockSpec((A_CHUNK, A_GROUPS), lambda i: (0, 0)),
            pl.BlockSpec((width, d), lambda i: (0, 0), pipeline_mode=once),
        ],
        out_specs=pl.BlockSpec((tm, d), lambda i: (i, 0)),
        out_shape=jax.ShapeDtypeStruct((n, d), F32),
        scratch_shapes=[
            pltpu.VMEM((tm, d), BF16),
            pltpu.VMEM((tm, 2 * width), BF16),
            pltpu.VMEM((tm, width), BF16),
            pltpu.VMEM((tm, width), BF16),
        ],
        compiler_params=pltpu.CompilerParams(dimension_semantics=("parallel",), vmem_limit_bytes=V7X_VMEM_RESIDENT_LIMIT_BYTES),
        name="gmlp_mixer",
    )(x, g, wi, vg, ws, bt, wo)


def _ffn_resident_body(*refs, f, tf, final):
    x_ref, g_ref, wi_ref, wo_ref = refs[:4]
    rest = list(refs[4:])
    fg_ref = rest.pop(0) if final else None
    o_ref, h_ref, t_ref = rest
    h_ref[...] = _rms(x_ref[...], g_ref[...]).astype(BF16)
    for c in range(f // tf):
        a = _dot(h_ref[...], wi_ref[:, c * tf : (c + 1) * tf])
        u = _dot(h_ref[...], wi_ref[:, f + c * tf : f + (c + 1) * tf])
        t_ref[:, c * tf : (c + 1) * tf] = ((a * jax.nn.sigmoid(a)) * u).astype(BF16)
    y = x_ref[...] + _dot(t_ref[...], wo_ref[...])
    o_ref[...] = _rms(y, fg_ref[...]) if final else y


def ffn_resident(x, g, w_in, w_out, final_g=None, tm=1024, tf=512):
    n, d = x.shape
    f = w_out.shape[0]
    tm, tf = min(tm, n), min(tf, f)
    final = final_g is not None
    once = pl.Buffered(1)
    in_specs = [
        pl.BlockSpec((tm, d), lambda i: (i, 0)),
        pl.BlockSpec((1, d), lambda i: (0, 0)),
        pl.BlockSpec((d, 2 * f), lambda i: (0, 0), pipeline_mode=once),
        pl.BlockSpec((f, d), lambda i: (0, 0), pipeline_mode=once),
    ]
    args = [x, g, w_in, w_out]
    if final:
        in_specs.append(pl.BlockSpec((1, d), lambda i: (0, 0)))
        args.append(final_g)
    return pl.pallas_call(
        functools.partial(_ffn_resident_body, f=f, tf=tf, final=final),
        grid=(n // tm,),
        in_specs=in_specs,
        out_specs=pl.BlockSpec((tm, d), lambda i: (i, 0)),
        out_shape=jax.ShapeDtypeStruct((n, d), F32),
        scratch_shapes=[pltpu.VMEM((tm, d), BF16), pltpu.VMEM((tm, f), BF16)],
        compiler_params=pltpu.CompilerParams(dimension_semantics=("parallel",), vmem_limit_bytes=V7X_VMEM_RESIDENT_LIMIT_BYTES),
        name="ffn_resident",
    )(*args)


TOK_ROWS = 8


def _router_body(x_ref, g_ref, whi_ref, wlo_ref, gate_ref, tok_ref, cnt_ref, carry_ref, *, ne, tm):
    @pl.when(pl.program_id(0) == 0)
    def _():
        carry_ref[...] = jnp.zeros_like(carry_ref)

    h = _rms(x_ref[...], g_ref[...])
    h_hi = h.astype(BF16)
    h_lo = (h - h_hi.astype(F32)).astype(BF16)
    logits = _dot(h_hi, whi_ref[...]) + _dot(h_lo, whi_ref[...]) + _dot(h_hi, wlo_ref[...])
    lane = lax.broadcasted_iota(jnp.int32, logits.shape, 1).astype(F32)
    neg = -jnp.inf
    l1 = jnp.where(lane < ne, logits, neg)
    m1 = jnp.max(l1, axis=-1, keepdims=True)
    i1 = jnp.min(jnp.where(l1 == m1, lane, float(V7X_LANES)), axis=-1, keepdims=True)
    l2 = jnp.where(lane == i1, neg, l1)
    m2 = jnp.max(l2, axis=-1, keepdims=True)
    i2 = jnp.min(jnp.where(l2 == m2, lane, float(V7X_LANES)), axis=-1, keepdims=True)
    e2 = jnp.exp(m2 - m1)
    den = 1.0 + e2
    gate_ref[...] = jnp.where(lane == 0.0, 1.0 / den, 0.0) + jnp.where(lane == 1.0, e2 / den, 0.0)

    sel1, sel2 = lane == i1, lane == i2
    onehot = jnp.where(sel1 | sel2, 1.0, 0.0)
    row = lax.broadcasted_iota(jnp.int32, (tm, tm), 0)
    col = lax.broadcasted_iota(jnp.int32, (tm, tm), 1)
    before = carry_ref[...] + _dot((col < row).astype(BF16), onehot.astype(BF16))
    carry_ref[...] = before[tm - 1 : tm, :] + onehot[tm - 1 : tm, :]
    cnt_ref[...] = carry_ref[...]
    r1 = jnp.sum(jnp.where(sel1, before, 0.0), axis=-1, keepdims=True)
    r2 = jnp.sum(jnp.where(sel2, before, 0.0), axis=-1, keepdims=True)
    table = (jnp.where(lane == 0.0, i1, 0.0) + jnp.where(lane == 1.0, i2, 0.0)
             + jnp.where(lane == 2.0, r1, 0.0) + jnp.where(lane == 3.0, r2, 0.0))
    pick = (lax.broadcasted_iota(jnp.int32, (TOK_ROWS, V7X_LANES), 0) == lax.broadcasted_iota(jnp.int32, (TOK_ROWS, V7X_LANES), 1)).astype(BF16)
    hi, mid, lo = _split3(table)
    tok_ref[...] = (_dot_nt(pick, hi) + _dot_nt(pick, mid) + _dot_nt(pick, lo)).astype(jnp.int32)


def moe_router(x, g, w_router, tm=512):
    n, d = x.shape
    ne = w_router.shape[1]
    tm = min(tm, n)
    wpad = jnp.pad(w_router, ((0, 0), (0, V7X_LANES - ne)))
    w_hi = wpad.astype(BF16)
    w_lo = (wpad - w_hi.astype(F32)).astype(BF16)
    return pl.pallas_call(
        functools.partial(_router_body, ne=ne, tm=tm),
        grid=(n // tm,),
        in_specs=[
            pl.BlockSpec((tm, d), lambda i: (i, 0)),
            pl.BlockSpec((1, d), lambda i: (0, 0)),
            pl.BlockSpec((d, V7X_LANES), lambda i: (0, 0)),
            pl.BlockSpec((d, V7X_LANES), lambda i: (0, 0)),
        ],
        out_specs=[
            pl.BlockSpec((tm, V7X_LANES), lambda i: (i, 0)),
            pl.BlockSpec((TOK_ROWS, tm), lambda i: (0, i)),
            pl.BlockSpec((1, V7X_LANES), lambda i: (0, 0)),
        ],
        out_shape=[
            jax.ShapeDtypeStruct((n, V7X_LANES), F32),
            jax.ShapeDtypeStruct((TOK_ROWS, n), jnp.int32),
            jax.ShapeDtypeStruct((1, V7X_LANES), F32),
        ],
        scratch_shapes=[pltpu.VMEM((1, V7X_LANES), F32)],
        compiler_params=_params("arbitrary"),
        name="moe_router",
    )(x, g, w_hi, w_lo)


def _plan_body(cnt_ref, tok_ref, off_ref, te_ref, tv_ref, nv_ref, pos_ref, *, ne, tg, n_tiles):
    off = jnp.int32(0)
    tile = jnp.int32(0)
    expert = tok_ref[0:TOP_K, :]
    group_start = jnp.zeros_like(expert)
    for e in range(ne):
        nt = (cnt_ref[e] + (tg - 1)) // tg
        off_ref[e] = off
        group_start = jnp.where(expert == e, off, group_start)

        def mark(ti, carry, e=e, first=tile):
            te_ref[ti] = jnp.int32(e)
            tv_ref[ti] = jnp.minimum(cnt_ref[e] - (ti - first) * tg, tg)
            return carry

        lax.fori_loop(tile, tile + nt, mark, 0)
        off = off + nt * tg
        tile = tile + nt
    pos_ref[...] = jnp.zeros_like(pos_ref)
    pos_ref[0:TOP_K, :] = group_start + tok_ref[TOP_K : 2 * TOP_K, :]
    nv_ref[0] = tile
    last = te_ref[jnp.maximum(tile - 1, 0)]

    def fill(ti, carry):
        te_ref[ti] = last
        tv_ref[ti] = jnp.int32(0)
        return carry

    lax.fori_loop(tile, n_tiles, fill, 0)


def moe_plan(cnt, tok, tg, n_tiles):
    ne = cnt.shape[0]
    smem = pl.BlockSpec(memory_space=pltpu.SMEM)
    vmem = pl.BlockSpec(memory_space=pltpu.VMEM)
    return pl.pallas_call(
        functools.partial(_plan_body, ne=ne, tg=tg, n_tiles=n_tiles),
        in_specs=[smem, vmem],
        out_specs=[smem, smem, smem, smem, vmem],
        out_shape=[
            jax.ShapeDtypeStruct((ne,), jnp.int32),
            jax.ShapeDtypeStruct((n_tiles,), jnp.int32),
            jax.ShapeDtypeStruct((n_tiles,), jnp.int32),
            jax.ShapeDtypeStruct((1,), jnp.int32),
            jax.ShapeDtypeStruct(tok.shape, jnp.int32),
        ],
        name="moe_plan",
    )(cnt, tok)


def _token_copy(src, s, dst, r, sem, ns):
    return pltpu.make_async_copy(src.at[pl.ds(pl.multiple_of(s * ns, ns), ns)], dst.at[pl.ds(pl.multiple_of(r * ns, ns), ns)], sem)


def _to_slabs(dst_ref, val, ns):
    rows = val.shape[0]
    for c in range(ns):
        dst_ref[pl.ds(c, rows, stride=ns), :] = val[:, c * V7X_LANES : (c + 1) * V7X_LANES]


def _slab(src_ref, c, rows, ns):
    return src_ref[pl.ds(c, rows, stride=ns), :]


def _dispatch_body(pos_hbm, off_ref, cnt_ref, x_ref, g_ref, xs_hbm, pos_smem, x3_ref, zero_ref, sem_idx, sem_row, *, td, tg, ne, n_tiles, ns):
    base = pl.multiple_of(pl.program_id(0) * td, td)
    idx = pltpu.make_async_copy(pos_hbm.at[:, pl.ds(base, td)], pos_smem, sem_idx)
    idx.start()
    _to_slabs(x3_ref, _rms(x_ref[...], g_ref[...]), ns)

    @pl.when(pl.program_id(0) == 0)
    def _():
        zero_ref[...] = jnp.zeros_like(zero_ref)
        for e in range(ne):
            start = off_ref[e] + cnt_ref[e]
            stop = off_ref[e] + (cnt_ref[e] + (tg - 1)) // tg * tg

            def zero_start(r, carry):
                _token_copy(zero_ref, 0, xs_hbm, r, sem_row, ns).start()
                return carry

            def zero_wait(r, carry):
                _token_copy(zero_ref, 0, xs_hbm, r, sem_row, ns).wait()
                return carry

            lax.fori_loop(start, stop, zero_start, 0)
            lax.fori_loop(start, stop, zero_wait, 0)

        def tile_copy(ti):
            return pltpu.make_async_copy(zero_ref, xs_hbm.at[pl.ds(pl.multiple_of(ti * (tg * ns), tg * ns), tg * ns)], sem_row)

        def tile_start(ti, carry):
            tile_copy(ti).start()
            return carry

        def tile_wait(ti, carry):
            tile_copy(ti).wait()
            return carry

        used = stop // tg
        lax.fori_loop(used, n_tiles, tile_start, 0)
        lax.fori_loop(used, n_tiles, tile_wait, 0)

    idx.wait()

    def start(t, c):
        for k in range(TOP_K):
            _token_copy(x3_ref, t, xs_hbm, pos_smem[k, t], sem_row, ns).start(priority=k % 2)
        return c

    lax.fori_loop(0, td, start, 0, unroll=8)
    for _ in range(TOP_K):
        pltpu.make_async_copy(x3_ref, xs_hbm.at[pl.ds(0, td * ns)], sem_row).wait()


def moe_dispatch(pos, off, cnt, x, g, tg, n_tiles, td=1024):
    n, d = x.shape
    ns = d // V7X_LANES
    assert ns % V7X_SUBLANES == 0
    td = min(td, n)
    smem = pl.BlockSpec(memory_space=pltpu.SMEM)
    hbm = pl.BlockSpec(memory_space=pl.ANY)
    return pl.pallas_call(
        functools.partial(_dispatch_body, td=td, tg=tg, ne=cnt.shape[0], n_tiles=n_tiles, ns=ns),
        grid=(n // td,),
        in_specs=[hbm, smem, smem, pl.BlockSpec((td, d), lambda i: (i, 0)), pl.BlockSpec((1, d), lambda i: (0, 0))],
        out_specs=hbm,
        out_shape=jax.ShapeDtypeStruct((n_tiles * tg * ns, V7X_LANES), x.dtype),
        scratch_shapes=[
            pltpu.SMEM((TOK_ROWS, td), jnp.int32),
            pltpu.VMEM((td * ns, V7X_LANES), x.dtype),
            pltpu.VMEM((tg * ns, V7X_LANES), x.dtype),
            pltpu.SemaphoreType.DMA(()),
            pltpu.SemaphoreType.DMA(()),
        ],
        compiler_params=pltpu.CompilerParams(dimension_semantics=("arbitrary",), has_side_effects=True, vmem_limit_bytes=V7X_VMEM_LIMIT_BYTES),
        name="moe_dispatch",
    )(pos, off, cnt, x, g)


GROUP_SHORT_TILE_DIV = 4
GROUP_STAGE_BYTES = 1024 * 1024


def _load_cast(src_hbm, dst_ref, stage_ref, sem, rows):
    n = dst_ref.shape[0] // rows

    def chunk(c, slot):
        return pltpu.make_async_copy(src_hbm.at[pl.ds(pl.multiple_of(c * rows, rows), rows)], stage_ref.at[slot], sem.at[slot])

    chunk(0, 0).start()

    def body(c, carry):
        for slot in range(2):

            @pl.when(c % 2 == slot)
            def _():
                @pl.when(c + 1 < n)
                def _():
                    chunk(c + 1, 1 - slot).start()

                chunk(c, slot).wait()
                dst_ref[pl.ds(pl.multiple_of(c * rows, rows), rows), :] = stage_ref[slot].astype(BF16)

        return carry

    lax.fori_loop(0, n, body, 0)


def _grouped_ffn_body(te_ref, tv_ref, nv_ref, x_ref, wi_hbm, wo_hbm, o_ref, wi_ref, wo_ref, si_ref, so_ref, sem_i, sem_o, h_ref, t_ref, *, layer, tg, ns, f, tf):
    i = pl.program_id(0)
    e = te_ref[i]

    @pl.when((i < nv_ref[0]) & ((i == 0) | (e != te_ref[jnp.maximum(i - 1, 0)])))
    def _():
        _load_cast(wi_hbm.at[layer, e], wi_ref, si_ref, sem_i, si_ref.shape[1])
        _load_cast(wo_hbm.at[layer, e], wo_ref, so_ref, sem_o, so_ref.shape[1])

    @pl.when(i < nv_ref[0])
    def _():
        for c in range(ns):
            h_ref[:, c * V7X_LANES : (c + 1) * V7X_LANES] = _slab(x_ref, c, tg, ns).astype(BF16)

        def run(m):
            for c in range(f // tf):
                a = _dot(h_ref[0:m, :], wi_ref[:, c * tf : (c + 1) * tf])
                u = _dot(h_ref[0:m, :], wi_ref[:, f + c * tf : f + (c + 1) * tf])
                t_ref[0:m, c * tf : (c + 1) * tf] = ((a * jax.nn.sigmoid(a)) * u).astype(BF16)
            y = _dot(t_ref[0:m, :], wo_ref[...])
            for c in range(ns):
                o_ref[pl.ds(c, m, stride=ns), :] = y[:, c * V7X_LANES : (c + 1) * V7X_LANES]
            if m < tg:
                o_ref[m * ns :, :] = jnp.zeros(((tg - m) * ns, V7X_LANES), F32)

        short = tv_ref[i] <= tg // GROUP_SHORT_TILE_DIV
        pl.when(short)(lambda: run(tg // GROUP_SHORT_TILE_DIV))
        pl.when(jnp.logical_not(short))(lambda: run(tg))

    @pl.when(i >= nv_ref[0])
    def _():
        o_ref[...] = jnp.zeros_like(o_ref)


def moe_grouped_ffn(te, tv, nv, xs, w_in, w_out, layer, tg, tf=512):
    d = w_in.shape[2]
    ns = d // V7X_LANES
    r = xs.shape[0] // ns
    f = w_out.shape[2]
    tf = _block(f, tf)
    n_tiles = r // tg
    hbm = pl.BlockSpec(memory_space=pl.ANY)
    rows_in = _block(d, GROUP_STAGE_BYTES // (2 * f * 4), V7X_SUBLANES)
    rows_out = _block(f, GROUP_STAGE_BYTES // (d * 4), V7X_SUBLANES)

    def tile(i, nv):
        return jnp.minimum(i, nv[0] - 1)

    grid_spec = pltpu.PrefetchScalarGridSpec(
        num_scalar_prefetch=3,
        grid=(n_tiles,),
        in_specs=[pl.BlockSpec((tg * ns, V7X_LANES), lambda i, te, tv, nv: (tile(i, nv), 0)), hbm, hbm],
        out_specs=pl.BlockSpec((tg * ns, V7X_LANES), lambda i, te, tv, nv: (i, 0)),
        scratch_shapes=[
            pltpu.VMEM((d, 2 * f), BF16),
            pltpu.VMEM((f, d), BF16),
            pltpu.VMEM((2, rows_in, 2 * f), F32),
            pltpu.VMEM((2, rows_out, d), F32),
            pltpu.SemaphoreType.DMA((2,)),
            pltpu.SemaphoreType.DMA((2,)),
            pltpu.VMEM((tg, d), BF16),
            pltpu.VMEM((tg, f), BF16),
        ],
    )
    return pl.pallas_call(
        functools.partial(_grouped_ffn_body, layer=layer, tg=tg, ns=ns, f=f, tf=tf),
        grid_spec=grid_spec,
        out_shape=jax.ShapeDtypeStruct((r * ns, V7X_LANES), F32),
        compiler_params=pltpu.CompilerParams(dimension_semantics=("arbitrary",), vmem_limit_bytes=V7X_VMEM_RESIDENT_LIMIT_BYTES),
        name="moe_grouped_ffn",
    )(te, tv, nv, xs, w_in, w_out)


def _combine_body(pos_hbm, gate_ref, x_ref, ys_hbm, *rest, tc, ns, final):
    rest = list(rest)
    fg_ref = rest.pop(0) if final else None
    o_ref, pos_smem, buf_ref, sem_idx, sem_row = rest
    s, nsteps = pl.program_id(0), pl.num_programs(0)

    def idx_copy(step, slot):
        return pltpu.make_async_copy(pos_hbm.at[:, pl.ds(pl.multiple_of(step * tc, tc), tc)], pos_smem.at[slot], sem_idx.at[slot])

    def gather(slot):
        def start(t, c):
            for k in range(TOP_K):
                _token_copy(ys_hbm, pos_smem[slot, k, t], buf_ref.at[slot, k], t, sem_row.at[slot], ns).start(priority=k % 2)
            return c

        lax.fori_loop(0, tc, start, 0, unroll=8)

    def step(slot):
        other = 1 - slot

        @pl.when(s == 0)
        def _():
            idx_copy(s, slot).start()
            idx_copy(s, slot).wait()
            gather(slot)

            @pl.when(nsteps > 1)
            def _():
                idx_copy(s + 1, other).start()

        @pl.when(s + 1 < nsteps)
        def _():
            idx_copy(s + 1, other).wait()
            gather(other)

        @pl.when(s + 2 < nsteps)
        def _():
            idx_copy(s + 2, slot).start()

        for k in range(TOP_K):
            pltpu.make_async_copy(ys_hbm.at[pl.ds(0, tc * ns)], buf_ref.at[slot, k], sem_row.at[slot]).wait()
        g0, g1 = gate_ref[:, 0:1], gate_ref[:, 1:2]
        for c in range(ns):
            cols = slice(c * V7X_LANES, (c + 1) * V7X_LANES)
            o_ref[:, cols] = x_ref[:, cols] + g0 * _slab(buf_ref.at[slot, 0], c, tc, ns) + g1 * _slab(buf_ref.at[slot, 1], c, tc, ns)
        if final:
            o_ref[...] = _rms(o_ref[...], fg_ref[...])

    for slot in range(2):
        pl.when(s % 2 == slot)(functools.partial(step, slot))


def moe_combine(pos, gates, x, ys, final_g=None, tc=512):
    n, d = x.shape
    ns = d // V7X_LANES
    tc = min(tc, n)
    final = final_g is not None
    hbm = pl.BlockSpec(memory_space=pl.ANY)
    in_specs = [hbm, pl.BlockSpec((tc, V7X_LANES), lambda i: (i, 0)), pl.BlockSpec((tc, d), lambda i: (i, 0)), hbm]
    args = [pos, gates, x, ys]
    if final:
        in_specs.append(pl.BlockSpec((1, d), lambda i: (0, 0)))
        args.append(final_g)
    return pl.pallas_call(
        functools.partial(_combine_body, tc=tc, ns=ns, final=final),
        grid=(n // tc,),
        in_specs=in_specs,
        out_specs=pl.BlockSpec((tc, d), lambda i: (i, 0)),
        out_shape=jax.ShapeDtypeStruct((n, d), F32),
        scratch_shapes=[
            pltpu.SMEM((2, TOK_ROWS, tc), jnp.int32),
            pltpu.VMEM((2, TOP_K, tc * ns, V7X_LANES), F32),
            pltpu.SemaphoreType.DMA((2,)),
            pltpu.SemaphoreType.DMA((2,)),
        ],
        compiler_params=_params("arbitrary"),
        name="moe_combine",
    )(*args)


def moe_ffn(x, g, w_router, w_in, w_out, layer, final_g=None, tg=1024):
    n, _ = x.shape
    ne = w_router.shape[1]
    tg = min(tg, n)
    n_tiles = TOP_K * n // tg + ne
    gates, tok, counts = moe_router(x, g, w_router)
    cnt = counts[0, :ne].astype(jnp.int32)
    off, te, tv, nv, pos = moe_plan(cnt, tok, tg, n_tiles)
    xs = moe_dispatch(pos, off, cnt, x, g, tg, n_tiles)
    ys = moe_grouped_ffn(te, tv, nv, xs, w_in, w_out, layer, tg)
    return moe_combine(pos, gates, x, ys, final_g)


def _gla_gate_body(x_ref, g_ref, wl_ref, w2_ref, b_ref, o_ref):
    h = _rms(x_ref[...], g_ref[...]).astype(BF16)
    g_low = _dot(h, wl_ref[...]).astype(BF16)
    o_ref[...] = _log_sigmoid(_dot(g_low, w2_ref[...]) + b_ref[...]) / B_TAU


def gla_gate(x, g, w_low, w_gate2, bias, tm=1024):
    n, d = x.shape
    dk = w_gate2.shape[1]
    tm = min(tm, n)
    wl = jnp.pad(w_low, ((0, 0), (0, V7X_LANES - B_GATE_RANK))).astype(BF16)
    w2 = jnp.pad(w_gate2, ((0, V7X_LANES - B_GATE_RANK), (0, 0))).astype(BF16)
    return pl.pallas_call(
        _gla_gate_body,
        grid=(n // tm,),
        in_specs=[
            pl.BlockSpec((tm, d), lambda i: (i, 0)),
            pl.BlockSpec((1, d), lambda i: (0, 0)),
            pl.BlockSpec((d, V7X_LANES), lambda i: (0, 0)),
            pl.BlockSpec((V7X_LANES, dk), lambda i: (0, 0)),
            pl.BlockSpec((1, dk), lambda i: (0, 0)),
        ],
        out_specs=pl.BlockSpec((tm, dk), lambda i: (i, 0)),
        out_shape=jax.ShapeDtypeStruct((n, dk), F32),
        compiler_params=_params("parallel"),
        name="gla_gate",
    )(x, g, wl, w2, bias)


def _gla_body(q_ref, k_ref, v_ref, r_ref, la_ref, og_ref, o_ref, st_ref, *, tc, hk, hv):
    @pl.when(pl.program_id(1) == 0)
    def _():
        st_ref[...] = jnp.zeros_like(st_ref)

    c = B_CHUNK
    tril = _tril_mask(c)
    ones_tril = tril.astype(BF16)
    scale = hk**-0.5
    for ci in range(tc // c):
        rs = slice(ci * c, (ci + 1) * c)
        hi, mid, lo = _split3(la_ref[rs, :])
        bcum = _dot(ones_tril, hi) + _dot(ones_tril, mid) + _dot(ones_tril, lo)
        b_last = bcum[c - 1 : c, :]
        e_pos = jnp.exp(bcum)
        e_neg = jnp.exp(-bcum)
        e_end = jnp.exp(b_last - bcum)
        dec = jnp.exp(b_last)
        for h in range(B_HEADS):
            ks = slice(h * hk, (h + 1) * hk)
            vs = slice(h * hv, (h + 1) * hv)
            q = q_ref[rs, ks].astype(F32) * scale
            k = k_ref[rs, ks].astype(F32)
            v = v_ref[rs, vs]
            q_d = (q * e_pos[:, ks]).astype(BF16)
            k_d = (k * e_neg[:, ks]).astype(BF16)
            k_end = (k * e_end[:, ks]).astype(BF16)
            att = jnp.where(tril, _dot_nt(q_d, k_d), 0.0)
            st = st_ref[h]
            o = _dot(att.astype(BF16), v) + _dot_nt(q_d, st.astype(BF16))
            st_ref[h] = dec[:, ks] * st + _dot_tn(v, k_end)
            y = _rms(o, og_ref[...]).astype(BF16).astype(F32)
            r = r_ref[rs, vs].astype(F32)
            o_ref[rs, vs] = (y * (r * jax.nn.sigmoid(r))).astype(BF16)


def gla_chunks(proj, la, o_g, bsz, tc=256):
    n = proj.shape[0]
    dk = la.shape[1]
    dv = (proj.shape[1] - 2 * dk) // 2
    s = n // bsz
    tc = min(tc, s)
    nt = s // tc
    hk, hv = dk // B_HEADS, dv // B_HEADS
    assert (2 * dk) % dv == 0
    v_blk = 2 * dk // dv
    row = lambda b, t: b * nt + t
    return pl.pallas_call(
        functools.partial(_gla_body, tc=tc, hk=hk, hv=hv),
        grid=(bsz, nt),
        in_specs=[
            pl.BlockSpec((tc, dk), lambda b, t: (row(b, t), 0)),
            pl.BlockSpec((tc, dk), lambda b, t: (row(b, t), 1)),
            pl.BlockSpec((tc, dv), lambda b, t: (row(b, t), v_blk)),
            pl.BlockSpec((tc, dv), lambda b, t: (row(b, t), v_blk + 1)),
            pl.BlockSpec((tc, dk), lambda b, t: (row(b, t), 0)),
            pl.BlockSpec((1, hv), lambda b, t: (0, 0)),
        ],
        out_specs=pl.BlockSpec((tc, dv), lambda b, t: (row(b, t), 0)),
        out_shape=jax.ShapeDtypeStruct((n, dv), BF16),
        scratch_shapes=[pltpu.VMEM((B_HEADS, hv, hk), F32)],
        compiler_params=_params("parallel", "arbitrary"),
        name="gla_chunks",
    )(proj, proj, proj, proj, la, o_g)


def _fox_gate_body(x_ref, g_ref, wf_ref, b_ref, crow_ref, carry_ref, *, tb, nh):
    @pl.when(pl.program_id(1) == 0)
    def _():
        carry_ref[...] = jnp.zeros_like(carry_ref)

    h = _rms(x_ref[...], g_ref[...]).astype(BF16)
    lane = lax.broadcasted_iota(jnp.int32, (tb, V7X_LANES), 1)
    log_f = jnp.where(lane < nh, _log_sigmoid(_dot(h, wf_ref[...]) + b_ref[...]), 0.0)
    ones_tril = _tril_mask(tb).astype(BF16)
    hi, mid, lo = _split3(log_f)
    c = carry_ref[...] + (_dot(ones_tril, hi) + _dot(ones_tril, mid) + _dot(ones_tril, lo))
    carry_ref[...] = c[tb - 1 : tb, :]
    sel = (lax.broadcasted_iota(jnp.int32, (nh, V7X_LANES), 0) == lax.broadcasted_iota(jnp.int32, (nh, V7X_LANES), 1)).astype(BF16)
    hi, mid, lo = _split3(c)
    crow_ref[...] = _dot_nt(sel, hi) + _dot_nt(sel, mid) + _dot_nt(sel, lo)


def fox_gate(x, g, w_f, f_bias, bsz, tb=512):
    n, d = x.shape
    nh = w_f.shape[1]
    s = n // bsz
    tb = min(tb, s)
    nt = s // tb
    wf = jnp.pad(w_f, ((0, 0), (0, V7X_LANES - nh))).astype(BF16)
    bias = jnp.pad(f_bias.reshape(1, nh), ((0, 0), (0, V7X_LANES - nh)))
    return pl.pallas_call(
        functools.partial(_fox_gate_body, tb=tb, nh=nh),
        grid=(bsz, nt),
        in_specs=[
            pl.BlockSpec((tb, d), lambda b, t: (b * nt + t, 0)),
            pl.BlockSpec((1, d), lambda b, t: (0, 0)),
            pl.BlockSpec((d, V7X_LANES), lambda b, t: (0, 0)),
            pl.BlockSpec((1, V7X_LANES), lambda b, t: (0, 0)),
        ],
        out_specs=pl.BlockSpec((None, nh, tb), lambda b, t: (b, 0, t)),
        out_shape=jax.ShapeDtypeStruct((bsz, nh, s), F32),
        scratch_shapes=[pltpu.VMEM((1, V7X_LANES), F32)],
        compiler_params=_params("parallel", "arbitrary"),
        name="fox_gate",
    )(x, g, wf, bias)


FOX_KEY_BLOCK = 512
FOX_SUBTILES = 8
FOX_ROW_CHUNK = 32


def _fox_flash_body(q_ref, k_ref, v_ref, crow_ref, o_ref, qs_ref, va_ref, s_ref, p_ref, al_ref, m_ref, acc_ref, *, t, nsub, scale):
    i = pl.program_id(2)
    dh = q_ref.shape[1]
    nl = t // V7X_LANES

    @pl.when(i == 0)
    def _():
        va_ref[:, :dh] = v_ref[...]
        va_ref[:, dh:] = jnp.ones((va_ref.shape[0], V7X_LANES), BF16)

    qs_ref[...] = (q_ref[...].astype(F32) * scale).astype(BF16)
    m_ref[...] = jnp.full_like(m_ref, -jnp.inf)
    acc_ref[...] = jnp.zeros_like(acc_ref)

    def block(sub, kb, masked):
        rows = slice(sub * t, (sub + 1) * t)
        off = pl.multiple_of(kb * t, t)
        s_ref[sub] = _dot_nt(qs_ref[rows, :], k_ref[pl.ds(off, t), :])
        bias = crow_ref[pl.ds(kb, 1), :]

        def chunk(c, carry):
            r = pl.multiple_of(c * FOX_ROW_CHUNK, FOX_ROW_CHUNK)
            sc = s_ref[sub, pl.ds(r, FOX_ROW_CHUNK), :] - bias
            if masked:
                row = r + lax.broadcasted_iota(jnp.int32, sc.shape, 0)
                col = lax.broadcasted_iota(jnp.int32, sc.shape, 1)
                sc = jnp.where(col <= row, sc, -jnp.inf)
            tiles = [sc[:, a * V7X_LANES : (a + 1) * V7X_LANES] for a in range(nl)]
            mx = functools.reduce(jnp.maximum, tiles)
            m_old = m_ref[pl.ds(sub * t + r, FOX_ROW_CHUNK), :]
            m_new = jnp.maximum(m_old, jnp.max(mx, axis=-1, keepdims=True))
            al_ref[sub, pl.ds(r, FOX_ROW_CHUNK), :] = jnp.exp(m_old - m_new)
            m_ref[pl.ds(sub * t + r, FOX_ROW_CHUNK), :] = m_new
            for a in range(nl):
                p_ref[sub, pl.ds(r, FOX_ROW_CHUNK), a * V7X_LANES : (a + 1) * V7X_LANES] = jnp.exp(tiles[a] - m_new).astype(BF16)
            return carry

        lax.fori_loop(0, t // FOX_ROW_CHUNK, chunk, 0, unroll=True)
        pv = _dot(p_ref[sub], va_ref[pl.ds(off, t), :])
        alpha = al_ref[sub]
        acc_ref[rows, :dh] = alpha * acc_ref[rows, :dh] + pv[:, :dh]
        acc_ref[rows, dh:] = alpha * acc_ref[rows, dh:] + pv[:, dh:]

    def below_diagonal(kb, carry):
        for sub in range(nsub):
            block(sub, kb, False)
        return carry

    first = i * nsub
    lax.fori_loop(0, first, below_diagonal, 0)
    for kb in range(nsub):
        for sub in range(kb, nsub):
            block(sub, first + kb, masked=(sub == kb))
    o_ref[...] = (acc_ref[:, :dh] / acc_ref[:, dh:]).astype(o_ref.dtype)


def fox_flash(proj, crow, bsz, nh):
    n = proj.shape[0]
    dh = proj.shape[1] // (3 * nh)
    assert dh == V7X_LANES
    s = n // bsz
    t = min(FOX_KEY_BLOCK, s)
    nsub = min(FOX_SUBTILES, s // t)
    tq = t * nsub
    nq = s // tq
    crow = crow.reshape(bsz, nh, s // t, t)
    return pl.pallas_call(
        functools.partial(_fox_flash_body, t=t, nsub=nsub, scale=dh**-0.5),
        grid=(bsz, nh, nq),
        in_specs=[
            pl.BlockSpec((tq, dh), lambda b, h, i: (b * nq + i, h)),
            pl.BlockSpec((s, dh), lambda b, h, i: (b, nh + h)),
            pl.BlockSpec((s, dh), lambda b, h, i: (b, 2 * nh + h)),
            pl.BlockSpec((None, None, s // t, t), lambda b, h, i: (b, h, 0, 0)),
        ],
        out_specs=pl.BlockSpec((tq, dh), lambda b, h, i: (b * nq + i, h)),
        out_shape=jax.ShapeDtypeStruct((n, nh * dh), BF16),
        scratch_shapes=[
            pltpu.VMEM((tq, dh), BF16),
            pltpu.VMEM((s, dh + V7X_LANES), BF16),
            pltpu.VMEM((nsub, t, t), F32),
            pltpu.VMEM((nsub, t, t), BF16),
            pltpu.VMEM((nsub, t, V7X_LANES), F32),
            pltpu.VMEM((tq, V7X_LANES), F32),
            pltpu.VMEM((tq, dh + V7X_LANES), F32),
        ],
        compiler_params=_params("parallel", "parallel", "arbitrary"),
        name="fox_flash",
    )(proj, proj, proj, crow)


def kernel(x, norm1_g, norm2_g, a_w_in, a_vnorm_g, a_w_s, a_b_s, a_w_out, b_w_in, b_w_gate2, b_gate_bias, b_onorm_g, b_w_out, c_w_in, c_f_bias, c_w_out, ffn_w_in, ffn_w_out, moe_router, moe_w_in, moe_w_out, final_g):
    bsz, s, d = x.shape
    depth = norm1_g.shape[0]
    xf = x.reshape(bsz * s, d)
    for i in range(depth):
        g1 = norm1_g[i].reshape(1, d)
        m, j = i % N_MIXERS, i // N_MIXERS
        if m == 0:
            xf = gmlp_mixer(xf, g1, a_w_in[j].astype(BF16), a_vnorm_g[j].reshape(1, -1), a_w_s[j], a_b_s[j].T, a_w_out[j].astype(BF16))
        elif m == 1:
            dk = b_w_gate2.shape[2]
            n_main = b_w_in.shape[2] - B_GATE_RANK
            proj = norm_matmul(xf, g1, b_w_in[j, :, :n_main].astype(BF16))
            la = gla_gate(xf, g1, b_w_in[j, :, n_main:], b_w_gate2[j], b_gate_bias[j].reshape(1, dk))
            o = gla_chunks(proj, la, b_onorm_g[j].reshape(1, -1), bsz)
            xf = matmul_residual(o, b_w_out[j].astype(BF16), xf)
        else:
            proj = norm_matmul(xf, g1, c_w_in[j, :, : 3 * d].astype(BF16))
            crow = fox_gate(xf, g1, c_w_in[j, :, 3 * d :], c_f_bias[j], bsz)
            o = fox_flash(proj, crow, bsz, C_HEADS)
            xf = matmul_residual(o, c_w_out[j].astype(BF16), xf)
        g2 = norm2_g[i].reshape(1, d)
        fg = final_g.reshape(1, d) if i == depth - 1 else None
        if i % 2 == 0:
            xf = ffn_resident(xf, g2, ffn_w_in[i // 2].astype(BF16), ffn_w_out[i // 2].astype(BF16), final_g=fg)
        else:
            xf = moe_ffn(xf, g2, moe_router[i // 2], moe_w_in, moe_w_out, i // 2, final_g=fg)
    return xf.reshape(bsz, s, d)
```

```python
import functools

import jax
import jax.numpy as jnp
from jax import lax
from jax.experimental import pallas as pl
from jax.experimental.pallas import tpu as pltpu

F32 = jnp.float32
BF16 = jnp.bfloat16

EPS = 1e-6
N_MIXERS = 3
TOP_K = 2
A_CHUNK = 128
A_GROUPS = 8
B_HEADS = 4
B_GATE_RANK = 16
B_TAU = 16.0
B_CHUNK = 64
C_HEADS = 8

V7X_LANES = 128
V7X_SUBLANES = 8
V7X_VMEM_LIMIT_BYTES = 56 * 1024 * 1024
V7X_VMEM_RESIDENT_LIMIT_BYTES = 62 * 1024 * 1024


def _params(*sem):
    return pltpu.CompilerParams(dimension_semantics=sem, vmem_limit_bytes=V7X_VMEM_LIMIT_BYTES)


def _block(total, target, align=V7X_LANES):
    if total <= target:
        return total
    return max(b for b in range(align, target + 1, align) if total % b == 0)


def _rms(x, g):
    ms = jnp.mean(x * x, axis=-1, keepdims=True)
    return x * lax.rsqrt(ms + EPS) * g


def _log_sigmoid(x):
    return jnp.minimum(x, 0.0) - jnp.log1p(jnp.exp(-jnp.abs(x)))


def _split3(x):
    hi = x.astype(BF16)
    r1 = x - hi.astype(F32)
    mid = r1.astype(BF16)
    lo = (r1 - mid.astype(F32)).astype(BF16)
    return hi, mid, lo


def _dot(a, b):
    return jnp.dot(a, b, preferred_element_type=F32)


def _dot_nt(a, b):
    return lax.dot_general(a, b, (((1,), (1,)), ((), ())), preferred_element_type=F32)


def _dot_tn(a, b):
    return lax.dot_general(a, b, (((0,), (0,)), ((), ())), preferred_element_type=F32)


def _tril_mask(n):
    row = lax.broadcasted_iota(jnp.int32, (n, n), 0)
    col = lax.broadcasted_iota(jnp.int32, (n, n), 1)
    return col <= row


def _norm_matmul_body(x_ref, g_ref, w_ref, o_ref, h_ref, *, tn):
    h_ref[...] = _rms(x_ref[...], g_ref[...]).astype(BF16)
    for c in range(w_ref.shape[1] // tn):
        cols = slice(c * tn, (c + 1) * tn)
        o_ref[:, cols] = _dot(h_ref[...], w_ref[:, cols]).astype(o_ref.dtype)


def norm_matmul(x, g, w, tm=1024, tn=1024):
    n, d = x.shape
    nout = w.shape[1]
    tm, tn = min(tm, n), _block(nout, tn)
    return pl.pallas_call(
        functools.partial(_norm_matmul_body, tn=tn),
        grid=(n // tm,),
        in_specs=[
            pl.BlockSpec((tm, d), lambda i: (i, 0)),
            pl.BlockSpec((1, d), lambda i: (0, 0)),
            pl.BlockSpec((d, nout), lambda i: (0, 0), pipeline_mode=pl.Buffered(1)),
        ],
        out_specs=pl.BlockSpec((tm, nout), lambda i: (i, 0)),
        out_shape=jax.ShapeDtypeStruct((n, nout), BF16),
        scratch_shapes=[pltpu.VMEM((tm, d), BF16)],
        compiler_params=_params("parallel"),
        name="norm_matmul",
    )(x, g, w)


def _matmul_residual_body(y_ref, w_ref, x_ref, o_ref):
    o_ref[...] = x_ref[...] + _dot(y_ref[...], w_ref[...])


def matmul_residual(y, w, x, tm=1024):
    n, k = y.shape
    d = w.shape[1]
    tm = min(tm, n)
    return pl.pallas_call(
        _matmul_residual_body,
        grid=(n // tm,),
        in_specs=[
            pl.BlockSpec((tm, k), lambda i: (i, 0)),
            pl.BlockSpec((k, d), lambda i: (0, 0)),
            pl.BlockSpec((tm, d), lambda i: (i, 0)),
        ],
        out_specs=pl.BlockSpec((tm, d), lambda i: (i, 0)),
        out_shape=jax.ShapeDtypeStruct((n, d), F32),
        compiler_params=_params("parallel"),
        name="matmul_residual",
    )(y, w, x)


def _gmlp_body(x_ref, g_ref, wi_ref, vg_ref, ws_ref, bt_ref, wo_ref, o_ref, h_ref, z_ref, vn_ref, y_ref, *, tm, width, tn):
    gd = width // A_GROUPS
    h_ref[...] = _rms(x_ref[...], g_ref[...]).astype(BF16)
    for c in range(2 * width // tn):
        cols = slice(c * tn, (c + 1) * tn)
        z_ref[:, cols] = jax.nn.gelu(_dot(h_ref[...], wi_ref[:, cols]), approximate=True).astype(BF16)
    vn_ref[...] = _rms(z_ref[:, width:].astype(F32), vg_ref[...]).astype(BF16)
    tril = _tril_mask(A_CHUNK)
    for g in range(A_GROUPS):
        w = jnp.where(tril, ws_ref[g], 0.0).astype(BF16)
        bias = bt_ref[:, g : g + 1]
        cs = slice(g * gd, (g + 1) * gd)
        for c in range(tm // A_CHUNK):
            rs = slice(c * A_CHUNK, (c + 1) * A_CHUNK)
            mixed = _dot(w, vn_ref[rs, cs]) + bias
            y_ref[rs, cs] = (z_ref[rs, cs].astype(F32) * mixed).astype(BF16)
    o_ref[...] = x_ref[...] + _dot(y_ref[...], wo_ref[...])


def gmlp_mixer(x, g, wi, vg, ws, bt, wo, tm=1024, tn=1024):
    n, d = x.shape
    width = wi.shape[1] // 2
    tm, tn = min(tm, n), _block(2 * width, tn)
    once = pl.Buffered(1)
    return pl.pallas_call(
        functools.partial(_gmlp_body, tm=tm, width=width, tn=tn),
        grid=(n // tm,),
        in_specs=[
            pl.BlockSpec((tm, d), lambda i: (i, 0)),
            pl.BlockSpec((1, d), lambda i: (0, 0)),
            pl.BlockSpec((d, 2 * width), lambda i: (0, 0), pipeline_mode=once),
            pl.BlockSpec((1, width), lambda i: (0, 0)),
            pl.BlockSpec((A_GROUPS, A_CHUNK, A_CHUNK), lambda i: (0, 0, 0)),
            pl.BlockSpec((A_CHUNK, A_GROUPS), lambda i: (0, 0)),
            pl.BlockSpec((width, d), lambda i: (0, 0), pipeline_mode=once),
        ],
        out_specs=pl.BlockSpec((tm, d), lambda i: (i, 0)),
        out_shape=jax.ShapeDtypeStruct((n, d), F32),
        scratch_shapes=[
            pltpu.VMEM((tm, d), BF16),
            pltpu.VMEM((tm, 2 * width), BF16),
            pltpu.VMEM((tm, width), BF16),
            pltpu.VMEM((tm, width), BF16),
        ],
        compiler_params=pltpu.CompilerParams(dimension_semantics=("parallel",), vmem_limit_bytes=V7X_VMEM_RESIDENT_LIMIT_BYTES),
        name="gmlp_mixer",
    )(x, g, wi, vg, ws, bt, wo)


def _ffn_resident_body(*refs, f, tf, final):
    x_ref, g_ref, wi_ref, wo_ref = refs[:4]
    rest = list(refs[4:])
    fg_ref = rest.pop(0) if final else None
    o_ref, h_ref, t_ref = rest
    h_ref[...] = _rms(x_ref[...], g_ref[...]).astype(BF16)
    for c in range(f // tf):
        a = _dot(h_ref[...], wi_ref[:, c * tf : (c + 1) * tf])
        u = _dot(h_ref[...], wi_ref[:, f + c * tf : f + (c + 1) * tf])
        t_ref[:, c * tf : (c + 1) * tf] = ((a * jax.nn.sigmoid(a)) * u).astype(BF16)
    y = x_ref[...] + _dot(t_ref[...], wo_ref[...])
    o_ref[...] = _rms(y, fg_ref[...]) if final else y


def ffn_resident(x, g, w_in, w_out, final_g=None, tm=1024, tf=512):
    n, d = x.shape
    f = w_out.shape[0]
    tm, tf = min(tm, n), min(tf, f)
    final = final_g is not None
    once = pl.Buffered(1)
    in_specs = [
        pl.BlockSpec((tm, d), lambda i: (i, 0)),
        pl.BlockSpec((1, d), lambda i: (0, 0)),
        pl.BlockSpec((d, 2 * f), lambda i: (0, 0), pipeline_mode=once),
        pl.BlockSpec((f, d), lambda i: (0, 0), pipeline_mode=once),
    ]
    args = [x, g, w_in, w_out]
    if final:
        in_specs.append(pl.BlockSpec((1, d), lambda i: (0, 0)))
        args.append(final_g)
    return pl.pallas_call(
        functools.partial(_ffn_resident_body, f=f, tf=tf, final=final),
        grid=(n // tm,),
        in_specs=in_specs,
        out_specs=pl.BlockSpec((tm, d), lambda i: (i, 0)),
        out_shape=jax.ShapeDtypeStruct((n, d), F32),
        scratch_shapes=[pltpu.VMEM((tm, d), BF16), pltpu.VMEM((tm, f), BF16)],
        compiler_params=pltpu.CompilerParams(dimension_semantics=("parallel",), vmem_limit_bytes=V7X_VMEM_RESIDENT_LIMIT_BYTES),
        name="ffn_resident",
    )(*args)


TOK_ROWS = 8


def _router_body(x_ref, g_ref, whi_ref, wlo_ref, gate_ref, tok_ref, cnt_ref, carry_ref, *, ne, tm):
    @pl.when(pl.program_id(0) == 0)
    def _():
        carry_ref[...] = jnp.zeros_like(carry_ref)

    h = _rms(x_ref[...], g_ref[...])
    h_hi = h.astype(BF16)
    h_lo = (h - h_hi.astype(F32)).astype(BF16)
    logits = _dot(h_hi, whi_ref[...]) + _dot(h_lo, whi_ref[...]) + _dot(h_hi, wlo_ref[...])
    lane = lax.broadcasted_iota(jnp.int32, logits.shape, 1).astype(F32)
    neg = -jnp.inf
    l1 = jnp.where(lane < ne, logits, neg)
    m1 = jnp.max(l1, axis=-1, keepdims=True)
    i1 = jnp.min(jnp.where(l1 == m1, lane, float(V7X_LANES)), axis=-1, keepdims=True)
    l2 = jnp.where(lane == i1, neg, l1)
    m2 = jnp.max(l2, axis=-1, keepdims=True)
    i2 = jnp.min(jnp.where(l2 == m2, lane, float(V7X_LANES)), axis=-1, keepdims=True)
    e2 = jnp.exp(m2 - m1)
    den = 1.0 + e2
    gate_ref[...] = jnp.where(lane == 0.0, 1.0 / den, 0.0) + jnp.where(lane == 1.0, e2 / den, 0.0)

    sel1, sel2 = lane == i1, lane == i2
    onehot = jnp.where(sel1 | sel2, 1.0, 0.0)
    row = lax.broadcasted_iota(jnp.int32, (tm, tm), 0)
    col = lax.broadcasted_iota(jnp.int32, (tm, tm), 1)
    before = carry_ref[...] + _dot((col < row).astype(BF16), onehot.astype(BF16))
    carry_ref[...] = before[tm - 1 : tm, :] + onehot[tm - 1 : tm, :]
    cnt_ref[...] = carry_ref[...]
    r1 = jnp.sum(jnp.where(sel1, before, 0.0), axis=-1, keepdims=True)
    r2 = jnp.sum(jnp.where(sel2, before, 0.0), axis=-1, keepdims=True)
    table = (jnp.where(lane == 0.0, i1, 0.0) + jnp.where(lane == 1.0, i2, 0.0)
             + jnp.where(lane == 2.0, r1, 0.0) + jnp.where(lane == 3.0, r2, 0.0))
    pick = (lax.broadcasted_iota(jnp.int32, (TOK_ROWS, V7X_LANES), 0) == lax.broadcasted_iota(jnp.int32, (TOK_ROWS, V7X_LANES), 1)).astype(BF16)
    hi, mid, lo = _split3(table)
    tok_ref[...] = (_dot_nt(pick, hi) + _dot_nt(pick, mid) + _dot_nt(pick, lo)).astype(jnp.int32)


def moe_router(x, g, w_router, tm=512):
    n, d = x.shape
    ne = w_router.shape[1]
    tm = min(tm, n)
    wpad = jnp.pad(w_router, ((0, 0), (0, V7X_LANES - ne)))
    w_hi = wpad.astype(BF16)
    w_lo = (wpad - w_hi.astype(F32)).astype(BF16)
    return pl.pallas_call(
        functools.partial(_router_body, ne=ne, tm=tm),
        grid=(n // tm,),
        in_specs=[
            pl.BlockSpec((tm, d), lambda i: (i, 0)),
            pl.BlockSpec((1, d), lambda i: (0, 0)),
            pl.BlockSpec((d, V7X_LANES), lambda i: (0, 0)),
            pl.BlockSpec((d, V7X_LANES), lambda i: (0, 0)),
        ],
        out_specs=[
            pl.BlockSpec((tm, V7X_LANES), lambda i: (i, 0)),
            pl.BlockSpec((TOK_ROWS, tm), lambda i: (0, i)),
            pl.BlockSpec((1, V7X_LANES), lambda i: (0, 0)),
        ],
        out_shape=[
            jax.ShapeDtypeStruct((n, V7X_LANES), F32),
            jax.ShapeDtypeStruct((TOK_ROWS, n), jnp.int32),
            jax.ShapeDtypeStruct((1, V7X_LANES), F32),
        ],
        scratch_shapes=[pltpu.VMEM((1, V7X_LANES), F32)],
        compiler_params=_params("arbitrary"),
        name="moe_router",
    )(x, g, w_hi, w_lo)


def _plan_body(cnt_ref, tok_ref, off_ref, te_ref, tv_ref, nv_ref, pos_ref, *, ne, tg, n_tiles):
    off = jnp.int32(0)
    tile = jnp.int32(0)
    expert = tok_ref[0:TOP_K, :]
    group_start = jnp.zeros_like(expert)
    for e in range(ne):
        nt = (cnt_ref[e] + (tg - 1)) // tg
        off_ref[e] = off
        group_start = jnp.where(expert == e, off, group_start)

        def mark(ti, carry, e=e, first=tile):
            te_ref[ti] = jnp.int32(e)
            tv_ref[ti] = jnp.minimum(cnt_ref[e] - (ti - first) * tg, tg)
            return carry

        lax.fori_loop(tile, tile + nt, mark, 0)
        off = off + nt * tg
        tile = tile + nt
    pos_ref[...] = jnp.zeros_like(pos_ref)
    pos_ref[0:TOP_K, :] = group_start + tok_ref[TOP_K : 2 * TOP_K, :]
    nv_ref[0] = tile
    last = te_ref[jnp.maximum(tile - 1, 0)]

    def fill(ti, carry):
        te_ref[ti] = last
        tv_ref[ti] = jnp.int32(0)
        return carry

    lax.fori_loop(tile, n_tiles, fill, 0)


def moe_plan(cnt, tok, tg, n_tiles):
    ne = cnt.shape[0]
    smem = pl.BlockSpec(memory_space=pltpu.SMEM)
    vmem = pl.BlockSpec(memory_space=pltpu.VMEM)
    return pl.pallas_call(
        functools.partial(_plan_body, ne=ne, tg=tg, n_tiles=n_tiles),
        in_specs=[smem, vmem],
        out_specs=[smem, smem, smem, smem, vmem],
        out_shape=[
            jax.ShapeDtypeStruct((ne,), jnp.int32),
            jax.ShapeDtypeStruct((n_tiles,), jnp.int32),
            jax.ShapeDtypeStruct((n_tiles,), jnp.int32),
            jax.ShapeDtypeStruct((1,), jnp.int32),
            jax.ShapeDtypeStruct(tok.shape, jnp.int32),
        ],
        name="moe_plan",
    )(cnt, tok)


def _token_copy(src, s, dst, r, sem, ns):
    return pltpu.make_async_copy(src.at[pl.ds(pl.multiple_of(s * ns, ns), ns)], dst.at[pl.ds(pl.multiple_of(r * ns, ns), ns)], sem)


def _to_slabs(dst_ref, val, ns):
    rows = val.shape[0]
    for c in range(ns):
        dst_ref[pl.ds(c, rows, stride=ns), :] = val[:, c * V7X_LANES : (c + 1) * V7X_LANES]


def _slab(src_ref, c, rows, ns):
    return src_ref[pl.ds(c, rows, stride=ns), :]


def _dispatch_body(pos_hbm, off_ref, cnt_ref, x_ref, g_ref, xs_hbm, pos_smem, x3_ref, zero_ref, sem_idx, sem_row, *, td, tg, ne, n_tiles, ns):
    base = pl.multiple_of(pl.program_id(0) * td, td)
    idx = pltpu.make_async_copy(pos_hbm.at[:, pl.ds(base, td)], pos_smem, sem_idx)
    idx.start()
    _to_slabs(x3_ref, _rms(x_ref[...], g_ref[...]), ns)

    @pl.when(pl.program_id(0) == 0)
    def _():
        zero_ref[...] = jnp.zeros_like(zero_ref)
        for e in range(ne):
            start = off_ref[e] + cnt_ref[e]
            stop = off_ref[e] + (cnt_ref[e] + (tg - 1)) // tg * tg

            def zero_start(r, carry):
                _token_copy(zero_ref, 0, xs_hbm, r, sem_row, ns).start()
                return carry

            def zero_wait(r, carry):
                _token_copy(zero_ref, 0, xs_hbm, r, sem_row, ns).wait()
                return carry

            lax.fori_loop(start, stop, zero_start, 0)
            lax.fori_loop(start, stop, zero_wait, 0)

        def tile_copy(ti):
            return pltpu.make_async_copy(zero_ref, xs_hbm.at[pl.ds(pl.multiple_of(ti * (tg * ns), tg * ns), tg * ns)], sem_row)

        def tile_start(ti, carry):
            tile_copy(ti).start()
            return carry

        def tile_wait(ti, carry):
            tile_copy(ti).wait()
            return carry

        used = stop // tg
        lax.fori_loop(used, n_tiles, tile_start, 0)
        lax.fori_loop(used, n_tiles, tile_wait, 0)

    idx.wait()

    def start(t, c):
        for k in range(TOP_K):
            _token_copy(x3_ref, t, xs_hbm, pos_smem[k, t], sem_row, ns).start(priority=k % 2)
        return c

    lax.fori_loop(0, td, start, 0, unroll=8)
    for _ in range(TOP_K):
        pltpu.make_async_copy(x3_ref, xs_hbm.at[pl.ds(0, td * ns)], sem_row).wait()


def moe_dispatch(pos, off, cnt, x, g, tg, n_tiles, td=1024):
    n, d = x.shape
    ns = d // V7X_LANES
    assert ns % V7X_SUBLANES == 0
    td = min(td, n)
    smem = pl.BlockSpec(memory_space=pltpu.SMEM)
    hbm = pl.BlockSpec(memory_space=pl.ANY)
    return pl.pallas_call(
        functools.partial(_dispatch_body, td=td, tg=tg, ne=cnt.shape[0], n_tiles=n_tiles, ns=ns),
        grid=(n // td,),
        in_specs=[hbm, smem, smem, pl.BlockSpec((td, d), lambda i: (i, 0)), pl.BlockSpec((1, d), lambda i: (0, 0))],
        out_specs=hbm,
        out_shape=jax.ShapeDtypeStruct((n_tiles * tg * ns, V7X_LANES), x.dtype),
        scratch_shapes=[
            pltpu.SMEM((TOK_ROWS, td), jnp.int32),
            pltpu.VMEM((td * ns, V7X_LANES), x.dtype),
            pltpu.VMEM((tg * ns, V7X_LANES), x.dtype),
            pltpu.SemaphoreType.DMA(()),
            pltpu.SemaphoreType.DMA(()),
        ],
        compiler_params=pltpu.CompilerParams(dimension_semantics=("arbitrary",), has_side_effects=True, vmem_limit_bytes=V7X_VMEM_LIMIT_BYTES),
        name="moe_dispatch",
    )(pos, off, cnt, x, g)


GROUP_SHORT_TILE_DIV = 4
GROUP_STAGE_SLOTS = 4
GROUP_STAGE_BYTES = 1024 * 1024


def _load_cast(src_hbm, dst_ref, stage_ref, sem, rows):
    n = dst_ref.shape[0] // rows
    slots = stage_ref.shape[0]
    ahead = slots - 1

    def chunk(c, slot):
        return pltpu.make_async_copy(src_hbm.at[pl.ds(pl.multiple_of(c * rows, rows), rows)], stage_ref.at[slot], sem.at[slot])

    for c in range(min(ahead, n)):
        chunk(c, c % slots).start()

    def body(c, carry):
        for slot in range(slots):

            @pl.when(c % slots == slot)
            def _():
                @pl.when(c + ahead < n)
                def _():
                    chunk(c + ahead, (slot + ahead) % slots).start()

                chunk(c, slot).wait()
                dst_ref[pl.ds(pl.multiple_of(c * rows, rows), rows), :] = stage_ref[slot].astype(BF16)

        return carry

    lax.fori_loop(0, n, body, 0)


def _grouped_ffn_body(te_ref, tv_ref, nv_ref, x_ref, wi_hbm, wo_hbm, o_ref, wi_ref, wo_ref, si_ref, so_ref, sem_i, sem_o, h_ref, t_ref, *, layer, tg, ns, f, tf):
    i = pl.program_id(0)
    e = te_ref[i]

    @pl.when((i < nv_ref[0]) & ((i == 0) | (e != te_ref[jnp.maximum(i - 1, 0)])))
    def _():
        _load_cast(wi_hbm.at[layer, e], wi_ref, si_ref, sem_i, si_ref.shape[1])
        _load_cast(wo_hbm.at[layer, e], wo_ref, so_ref, sem_o, so_ref.shape[1])

    @pl.when(i < nv_ref[0])
    def _():
        for c in range(ns):
            h_ref[:, c * V7X_LANES : (c + 1) * V7X_LANES] = _slab(x_ref, c, tg, ns).astype(BF16)

        def run(m):
            for c in range(f // tf):
                a = _dot(h_ref[0:m, :], wi_ref[:, c * tf : (c + 1) * tf])
                u = _dot(h_ref[0:m, :], wi_ref[:, f + c * tf : f + (c + 1) * tf])
                t_ref[0:m, c * tf : (c + 1) * tf] = ((a * jax.nn.sigmoid(a)) * u).astype(BF16)
            y = _dot(t_ref[0:m, :], wo_ref[...])
            for c in range(ns):
                o_ref[pl.ds(c, m, stride=ns), :] = y[:, c * V7X_LANES : (c + 1) * V7X_LANES]
            if m < tg:
                o_ref[m * ns :, :] = jnp.zeros(((tg - m) * ns, V7X_LANES), F32)

        short = tv_ref[i] <= tg // GROUP_SHORT_TILE_DIV
        pl.when(short)(lambda: run(tg // GROUP_SHORT_TILE_DIV))
        pl.when(jnp.logical_not(short))(lambda: run(tg))

    @pl.when(i >= nv_ref[0])
    def _():
        o_ref[...] = jnp.zeros_like(o_ref)


def moe_grouped_ffn(te, tv, nv, xs, w_in, w_out, layer, tg, tf=512):
    d = w_in.shape[2]
    ns = d // V7X_LANES
    r = xs.shape[0] // ns
    f = w_out.shape[2]
    tf = _block(f, tf)
    n_tiles = r // tg
    hbm = pl.BlockSpec(memory_space=pl.ANY)
    rows_in = _block(d, GROUP_STAGE_BYTES // (2 * f * 4), V7X_SUBLANES)
    rows_out = _block(f, GROUP_STAGE_BYTES // (d * 4), V7X_SUBLANES)

    def tile(i, nv):
        return jnp.minimum(i, nv[0] - 1)

    grid_spec = pltpu.PrefetchScalarGridSpec(
        num_scalar_prefetch=3,
        grid=(n_tiles,),
        in_specs=[pl.BlockSpec((tg * ns, V7X_LANES), lambda i, te, tv, nv: (tile(i, nv), 0)), hbm, hbm],
        out_specs=pl.BlockSpec((tg * ns, V7X_LANES), lambda i, te, tv, nv: (i, 0)),
        scratch_shapes=[
            pltpu.VMEM((d, 2 * f), BF16),
            pltpu.VMEM((f, d), BF16),
            pltpu.VMEM((GROUP_STAGE_SLOTS, rows_in, 2 * f), F32),
            pltpu.VMEM((GROUP_STAGE_SLOTS, rows_out, d), F32),
            pltpu.SemaphoreType.DMA((GROUP_STAGE_SLOTS,)),
            pltpu.SemaphoreType.DMA((GROUP_STAGE_SLOTS,)),
            pltpu.VMEM((tg, d), BF16),
            pltpu.VMEM((tg, f), BF16),
        ],
    )
    return pl.pallas_call(
        functools.partial(_grouped_ffn_body, layer=layer, tg=tg, ns=ns, f=f, tf=tf),
        grid_spec=grid_spec,
        out_shape=jax.ShapeDtypeStruct((r * ns, V7X_LANES), F32),
        compiler_params=pltpu.CompilerParams(dimension_semantics=("arbitrary",), vmem_limit_bytes=V7X_VMEM_RESIDENT_LIMIT_BYTES),
        name="moe_grouped_ffn",
    )(te, tv, nv, xs, w_in, w_out)


def _combine_body(pos_hbm, gate_ref, x_ref, ys_hbm, *rest, tc, ns, final):
    rest = list(rest)
    fg_ref = rest.pop(0) if final else None
    o_ref, pos_smem, buf_ref, sem_idx, sem_row = rest
    s, nsteps = pl.program_id(0), pl.num_programs(0)

    def idx_copy(step, slot):
        return pltpu.make_async_copy(pos_hbm.at[:, pl.ds(pl.multiple_of(step * tc, tc), tc)], pos_smem.at[slot], sem_idx.at[slot])

    def gather(slot):
        def start(t, c):
            for k in range(TOP_K):
                _token_copy(ys_hbm, pos_smem[slot, k, t], buf_ref.at[slot, k], t, sem_row.at[slot], ns).start(priority=k % 2)
            return c

        lax.fori_loop(0, tc, start, 0, unroll=8)

    def step(slot):
        other = 1 - slot

        @pl.when(s == 0)
        def _():
            idx_copy(s, slot).start()
            idx_copy(s, slot).wait()
            gather(slot)

            @pl.when(nsteps > 1)
            def _():
                idx_copy(s + 1, other).start()

        @pl.when(s + 1 < nsteps)
        def _():
            idx_copy(s + 1, other).wait()
            gather(other)

        @pl.when(s + 2 < nsteps)
        def _():
            idx_copy(s + 2, slot).start()

        for k in range(TOP_K):
            pltpu.make_async_copy(ys_hbm.at[pl.ds(0, tc * ns)], buf_ref.at[slot, k], sem_row.at[slot]).wait()
        g0, g1 = gate_ref[:, 0:1], gate_ref[:, 1:2]
        for c in range(ns):
            cols = slice(c * V7X_LANES, (c + 1) * V7X_LANES)
            o_ref[:, cols] = x_ref[:, cols] + g0 * _slab(buf_ref.at[slot, 0], c, tc, ns) + g1 * _slab(buf_ref.at[slot, 1], c, tc, ns)
        if final:
            o_ref[...] = _rms(o_ref[...], fg_ref[...])

    for slot in range(2):
        pl.when(s % 2 == slot)(functools.partial(step, slot))


def moe_combine(pos, gates, x, ys, final_g=None, tc=512):
    n, d = x.shape
    ns = d // V7X_LANES
    tc = min(tc, n)
    final = final_g is not None
    hbm = pl.BlockSpec(memory_space=pl.ANY)
    in_specs = [hbm, pl.BlockSpec((tc, V7X_LANES), lambda i: (i, 0)), pl.BlockSpec((tc, d), lambda i: (i, 0)), hbm]
    args = [pos, gates, x, ys]
    if final:
        in_specs.append(pl.BlockSpec((1, d), lambda i: (0, 0)))
        args.append(final_g)
    return pl.pallas_call(
        functools.partial(_combine_body, tc=tc, ns=ns, final=final),
        grid=(n // tc,),
        in_specs=in_specs,
        out_specs=pl.BlockSpec((tc, d), lambda i: (i, 0)),
        out_shape=jax.ShapeDtypeStruct((n, d), F32),
        scratch_shapes=[
            pltpu.SMEM((2, TOK_ROWS, tc), jnp.int32),
            pltpu.VMEM((2, TOP_K, tc * ns, V7X_LANES), F32),
            pltpu.SemaphoreType.DMA((2,)),
            pltpu.SemaphoreType.DMA((2,)),
        ],
        compiler_params=_params("arbitrary"),
        name="moe_combine",
    )(*args)


def moe_ffn(x, g, w_router, w_in, w_out, layer, final_g=None, tg=1024):
    n, _ = x.shape
    ne = w_router.shape[1]
    tg = min(tg, n)
    n_tiles = TOP_K * n // tg + ne
    gates, tok, counts = moe_router(x, g, w_router)
    cnt = counts[0, :ne].astype(jnp.int32)
    off, te, tv, nv, pos = moe_plan(cnt, tok, tg, n_tiles)
    xs = moe_dispatch(pos, off, cnt, x, g, tg, n_tiles)
    ys = moe_grouped_ffn(te, tv, nv, xs, w_in, w_out, layer, tg)
    return moe_combine(pos, gates, x, ys, final_g)


def _gla_gate_body(x_ref, g_ref, wl_ref, w2_ref, b_ref, o_ref):
    h = _rms(x_ref[...], g_ref[...]).astype(BF16)
    g_low = _dot(h, wl_ref[...]).astype(BF16)
    o_ref[...] = _log_sigmoid(_dot(g_low, w2_ref[...]) + b_ref[...]) / B_TAU


def gla_gate(x, g, w_low, w_gate2, bias, tm=1024):
    n, d = x.shape
    dk = w_gate2.shape[1]
    tm = min(tm, n)
    wl = jnp.pad(w_low, ((0, 0), (0, V7X_LANES - B_GATE_RANK))).astype(BF16)
    w2 = jnp.pad(w_gate2, ((0, V7X_LANES - B_GATE_RANK), (0, 0))).astype(BF16)
    return pl.pallas_call(
        _gla_gate_body,
        grid=(n // tm,),
        in_specs=[
            pl.BlockSpec((tm, d), lambda i: (i, 0)),
            pl.BlockSpec((1, d), lambda i: (0, 0)),
            pl.BlockSpec((d, V7X_LANES), lambda i: (0, 0)),
            pl.BlockSpec((V7X_LANES, dk), lambda i: (0, 0)),
            pl.BlockSpec((1, dk), lambda i: (0, 0)),
        ],
        out_specs=pl.BlockSpec((tm, dk), lambda i: (i, 0)),
        out_shape=jax.ShapeDtypeStruct((n, dk), F32),
        compiler_params=_params("parallel"),
        name="gla_gate",
    )(x, g, wl, w2, bias)


def _gla_body(q_ref, k_ref, v_ref, r_ref, la_ref, og_ref, o_ref, st_ref, *, tc, hk, hv):
    @pl.when(pl.program_id(1) == 0)
    def _():
        st_ref[...] = jnp.zeros_like(st_ref)

    c = B_CHUNK
    tril = _tril_mask(c)
    ones_tril = tril.astype(BF16)
    scale = hk**-0.5
    for ci in range(tc // c):
        rs = slice(ci * c, (ci + 1) * c)
        hi, mid, lo = _split3(la_ref[rs, :])
        bcum = _dot(ones_tril, hi) + _dot(ones_tril, mid) + _dot(ones_tril, lo)
        b_last = bcum[c - 1 : c, :]
        e_pos = jnp.exp(bcum)
        e_neg = jnp.exp(-bcum)
        e_end = jnp.exp(b_last - bcum)
        dec = jnp.exp(b_last)
        for h in range(B_HEADS):
            ks = slice(h * hk, (h + 1) * hk)
            vs = slice(h * hv, (h + 1) * hv)
            q = q_ref[rs, ks].astype(F32) * scale
            k = k_ref[rs, ks].astype(F32)
            v = v_ref[rs, vs]
            q_d = (q * e_pos[:, ks]).astype(BF16)
            k_d = (k * e_neg[:, ks]).astype(BF16)
            k_end = (k * e_end[:, ks]).astype(BF16)
            att = jnp.where(tril, _dot_nt(q_d, k_d), 0.0)
            st = st_ref[h]
            o = _dot(att.astype(BF16), v) + _dot_nt(q_d, st.astype(BF16))
            st_ref[h] = dec[:, ks] * st + _dot_tn(v, k_end)
            y = _rms(o, og_ref[...]).astype(BF16).astype(F32)
            r = r_ref[rs, vs].astype(F32)
            o_ref[rs, vs] = (y * (r * jax.nn.sigmoid(r))).astype(BF16)


def gla_chunks(proj, la, o_g, bsz, tc=256):
    n = proj.shape[0]
    dk = la.shape[1]
    dv = (proj.shape[1] - 2 * dk) // 2
    s = n // bsz
    tc = min(tc, s)
    nt = s // tc
    hk, hv = dk // B_HEADS, dv // B_HEADS
    assert (2 * dk) % dv == 0
    v_blk = 2 * dk // dv
    row = lambda b, t: b * nt + t
    return pl.pallas_call(
        functools.partial(_gla_body, tc=tc, hk=hk, hv=hv),
        grid=(bsz, nt),
        in_specs=[
            pl.BlockSpec((tc, dk), lambda b, t: (row(b, t), 0)),
            pl.BlockSpec((tc, dk), lambda b, t: (row(b, t), 1)),
            pl.BlockSpec((tc, dv), lambda b, t: (row(b, t), v_blk)),
            pl.BlockSpec((tc, dv), lambda b, t: (row(b, t), v_blk + 1)),
            pl.BlockSpec((tc, dk), lambda b, t: (row(b, t), 0)),
            pl.BlockSpec((1, hv), lambda b, t: (0, 0)),
        ],
        out_specs=pl.BlockSpec((tc, dv), lambda b, t: (row(b, t), 0)),
        out_shape=jax.ShapeDtypeStruct((n, dv), BF16),
        scratch_shapes=[pltpu.VMEM((B_HEADS, hv, hk), F32)],
        compiler_params=_params("parallel", "arbitrary"),
        name="gla_chunks",
    )(proj, proj, proj, proj, la, o_g)


def _fox_gate_body(x_ref, g_ref, wf_ref, b_ref, crow_ref, carry_ref, *, tb, nh):
    @pl.when(pl.program_id(1) == 0)
    def _():
        carry_ref[...] = jnp.zeros_like(carry_ref)

    h = _rms(x_ref[...], g_ref[...]).astype(BF16)
    lane = lax.broadcasted_iota(jnp.int32, (tb, V7X_LANES), 1)
    log_f = jnp.where(lane < nh, _log_sigmoid(_dot(h, wf_ref[...]) + b_ref[...]), 0.0)
    ones_tril = _tril_mask(tb).astype(BF16)
    hi, mid, lo = _split3(log_f)
    c = carry_ref[...] + (_dot(ones_tril, hi) + _dot(ones_tril, mid) + _dot(ones_tril, lo))
    carry_ref[...] = c[tb - 1 : tb, :]
    sel = (lax.broadcasted_iota(jnp.int32, (nh, V7X_LANES), 0) == lax.broadcasted_iota(jnp.int32, (nh, V7X_LANES), 1)).astype(BF16)
    hi, mid, lo = _split3(c)
    crow_ref[...] = _dot_nt(sel, hi) + _dot_nt(sel, mid) + _dot_nt(sel, lo)


def fox_gate(x, g, w_f, f_bias, bsz, tb=512):
    n, d = x.shape
    nh = w_f.shape[1]
    s = n // bsz
    tb = min(tb, s)
    nt = s // tb
    wf = jnp.pad(w_f, ((0, 0), (0, V7X_LANES - nh))).astype(BF16)
    bias = jnp.pad(f_bias.reshape(1, nh), ((0, 0), (0, V7X_LANES - nh)))
    return pl.pallas_call(
        functools.partial(_fox_gate_body, tb=tb, nh=nh),
        grid=(bsz, nt),
        in_specs=[
            pl.BlockSpec((tb, d), lambda b, t: (b * nt + t, 0)),
            pl.BlockSpec((1, d), lambda b, t: (0, 0)),
            pl.BlockSpec((d, V7X_LANES), lambda b, t: (0, 0)),
            pl.BlockSpec((1, V7X_LANES), lambda b, t: (0, 0)),
        ],
        out_specs=pl.BlockSpec((None, nh, tb), lambda b, t: (b, 0, t)),
        out_shape=jax.ShapeDtypeStruct((bsz, nh, s), F32),
        scratch_shapes=[pltpu.VMEM((1, V7X_LANES), F32)],
        compiler_params=_params("parallel", "arbitrary"),
        name="fox_gate",
    )(x, g, wf, bias)


FOX_KEY_BLOCK = 512
FOX_SUBTILES = 8
FOX_ROW_CHUNK = 32


def _fox_flash_body(q_ref, k_ref, v_ref, crow_ref, o_ref, qs_ref, va_ref, s_ref, p_ref, al_ref, m_ref, acc_ref, *, t, nsub, scale):
    i = pl.program_id(2)
    dh = q_ref.shape[1]
    nl = t // V7X_LANES

    @pl.when(i == 0)
    def _():
        va_ref[:, :dh] = v_ref[...]
        va_ref[:, dh:] = jnp.ones((va_ref.shape[0], V7X_LANES), BF16)

    qs_ref[...] = (q_ref[...].astype(F32) * scale).astype(BF16)
    m_ref[...] = jnp.full_like(m_ref, -jnp.inf)
    acc_ref[...] = jnp.zeros_like(acc_ref)

    def block(sub, kb, masked):
        rows = slice(sub * t, (sub + 1) * t)
        off = pl.multiple_of(kb * t, t)
        s_ref[sub] = _dot_nt(qs_ref[rows, :], k_ref[pl.ds(off, t), :])
        bias = crow_ref[pl.ds(kb, 1), :]

        def chunk(c, carry):
            r = pl.multiple_of(c * FOX_ROW_CHUNK, FOX_ROW_CHUNK)
            sc = s_ref[sub, pl.ds(r, FOX_ROW_CHUNK), :] - bias
            if masked:
                row = r + lax.broadcasted_iota(jnp.int32, sc.shape, 0)
                col = lax.broadcasted_iota(jnp.int32, sc.shape, 1)
                sc = jnp.where(col <= row, sc, -jnp.inf)
            tiles = [sc[:, a * V7X_LANES : (a + 1) * V7X_LANES] for a in range(nl)]
            mx = functools.reduce(jnp.maximum, tiles)
            m_old = m_ref[pl.ds(sub * t + r, FOX_ROW_CHUNK), :]
            m_new = jnp.maximum(m_old, jnp.max(mx, axis=-1, keepdims=True))
            al_ref[sub, pl.ds(r, FOX_ROW_CHUNK), :] = jnp.exp(m_old - m_new)
            m_ref[pl.ds(sub * t + r, FOX_ROW_CHUNK), :] = m_new
            for a in range(nl):
                p_ref[sub, pl.ds(r, FOX_ROW_CHUNK), a * V7X_LANES : (a + 1) * V7X_LANES] = jnp.exp(tiles[a] - m_new).astype(BF16)
            return carry

        lax.fori_loop(0, t // FOX_ROW_CHUNK, chunk, 0, unroll=True)
        pv = _dot(p_ref[sub], va_ref[pl.ds(off, t), :])
        alpha = al_ref[sub]
        acc_ref[rows, :dh] = alpha * acc_ref[rows, :dh] + pv[:, :dh]
        acc_ref[rows, dh:] = alpha * acc_ref[rows, dh:] + pv[:, dh:]

    def below_diagonal(kb, carry):
        for sub in range(nsub):
            block(sub, kb, False)
        return carry

    first = i * nsub
    lax.fori_loop(0, first, below_diagonal, 0)
    for kb in range(nsub):
        for sub in range(kb, nsub):
            block(sub, first + kb, masked=(sub == kb))
    o_ref[...] = (acc_ref[:, :dh] / acc_ref[:, dh:]).astype(o_ref.dtype)


def fox_flash(proj, crow, bsz, nh):
    n = proj.shape[0]
    dh = proj.shape[1] // (3 * nh)
    assert dh == V7X_LANES
    s = n // bsz
    t = min(FOX_KEY_BLOCK, s)
    nsub = min(FOX_SUBTILES, s // t)
    tq = t * nsub
    nq = s // tq
    crow = crow.reshape(bsz, nh, s // t, t)
    return pl.pallas_call(
        functools.partial(_fox_flash_body, t=t, nsub=nsub, scale=dh**-0.5),
        grid=(bsz, nh, nq),
        in_specs=[
            pl.BlockSpec((tq, dh), lambda b, h, i: (b * nq + i, h)),
            pl.BlockSpec((s, dh), lambda b, h, i: (b, nh + h)),
            pl.BlockSpec((s, dh), lambda b, h, i: (b, 2 * nh + h)),
            pl.BlockSpec((None, None, s // t, t), lambda b, h, i: (b, h, 0, 0)),
        ],
        out_specs=pl.BlockSpec((tq, dh), lambda b, h, i: (b * nq + i, h)),
        out_shape=jax.ShapeDtypeStruct((n, nh * dh), BF16),
        scratch_shapes=[
            pltpu.VMEM((tq, dh), BF16),
            pltpu.VMEM((s, dh + V7X_LANES), BF16),
            pltpu.VMEM((nsub, t, t), F32),
            pltpu.VMEM((nsub, t, t), BF16),
            pltpu.VMEM((nsub, t, V7X_LANES), F32),
            pltpu.VMEM((tq, V7X_LANES), F32),
            pltpu.VMEM((tq, dh + V7X_LANES), F32),
        ],
        compiler_params=_params("parallel", "parallel", "arbitrary"),
        name="fox_flash",
    )(proj, proj, proj, crow)


def kernel(x, norm1_g, norm2_g, a_w_in, a_vnorm_g, a_w_s, a_b_s, a_w_out, b_w_in, b_w_gate2, b_gate_bias, b_onorm_g, b_w_out, c_w_in, c_f_bias, c_w_out, ffn_w_in, ffn_w_out, moe_router, moe_w_in, moe_w_out, final_g):
    bsz, s, d = x.shape
    depth = norm1_g.shape[0]
    xf = x.reshape(bsz * s, d)
    for i in range(depth):
        g1 = norm1_g[i].reshape(1, d)
        m, j = i % N_MIXERS, i // N_MIXERS
        if m == 0:
            xf = gmlp_mixer(xf, g1, a_w_in[j].astype(BF16), a_vnorm_g[j].reshape(1, -1), a_w_s[j], a_b_s[j].T, a_w_out[j].astype(BF16))
        elif m == 1:
            dk = b_w_gate2.shape[2]
            n_main = b_w_in.shape[2] - B_GATE_RANK
            proj = norm_matmul(xf, g1, b_w_in[j, :, :n_main].astype(BF16))
            la = gla_gate(xf, g1, b_w_in[j, :, n_main:], b_w_gate2[j], b_gate_bias[j].reshape(1, dk))
            o = gla_chunks(proj, la, b_onorm_g[j].reshape(1, -1), bsz)
            xf = matmul_residual(o, b_w_out[j].astype(BF16), xf)
        else:
            proj = norm_matmul(xf, g1, c_w_in[j, :, : 3 * d].astype(BF16))
            crow = fox_gate(xf, g1, c_w_in[j, :, 3 * d :], c_f_bias[j], bsz)
            o = fox_flash(proj, crow, bsz, C_HEADS)
            xf = matmul_residual(o, c_w_out[j].astype(BF16), xf)
        g2 = norm2_g[i].reshape(1, d)
        fg = final_g.reshape(1, d) if i == depth - 1 else None
        if i % 2 == 0:
            xf = ffn_resident(xf, g2, ffn_w_in[i // 2].astype(BF16), ffn_w_out[i // 2].astype(BF16), final_g=fg)
        else:
            xf = moe_ffn(xf, g2, moe_router[i // 2], moe_w_in, moe_w_out, i // 2, final_g=fg)
    return xf.reshape(bsz, s, d)
```

```python
import functools

import jax
import jax.numpy as jnp
from jax import lax
from jax.experimental import pallas as pl
from jax.experimental.pallas import tpu as pltpu

F32 = jnp.float32
BF16 = jnp.bfloat16

EPS = 1e-6
LOG2_E = 1.4426950408889634
N_MIXERS = 3
TOP_K = 2
A_CHUNK = 128
A_GROUPS = 8
B_HEADS = 4
B_GATE_RANK = 16
B_TAU = 16.0
B_CHUNK = 64
C_HEADS = 8

V7X_LANES = 128
V7X_SUBLANES = 8
V7X_VMEM_LIMIT_BYTES = 56 * 1024 * 1024
V7X_VMEM_RESIDENT_LIMIT_BYTES = 62 * 1024 * 1024


def _params(*sem):
    return pltpu.CompilerParams(dimension_semantics=sem, vmem_limit_bytes=V7X_VMEM_LIMIT_BYTES)


def _block(total, target, align=V7X_LANES):
    if total <= target:
        return total
    return max(b for b in range(align, target + 1, align) if total % b == 0)


def _rms(x, g):
    ms = jnp.mean(x * x, axis=-1, keepdims=True)
    return x * lax.rsqrt(ms + EPS) * g


def _log_sigmoid(x):
    return jnp.minimum(x, 0.0) - jnp.log1p(jnp.exp(-jnp.abs(x)))


def _split3(x):
    hi = x.astype(BF16)
    r1 = x - hi.astype(F32)
    mid = r1.astype(BF16)
    lo = (r1 - mid.astype(F32)).astype(BF16)
    return hi, mid, lo


def _dot(a, b):
    return jnp.dot(a, b, preferred_element_type=F32)


def _dot_nt(a, b):
    return lax.dot_general(a, b, (((1,), (1,)), ((), ())), preferred_element_type=F32)


def _dot_tn(a, b):
    return lax.dot_general(a, b, (((0,), (0,)), ((), ())), preferred_element_type=F32)


def _tril_mask(n):
    row = lax.broadcasted_iota(jnp.int32, (n, n), 0)
    col = lax.broadcasted_iota(jnp.int32, (n, n), 1)
    return col <= row


def _norm_matmul_body(x_ref, g_ref, w_ref, o_ref, h_ref, *, tn):
    h_ref[...] = _rms(x_ref[...], g_ref[...]).astype(BF16)
    for c in range(w_ref.shape[1] // tn):
        cols = slice(c * tn, (c + 1) * tn)
        o_ref[:, cols] = _dot(h_ref[...], w_ref[:, cols]).astype(o_ref.dtype)


def norm_matmul(x, g, w, tm=1024, tn=1024):
    n, d = x.shape
    nout = w.shape[1]
    tm, tn = min(tm, n), _block(nout, tn)
    return pl.pallas_call(
        functools.partial(_norm_matmul_body, tn=tn),
        grid=(n // tm,),
        in_specs=[
            pl.BlockSpec((tm, d), lambda i: (i, 0)),
            pl.BlockSpec((1, d), lambda i: (0, 0)),
            pl.BlockSpec((d, nout), lambda i: (0, 0), pipeline_mode=pl.Buffered(1)),
        ],
        out_specs=pl.BlockSpec((tm, nout), lambda i: (i, 0)),
        out_shape=jax.ShapeDtypeStruct((n, nout), BF16),
        scratch_shapes=[pltpu.VMEM((tm, d), BF16)],
        compiler_params=_params("parallel"),
        name="norm_matmul",
    )(x, g, w)


def _matmul_residual_body(y_ref, w_ref, x_ref, o_ref):
    o_ref[...] = x_ref[...] + _dot(y_ref[...], w_ref[...])


def matmul_residual(y, w, x, tm=1024):
    n, k = y.shape
    d = w.shape[1]
    tm = min(tm, n)
    return pl.pallas_call(
        _matmul_residual_body,
        grid=(n // tm,),
        in_specs=[
            pl.BlockSpec((tm, k), lambda i: (i, 0)),
            pl.BlockSpec((k, d), lambda i: (0, 0)),
            pl.BlockSpec((tm, d), lambda i: (i, 0)),
        ],
        out_specs=pl.BlockSpec((tm, d), lambda i: (i, 0)),
        out_shape=jax.ShapeDtypeStruct((n, d), F32),
        compiler_params=_params("parallel"),
        name="matmul_residual",
    )(y, w, x)


def _gmlp_body(x_ref, g_ref, wi_ref, vg_ref, ws_ref, bt_ref, wo_ref, o_ref, h_ref, z_ref, vn_ref, y_ref, *, tm, width, tn):
    gd = width // A_GROUPS
    h_ref[...] = _rms(x_ref[...], g_ref[...]).astype(BF16)
    for c in range(2 * width // tn):
        cols = slice(c * tn, (c + 1) * tn)
        z_ref[:, cols] = jax.nn.gelu(_dot(h_ref[...], wi_ref[:, cols]), approximate=True).astype(BF16)
    vn_ref[...] = _rms(z_ref[:, width:].astype(F32), vg_ref[...]).astype(BF16)
    tril = _tril_mask(A_CHUNK)
    for g in range(A_GROUPS):
        w = jnp.where(tril, ws_ref[g], 0.0).astype(BF16)
        bias = bt_ref[:, g : g + 1]
        cs = slice(g * gd, (g + 1) * gd)
        for c in range(tm // A_CHUNK):
            rs = slice(c * A_CHUNK, (c + 1) * A_CHUNK)
            mixed = _dot(w, vn_ref[rs, cs]) + bias
            y_ref[rs, cs] = (z_ref[rs, cs].astype(F32) * mixed).astype(BF16)
    o_ref[...] = x_ref[...] + _dot(y_ref[...], wo_ref[...])


def gmlp_mixer(x, g, wi, vg, ws, bt, wo, tm=1024, tn=1024):
    n, d = x.shape
    width = wi.shape[1] // 2
    tm, tn = min(tm, n), _block(2 * width, tn)
    once = pl.Buffered(1)
    return pl.pallas_call(
        functools.partial(_gmlp_body, tm=tm, width=width, tn=tn),
        grid=(n // tm,),
        in_specs=[
            pl.BlockSpec((tm, d), lambda i: (i, 0)),
            pl.BlockSpec((1, d), lambda i: (0, 0)),
            pl.BlockSpec((d, 2 * width), lambda i: (0, 0), pipeline_mode=once),
            pl.BlockSpec((1, width), lambda i: (0, 0)),
            pl.BlockSpec((A_GROUPS, A_CHUNK, A_CHUNK), lambda i: (0, 0, 0)),
            pl.BlockSpec((A_CHUNK, A_GROUPS), lambda i: (0, 0)),
            pl.BlockSpec((width, d), lambda i: (0, 0), pipeline_mode=once),
        ],
        out_specs=pl.BlockSpec((tm, d), lambda i: (i, 0)),
        out_shape=jax.ShapeDtypeStruct((n, d), F32),
        scratch_shapes=[
            pltpu.VMEM((tm, d), BF16),
            pltpu.VMEM((tm, 2 * width), BF16),
            pltpu.VMEM((tm, width), BF16),
            pltpu.VMEM((tm, width), BF16),
        ],
        compiler_params=pltpu.CompilerParams(dimension_semantics=("parallel",), vmem_limit_bytes=V7X_VMEM_RESIDENT_LIMIT_BYTES),
        name="gmlp_mixer",
    )(x, g, wi, vg, ws, bt, wo)


def _ffn_resident_body(*refs, f, tf, final):
    x_ref, g_ref, wi_ref, wo_ref = refs[:4]
    rest = list(refs[4:])
    fg_ref = rest.pop(0) if final else None
    o_ref, h_ref, t_ref = rest
    h_ref[...] = _rms(x_ref[...], g_ref[...]).astype(BF16)
    for c in range(f // tf):
        a = _dot(h_ref[...], wi_ref[:, c * tf : (c + 1) * tf])
        u = _dot(h_ref[...], wi_ref[:, f + c * tf : f + (c + 1) * tf])
        t_ref[:, c * tf : (c + 1) * tf] = ((a * jax.nn.sigmoid(a)) * u).astype(BF16)
    y = x_ref[...] + _dot(t_ref[...], wo_ref[...])
    o_ref[...] = _rms(y, fg_ref[...]) if final else y


def ffn_resident(x, g, w_in, w_out, final_g=None, tm=1024, tf=512):
    n, d = x.shape
    f = w_out.shape[0]
    tm, tf = min(tm, n), min(tf, f)
    final = final_g is not None
    once = pl.Buffered(1)
    in_specs = [
        pl.BlockSpec((tm, d), lambda i: (i, 0)),
        pl.BlockSpec((1, d), lambda i: (0, 0)),
        pl.BlockSpec((d, 2 * f), lambda i: (0, 0), pipeline_mode=once),
        pl.BlockSpec((f, d), lambda i: (0, 0), pipeline_mode=once),
    ]
    args = [x, g, w_in, w_out]
    if final:
        in_specs.append(pl.BlockSpec((1, d), lambda i: (0, 0)))
        args.append(final_g)
    return pl.pallas_call(
        functools.partial(_ffn_resident_body, f=f, tf=tf, final=final),
        grid=(n // tm,),
        in_specs=in_specs,
        out_specs=pl.BlockSpec((tm, d), lambda i: (i, 0)),
        out_shape=jax.ShapeDtypeStruct((n, d), F32),
        scratch_shapes=[pltpu.VMEM((tm, d), BF16), pltpu.VMEM((tm, f), BF16)],
        compiler_params=pltpu.CompilerParams(dimension_semantics=("parallel",), vmem_limit_bytes=V7X_VMEM_RESIDENT_LIMIT_BYTES),
        name="ffn_resident",
    )(*args)


TOK_ROWS = 8


def _router_body(x_ref, g_ref, whi_ref, wlo_ref, gate_ref, tok_ref, cnt_ref, carry_ref, *, ne, tm):
    @pl.when(pl.program_id(0) == 0)
    def _():
        carry_ref[...] = jnp.zeros_like(carry_ref)

    h = _rms(x_ref[...], g_ref[...])
    h_hi = h.astype(BF16)
    h_lo = (h - h_hi.astype(F32)).astype(BF16)
    logits = _dot(h_hi, whi_ref[...]) + _dot(h_lo, whi_ref[...]) + _dot(h_hi, wlo_ref[...])
    lane = lax.broadcasted_iota(jnp.int32, logits.shape, 1).astype(F32)
    neg = -jnp.inf
    l1 = jnp.where(lane < ne, logits, neg)
    m1 = jnp.max(l1, axis=-1, keepdims=True)
    i1 = jnp.min(jnp.where(l1 == m1, lane, float(V7X_LANES)), axis=-1, keepdims=True)
    l2 = jnp.where(lane == i1, neg, l1)
    m2 = jnp.max(l2, axis=-1, keepdims=True)
    i2 = jnp.min(jnp.where(l2 == m2, lane, float(V7X_LANES)), axis=-1, keepdims=True)
    e2 = jnp.exp(m2 - m1)
    den = 1.0 + e2
    gate_ref[...] = jnp.where(lane == 0.0, 1.0 / den, 0.0) + jnp.where(lane == 1.0, e2 / den, 0.0)

    sel1, sel2 = lane == i1, lane == i2
    onehot = jnp.where(sel1 | sel2, 1.0, 0.0)
    row = lax.broadcasted_iota(jnp.int32, (tm, tm), 0)
    col = lax.broadcasted_iota(jnp.int32, (tm, tm), 1)
    before = carry_ref[...] + _dot((col < row).astype(BF16), onehot.astype(BF16))
    carry_ref[...] = before[tm - 1 : tm, :] + onehot[tm - 1 : tm, :]
    cnt_ref[...] = carry_ref[...]
    r1 = jnp.sum(jnp.where(sel1, before, 0.0), axis=-1, keepdims=True)
    r2 = jnp.sum(jnp.where(sel2, before, 0.0), axis=-1, keepdims=True)
    table = (jnp.where(lane == 0.0, i1, 0.0) + jnp.where(lane == 1.0, i2, 0.0)
             + jnp.where(lane == 2.0, r1, 0.0) + jnp.where(lane == 3.0, r2, 0.0))
    pick = (lax.broadcasted_iota(jnp.int32, (TOK_ROWS, V7X_LANES), 0) == lax.broadcasted_iota(jnp.int32, (TOK_ROWS, V7X_LANES), 1)).astype(BF16)
    hi, mid, lo = _split3(table)
    tok_ref[...] = (_dot_nt(pick, hi) + _dot_nt(pick, mid) + _dot_nt(pick, lo)).astype(jnp.int32)


def moe_router(x, g, w_router, tm=512):
    n, d = x.shape
    ne = w_router.shape[1]
    tm = min(tm, n)
    wpad = jnp.pad(w_router, ((0, 0), (0, V7X_LANES - ne)))
    w_hi = wpad.astype(BF16)
    w_lo = (wpad - w_hi.astype(F32)).astype(BF16)
    return pl.pallas_call(
        functools.partial(_router_body, ne=ne, tm=tm),
        grid=(n // tm,),
        in_specs=[
            pl.BlockSpec((tm, d), lambda i: (i, 0)),
            pl.BlockSpec((1, d), lambda i: (0, 0)),
            pl.BlockSpec((d, V7X_LANES), lambda i: (0, 0)),
            pl.BlockSpec((d, V7X_LANES), lambda i: (0, 0)),
        ],
        out_specs=[
            pl.BlockSpec((tm, V7X_LANES), lambda i: (i, 0)),
            pl.BlockSpec((TOK_ROWS, tm), lambda i: (0, i)),
            pl.BlockSpec((1, V7X_LANES), lambda i: (0, 0)),
        ],
        out_shape=[
            jax.ShapeDtypeStruct((n, V7X_LANES), F32),
            jax.ShapeDtypeStruct((TOK_ROWS, n), jnp.int32),
            jax.ShapeDtypeStruct((1, V7X_LANES), F32),
        ],
        scratch_shapes=[pltpu.VMEM((1, V7X_LANES), F32)],
        compiler_params=_params("arbitrary"),
        name="moe_router",
    )(x, g, w_hi, w_lo)


def _plan_body(cnt_ref, tok_ref, off_ref, te_ref, tv_ref, nv_ref, pos_ref, *, ne, tg, n_tiles):
    off = jnp.int32(0)
    tile = jnp.int32(0)
    expert = tok_ref[0:TOP_K, :]
    group_start = jnp.zeros_like(expert)
    for e in range(ne):
        nt = (cnt_ref[e] + (tg - 1)) // tg
        off_ref[e] = off
        group_start = jnp.where(expert == e, off, group_start)

        def mark(ti, carry, e=e, first=tile):
            te_ref[ti] = jnp.int32(e)
            tv_ref[ti] = jnp.minimum(cnt_ref[e] - (ti - first) * tg, tg)
            return carry

        lax.fori_loop(tile, tile + nt, mark, 0)
        off = off + nt * tg
        tile = tile + nt
    pos_ref[...] = jnp.zeros_like(pos_ref)
    pos_ref[0:TOP_K, :] = group_start + tok_ref[TOP_K : 2 * TOP_K, :]
    nv_ref[0] = tile
    last = te_ref[jnp.maximum(tile - 1, 0)]

    def fill(ti, carry):
        te_ref[ti] = last
        tv_ref[ti] = jnp.int32(0)
        return carry

    lax.fori_loop(tile, n_tiles, fill, 0)


def moe_plan(cnt, tok, tg, n_tiles):
    ne = cnt.shape[0]
    smem = pl.BlockSpec(memory_space=pltpu.SMEM)
    vmem = pl.BlockSpec(memory_space=pltpu.VMEM)
    return pl.pallas_call(
        functools.partial(_plan_body, ne=ne, tg=tg, n_tiles=n_tiles),
        in_specs=[smem, vmem],
        out_specs=[smem, smem, smem, smem, vmem],
        out_shape=[
            jax.ShapeDtypeStruct((ne,), jnp.int32),
            jax.ShapeDtypeStruct((n_tiles,), jnp.int32),
            jax.ShapeDtypeStruct((n_tiles,), jnp.int32),
            jax.ShapeDtypeStruct((1,), jnp.int32),
            jax.ShapeDtypeStruct(tok.shape, jnp.int32),
        ],
        name="moe_plan",
    )(cnt, tok)


def _token_copy(src, s, dst, r, sem, ns):
    return pltpu.make_async_copy(src.at[pl.ds(pl.multiple_of(s * ns, ns), ns)], dst.at[pl.ds(pl.multiple_of(r * ns, ns), ns)], sem)


def _to_slabs(dst_ref, val, ns):
    rows = val.shape[0]
    for c in range(ns):
        dst_ref[pl.ds(c, rows, stride=ns), :] = val[:, c * V7X_LANES : (c + 1) * V7X_LANES]


def _slab(src_ref, c, rows, ns):
    return src_ref[pl.ds(c, rows, stride=ns), :]


def _dispatch_body(pos_hbm, off_ref, cnt_ref, x_ref, g_ref, xs_hbm, pos_smem, x3_ref, zero_ref, sem_idx, sem_row, *, td, tg, ne, n_tiles, ns):
    base = pl.multiple_of(pl.program_id(0) * td, td)
    idx = pltpu.make_async_copy(pos_hbm.at[:, pl.ds(base, td)], pos_smem, sem_idx)
    idx.start()
    _to_slabs(x3_ref, _rms(x_ref[...], g_ref[...]), ns)

    @pl.when(pl.program_id(0) == 0)
    def _():
        zero_ref[...] = jnp.zeros_like(zero_ref)
        for e in range(ne):
            start = off_ref[e] + cnt_ref[e]
            stop = off_ref[e] + (cnt_ref[e] + (tg - 1)) // tg * tg

            def zero_start(r, carry):
                _token_copy(zero_ref, 0, xs_hbm, r, sem_row, ns).start()
                return carry

            def zero_wait(r, carry):
                _token_copy(zero_ref, 0, xs_hbm, r, sem_row, ns).wait()
                return carry

            lax.fori_loop(start, stop, zero_start, 0)
            lax.fori_loop(start, stop, zero_wait, 0)

        def tile_copy(ti):
            return pltpu.make_async_copy(zero_ref, xs_hbm.at[pl.ds(pl.multiple_of(ti * (tg * ns), tg * ns), tg * ns)], sem_row)

        def tile_start(ti, carry):
            tile_copy(ti).start()
            return carry

        def tile_wait(ti, carry):
            tile_copy(ti).wait()
            return carry

        used = stop // tg
        lax.fori_loop(used, n_tiles, tile_start, 0)
        lax.fori_loop(used, n_tiles, tile_wait, 0)

    idx.wait()

    def start(t, c):
        for k in range(TOP_K):
            _token_copy(x3_ref, t, xs_hbm, pos_smem[k, t], sem_row, ns).start(priority=k % 2)
        return c

    lax.fori_loop(0, td, start, 0, unroll=8)
    for _ in range(TOP_K):
        pltpu.make_async_copy(x3_ref, xs_hbm.at[pl.ds(0, td * ns)], sem_row).wait()


def moe_dispatch(pos, off, cnt, x, g, tg, n_tiles, td=1024):
    n, d = x.shape
    ns = d // V7X_LANES
    assert ns % V7X_SUBLANES == 0
    td = min(td, n)
    smem = pl.BlockSpec(memory_space=pltpu.SMEM)
    hbm = pl.BlockSpec(memory_space=pl.ANY)
    return pl.pallas_call(
        functools.partial(_dispatch_body, td=td, tg=tg, ne=cnt.shape[0], n_tiles=n_tiles, ns=ns),
        grid=(n // td,),
        in_specs=[hbm, smem, smem, pl.BlockSpec((td, d), lambda i: (i, 0)), pl.BlockSpec((1, d), lambda i: (0, 0))],
        out_specs=hbm,
        out_shape=jax.ShapeDtypeStruct((n_tiles * tg * ns, V7X_LANES), x.dtype),
        scratch_shapes=[
            pltpu.SMEM((TOK_ROWS, td), jnp.int32),
            pltpu.VMEM((td * ns, V7X_LANES), x.dtype),
            pltpu.VMEM((tg * ns, V7X_LANES), x.dtype),
            pltpu.SemaphoreType.DMA(()),
            pltpu.SemaphoreType.DMA(()),
        ],
        compiler_params=pltpu.CompilerParams(dimension_semantics=("arbitrary",), has_side_effects=True, vmem_limit_bytes=V7X_VMEM_LIMIT_BYTES),
        name="moe_dispatch",
    )(pos, off, cnt, x, g)


GROUP_SHORT_TILE_DIV = 4
GROUP_STAGE_SLOTS = 4
GROUP_STAGE_BYTES = 1024 * 1024


def _load_cast(src_hbm, dst_ref, stage_ref, sem, rows):
    n = dst_ref.shape[0] // rows
    slots = stage_ref.shape[0]
    ahead = slots - 1

    def chunk(c, slot):
        return pltpu.make_async_copy(src_hbm.at[pl.ds(pl.multiple_of(c * rows, rows), rows)], stage_ref.at[slot], sem.at[slot])

    for c in range(min(ahead, n)):
        chunk(c, c % slots).start()

    def body(c, carry):
        for slot in range(slots):

            @pl.when(c % slots == slot)
            def _():
                @pl.when(c + ahead < n)
                def _():
                    chunk(c + ahead, (slot + ahead) % slots).start()

                chunk(c, slot).wait()
                dst_ref[pl.ds(pl.multiple_of(c * rows, rows), rows), :] = stage_ref[slot].astype(BF16)

        return carry

    lax.fori_loop(0, n, body, 0)


def _grouped_ffn_body(te_ref, tv_ref, nv_ref, x_ref, wi_hbm, wo_hbm, o_ref, wi_ref, wo_ref, si_ref, so_ref, sem_i, sem_o, h_ref, t_ref, *, layer, tg, ns, f, tf):
    i = pl.program_id(0)
    e = te_ref[i]

    @pl.when((i < nv_ref[0]) & ((i == 0) | (e != te_ref[jnp.maximum(i - 1, 0)])))
    def _():
        _load_cast(wi_hbm.at[layer, e], wi_ref, si_ref, sem_i, si_ref.shape[1])
        _load_cast(wo_hbm.at[layer, e], wo_ref, so_ref, sem_o, so_ref.shape[1])

    @pl.when(i < nv_ref[0])
    def _():
        for c in range(ns):
            h_ref[:, c * V7X_LANES : (c + 1) * V7X_LANES] = _slab(x_ref, c, tg, ns).astype(BF16)

        def run(m):
            for c in range(f // tf):
                a = _dot(h_ref[0:m, :], wi_ref[:, c * tf : (c + 1) * tf])
                u = _dot(h_ref[0:m, :], wi_ref[:, f + c * tf : f + (c + 1) * tf])
                t_ref[0:m, c * tf : (c + 1) * tf] = ((a * jax.nn.sigmoid(a)) * u).astype(BF16)
            y = _dot(t_ref[0:m, :], wo_ref[...])
            for c in range(ns):
                o_ref[pl.ds(c, m, stride=ns), :] = y[:, c * V7X_LANES : (c + 1) * V7X_LANES]
            if m < tg:
                o_ref[m * ns :, :] = jnp.zeros(((tg - m) * ns, V7X_LANES), F32)

        short = tv_ref[i] <= tg // GROUP_SHORT_TILE_DIV
        pl.when(short)(lambda: run(tg // GROUP_SHORT_TILE_DIV))
        pl.when(jnp.logical_not(short))(lambda: run(tg))

    @pl.when(i >= nv_ref[0])
    def _():
        o_ref[...] = jnp.zeros_like(o_ref)


def moe_grouped_ffn(te, tv, nv, xs, w_in, w_out, layer, tg, tf=512):
    d = w_in.shape[2]
    ns = d // V7X_LANES
    r = xs.shape[0] // ns
    f = w_out.shape[2]
    tf = _block(f, tf)
    n_tiles = r // tg
    hbm = pl.BlockSpec(memory_space=pl.ANY)
    rows_in = _block(d, GROUP_STAGE_BYTES // (2 * f * 4), V7X_SUBLANES)
    rows_out = _block(f, GROUP_STAGE_BYTES // (d * 4), V7X_SUBLANES)

    def tile(i, nv):
        return jnp.minimum(i, nv[0] - 1)

    grid_spec = pltpu.PrefetchScalarGridSpec(
        num_scalar_prefetch=3,
        grid=(n_tiles,),
        in_specs=[pl.BlockSpec((tg * ns, V7X_LANES), lambda i, te, tv, nv: (tile(i, nv), 0)), hbm, hbm],
        out_specs=pl.BlockSpec((tg * ns, V7X_LANES), lambda i, te, tv, nv: (i, 0)),
        scratch_shapes=[
            pltpu.VMEM((d, 2 * f), BF16),
            pltpu.VMEM((f, d), BF16),
            pltpu.VMEM((GROUP_STAGE_SLOTS, rows_in, 2 * f), F32),
            pltpu.VMEM((GROUP_STAGE_SLOTS, rows_out, d), F32),
            pltpu.SemaphoreType.DMA((GROUP_STAGE_SLOTS,)),
            pltpu.SemaphoreType.DMA((GROUP_STAGE_SLOTS,)),
            pltpu.VMEM((tg, d), BF16),
            pltpu.VMEM((tg, f), BF16),
        ],
    )
    return pl.pallas_call(
        functools.partial(_grouped_ffn_body, layer=layer, tg=tg, ns=ns, f=f, tf=tf),
        grid_spec=grid_spec,
        out_shape=jax.ShapeDtypeStruct((r * ns, V7X_LANES), F32),
        compiler_params=pltpu.CompilerParams(dimension_semantics=("arbitrary",), vmem_limit_bytes=V7X_VMEM_RESIDENT_LIMIT_BYTES),
        name="moe_grouped_ffn",
    )(te, tv, nv, xs, w_in, w_out)


def _combine_body(pos_hbm, gate_ref, x_ref, ys_hbm, *rest, tc, ns, final):
    rest = list(rest)
    fg_ref = rest.pop(0) if final else None
    o_ref, pos_smem, buf_ref, sem_idx, sem_row = rest
    s, nsteps = pl.program_id(0), pl.num_programs(0)

    def idx_copy(step, slot):
        return pltpu.make_async_copy(pos_hbm.at[:, pl.ds(pl.multiple_of(step * tc, tc), tc)], pos_smem.at[slot], sem_idx.at[slot])

    def gather(slot):
        def start(t, c):
            for k in range(TOP_K):
                _token_copy(ys_hbm, pos_smem[slot, k, t], buf_ref.at[slot, k], t, sem_row.at[slot], ns).start(priority=k % 2)
            return c

        lax.fori_loop(0, tc, start, 0, unroll=8)

    def step(slot):
        other = 1 - slot

        @pl.when(s == 0)
        def _():
            idx_copy(s, slot).start()
            idx_copy(s, slot).wait()
            gather(slot)

            @pl.when(nsteps > 1)
            def _():
                idx_copy(s + 1, other).start()

        @pl.when(s + 1 < nsteps)
        def _():
            idx_copy(s + 1, other).wait()
            gather(other)

        @pl.when(s + 2 < nsteps)
        def _():
            idx_copy(s + 2, slot).start()

        for k in range(TOP_K):
            pltpu.make_async_copy(ys_hbm.at[pl.ds(0, tc * ns)], buf_ref.at[slot, k], sem_row.at[slot]).wait()
        g0, g1 = gate_ref[:, 0:1], gate_ref[:, 1:2]
        for c in range(ns):
            cols = slice(c * V7X_LANES, (c + 1) * V7X_LANES)
            o_ref[:, cols] = x_ref[:, cols] + g0 * _slab(buf_ref.at[slot, 0], c, tc, ns) + g1 * _slab(buf_ref.at[slot, 1], c, tc, ns)
        if final:
            o_ref[...] = _rms(o_ref[...], fg_ref[...])

    for slot in range(2):
        pl.when(s % 2 == slot)(functools.partial(step, slot))


def moe_combine(pos, gates, x, ys, final_g=None, tc=512):
    n, d = x.shape
    ns = d // V7X_LANES
    tc = min(tc, n)
    final = final_g is not None
    hbm = pl.BlockSpec(memory_space=pl.ANY)
    in_specs = [hbm, pl.BlockSpec((tc, V7X_LANES), lambda i: (i, 0)), pl.BlockSpec((tc, d), lambda i: (i, 0)), hbm]
    args = [pos, gates, x, ys]
    if final:
        in_specs.append(pl.BlockSpec((1, d), lambda i: (0, 0)))
        args.append(final_g)
    return pl.pallas_call(
        functools.partial(_combine_body, tc=tc, ns=ns, final=final),
        grid=(n // tc,),
        in_specs=in_specs,
        out_specs=pl.BlockSpec((tc, d), lambda i: (i, 0)),
        out_shape=jax.ShapeDtypeStruct((n, d), F32),
        scratch_shapes=[
            pltpu.SMEM((2, TOK_ROWS, tc), jnp.int32),
            pltpu.VMEM((2, TOP_K, tc * ns, V7X_LANES), F32),
            pltpu.SemaphoreType.DMA((2,)),
            pltpu.SemaphoreType.DMA((2,)),
        ],
        compiler_params=_params("arbitrary"),
        name="moe_combine",
    )(*args)


def moe_ffn(x, g, w_router, w_in, w_out, layer, final_g=None, tg=1024):
    n, _ = x.shape
    ne = w_router.shape[1]
    tg = min(tg, n)
    n_tiles = TOP_K * n // tg + ne
    gates, tok, counts = moe_router(x, g, w_router)
    cnt = counts[0, :ne].astype(jnp.int32)
    off, te, tv, nv, pos = moe_plan(cnt, tok, tg, n_tiles)
    xs = moe_dispatch(pos, off, cnt, x, g, tg, n_tiles)
    ys = moe_grouped_ffn(te, tv, nv, xs, w_in, w_out, layer, tg)
    return moe_combine(pos, gates, x, ys, final_g)


def _gla_gate_body(x_ref, g_ref, wl_ref, w2_ref, b_ref, o_ref):
    h = _rms(x_ref[...], g_ref[...]).astype(BF16)
    g_low = _dot(h, wl_ref[...]).astype(BF16)
    o_ref[...] = _log_sigmoid(_dot(g_low, w2_ref[...]) + b_ref[...]) / B_TAU


def gla_gate(x, g, w_low, w_gate2, bias, tm=1024):
    n, d = x.shape
    dk = w_gate2.shape[1]
    tm = min(tm, n)
    wl = jnp.pad(w_low, ((0, 0), (0, V7X_LANES - B_GATE_RANK))).astype(BF16)
    w2 = jnp.pad(w_gate2, ((0, V7X_LANES - B_GATE_RANK), (0, 0))).astype(BF16)
    return pl.pallas_call(
        _gla_gate_body,
        grid=(n // tm,),
        in_specs=[
            pl.BlockSpec((tm, d), lambda i: (i, 0)),
            pl.BlockSpec((1, d), lambda i: (0, 0)),
            pl.BlockSpec((d, V7X_LANES), lambda i: (0, 0)),
            pl.BlockSpec((V7X_LANES, dk), lambda i: (0, 0)),
            pl.BlockSpec((1, dk), lambda i: (0, 0)),
        ],
        out_specs=pl.BlockSpec((tm, dk), lambda i: (i, 0)),
        out_shape=jax.ShapeDtypeStruct((n, dk), F32),
        compiler_params=_params("parallel"),
        name="gla_gate",
    )(x, g, wl, w2, bias)


def _gla_body(q_ref, k_ref, v_ref, r_ref, la_ref, og_ref, o_ref, st_ref, *, tc, hk, hv):
    @pl.when(pl.program_id(1) == 0)
    def _():
        st_ref[...] = jnp.zeros_like(st_ref)

    c = B_CHUNK
    tril = _tril_mask(c)
    ones_tril = tril.astype(BF16)
    scale = hk**-0.5
    for ci in range(tc // c):
        rs = slice(ci * c, (ci + 1) * c)
        hi, mid, lo = _split3(la_ref[rs, :])
        bcum = _dot(ones_tril, hi) + _dot(ones_tril, mid) + _dot(ones_tril, lo)
        b_last = bcum[c - 1 : c, :]
        e_pos = jnp.exp(bcum)
        e_neg = jnp.exp(-bcum)
        e_end = jnp.exp(b_last - bcum)
        dec = jnp.exp(b_last)
        for h in range(B_HEADS):
            ks = slice(h * hk, (h + 1) * hk)
            vs = slice(h * hv, (h + 1) * hv)
            q = q_ref[rs, ks].astype(F32) * scale
            k = k_ref[rs, ks].astype(F32)
            v = v_ref[rs, vs]
            q_d = (q * e_pos[:, ks]).astype(BF16)
            k_d = (k * e_neg[:, ks]).astype(BF16)
            k_end = (k * e_end[:, ks]).astype(BF16)
            att = jnp.where(tril, _dot_nt(q_d, k_d), 0.0)
            st = st_ref[h]
            o = _dot(att.astype(BF16), v) + _dot_nt(q_d, st.astype(BF16))
            st_ref[h] = dec[:, ks] * st + _dot_tn(v, k_end)
            y = _rms(o, og_ref[...]).astype(BF16).astype(F32)
            r = r_ref[rs, vs].astype(F32)
            o_ref[rs, vs] = (y * (r * jax.nn.sigmoid(r))).astype(BF16)


def gla_chunks(proj, la, o_g, bsz, tc=256):
    n = proj.shape[0]
    dk = la.shape[1]
    dv = (proj.shape[1] - 2 * dk) // 2
    s = n // bsz
    tc = min(tc, s)
    nt = s // tc
    hk, hv = dk // B_HEADS, dv // B_HEADS
    assert (2 * dk) % dv == 0
    v_blk = 2 * dk // dv
    row = lambda b, t: b * nt + t
    return pl.pallas_call(
        functools.partial(_gla_body, tc=tc, hk=hk, hv=hv),
        grid=(bsz, nt),
        in_specs=[
            pl.BlockSpec((tc, dk), lambda b, t: (row(b, t), 0)),
            pl.BlockSpec((tc, dk), lambda b, t: (row(b, t), 1)),
            pl.BlockSpec((tc, dv), lambda b, t: (row(b, t), v_blk)),
            pl.BlockSpec((tc, dv), lambda b, t: (row(b, t), v_blk + 1)),
            pl.BlockSpec((tc, dk), lambda b, t: (row(b, t), 0)),
            pl.BlockSpec((1, hv), lambda b, t: (0, 0)),
        ],
        out_specs=pl.BlockSpec((tc, dv), lambda b, t: (row(b, t), 0)),
        out_shape=jax.ShapeDtypeStruct((n, dv), BF16),
        scratch_shapes=[pltpu.VMEM((B_HEADS, hv, hk), F32)],
        compiler_params=_params("parallel", "arbitrary"),
        name="gla_chunks",
    )(proj, proj, proj, proj, la, o_g)


def _fox_gate_body(x_ref, g_ref, wf_ref, b_ref, crow_ref, carry_ref, *, tb, nh):
    @pl.when(pl.program_id(1) == 0)
    def _():
        carry_ref[...] = jnp.zeros_like(carry_ref)

    h = _rms(x_ref[...], g_ref[...]).astype(BF16)
    lane = lax.broadcasted_iota(jnp.int32, (tb, V7X_LANES), 1)
    log_f = jnp.where(lane < nh, _log_sigmoid(_dot(h, wf_ref[...]) + b_ref[...]), 0.0)
    ones_tril = _tril_mask(tb).astype(BF16)
    hi, mid, lo = _split3(log_f)
    c = carry_ref[...] + (_dot(ones_tril, hi) + _dot(ones_tril, mid) + _dot(ones_tril, lo))
    carry_ref[...] = c[tb - 1 : tb, :]
    sel = (lax.broadcasted_iota(jnp.int32, (nh, V7X_LANES), 0) == lax.broadcasted_iota(jnp.int32, (nh, V7X_LANES), 1)).astype(BF16)
    hi, mid, lo = _split3(c)
    crow_ref[...] = _dot_nt(sel, hi) + _dot_nt(sel, mid) + _dot_nt(sel, lo)


def fox_gate(x, g, w_f, f_bias, bsz, tb=512):
    n, d = x.shape
    nh = w_f.shape[1]
    s = n // bsz
    tb = min(tb, s)
    nt = s // tb
    wf = jnp.pad(w_f, ((0, 0), (0, V7X_LANES - nh))).astype(BF16)
    bias = jnp.pad(f_bias.reshape(1, nh), ((0, 0), (0, V7X_LANES - nh)))
    return pl.pallas_call(
        functools.partial(_fox_gate_body, tb=tb, nh=nh),
        grid=(bsz, nt),
        in_specs=[
            pl.BlockSpec((tb, d), lambda b, t: (b * nt + t, 0)),
            pl.BlockSpec((1, d), lambda b, t: (0, 0)),
            pl.BlockSpec((d, V7X_LANES), lambda b, t: (0, 0)),
            pl.BlockSpec((1, V7X_LANES), lambda b, t: (0, 0)),
        ],
        out_specs=pl.BlockSpec((None, nh, tb), lambda b, t: (b, 0, t)),
        out_shape=jax.ShapeDtypeStruct((bsz, nh, s), F32),
        scratch_shapes=[pltpu.VMEM((1, V7X_LANES), F32)],
        compiler_params=_params("parallel", "arbitrary"),
        name="fox_gate",
    )(x, g, wf, bias)


FOX_KEY_BLOCK = 512
FOX_SUBTILES = 8
FOX_ROW_CHUNK = 32


def _fox_flash_body(q_ref, k_ref, v_ref, crow_ref, o_ref, qs_ref, va_ref, s_ref, p_ref, al_ref, m_ref, acc_ref, *, t, nsub, scale):
    i = pl.program_id(2)
    dh = q_ref.shape[1]
    nl = t // V7X_LANES

    @pl.when(i == 0)
    def _():
        va_ref[:, :dh] = v_ref[...]
        va_ref[:, dh:] = jnp.ones((va_ref.shape[0], V7X_LANES), BF16)

    qs_ref[...] = (q_ref[...].astype(F32) * (scale * LOG2_E)).astype(BF16)
    m_ref[...] = jnp.full_like(m_ref, -jnp.inf)
    acc_ref[...] = jnp.zeros_like(acc_ref)

    def block(sub, kb, masked):
        rows = slice(sub * t, (sub + 1) * t)
        off = pl.multiple_of(kb * t, t)
        s_ref[sub] = _dot_nt(qs_ref[rows, :], k_ref[pl.ds(off, t), :])
        bias = crow_ref[pl.ds(kb, 1), :] * LOG2_E

        def chunk(c, carry):
            r = pl.multiple_of(c * FOX_ROW_CHUNK, FOX_ROW_CHUNK)
            sc = s_ref[sub, pl.ds(r, FOX_ROW_CHUNK), :] - bias
            if masked:
                row = r + lax.broadcasted_iota(jnp.int32, sc.shape, 0)
                col = lax.broadcasted_iota(jnp.int32, sc.shape, 1)
                sc = jnp.where(col <= row, sc, -jnp.inf)
            tiles = [sc[:, a * V7X_LANES : (a + 1) * V7X_LANES] for a in range(nl)]
            mx = functools.reduce(jnp.maximum, tiles)
            m_old = m_ref[pl.ds(sub * t + r, FOX_ROW_CHUNK), :]
            m_new = jnp.maximum(m_old, jnp.max(mx, axis=-1, keepdims=True))
            al_ref[sub, pl.ds(r, FOX_ROW_CHUNK), :] = jnp.exp2(m_old - m_new)
            m_ref[pl.ds(sub * t + r, FOX_ROW_CHUNK), :] = m_new
            for a in range(nl):
                p_ref[sub, pl.ds(r, FOX_ROW_CHUNK), a * V7X_LANES : (a + 1) * V7X_LANES] = jnp.exp2(tiles[a] - m_new).astype(BF16)
            return carry

        lax.fori_loop(0, t // FOX_ROW_CHUNK, chunk, 0, unroll=True)
        pv = _dot(p_ref[sub], va_ref[pl.ds(off, t), :])
        alpha = al_ref[sub]
        acc_ref[rows, :dh] = alpha * acc_ref[rows, :dh] + pv[:, :dh]
        acc_ref[rows, dh:] = alpha * acc_ref[rows, dh:] + pv[:, dh:]

    def below_diagonal(kb, carry):
        for sub in range(nsub):
            block(sub, kb, False)
        return carry

    first = i * nsub
    lax.fori_loop(0, first, below_diagonal, 0)
    for kb in range(nsub):
        for sub in range(kb, nsub):
            block(sub, first + kb, masked=(sub == kb))
    o_ref[...] = (acc_ref[:, :dh] / acc_ref[:, dh:]).astype(o_ref.dtype)


def fox_flash(proj, crow, bsz, nh):
    n = proj.shape[0]
    dh = proj.shape[1] // (3 * nh)
    assert dh == V7X_LANES
    s = n // bsz
    t = min(FOX_KEY_BLOCK, s)
    nsub = min(FOX_SUBTILES, s // t)
    tq = t * nsub
    nq = s // tq
    crow = crow.reshape(bsz, nh, s // t, t)
    return pl.pallas_call(
        functools.partial(_fox_flash_body, t=t, nsub=nsub, scale=dh**-0.5),
        grid=(bsz, nh, nq),
        in_specs=[
            pl.BlockSpec((tq, dh), lambda b, h, i: (b * nq + i, h)),
            pl.BlockSpec((s, dh), lambda b, h, i: (b, nh + h)),
            pl.BlockSpec((s, dh), lambda b, h, i: (b, 2 * nh + h)),
            pl.BlockSpec((None, None, s // t, t), lambda b, h, i: (b, h, 0, 0)),
        ],
        out_specs=pl.BlockSpec((tq, dh), lambda b, h, i: (b * nq + i, h)),
        out_shape=jax.ShapeDtypeStruct((n, nh * dh), BF16),
        scratch_shapes=[
            pltpu.VMEM((tq, dh), BF16),
            pltpu.VMEM((s, dh + V7X_LANES), BF16),
            pltpu.VMEM((nsub, t, t), F32),
            pltpu.VMEM((nsub, t, t), BF16),
            pltpu.VMEM((nsub, t, V7X_LANES), F32),
            pltpu.VMEM((tq, V7X_LANES), F32),
            pltpu.VMEM((tq, dh + V7X_LANES), F32),
        ],
        compiler_params=_params("parallel", "parallel", "arbitrary"),
        name="fox_flash",
    )(proj, proj, proj, crow)


def kernel(x, norm1_g, norm2_g, a_w_in, a_vnorm_g, a_w_s, a_b_s, a_w_out, b_w_in, b_w_gate2, b_gate_bias, b_onorm_g, b_w_out, c_w_in, c_f_bias, c_w_out, ffn_w_in, ffn_w_out, moe_router, moe_w_in, moe_w_out, final_g):
    bsz, s, d = x.shape
    depth = norm1_g.shape[0]
    xf = x.reshape(bsz * s, d)
    for i in range(depth):
        g1 = norm1_g[i].reshape(1, d)
        m, j = i % N_MIXERS, i // N_MIXERS
        if m == 0:
            xf = gmlp_mixer(xf, g1, a_w_in[j].astype(BF16), a_vnorm_g[j].reshape(1, -1), a_w_s[j], a_b_s[j].T, a_w_out[j].astype(BF16))
        elif m == 1:
            dk = b_w_gate2.shape[2]
            n_main = b_w_in.shape[2] - B_GATE_RANK
            proj = norm_matmul(xf, g1, b_w_in[j, :, :n_main].astype(BF16))
            la = gla_gate(xf, g1, b_w_in[j, :, n_main:], b_w_gate2[j], b_gate_bias[j].reshape(1, dk))
            o = gla_chunks(proj, la, b_onorm_g[j].reshape(1, -1), bsz)
            xf = matmul_residual(o, b_w_out[j].astype(BF16), xf)
        else:
            proj = norm_matmul(xf, g1, c_w_in[j, :, : 3 * d].astype(BF16))
            crow = fox_gate(xf, g1, c_w_in[j, :, 3 * d :], c_f_bias[j], bsz)
            o = fox_flash(proj, crow, bsz, C_HEADS)
            xf = matmul_residual(o, c_w_out[j].astype(BF16), xf)
        g2 = norm2_g[i].reshape(1, d)
        fg = final_g.reshape(1, d) if i == depth - 1 else None
        if i % 2 == 0:
            xf = ffn_resident(xf, g2, ffn_w_in[i // 2].astype(BF16), ffn_w_out[i // 2].astype(BF16), final_g=fg)
        else:
            xf = moe_ffn(xf, g2, moe_router[i // 2], moe_w_in, moe_w_out, i // 2, final_g=fg)
    return xf.reshape(bsz, s, d)
```

```python
import functools

import jax
import jax.numpy as jnp
from jax import lax
from jax.experimental import pallas as pl
from jax.experimental.pallas import tpu as pltpu

F32 = jnp.float32
BF16 = jnp.bfloat16

EPS = 1e-6
LOG2_E = 1.4426950408889634
N_MIXERS = 3
TOP_K = 2
A_CHUNK = 128
A_GROUPS = 8
B_HEADS = 4
B_GATE_RANK = 16
B_TAU = 16.0
B_CHUNK = 64
C_HEADS = 8

V7X_LANES = 128
V7X_SUBLANES = 8
V7X_VMEM_LIMIT_BYTES = 56 * 1024 * 1024
V7X_VMEM_RESIDENT_LIMIT_BYTES = 62 * 1024 * 1024


def _params(*sem):
    return pltpu.CompilerParams(dimension_semantics=sem, vmem_limit_bytes=V7X_VMEM_LIMIT_BYTES)


def _block(total, target, align=V7X_LANES):
    if total <= target:
        return total
    return max(b for b in range(align, target + 1, align) if total % b == 0)


def _rms(x, g):
    ms = jnp.mean(x * x, axis=-1, keepdims=True)
    return x * lax.rsqrt(ms + EPS) * g


def _log_sigmoid(x):
    return jnp.minimum(x, 0.0) - jnp.log1p(jnp.exp(-jnp.abs(x)))


def _split3(x):
    hi = x.astype(BF16)
    r1 = x - hi.astype(F32)
    mid = r1.astype(BF16)
    lo = (r1 - mid.astype(F32)).astype(BF16)
    return hi, mid, lo


def _dot(a, b):
    return jnp.dot(a, b, preferred_element_type=F32)


def _dot_nt(a, b):
    return lax.dot_general(a, b, (((1,), (1,)), ((), ())), preferred_element_type=F32)


def _dot_tn(a, b):
    return lax.dot_general(a, b, (((0,), (0,)), ((), ())), preferred_element_type=F32)


def _tril_mask(n):
    row = lax.broadcasted_iota(jnp.int32, (n, n), 0)
    col = lax.broadcasted_iota(jnp.int32, (n, n), 1)
    return col <= row


def _norm_matmul_body(x_ref, g_ref, w_ref, o_ref, h_ref, *, tn):
    h_ref[...] = _rms(x_ref[...], g_ref[...]).astype(BF16)
    for c in range(w_ref.shape[1] // tn):
        cols = slice(c * tn, (c + 1) * tn)
        o_ref[:, cols] = _dot(h_ref[...], w_ref[:, cols]).astype(o_ref.dtype)


def norm_matmul(x, g, w, tm=1024, tn=1024):
    n, d = x.shape
    nout = w.shape[1]
    tm, tn = min(tm, n), _block(nout, tn)
    return pl.pallas_call(
        functools.partial(_norm_matmul_body, tn=tn),
        grid=(n // tm,),
        in_specs=[
            pl.BlockSpec((tm, d), lambda i: (i, 0)),
            pl.BlockSpec((1, d), lambda i: (0, 0)),
            pl.BlockSpec((d, nout), lambda i: (0, 0), pipeline_mode=pl.Buffered(1)),
        ],
        out_specs=pl.BlockSpec((tm, nout), lambda i: (i, 0)),
        out_shape=jax.ShapeDtypeStruct((n, nout), BF16),
        scratch_shapes=[pltpu.VMEM((tm, d), BF16)],
        compiler_params=_params("parallel"),
        name="norm_matmul",
    )(x, g, w)


def _matmul_residual_body(y_ref, w_ref, x_ref, o_ref):
    o_ref[...] = x_ref[...] + _dot(y_ref[...], w_ref[...])


def matmul_residual(y, w, x, tm=1024):
    n, k = y.shape
    d = w.shape[1]
    tm = min(tm, n)
    return pl.pallas_call(
        _matmul_residual_body,
        grid=(n // tm,),
        in_specs=[
            pl.BlockSpec((tm, k), lambda i: (i, 0)),
            pl.BlockSpec((k, d), lambda i: (0, 0)),
            pl.BlockSpec((tm, d), lambda i: (i, 0)),
        ],
        out_specs=pl.BlockSpec((tm, d), lambda i: (i, 0)),
        out_shape=jax.ShapeDtypeStruct((n, d), F32),
        compiler_params=_params("parallel"),
        name="matmul_residual",
    )(y, w, x)


def _gmlp_body(x_ref, g_ref, wi_ref, vg_ref, ws_ref, bt_ref, wo_ref, o_ref, h_ref, z_ref, vn_ref, y_ref, *, tm, width, tn):
    gd = width // A_GROUPS
    h_ref[...] = _rms(x_ref[...], g_ref[...]).astype(BF16)
    for c in range(2 * width // tn):
        cols = slice(c * tn, (c + 1) * tn)
        z_ref[:, cols] = jax.nn.gelu(_dot(h_ref[...], wi_ref[:, cols]), approximate=True).astype(BF16)
    vn_ref[...] = _rms(z_ref[:, width:].astype(F32), vg_ref[...]).astype(BF16)
    tril = _tril_mask(A_CHUNK)
    for g in range(A_GROUPS):
        w = jnp.where(tril, ws_ref[g], 0.0).astype(BF16)
        bias = bt_ref[:, g : g + 1]
        cs = slice(g * gd, (g + 1) * gd)
        for c in range(tm // A_CHUNK):
            rs = slice(c * A_CHUNK, (c + 1) * A_CHUNK)
            mixed = _dot(w, vn_ref[rs, cs]) + bias
            y_ref[rs, cs] = (z_ref[rs, cs].astype(F32) * mixed).astype(BF16)
    o_ref[...] = x_ref[...] + _dot(y_ref[...], wo_ref[...])


def gmlp_mixer(x, g, wi, vg, ws, bt, wo, tm=1024, tn=1024):
    n, d = x.shape
    width = wi.shape[1] // 2
    tm, tn = min(tm, n), _block(2 * width, tn)
    once = pl.Buffered(1)
    return pl.pallas_call(
        functools.partial(_gmlp_body, tm=tm, width=width, tn=tn),
        grid=(n // tm,),
        in_specs=[
            pl.BlockSpec((tm, d), lambda i: (i, 0)),
            pl.BlockSpec((1, d), lambda i: (0, 0)),
            pl.BlockSpec((d, 2 * width), lambda i: (0, 0), pipeline_mode=once),
            pl.BlockSpec((1, width), lambda i: (0, 0)),
            pl.BlockSpec((A_GROUPS, A_CHUNK, A_CHUNK), lambda i: (0, 0, 0)),
            pl.BlockSpec((A_CHUNK, A_GROUPS), lambda i: (0, 0)),
            pl.BlockSpec((width, d), lambda i: (0, 0), pipeline_mode=once),
        ],
        out_specs=pl.BlockSpec((tm, d), lambda i: (i, 0)),
        out_shape=jax.ShapeDtypeStruct((n, d), F32),
        scratch_shapes=[
            pltpu.VMEM((tm, d), BF16),
            pltpu.VMEM((tm, 2 * width), BF16),
            pltpu.VMEM((tm, width), BF16),
            pltpu.VMEM((tm, width), BF16),
        ],
        compiler_params=pltpu.CompilerParams(dimension_semantics=("parallel",), vmem_limit_bytes=V7X_VMEM_RESIDENT_LIMIT_BYTES),
        name="gmlp_mixer",
    )(x, g, wi, vg, ws, bt, wo)


def _ffn_resident_body(*refs, f, tf, final):
    x_ref, g_ref, wi_ref, wo_ref = refs[:4]
    rest = list(refs[4:])
    fg_ref = rest.pop(0) if final else None
    o_ref, h_ref, t_ref = rest
    h_ref[...] = _rms(x_ref[...], g_ref[...]).astype(BF16)
    for c in range(f // tf):
        a = _dot(h_ref[...], wi_ref[:, c * tf : (c + 1) * tf])
        u = _dot(h_ref[...], wi_ref[:, f + c * tf : f + (c + 1) * tf])
        t_ref[:, c * tf : (c + 1) * tf] = ((a * jax.nn.sigmoid(a)) * u).astype(BF16)
    y = x_ref[...] + _dot(t_ref[...], wo_ref[...])
    o_ref[...] = _rms(y, fg_ref[...]) if final else y


def ffn_resident(x, g, w_in, w_out, final_g=None, tm=1024, tf=512):
    n, d = x.shape
    f = w_out.shape[0]
    tm, tf = min(tm, n), min(tf, f)
    final = final_g is not None
    once = pl.Buffered(1)
    in_specs = [
        pl.BlockSpec((tm, d), lambda i: (i, 0)),
        pl.BlockSpec((1, d), lambda i: (0, 0)),
        pl.BlockSpec((d, 2 * f), lambda i: (0, 0), pipeline_mode=once),
        pl.BlockSpec((f, d), lambda i: (0, 0), pipeline_mode=once),
    ]
    args = [x, g, w_in, w_out]
    if final:
        in_specs.append(pl.BlockSpec((1, d), lambda i: (0, 0)))
        args.append(final_g)
    return pl.pallas_call(
        functools.partial(_ffn_resident_body, f=f, tf=tf, final=final),
        grid=(n // tm,),
        in_specs=in_specs,
        out_specs=pl.BlockSpec((tm, d), lambda i: (i, 0)),
        out_shape=jax.ShapeDtypeStruct((n, d), F32),
        scratch_shapes=[pltpu.VMEM((tm, d), BF16), pltpu.VMEM((tm, f), BF16)],
        compiler_params=pltpu.CompilerParams(dimension_semantics=("parallel",), vmem_limit_bytes=V7X_VMEM_RESIDENT_LIMIT_BYTES),
        name="ffn_resident",
    )(*args)


TOK_ROWS = 8


def _router_body(x_ref, g_ref, whi_ref, wlo_ref, gate_ref, tok_ref, cnt_ref, carry_ref, *, ne, tm):
    @pl.when(pl.program_id(0) == 0)
    def _():
        carry_ref[...] = jnp.zeros_like(carry_ref)

    h = _rms(x_ref[...], g_ref[...])
    h_hi = h.astype(BF16)
    h_lo = (h - h_hi.astype(F32)).astype(BF16)
    logits = _dot(h_hi, whi_ref[...]) + _dot(h_lo, whi_ref[...]) + _dot(h_hi, wlo_ref[...])
    lane = lax.broadcasted_iota(jnp.int32, logits.shape, 1).astype(F32)
    neg = -jnp.inf
    l1 = jnp.where(lane < ne, logits, neg)
    m1 = jnp.max(l1, axis=-1, keepdims=True)
    i1 = jnp.min(jnp.where(l1 == m1, lane, float(V7X_LANES)), axis=-1, keepdims=True)
    l2 = jnp.where(lane == i1, neg, l1)
    m2 = jnp.max(l2, axis=-1, keepdims=True)
    i2 = jnp.min(jnp.where(l2 == m2, lane, float(V7X_LANES)), axis=-1, keepdims=True)
    e2 = jnp.exp(m2 - m1)
    den = 1.0 + e2
    gate_ref[...] = jnp.where(lane == 0.0, 1.0 / den, 0.0) + jnp.where(lane == 1.0, e2 / den, 0.0)

    sel1, sel2 = lane == i1, lane == i2
    onehot = jnp.where(sel1 | sel2, 1.0, 0.0)
    row = lax.broadcasted_iota(jnp.int32, (tm, tm), 0)
    col = lax.broadcasted_iota(jnp.int32, (tm, tm), 1)
    before = carry_ref[...] + _dot((col < row).astype(BF16), onehot.astype(BF16))
    carry_ref[...] = before[tm - 1 : tm, :] + onehot[tm - 1 : tm, :]
    cnt_ref[...] = carry_ref[...]
    r1 = jnp.sum(jnp.where(sel1, before, 0.0), axis=-1, keepdims=True)
    r2 = jnp.sum(jnp.where(sel2, before, 0.0), axis=-1, keepdims=True)
    table = (jnp.where(lane == 0.0, i1, 0.0) + jnp.where(lane == 1.0, i2, 0.0)
             + jnp.where(lane == 2.0, r1, 0.0) + jnp.where(lane == 3.0, r2, 0.0))
    pick = (lax.broadcasted_iota(jnp.int32, (TOK_ROWS, V7X_LANES), 0) == lax.broadcasted_iota(jnp.int32, (TOK_ROWS, V7X_LANES), 1)).astype(BF16)
    hi, mid, lo = _split3(table)
    tok_ref[...] = (_dot_nt(pick, hi) + _dot_nt(pick, mid) + _dot_nt(pick, lo)).astype(jnp.int32)


def moe_router(x, g, w_router, tm=512):
    n, d = x.shape
    ne = w_router.shape[1]
    tm = min(tm, n)
    wpad = jnp.pad(w_router, ((0, 0), (0, V7X_LANES - ne)))
    w_hi = wpad.astype(BF16)
    w_lo = (wpad - w_hi.astype(F32)).astype(BF16)
    return pl.pallas_call(
        functools.partial(_router_body, ne=ne, tm=tm),
        grid=(n // tm,),
        in_specs=[
            pl.BlockSpec((tm, d), lambda i: (i, 0)),
            pl.BlockSpec((1, d), lambda i: (0, 0)),
            pl.BlockSpec((d, V7X_LANES), lambda i: (0, 0)),
            pl.BlockSpec((d, V7X_LANES), lambda i: (0, 0)),
        ],
        out_specs=[
            pl.BlockSpec((tm, V7X_LANES), lambda i: (i, 0)),
            pl.BlockSpec((TOK_ROWS, tm), lambda i: (0, i)),
            pl.BlockSpec((1, V7X_LANES), lambda i: (0, 0)),
        ],
        out_shape=[
            jax.ShapeDtypeStruct((n, V7X_LANES), F32),
            jax.ShapeDtypeStruct((TOK_ROWS, n), jnp.int32),
            jax.ShapeDtypeStruct((1, V7X_LANES), F32),
        ],
        scratch_shapes=[pltpu.VMEM((1, V7X_LANES), F32)],
        compiler_params=_params("arbitrary"),
        name="moe_router",
    )(x, g, w_hi, w_lo)


def _plan_body(cnt_ref, tok_ref, off_ref, te_ref, tv_ref, nv_ref, pos_ref, *, ne, tg, n_tiles):
    off = jnp.int32(0)
    tile = jnp.int32(0)
    expert = tok_ref[0:TOP_K, :]
    group_start = jnp.zeros_like(expert)
    for e in range(ne):
        nt = (cnt_ref[e] + (tg - 1)) // tg
        off_ref[e] = off
        group_start = jnp.where(expert == e, off, group_start)

        def mark(ti, carry, e=e, first=tile):
            te_ref[ti] = jnp.int32(e)
            tv_ref[ti] = jnp.minimum(cnt_ref[e] - (ti - first) * tg, tg)
            return carry

        lax.fori_loop(tile, tile + nt, mark, 0)
        off = off + nt * tg
        tile = tile + nt
    pos_ref[...] = jnp.zeros_like(pos_ref)
    pos_ref[0:TOP_K, :] = group_start + tok_ref[TOP_K : 2 * TOP_K, :]
    nv_ref[0] = tile
    last = te_ref[jnp.maximum(tile - 1, 0)]

    def fill(ti, carry):
        te_ref[ti] = last
        tv_ref[ti] = jnp.int32(0)
        return carry

    lax.fori_loop(tile, n_tiles, fill, 0)


def moe_plan(cnt, tok, tg, n_tiles):
    ne = cnt.shape[0]
    smem = pl.BlockSpec(memory_space=pltpu.SMEM)
    vmem = pl.BlockSpec(memory_space=pltpu.VMEM)
    return pl.pallas_call(
        functools.partial(_plan_body, ne=ne, tg=tg, n_tiles=n_tiles),
        in_specs=[smem, vmem],
        out_specs=[smem, smem, smem, smem, vmem],
        out_shape=[
            jax.ShapeDtypeStruct((ne,), jnp.int32),
            jax.ShapeDtypeStruct((n_tiles,), jnp.int32),
            jax.ShapeDtypeStruct((n_tiles,), jnp.int32),
            jax.ShapeDtypeStruct((1,), jnp.int32),
            jax.ShapeDtypeStruct(tok.shape, jnp.int32),
        ],
        name="moe_plan",
    )(cnt, tok)


def _token_copy(src, s, dst, r, sem, ns):
    return pltpu.make_async_copy(src.at[pl.ds(pl.multiple_of(s * ns, ns), ns)], dst.at[pl.ds(pl.multiple_of(r * ns, ns), ns)], sem)


def _to_slabs(dst_ref, val, ns):
    rows = val.shape[0]
    for c in range(ns):
        dst_ref[pl.ds(c, rows, stride=ns), :] = val[:, c * V7X_LANES : (c + 1) * V7X_LANES]


def _slab(src_ref, c, rows, ns):
    return src_ref[pl.ds(c, rows, stride=ns), :]


def _dispatch_body(pos_hbm, off_ref, cnt_ref, x_ref, g_ref, xs_hbm, pos_smem, x3_ref, zero_ref, sem_idx, sem_row, sem_zero, *, td, tg, ne, n_tiles, ns):
    s, nsteps = pl.program_id(0), pl.num_programs(0)

    def idx_copy(step, slot):
        return pltpu.make_async_copy(pos_hbm.at[:, pl.ds(pl.multiple_of(step * td, td), td)], pos_smem.at[slot], sem_idx.at[slot])

    def wait_scatter(slot):
        for _ in range(TOP_K):
            pltpu.make_async_copy(x3_ref.at[slot], xs_hbm.at[pl.ds(0, td * ns)], sem_row.at[slot]).wait()

    @pl.when(s == 0)
    def _():
        zero_ref[...] = jnp.zeros_like(zero_ref)
        for e in range(ne):
            start = off_ref[e] + cnt_ref[e]
            stop = off_ref[e] + (cnt_ref[e] + (tg - 1)) // tg * tg

            def zero_start(r, carry):
                _token_copy(zero_ref, 0, xs_hbm, r, sem_zero, ns).start()
                return carry

            def zero_wait(r, carry):
                _token_copy(zero_ref, 0, xs_hbm, r, sem_zero, ns).wait()
                return carry

            lax.fori_loop(start, stop, zero_start, 0)
            lax.fori_loop(start, stop, zero_wait, 0)

        def tile_copy(ti):
            return pltpu.make_async_copy(zero_ref, xs_hbm.at[pl.ds(pl.multiple_of(ti * (tg * ns), tg * ns), tg * ns)], sem_zero)

        def tile_start(ti, carry):
            tile_copy(ti).start()
            return carry

        def tile_wait(ti, carry):
            tile_copy(ti).wait()
            return carry

        used = stop // tg
        lax.fori_loop(used, n_tiles, tile_start, 0)
        lax.fori_loop(used, n_tiles, tile_wait, 0)

    def step(slot):
        other = 1 - slot

        @pl.when(s == 0)
        def _():
            idx_copy(s, slot).start()

        @pl.when(s >= 2)
        def _():
            wait_scatter(slot)

        _to_slabs(x3_ref.at[slot], _rms(x_ref[...], g_ref[...]), ns)

        @pl.when(s + 1 < nsteps)
        def _():
            idx_copy(s + 1, other).start()

        idx_copy(s, slot).wait()

        def start(t, c):
            for k in range(TOP_K):
                _token_copy(x3_ref.at[slot], t, xs_hbm, pos_smem[slot, k, t], sem_row.at[slot], ns).start(priority=k % 2)
            return c

        lax.fori_loop(0, td, start, 0, unroll=8)

        @pl.when(s == nsteps - 1)
        def _():
            @pl.when(nsteps > 1)
            def _():
                wait_scatter(other)

            wait_scatter(slot)

    for slot in range(2):
        pl.when(s % 2 == slot)(functools.partial(step, slot))


def moe_dispatch(pos, off, cnt, x, g, tg, n_tiles, td=1024):
    n, d = x.shape
    ns = d // V7X_LANES
    assert ns % V7X_SUBLANES == 0
    td = min(td, n)
    smem = pl.BlockSpec(memory_space=pltpu.SMEM)
    hbm = pl.BlockSpec(memory_space=pl.ANY)
    return pl.pallas_call(
        functools.partial(_dispatch_body, td=td, tg=tg, ne=cnt.shape[0], n_tiles=n_tiles, ns=ns),
        grid=(n // td,),
        in_specs=[hbm, smem, smem, pl.BlockSpec((td, d), lambda i: (i, 0)), pl.BlockSpec((1, d), lambda i: (0, 0))],
        out_specs=hbm,
        out_shape=jax.ShapeDtypeStruct((n_tiles * tg * ns, V7X_LANES), x.dtype),
        scratch_shapes=[
            pltpu.SMEM((2, TOK_ROWS, td), jnp.int32),
            pltpu.VMEM((2, td * ns, V7X_LANES), x.dtype),
            pltpu.VMEM((tg * ns, V7X_LANES), x.dtype),
            pltpu.SemaphoreType.DMA((2,)),
            pltpu.SemaphoreType.DMA((2,)),
            pltpu.SemaphoreType.DMA(()),
        ],
        compiler_params=pltpu.CompilerParams(dimension_semantics=("arbitrary",), has_side_effects=True, vmem_limit_bytes=V7X_VMEM_LIMIT_BYTES),
        name="moe_dispatch",
    )(pos, off, cnt, x, g)


GROUP_SHORT_TILE_DIV = 4
GROUP_STAGE_SLOTS = 4
GROUP_STAGE_BYTES = 1024 * 1024


def _load_cast(src_hbm, dst_ref, stage_ref, sem, rows):
    n = dst_ref.shape[0] // rows
    slots = stage_ref.shape[0]
    ahead = slots - 1

    def chunk(c, slot):
        return pltpu.make_async_copy(src_hbm.at[pl.ds(pl.multiple_of(c * rows, rows), rows)], stage_ref.at[slot], sem.at[slot])

    for c in range(min(ahead, n)):
        chunk(c, c % slots).start()

    def body(c, carry):
        for slot in range(slots):

            @pl.when(c % slots == slot)
            def _():
                @pl.when(c + ahead < n)
                def _():
                    chunk(c + ahead, (slot + ahead) % slots).start()

                chunk(c, slot).wait()
                dst_ref[pl.ds(pl.multiple_of(c * rows, rows), rows), :] = stage_ref[slot].astype(BF16)

        return carry

    lax.fori_loop(0, n, body, 0)


def _grouped_ffn_body(te_ref, tv_ref, nv_ref, x_ref, wi_hbm, wo_hbm, o_ref, wi_ref, wo_ref, si_ref, so_ref, sem_i, sem_o, h_ref, t_ref, *, layer, tg, ns, f, tf):
    i = pl.program_id(0)
    e = te_ref[i]

    @pl.when((i < nv_ref[0]) & ((i == 0) | (e != te_ref[jnp.maximum(i - 1, 0)])))
    def _():
        _load_cast(wi_hbm.at[layer, e], wi_ref, si_ref, sem_i, si_ref.shape[1])
        _load_cast(wo_hbm.at[layer, e], wo_ref, so_ref, sem_o, so_ref.shape[1])

    @pl.when(i < nv_ref[0])
    def _():
        for c in range(ns):
            h_ref[:, c * V7X_LANES : (c + 1) * V7X_LANES] = _slab(x_ref, c, tg, ns).astype(BF16)

        def run(m):
            for c in range(f // tf):
                a = _dot(h_ref[0:m, :], wi_ref[:, c * tf : (c + 1) * tf])
                u = _dot(h_ref[0:m, :], wi_ref[:, f + c * tf : f + (c + 1) * tf])
                t_ref[0:m, c * tf : (c + 1) * tf] = ((a * jax.nn.sigmoid(a)) * u).astype(BF16)
            y = _dot(t_ref[0:m, :], wo_ref[...])
            for c in range(ns):
                o_ref[pl.ds(c, m, stride=ns), :] = y[:, c * V7X_LANES : (c + 1) * V7X_LANES]
            if m < tg:
                o_ref[m * ns :, :] = jnp.zeros(((tg - m) * ns, V7X_LANES), F32)

        short = tv_ref[i] <= tg // GROUP_SHORT_TILE_DIV
        pl.when(short)(lambda: run(tg // GROUP_SHORT_TILE_DIV))
        pl.when(jnp.logical_not(short))(lambda: run(tg))

    @pl.when(i >= nv_ref[0])
    def _():
        o_ref[...] = jnp.zeros_like(o_ref)


def moe_grouped_ffn(te, tv, nv, xs, w_in, w_out, layer, tg, tf=512):
    d = w_in.shape[2]
    ns = d // V7X_LANES
    r = xs.shape[0] // ns
    f = w_out.shape[2]
    tf = _block(f, tf)
    n_tiles = r // tg
    hbm = pl.BlockSpec(memory_space=pl.ANY)
    rows_in = _block(d, GROUP_STAGE_BYTES // (2 * f * 4), V7X_SUBLANES)
    rows_out = _block(f, GROUP_STAGE_BYTES // (d * 4), V7X_SUBLANES)

    def tile(i, nv):
        return jnp.minimum(i, nv[0] - 1)

    grid_spec = pltpu.PrefetchScalarGridSpec(
        num_scalar_prefetch=3,
        grid=(n_tiles,),
        in_specs=[pl.BlockSpec((tg * ns, V7X_LANES), lambda i, te, tv, nv: (tile(i, nv), 0)), hbm, hbm],
        out_specs=pl.BlockSpec((tg * ns, V7X_LANES), lambda i, te, tv, nv: (i, 0)),
        scratch_shapes=[
            pltpu.VMEM((d, 2 * f), BF16),
            pltpu.VMEM((f, d), BF16),
            pltpu.VMEM((GROUP_STAGE_SLOTS, rows_in, 2 * f), F32),
            pltpu.VMEM((GROUP_STAGE_SLOTS, rows_out, d), F32),
            pltpu.SemaphoreType.DMA((GROUP_STAGE_SLOTS,)),
            pltpu.SemaphoreType.DMA((GROUP_STAGE_SLOTS,)),
            pltpu.VMEM((tg, d), BF16),
            pltpu.VMEM((tg, f), BF16),
        ],
    )
    return pl.pallas_call(
        functools.partial(_grouped_ffn_body, layer=layer, tg=tg, ns=ns, f=f, tf=tf),
        grid_spec=grid_spec,
        out_shape=jax.ShapeDtypeStruct((r * ns, V7X_LANES), F32),
        compiler_params=pltpu.CompilerParams(dimension_semantics=("arbitrary",), vmem_limit_bytes=V7X_VMEM_RESIDENT_LIMIT_BYTES),
        name="moe_grouped_ffn",
    )(te, tv, nv, xs, w_in, w_out)


def _combine_body(pos_hbm, gate_ref, x_ref, ys_hbm, *rest, tc, ns, final):
    rest = list(rest)
    fg_ref = rest.pop(0) if final else None
    o_ref, pos_smem, buf_ref, sem_idx, sem_row = rest
    s, nsteps = pl.program_id(0), pl.num_programs(0)

    def idx_copy(step, slot):
        return pltpu.make_async_copy(pos_hbm.at[:, pl.ds(pl.multiple_of(step * tc, tc), tc)], pos_smem.at[slot], sem_idx.at[slot])

    def gather(slot):
        def start(t, c):
            for k in range(TOP_K):
                _token_copy(ys_hbm, pos_smem[slot, k, t], buf_ref.at[slot, k], t, sem_row.at[slot], ns).start(priority=k % 2)
            return c

        lax.fori_loop(0, tc, start, 0, unroll=8)

    def step(slot):
        other = 1 - slot

        @pl.when(s == 0)
        def _():
            idx_copy(s, slot).start()
            idx_copy(s, slot).wait()
            gather(slot)

            @pl.when(nsteps > 1)
            def _():
                idx_copy(s + 1, other).start()

        @pl.when(s + 1 < nsteps)
        def _():
            idx_copy(s + 1, other).wait()
            gather(other)

        @pl.when(s + 2 < nsteps)
        def _():
            idx_copy(s + 2, slot).start()

        for k in range(TOP_K):
            pltpu.make_async_copy(ys_hbm.at[pl.ds(0, tc * ns)], buf_ref.at[slot, k], sem_row.at[slot]).wait()
        g0, g1 = gate_ref[:, 0:1], gate_ref[:, 1:2]
        for c in range(ns):
            cols = slice(c * V7X_LANES, (c + 1) * V7X_LANES)
            o_ref[:, cols] = x_ref[:, cols] + g0 * _slab(buf_ref.at[slot, 0], c, tc, ns) + g1 * _slab(buf_ref.at[slot, 1], c, tc, ns)
        if final:
            o_ref[...] = _rms(o_ref[...], fg_ref[...])

    for slot in range(2):
        pl.when(s % 2 == slot)(functools.partial(step, slot))


def moe_combine(pos, gates, x, ys, final_g=None, tc=512):
    n, d = x.shape
    ns = d // V7X_LANES
    tc = min(tc, n)
    final = final_g is not None
    hbm = pl.BlockSpec(memory_space=pl.ANY)
    in_specs = [hbm, pl.BlockSpec((tc, V7X_LANES), lambda i: (i, 0)), pl.BlockSpec((tc, d), lambda i: (i, 0)), hbm]
    args = [pos, gates, x, ys]
    if final:
        in_specs.append(pl.BlockSpec((1, d), lambda i: (0, 0)))
        args.append(final_g)
    return pl.pallas_call(
        functools.partial(_combine_body, tc=tc, ns=ns, final=final),
        grid=(n // tc,),
        in_specs=in_specs,
        out_specs=pl.BlockSpec((tc, d), lambda i: (i, 0)),
        out_shape=jax.ShapeDtypeStruct((n, d), F32),
        scratch_shapes=[
            pltpu.SMEM((2, TOK_ROWS, tc), jnp.int32),
            pltpu.VMEM((2, TOP_K, tc * ns, V7X_LANES), F32),
            pltpu.SemaphoreType.DMA((2,)),
            pltpu.SemaphoreType.DMA((2,)),
        ],
        compiler_params=_params("arbitrary"),
        name="moe_combine",
    )(*args)


def moe_ffn(x, g, w_router, w_in, w_out, layer, final_g=None, tg=1024):
    n, _ = x.shape
    ne = w_router.shape[1]
    tg = min(tg, n)
    n_tiles = TOP_K * n // tg + ne
    gates, tok, counts = moe_router(x, g, w_router)
    cnt = counts[0, :ne].astype(jnp.int32)
    off, te, tv, nv, pos = moe_plan(cnt, tok, tg, n_tiles)
    xs = moe_dispatch(pos, off, cnt, x, g, tg, n_tiles)
    ys = moe_grouped_ffn(te, tv, nv, xs, w_in, w_out, layer, tg)
    return moe_combine(pos, gates, x, ys, final_g)


def _gla_gate_body(x_ref, g_ref, wl_ref, w2_ref, b_ref, o_ref):
    h = _rms(x_ref[...], g_ref[...]).astype(BF16)
    g_low = _dot(h, wl_ref[...]).astype(BF16)
    o_ref[...] = _log_sigmoid(_dot(g_low, w2_ref[...]) + b_ref[...]) / B_TAU


def gla_gate(x, g, w_low, w_gate2, bias, tm=1024):
    n, d = x.shape
    dk = w_gate2.shape[1]
    tm = min(tm, n)
    wl = jnp.pad(w_low, ((0, 0), (0, V7X_LANES - B_GATE_RANK))).astype(BF16)
    w2 = jnp.pad(w_gate2, ((0, V7X_LANES - B_GATE_RANK), (0, 0))).astype(BF16)
    return pl.pallas_call(
        _gla_gate_body,
        grid=(n // tm,),
        in_specs=[
            pl.BlockSpec((tm, d), lambda i: (i, 0)),
            pl.BlockSpec((1, d), lambda i: (0, 0)),
            pl.BlockSpec((d, V7X_LANES), lambda i: (0, 0)),
            pl.BlockSpec((V7X_LANES, dk), lambda i: (0, 0)),
            pl.BlockSpec((1, dk), lambda i: (0, 0)),
        ],
        out_specs=pl.BlockSpec((tm, dk), lambda i: (i, 0)),
        out_shape=jax.ShapeDtypeStruct((n, dk), F32),
        compiler_params=_params("parallel"),
        name="gla_gate",
    )(x, g, wl, w2, bias)


def _gla_body(q_ref, k_ref, v_ref, r_ref, la_ref, og_ref, o_ref, st_ref, *, tc, hk, hv):
    @pl.when(pl.program_id(1) == 0)
    def _():
        st_ref[...] = jnp.zeros_like(st_ref)

    c = B_CHUNK
    tril = _tril_mask(c)
    ones_tril = tril.astype(BF16)
    scale = hk**-0.5
    for ci in range(tc // c):
        rs = slice(ci * c, (ci + 1) * c)
        hi, mid, lo = _split3(la_ref[rs, :])
        bcum = _dot(ones_tril, hi) + _dot(ones_tril, mid) + _dot(ones_tril, lo)
        b_last = bcum[c - 1 : c, :]
        e_pos = jnp.exp(bcum)
        e_neg = jnp.exp(-bcum)
        e_end = jnp.exp(b_last - bcum)
        dec = jnp.exp(b_last)
        for h in range(B_HEADS):
            ks = slice(h * hk, (h + 1) * hk)
            vs = slice(h * hv, (h + 1) * hv)
            q = q_ref[rs, ks].astype(F32) * scale
            k = k_ref[rs, ks].astype(F32)
            v = v_ref[rs, vs]
            q_d = (q * e_pos[:, ks]).astype(BF16)
            k_d = (k * e_neg[:, ks]).astype(BF16)
            k_end = (k * e_end[:, ks]).astype(BF16)
            att = jnp.where(tril, _dot_nt(q_d, k_d), 0.0)
            st = st_ref[h]
            o = _dot(att.astype(BF16), v) + _dot_nt(q_d, st.astype(BF16))
            st_ref[h] = dec[:, ks] * st + _dot_tn(v, k_end)
            y = _rms(o, og_ref[...]).astype(BF16).astype(F32)
            r = r_ref[rs, vs].astype(F32)
            o_ref[rs, vs] = (y * (r * jax.nn.sigmoid(r))).astype(BF16)


def gla_chunks(proj, la, o_g, bsz, tc=256):
    n = proj.shape[0]
    dk = la.shape[1]
    dv = (proj.shape[1] - 2 * dk) // 2
    s = n // bsz
    tc = min(tc, s)
    nt = s // tc
    hk, hv = dk // B_HEADS, dv // B_HEADS
    assert (2 * dk) % dv == 0
    v_blk = 2 * dk // dv
    row = lambda b, t: b * nt + t
    return pl.pallas_call(
        functools.partial(_gla_body, tc=tc, hk=hk, hv=hv),
        grid=(bsz, nt),
        in_specs=[
            pl.BlockSpec((tc, dk), lambda b, t: (row(b, t), 0)),
            pl.BlockSpec((tc, dk), lambda b, t: (row(b, t), 1)),
            pl.BlockSpec((tc, dv), lambda b, t: (row(b, t), v_blk)),
            pl.BlockSpec((tc, dv), lambda b, t: (row(b, t), v_blk + 1)),
            pl.BlockSpec((tc, dk), lambda b, t: (row(b, t), 0)),
            pl.BlockSpec((1, hv), lambda b, t: (0, 0)),
        ],
        out_specs=pl.BlockSpec((tc, dv), lambda b, t: (row(b, t), 0)),
        out_shape=jax.ShapeDtypeStruct((n, dv), BF16),
        scratch_shapes=[pltpu.VMEM((B_HEADS, hv, hk), F32)],
        compiler_params=_params("parallel", "arbitrary"),
        name="gla_chunks",
    )(proj, proj, proj, proj, la, o_g)


def _fox_gate_body(x_ref, g_ref, wf_ref, b_ref, crow_ref, carry_ref, *, tb, nh):
    @pl.when(pl.program_id(1) == 0)
    def _():
        carry_ref[...] = jnp.zeros_like(carry_ref)

    h = _rms(x_ref[...], g_ref[...]).astype(BF16)
    lane = lax.broadcasted_iota(jnp.int32, (tb, V7X_LANES), 1)
    log_f = jnp.where(lane < nh, _log_sigmoid(_dot(h, wf_ref[...]) + b_ref[...]), 0.0)
    ones_tril = _tril_mask(tb).astype(BF16)
    hi, mid, lo = _split3(log_f)
    c = carry_ref[...] + (_dot(ones_tril, hi) + _dot(ones_tril, mid) + _dot(ones_tril, lo))
    carry_ref[...] = c[tb - 1 : tb, :]
    sel = (lax.broadcasted_iota(jnp.int32, (nh, V7X_LANES), 0) == lax.broadcasted_iota(jnp.int32, (nh, V7X_LANES), 1)).astype(BF16)
    hi, mid, lo = _split3(c)
    crow_ref[...] = _dot_nt(sel, hi) + _dot_nt(sel, mid) + _dot_nt(sel, lo)


def fox_gate(x, g, w_f, f_bias, bsz, tb=512):
    n, d = x.shape
    nh = w_f.shape[1]
    s = n // bsz
    tb = min(tb, s)
    nt = s // tb
    wf = jnp.pad(w_f, ((0, 0), (0, V7X_LANES - nh))).astype(BF16)
    bias = jnp.pad(f_bias.reshape(1, nh), ((0, 0), (0, V7X_LANES - nh)))
    return pl.pallas_call(
        functools.partial(_fox_gate_body, tb=tb, nh=nh),
        grid=(bsz, nt),
        in_specs=[
            pl.BlockSpec((tb, d), lambda b, t: (b * nt + t, 0)),
            pl.BlockSpec((1, d), lambda b, t: (0, 0)),
            pl.BlockSpec((d, V7X_LANES), lambda b, t: (0, 0)),
            pl.BlockSpec((1, V7X_LANES), lambda b, t: (0, 0)),
        ],
        out_specs=pl.BlockSpec((None, nh, tb), lambda b, t: (b, 0, t)),
        out_shape=jax.ShapeDtypeStruct((bsz, nh, s), F32),
        scratch_shapes=[pltpu.VMEM((1, V7X_LANES), F32)],
        compiler_params=_params("parallel", "arbitrary"),
        name="fox_gate",
    )(x, g, wf, bias)


FOX_KEY_BLOCK = 512
FOX_SUBTILES = 8
FOX_ROW_CHUNK = 32


def _fox_flash_body(q_ref, k_ref, v_ref, crow_ref, o_ref, qs_ref, va_ref, s_ref, p_ref, al_ref, m_ref, acc_ref, *, t, nsub, scale):
    i = pl.program_id(2)
    dh = q_ref.shape[1]
    nl = t // V7X_LANES

    @pl.when(i == 0)
    def _():
        va_ref[:, :dh] = v_ref[...]
        va_ref[:, dh:] = jnp.ones((va_ref.shape[0], V7X_LANES), BF16)

    qs_ref[...] = (q_ref[...].astype(F32) * (scale * LOG2_E)).astype(BF16)
    m_ref[...] = jnp.full_like(m_ref, -jnp.inf)
    acc_ref[...] = jnp.zeros_like(acc_ref)

    def block(sub, kb, masked):
        rows = slice(sub * t, (sub + 1) * t)
        off = pl.multiple_of(kb * t, t)
        s_ref[sub] = _dot_nt(qs_ref[rows, :], k_ref[pl.ds(off, t), :])
        bias = crow_ref[pl.ds(kb, 1), :] * LOG2_E

        def chunk(c, carry):
            r = pl.multiple_of(c * FOX_ROW_CHUNK, FOX_ROW_CHUNK)
            sc = s_ref[sub, pl.ds(r, FOX_ROW_CHUNK), :] - bias
            if masked:
                row = r + lax.broadcasted_iota(jnp.int32, sc.shape, 0)
                col = lax.broadcasted_iota(jnp.int32, sc.shape, 1)
                sc = jnp.where(col <= row, sc, -jnp.inf)
            tiles = [sc[:, a * V7X_LANES : (a + 1) * V7X_LANES] for a in range(nl)]
            mx = functools.reduce(jnp.maximum, tiles)
            m_old = m_ref[pl.ds(sub * t + r, FOX_ROW_CHUNK), :]
            m_new = jnp.maximum(m_old, jnp.max(mx, axis=-1, keepdims=True))
            al_ref[sub, pl.ds(r, FOX_ROW_CHUNK), :] = jnp.exp2(m_old - m_new)
            m_ref[pl.ds(sub * t + r, FOX_ROW_CHUNK), :] = m_new
            for a in range(nl):
                p_ref[sub, pl.ds(r, FOX_ROW_CHUNK), a * V7X_LANES : (a + 1) * V7X_LANES] = jnp.exp2(tiles[a] - m_new).astype(BF16)
            return carry

        lax.fori_loop(0, t // FOX_ROW_CHUNK, chunk, 0, unroll=True)
        pv = _dot(p_ref[sub], va_ref[pl.ds(off, t), :])
        alpha = al_ref[sub]
        acc_ref[rows, :dh] = alpha * acc_ref[rows, :dh] + pv[:, :dh]
        acc_ref[rows, dh:] = alpha * acc_ref[rows, dh:] + pv[:, dh:]

    def below_diagonal(kb, carry):
        for sub in range(nsub):
            block(sub, kb, False)
        return carry

    first = i * nsub
    lax.fori_loop(0, first, below_diagonal, 0)
    for kb in range(nsub):
        for sub in range(kb, nsub):
            block(sub, first + kb, masked=(sub == kb))
    o_ref[...] = (acc_ref[:, :dh] / acc_ref[:, dh:]).astype(o_ref.dtype)


def fox_flash(proj, crow, bsz, nh):
    n = proj.shape[0]
    dh = proj.shape[1] // (3 * nh)
    assert dh == V7X_LANES
    s = n // bsz
    t = min(FOX_KEY_BLOCK, s)
    nsub = min(FOX_SUBTILES, s // t)
    tq = t * nsub
    nq = s // tq
    crow = crow.reshape(bsz, nh, s // t, t)
    return pl.pallas_call(
        functools.partial(_fox_flash_body, t=t, nsub=nsub, scale=dh**-0.5),
        grid=(bsz, nh, nq),
        in_specs=[
            pl.BlockSpec((tq, dh), lambda b, h, i: (b * nq + i, h)),
            pl.BlockSpec((s, dh), lambda b, h, i: (b, nh + h)),
            pl.BlockSpec((s, dh), lambda b, h, i: (b, 2 * nh + h)),
            pl.BlockSpec((None, None, s // t, t), lambda b, h, i: (b, h, 0, 0)),
        ],
        out_specs=pl.BlockSpec((tq, dh), lambda b, h, i: (b * nq + i, h)),
        out_shape=jax.ShapeDtypeStruct((n, nh * dh), BF16),
        scratch_shapes=[
            pltpu.VMEM((tq, dh), BF16),
            pltpu.VMEM((s, dh + V7X_LANES), BF16),
            pltpu.VMEM((nsub, t, t), F32),
            pltpu.VMEM((nsub, t, t), BF16),
            pltpu.VMEM((nsub, t, V7X_LANES), F32),
            pltpu.VMEM((tq, V7X_LANES), F32),
            pltpu.VMEM((tq, dh + V7X_LANES), F32),
        ],
        compiler_params=_params("parallel", "parallel", "arbitrary"),
        name="fox_flash",
    )(proj, proj, proj, crow)


def kernel(x, norm1_g, norm2_g, a_w_in, a_vnorm_g, a_w_s, a_b_s, a_w_out, b_w_in, b_w_gate2, b_gate_bias, b_onorm_g, b_w_out, c_w_in, c_f_bias, c_w_out, ffn_w_in, ffn_w_out, moe_router, moe_w_in, moe_w_out, final_g):
    bsz, s, d = x.shape
    depth = norm1_g.shape[0]
    xf = x.reshape(bsz * s, d)
    for i in range(depth):
        g1 = norm1_g[i].reshape(1, d)
        m, j = i % N_MIXERS, i // N_MIXERS
        if m == 0:
            xf = gmlp_mixer(xf, g1, a_w_in[j].astype(BF16), a_vnorm_g[j].reshape(1, -1), a_w_s[j], a_b_s[j].T, a_w_out[j].astype(BF16))
        elif m == 1:
            dk = b_w_gate2.shape[2]
            n_main = b_w_in.shape[2] - B_GATE_RANK
            proj = norm_matmul(xf, g1, b_w_in[j, :, :n_main].astype(BF16))
            la = gla_gate(xf, g1, b_w_in[j, :, n_main:], b_w_gate2[j], b_gate_bias[j].reshape(1, dk))
            o = gla_chunks(proj, la, b_onorm_g[j].reshape(1, -1), bsz)
            xf = matmul_residual(o, b_w_out[j].astype(BF16), xf)
        else:
            proj = norm_matmul(xf, g1, c_w_in[j, :, : 3 * d].astype(BF16))
            crow = fox_gate(xf, g1, c_w_in[j, :, 3 * d :], c_f_bias[j], bsz)
            o = fox_flash(proj, crow, bsz, C_HEADS)
            xf = matmul_residual(o, c_w_out[j].astype(BF16), xf)
        g2 = norm2_g[i].reshape(1, d)
        fg = final_g.reshape(1, d) if i == depth - 1 else None
        if i % 2 == 0:
            xf = ffn_resident(xf, g2, ffn_w_in[i // 2].astype(BF16), ffn_w_out[i // 2].astype(BF16), final_g=fg)
        else:
            xf = moe_ffn(xf, g2, moe_router[i // 2], moe_w_in, moe_w_out, i // 2, final_g=fg)
    return xf.reshape(bsz, s, d)
```

```python
import functools

import jax
import jax.numpy as jnp
from jax import lax
from jax.experimental import pallas as pl
from jax.experimental.pallas import tpu as pltpu

F32 = jnp.float32
BF16 = jnp.bfloat16

EPS = 1e-6
LOG2_E = 1.4426950408889634
N_MIXERS = 3
TOP_K = 2
A_CHUNK = 128
A_GROUPS = 8
B_HEADS = 4
B_GATE_RANK = 16
B_TAU = 16.0
B_CHUNK = 64
C_HEADS = 8

V7X_LANES = 128
V7X_SUBLANES = 8
V7X_VMEM_LIMIT_BYTES = 56 * 1024 * 1024
V7X_VMEM_RESIDENT_LIMIT_BYTES = 62 * 1024 * 1024


def _params(*sem):
    return pltpu.CompilerParams(dimension_semantics=sem, vmem_limit_bytes=V7X_VMEM_LIMIT_BYTES)


def _block(total, target, align=V7X_LANES):
    if total <= target:
        return total
    return max(b for b in range(align, target + 1, align) if total % b == 0)


def _rms(x, g):
    ms = jnp.mean(x * x, axis=-1, keepdims=True)
    return x * lax.rsqrt(ms + EPS) * g


def _log_sigmoid(x):
    return jnp.minimum(x, 0.0) - jnp.log1p(jnp.exp(-jnp.abs(x)))


def _split3(x):
    hi = x.astype(BF16)
    r1 = x - hi.astype(F32)
    mid = r1.astype(BF16)
    lo = (r1 - mid.astype(F32)).astype(BF16)
    return hi, mid, lo


def _dot(a, b):
    return jnp.dot(a, b, preferred_element_type=F32)


def _dot_nt(a, b):
    return lax.dot_general(a, b, (((1,), (1,)), ((), ())), preferred_element_type=F32)


def _dot_tn(a, b):
    return lax.dot_general(a, b, (((0,), (0,)), ((), ())), preferred_element_type=F32)


def _tril_mask(n):
    row = lax.broadcasted_iota(jnp.int32, (n, n), 0)
    col = lax.broadcasted_iota(jnp.int32, (n, n), 1)
    return col <= row


def _norm_matmul_body(x_ref, g_ref, w_ref, o_ref, h_ref, *, tn):
    h_ref[...] = _rms(x_ref[...], g_ref[...]).astype(BF16)
    for c in range(w_ref.shape[1] // tn):
        cols = slice(c * tn, (c + 1) * tn)
        o_ref[:, cols] = _dot(h_ref[...], w_ref[:, cols]).astype(o_ref.dtype)


def norm_matmul(x, g, w, tm=1024, tn=1024):
    n, d = x.shape
    nout = w.shape[1]
    tm, tn = min(tm, n), _block(nout, tn)
    return pl.pallas_call(
        functools.partial(_norm_matmul_body, tn=tn),
        grid=(n // tm,),
        in_specs=[
            pl.BlockSpec((tm, d), lambda i: (i, 0)),
            pl.BlockSpec((1, d), lambda i: (0, 0)),
            pl.BlockSpec((d, nout), lambda i: (0, 0), pipeline_mode=pl.Buffered(1)),
        ],
        out_specs=pl.BlockSpec((tm, nout), lambda i: (i, 0)),
        out_shape=jax.ShapeDtypeStruct((n, nout), BF16),
        scratch_shapes=[pltpu.VMEM((tm, d), BF16)],
        compiler_params=_params("parallel"),
        name="norm_matmul",
    )(x, g, w)


def _matmul_residual_body(y_ref, w_ref, x_ref, o_ref):
    o_ref[...] = x_ref[...] + _dot(y_ref[...], w_ref[...])


def matmul_residual(y, w, x, tm=1024):
    n, k = y.shape
    d = w.shape[1]
    tm = min(tm, n)
    return pl.pallas_call(
        _matmul_residual_body,
        grid=(n // tm,),
        in_specs=[
            pl.BlockSpec((tm, k), lambda i: (i, 0)),
            pl.BlockSpec((k, d), lambda i: (0, 0)),
            pl.BlockSpec((tm, d), lambda i: (i, 0)),
        ],
        out_specs=pl.BlockSpec((tm, d), lambda i: (i, 0)),
        out_shape=jax.ShapeDtypeStruct((n, d), F32),
        compiler_params=_params("parallel"),
        name="matmul_residual",
    )(y, w, x)


def _gmlp_body(x_ref, g_ref, wi_ref, vg_ref, ws_ref, bt_ref, wo_ref, o_ref, h_ref, z_ref, vn_ref, y_ref, *, tm, width, tn):
    gd = width // A_GROUPS
    h_ref[...] = _rms(x_ref[...], g_ref[...]).astype(BF16)
    for c in range(2 * width // tn):
        cols = slice(c * tn, (c + 1) * tn)
        z_ref[:, cols] = jax.nn.gelu(_dot(h_ref[...], wi_ref[:, cols]), approximate=True).astype(BF16)
    vn_ref[...] = _rms(z_ref[:, width:].astype(F32), vg_ref[...]).astype(BF16)
    tril = _tril_mask(A_CHUNK)
    for g in range(A_GROUPS):
        w = jnp.where(tril, ws_ref[g], 0.0).astype(BF16)
        bias = bt_ref[:, g : g + 1]
        cs = slice(g * gd, (g + 1) * gd)
        for c in range(tm // A_CHUNK):
            rs = slice(c * A_CHUNK, (c + 1) * A_CHUNK)
            mixed = _dot(w, vn_ref[rs, cs]) + bias
            y_ref[rs, cs] = (z_ref[rs, cs].astype(F32) * mixed).astype(BF16)
    o_ref[...] = x_ref[...] + _dot(y_ref[...], wo_ref[...])


def gmlp_mixer(x, g, wi, vg, ws, bt, wo, tm=1024, tn=1024):
    n, d = x.shape
    width = wi.shape[1] // 2
    tm, tn = min(tm, n), _block(2 * width, tn)
    once = pl.Buffered(1)
    return pl.pallas_call(
        functools.partial(_gmlp_body, tm=tm, width=width, tn=tn),
        grid=(n // tm,),
        in_specs=[
            pl.BlockSpec((tm, d), lambda i: (i, 0)),
            pl.BlockSpec((1, d), lambda i: (0, 0)),
            pl.BlockSpec((d, 2 * width), lambda i: (0, 0), pipeline_mode=once),
            pl.BlockSpec((1, width), lambda i: (0, 0)),
            pl.BlockSpec((A_GROUPS, A_CHUNK, A_CHUNK), lambda i: (0, 0, 0)),
            pl.BlockSpec((A_CHUNK, A_GROUPS), lambda i: (0, 0)),
            pl.BlockSpec((width, d), lambda i: (0, 0), pipeline_mode=once),
        ],
        out_specs=pl.BlockSpec((tm, d), lambda i: (i, 0)),
        out_shape=jax.ShapeDtypeStruct((n, d), F32),
        scratch_shapes=[
            pltpu.VMEM((tm, d), BF16),
            pltpu.VMEM((tm, 2 * width), BF16),
            pltpu.VMEM((tm, width), BF16),
            pltpu.VMEM((tm, width), BF16),
        ],
        compiler_params=pltpu.CompilerParams(dimension_semantics=("parallel",), vmem_limit_bytes=V7X_VMEM_RESIDENT_LIMIT_BYTES),
        name="gmlp_mixer",
    )(x, g, wi, vg, ws, bt, wo)


def _ffn_resident_body(*refs, f, tf, final, mixer):
    x_ref, g_ref, wi_ref, wo_ref = refs[:4]
    rest = list(refs[4:])
    fg_ref = rest.pop(0) if final else None
    m_ref, wm_ref = (rest.pop(0), rest.pop(0)) if mixer else (None, None)
    o_ref, h_ref, t_ref = rest
    xin = x_ref
    if mixer:
        o_ref[...] = x_ref[...] + _dot(m_ref[...], wm_ref[...])
        xin = o_ref
    h_ref[...] = _rms(xin[...], g_ref[...]).astype(BF16)
    for c in range(f // tf):
        a = _dot(h_ref[...], wi_ref[:, c * tf : (c + 1) * tf])
        u = _dot(h_ref[...], wi_ref[:, f + c * tf : f + (c + 1) * tf])
        t_ref[:, c * tf : (c + 1) * tf] = ((a * jax.nn.sigmoid(a)) * u).astype(BF16)
    y = xin[...] + _dot(t_ref[...], wo_ref[...])
    o_ref[...] = _rms(y, fg_ref[...]) if final else y


def ffn_resident(x, g, w_in, w_out, final_g=None, mixer_out=None, mixer_w=None, tm=1024, tf=512):
    n, d = x.shape
    f = w_out.shape[0]
    tm, tf = min(tm, n), min(tf, f)
    final = final_g is not None
    mixer = mixer_out is not None
    once = pl.Buffered(1)
    in_specs = [
        pl.BlockSpec((tm, d), lambda i: (i, 0)),
        pl.BlockSpec((1, d), lambda i: (0, 0)),
        pl.BlockSpec((d, 2 * f), lambda i: (0, 0), pipeline_mode=once),
        pl.BlockSpec((f, d), lambda i: (0, 0), pipeline_mode=once),
    ]
    args = [x, g, w_in, w_out]
    if final:
        in_specs.append(pl.BlockSpec((1, d), lambda i: (0, 0)))
        args.append(final_g)
    if mixer:
        km = mixer_w.shape[0]
        in_specs += [pl.BlockSpec((tm, km), lambda i: (i, 0)), pl.BlockSpec((km, d), lambda i: (0, 0), pipeline_mode=once)]
        args += [mixer_out, mixer_w]
    return pl.pallas_call(
        functools.partial(_ffn_resident_body, f=f, tf=tf, final=final, mixer=mixer),
        grid=(n // tm,),
        in_specs=in_specs,
        out_specs=pl.BlockSpec((tm, d), lambda i: (i, 0)),
        out_shape=jax.ShapeDtypeStruct((n, d), F32),
        scratch_shapes=[pltpu.VMEM((tm, d), BF16), pltpu.VMEM((tm, f), BF16)],
        compiler_params=pltpu.CompilerParams(dimension_semantics=("parallel",), vmem_limit_bytes=V7X_VMEM_RESIDENT_LIMIT_BYTES),
        name="ffn_resident",
    )(*args)


TOK_ROWS = 8


def _router_body(x_ref, g_ref, whi_ref, wlo_ref, gate_ref, tok_ref, cnt_ref, carry_ref, *, ne, tm):
    @pl.when(pl.program_id(0) == 0)
    def _():
        carry_ref[...] = jnp.zeros_like(carry_ref)

    h = _rms(x_ref[...], g_ref[...])
    h_hi = h.astype(BF16)
    h_lo = (h - h_hi.astype(F32)).astype(BF16)
    logits = _dot(h_hi, whi_ref[...]) + _dot(h_lo, whi_ref[...]) + _dot(h_hi, wlo_ref[...])
    lane = lax.broadcasted_iota(jnp.int32, logits.shape, 1).astype(F32)
    neg = -jnp.inf
    l1 = jnp.where(lane < ne, logits, neg)
    m1 = jnp.max(l1, axis=-1, keepdims=True)
    i1 = jnp.min(jnp.where(l1 == m1, lane, float(V7X_LANES)), axis=-1, keepdims=True)
    l2 = jnp.where(lane == i1, neg, l1)
    m2 = jnp.max(l2, axis=-1, keepdims=True)
    i2 = jnp.min(jnp.where(l2 == m2, lane, float(V7X_LANES)), axis=-1, keepdims=True)
    e2 = jnp.exp(m2 - m1)
    den = 1.0 + e2
    gate_ref[...] = jnp.where(lane == 0.0, 1.0 / den, 0.0) + jnp.where(lane == 1.0, e2 / den, 0.0)

    sel1, sel2 = lane == i1, lane == i2
    onehot = jnp.where(sel1 | sel2, 1.0, 0.0)
    row = lax.broadcasted_iota(jnp.int32, (tm, tm), 0)
    col = lax.broadcasted_iota(jnp.int32, (tm, tm), 1)
    before = carry_ref[...] + _dot((col < row).astype(BF16), onehot.astype(BF16))
    carry_ref[...] = before[tm - 1 : tm, :] + onehot[tm - 1 : tm, :]
    cnt_ref[...] = carry_ref[...]
    r1 = jnp.sum(jnp.where(sel1, before, 0.0), axis=-1, keepdims=True)
    r2 = jnp.sum(jnp.where(sel2, before, 0.0), axis=-1, keepdims=True)
    table = (jnp.where(lane == 0.0, i1, 0.0) + jnp.where(lane == 1.0, i2, 0.0)
             + jnp.where(lane == 2.0, r1, 0.0) + jnp.where(lane == 3.0, r2, 0.0))
    pick = (lax.broadcasted_iota(jnp.int32, (TOK_ROWS, V7X_LANES), 0) == lax.broadcasted_iota(jnp.int32, (TOK_ROWS, V7X_LANES), 1)).astype(BF16)
    hi, mid, lo = _split3(table)
    tok_ref[...] = (_dot_nt(pick, hi) + _dot_nt(pick, mid) + _dot_nt(pick, lo)).astype(jnp.int32)


def moe_router(x, g, w_router, tm=512):
    n, d = x.shape
    ne = w_router.shape[1]
    tm = min(tm, n)
    wpad = jnp.pad(w_router, ((0, 0), (0, V7X_LANES - ne)))
    w_hi = wpad.astype(BF16)
    w_lo = (wpad - w_hi.astype(F32)).astype(BF16)
    return pl.pallas_call(
        functools.partial(_router_body, ne=ne, tm=tm),
        grid=(n // tm,),
        in_specs=[
            pl.BlockSpec((tm, d), lambda i: (i, 0)),
            pl.BlockSpec((1, d), lambda i: (0, 0)),
            pl.BlockSpec((d, V7X_LANES), lambda i: (0, 0)),
            pl.BlockSpec((d, V7X_LANES), lambda i: (0, 0)),
        ],
        out_specs=[
            pl.BlockSpec((tm, V7X_LANES), lambda i: (i, 0)),
            pl.BlockSpec((TOK_ROWS, tm), lambda i: (0, i)),
            pl.BlockSpec((1, V7X_LANES), lambda i: (0, 0)),
        ],
        out_shape=[
            jax.ShapeDtypeStruct((n, V7X_LANES), F32),
            jax.ShapeDtypeStruct((TOK_ROWS, n), jnp.int32),
            jax.ShapeDtypeStruct((1, V7X_LANES), F32),
        ],
        scratch_shapes=[pltpu.VMEM((1, V7X_LANES), F32)],
        compiler_params=_params("arbitrary"),
        name="moe_router",
    )(x, g, w_hi, w_lo)


def _plan_body(cnt_ref, tok_ref, off_ref, te_ref, tv_ref, nv_ref, pos_ref, *, ne, tg, n_tiles):
    off = jnp.int32(0)
    tile = jnp.int32(0)
    expert = tok_ref[0:TOP_K, :]
    group_start = jnp.zeros_like(expert)
    for e in range(ne):
        nt = (cnt_ref[e] + (tg - 1)) // tg
        off_ref[e] = off
        group_start = jnp.where(expert == e, off, group_start)

        def mark(ti, carry, e=e, first=tile):
            te_ref[ti] = jnp.int32(e)
            tv_ref[ti] = jnp.minimum(cnt_ref[e] - (ti - first) * tg, tg)
            return carry

        lax.fori_loop(tile, tile + nt, mark, 0)
        off = off + nt * tg
        tile = tile + nt
    pos_ref[...] = jnp.zeros_like(pos_ref)
    pos_ref[0:TOP_K, :] = group_start + tok_ref[TOP_K : 2 * TOP_K, :]
    nv_ref[0] = tile
    last = te_ref[jnp.maximum(tile - 1, 0)]

    def fill(ti, carry):
        te_ref[ti] = last
        tv_ref[ti] = jnp.int32(0)
        return carry

    lax.fori_loop(tile, n_tiles, fill, 0)


def moe_plan(cnt, tok, tg, n_tiles):
    ne = cnt.shape[0]
    smem = pl.BlockSpec(memory_space=pltpu.SMEM)
    vmem = pl.BlockSpec(memory_space=pltpu.VMEM)
    return pl.pallas_call(
        functools.partial(_plan_body, ne=ne, tg=tg, n_tiles=n_tiles),
        in_specs=[smem, vmem],
        out_specs=[smem, smem, smem, smem, vmem],
        out_shape=[
            jax.ShapeDtypeStruct((ne,), jnp.int32),
            jax.ShapeDtypeStruct((n_tiles,), jnp.int32),
            jax.ShapeDtypeStruct((n_tiles,), jnp.int32),
            jax.ShapeDtypeStruct((1,), jnp.int32),
            jax.ShapeDtypeStruct(tok.shape, jnp.int32),
        ],
        name="moe_plan",
    )(cnt, tok)


def _token_copy(src, s, dst, r, sem, ns):
    return pltpu.make_async_copy(src.at[pl.ds(pl.multiple_of(s * ns, ns), ns)], dst.at[pl.ds(pl.multiple_of(r * ns, ns), ns)], sem)


def _to_slabs(dst_ref, val, ns):
    rows = val.shape[0]
    for c in range(ns):
        dst_ref[pl.ds(c, rows, stride=ns), :] = val[:, c * V7X_LANES : (c + 1) * V7X_LANES]


def _slab(src_ref, c, rows, ns):
    return src_ref[pl.ds(c, rows, stride=ns), :]


def _dispatch_body(pos_hbm, off_ref, cnt_ref, x_ref, g_ref, xs_hbm, pos_smem, x3_ref, zero_ref, sem_idx, sem_row, sem_zero, *, td, tg, ne, n_tiles, ns):
    s, nsteps = pl.program_id(0), pl.num_programs(0)

    def idx_copy(step, slot):
        return pltpu.make_async_copy(pos_hbm.at[:, pl.ds(pl.multiple_of(step * td, td), td)], pos_smem.at[slot], sem_idx.at[slot])

    def wait_scatter(slot):
        for _ in range(TOP_K):
            pltpu.make_async_copy(x3_ref.at[slot], xs_hbm.at[pl.ds(0, td * ns)], sem_row.at[slot]).wait()

    @pl.when(s == 0)
    def _():
        zero_ref[...] = jnp.zeros_like(zero_ref)
        for e in range(ne):
            start = off_ref[e] + cnt_ref[e]
            stop = off_ref[e] + (cnt_ref[e] + (tg - 1)) // tg * tg

            def zero_start(r, carry):
                _token_copy(zero_ref, 0, xs_hbm, r, sem_zero, ns).start()
                return carry

            def zero_wait(r, carry):
                _token_copy(zero_ref, 0, xs_hbm, r, sem_zero, ns).wait()
                return carry

            lax.fori_loop(start, stop, zero_start, 0)
            lax.fori_loop(start, stop, zero_wait, 0)

        def tile_copy(ti):
            return pltpu.make_async_copy(zero_ref, xs_hbm.at[pl.ds(pl.multiple_of(ti * (tg * ns), tg * ns), tg * ns)], sem_zero)

        def tile_start(ti, carry):
            tile_copy(ti).start()
            return carry

        def tile_wait(ti, carry):
            tile_copy(ti).wait()
            return carry

        used = stop // tg
        lax.fori_loop(used, n_tiles, tile_start, 0)
        lax.fori_loop(used, n_tiles, tile_wait, 0)

    def step(slot):
        other = 1 - slot

        @pl.when(s == 0)
        def _():
            idx_copy(s, slot).start()

        @pl.when(s >= 2)
        def _():
            wait_scatter(slot)

        _to_slabs(x3_ref.at[slot], _rms(x_ref[...], g_ref[...]), ns)

        @pl.when(s + 1 < nsteps)
        def _():
            idx_copy(s + 1, other).start()

        idx_copy(s, slot).wait()

        def start(t, c):
            for k in range(TOP_K):
                _token_copy(x3_ref.at[slot], t, xs_hbm, pos_smem[slot, k, t], sem_row.at[slot], ns).start(priority=k % 2)
            return c

        lax.fori_loop(0, td, start, 0, unroll=8)

        @pl.when(s == nsteps - 1)
        def _():
            @pl.when(nsteps > 1)
            def _():
                wait_scatter(other)

            wait_scatter(slot)

    for slot in range(2):
        pl.when(s % 2 == slot)(functools.partial(step, slot))


def moe_dispatch(pos, off, cnt, x, g, tg, n_tiles, td=1024):
    n, d = x.shape
    ns = d // V7X_LANES
    assert ns % V7X_SUBLANES == 0
    td = min(td, n)
    smem = pl.BlockSpec(memory_space=pltpu.SMEM)
    hbm = pl.BlockSpec(memory_space=pl.ANY)
    return pl.pallas_call(
        functools.partial(_dispatch_body, td=td, tg=tg, ne=cnt.shape[0], n_tiles=n_tiles, ns=ns),
        grid=(n // td,),
        in_specs=[hbm, smem, smem, pl.BlockSpec((td, d), lambda i: (i, 0)), pl.BlockSpec((1, d), lambda i: (0, 0))],
        out_specs=hbm,
        out_shape=jax.ShapeDtypeStruct((n_tiles * tg * ns, V7X_LANES), x.dtype),
        scratch_shapes=[
            pltpu.SMEM((2, TOK_ROWS, td), jnp.int32),
            pltpu.VMEM((2, td * ns, V7X_LANES), x.dtype),
            pltpu.VMEM((tg * ns, V7X_LANES), x.dtype),
            pltpu.SemaphoreType.DMA((2,)),
            pltpu.SemaphoreType.DMA((2,)),
            pltpu.SemaphoreType.DMA(()),
        ],
        compiler_params=pltpu.CompilerParams(dimension_semantics=("arbitrary",), has_side_effects=True, vmem_limit_bytes=V7X_VMEM_LIMIT_BYTES),
        name="moe_dispatch",
    )(pos, off, cnt, x, g)


GROUP_SHORT_TILE_DIV = 4
GROUP_STAGE_SLOTS = 4
GROUP_STAGE_BYTES = 1024 * 1024


def _load_cast(src_hbm, dst_ref, stage_ref, sem, rows):
    n = dst_ref.shape[0] // rows
    slots = stage_ref.shape[0]
    ahead = slots - 1

    def chunk(c, slot):
        return pltpu.make_async_copy(src_hbm.at[pl.ds(pl.multiple_of(c * rows, rows), rows)], stage_ref.at[slot], sem.at[slot])

    for c in range(min(ahead, n)):
        chunk(c, c % slots).start()

    def body(c, carry):
        for slot in range(slots):

            @pl.when(c % slots == slot)
            def _():
                @pl.when(c + ahead < n)
                def _():
                    chunk(c + ahead, (slot + ahead) % slots).start()

                chunk(c, slot).wait()
                dst_ref[pl.ds(pl.multiple_of(c * rows, rows), rows), :] = stage_ref[slot].astype(BF16)

        return carry

    lax.fori_loop(0, n, body, 0)


def _grouped_ffn_body(te_ref, tv_ref, nv_ref, x_ref, wi_hbm, wo_hbm, o_ref, wi_ref, wo_ref, si_ref, so_ref, sem_i, sem_o, h_ref, t_ref, *, layer, tg, ns, f, tf):
    i = pl.program_id(0)
    e = te_ref[i]

    @pl.when((i < nv_ref[0]) & ((i == 0) | (e != te_ref[jnp.maximum(i - 1, 0)])))
    def _():
        _load_cast(wi_hbm.at[layer, e], wi_ref, si_ref, sem_i, si_ref.shape[1])
        _load_cast(wo_hbm.at[layer, e], wo_ref, so_ref, sem_o, so_ref.shape[1])

    @pl.when(i < nv_ref[0])
    def _():
        for c in range(ns):
            h_ref[:, c * V7X_LANES : (c + 1) * V7X_LANES] = _slab(x_ref, c, tg, ns).astype(BF16)

        def run(m):
            for c in range(f // tf):
                a = _dot(h_ref[0:m, :], wi_ref[:, c * tf : (c + 1) * tf])
                u = _dot(h_ref[0:m, :], wi_ref[:, f + c * tf : f + (c + 1) * tf])
                t_ref[0:m, c * tf : (c + 1) * tf] = ((a * jax.nn.sigmoid(a)) * u).astype(BF16)
            y = _dot(t_ref[0:m, :], wo_ref[...])
            for c in range(ns):
                o_ref[pl.ds(c, m, stride=ns), :] = y[:, c * V7X_LANES : (c + 1) * V7X_LANES]
            if m < tg:
                o_ref[m * ns :, :] = jnp.zeros(((tg - m) * ns, V7X_LANES), F32)

        short = tv_ref[i] <= tg // GROUP_SHORT_TILE_DIV
        pl.when(short)(lambda: run(tg // GROUP_SHORT_TILE_DIV))
        pl.when(jnp.logical_not(short))(lambda: run(tg))

    @pl.when(i >= nv_ref[0])
    def _():
        o_ref[...] = jnp.zeros_like(o_ref)


def moe_grouped_ffn(te, tv, nv, xs, w_in, w_out, layer, tg, tf=512):
    d = w_in.shape[2]
    ns = d // V7X_LANES
    r = xs.shape[0] // ns
    f = w_out.shape[2]
    tf = _block(f, tf)
    n_tiles = r // tg
    hbm = pl.BlockSpec(memory_space=pl.ANY)
    rows_in = _block(d, GROUP_STAGE_BYTES // (2 * f * 4), V7X_SUBLANES)
    rows_out = _block(f, GROUP_STAGE_BYTES // (d * 4), V7X_SUBLANES)

    def tile(i, nv):
        return jnp.minimum(i, nv[0] - 1)

    grid_spec = pltpu.PrefetchScalarGridSpec(
        num_scalar_prefetch=3,
        grid=(n_tiles,),
        in_specs=[pl.BlockSpec((tg * ns, V7X_LANES), lambda i, te, tv, nv: (tile(i, nv), 0)), hbm, hbm],
        out_specs=pl.BlockSpec((tg * ns, V7X_LANES), lambda i, te, tv, nv: (i, 0)),
        scratch_shapes=[
            pltpu.VMEM((d, 2 * f), BF16),
            pltpu.VMEM((f, d), BF16),
            pltpu.VMEM((GROUP_STAGE_SLOTS, rows_in, 2 * f), F32),
            pltpu.VMEM((GROUP_STAGE_SLOTS, rows_out, d), F32),
            pltpu.SemaphoreType.DMA((GROUP_STAGE_SLOTS,)),
            pltpu.SemaphoreType.DMA((GROUP_STAGE_SLOTS,)),
            pltpu.VMEM((tg, d), BF16),
            pltpu.VMEM((tg, f), BF16),
        ],
    )
    return pl.pallas_call(
        functools.partial(_grouped_ffn_body, layer=layer, tg=tg, ns=ns, f=f, tf=tf),
        grid_spec=grid_spec,
        out_shape=jax.ShapeDtypeStruct((r * ns, V7X_LANES), F32),
        compiler_params=pltpu.CompilerParams(dimension_semantics=("arbitrary",), vmem_limit_bytes=V7X_VMEM_RESIDENT_LIMIT_BYTES),
        name="moe_grouped_ffn",
    )(te, tv, nv, xs, w_in, w_out)


def _combine_body(pos_hbm, gate_ref, x_ref, ys_hbm, *rest, tc, ns, final):
    rest = list(rest)
    fg_ref = rest.pop(0) if final else None
    o_ref, pos_smem, buf_ref, sem_idx, sem_row = rest
    s, nsteps = pl.program_id(0), pl.num_programs(0)

    def idx_copy(step, slot):
        return pltpu.make_async_copy(pos_hbm.at[:, pl.ds(pl.multiple_of(step * tc, tc), tc)], pos_smem.at[slot], sem_idx.at[slot])

    def gather(slot):
        def start(t, c):
            for k in range(TOP_K):
                _token_copy(ys_hbm, pos_smem[slot, k, t], buf_ref.at[slot, k], t, sem_row.at[slot], ns).start(priority=k % 2)
            return c

        lax.fori_loop(0, tc, start, 0, unroll=8)

    def step(slot):
        other = 1 - slot

        @pl.when(s == 0)
        def _():
            idx_copy(s, slot).start()
            idx_copy(s, slot).wait()
            gather(slot)

            @pl.when(nsteps > 1)
            def _():
                idx_copy(s + 1, other).start()

        @pl.when(s + 1 < nsteps)
        def _():
            idx_copy(s + 1, other).wait()
            gather(other)

        @pl.when(s + 2 < nsteps)
        def _():
            idx_copy(s + 2, slot).start()

        for k in range(TOP_K):
            pltpu.make_async_copy(ys_hbm.at[pl.ds(0, tc * ns)], buf_ref.at[slot, k], sem_row.at[slot]).wait()
        g0, g1 = gate_ref[:, 0:1], gate_ref[:, 1:2]
        for c in range(ns):
            cols = slice(c * V7X_LANES, (c + 1) * V7X_LANES)
            o_ref[:, cols] = x_ref[:, cols] + g0 * _slab(buf_ref.at[slot, 0], c, tc, ns) + g1 * _slab(buf_ref.at[slot, 1], c, tc, ns)
        if final:
            o_ref[...] = _rms(o_ref[...], fg_ref[...])

    for slot in range(2):
        pl.when(s % 2 == slot)(functools.partial(step, slot))


def moe_combine(pos, gates, x, ys, final_g=None, tc=512):
    n, d = x.shape
    ns = d // V7X_LANES
    tc = min(tc, n)
    final = final_g is not None
    hbm = pl.BlockSpec(memory_space=pl.ANY)
    in_specs = [hbm, pl.BlockSpec((tc, V7X_LANES), lambda i: (i, 0)), pl.BlockSpec((tc, d), lambda i: (i, 0)), hbm]
    args = [pos, gates, x, ys]
    if final:
        in_specs.append(pl.BlockSpec((1, d), lambda i: (0, 0)))
        args.append(final_g)
    return pl.pallas_call(
        functools.partial(_combine_body, tc=tc, ns=ns, final=final),
        grid=(n // tc,),
        in_specs=in_specs,
        out_specs=pl.BlockSpec((tc, d), lambda i: (i, 0)),
        out_shape=jax.ShapeDtypeStruct((n, d), F32),
        scratch_shapes=[
            pltpu.SMEM((2, TOK_ROWS, tc), jnp.int32),
            pltpu.VMEM((2, TOP_K, tc * ns, V7X_LANES), F32),
            pltpu.SemaphoreType.DMA((2,)),
            pltpu.SemaphoreType.DMA((2,)),
        ],
        compiler_params=_params("arbitrary"),
        name="moe_combine",
    )(*args)


def moe_ffn(x, g, w_router, w_in, w_out, layer, final_g=None, tg=1024):
    n, _ = x.shape
    ne = w_router.shape[1]
    tg = min(tg, n)
    n_tiles = TOP_K * n // tg + ne
    gates, tok, counts = moe_router(x, g, w_router)
    cnt = counts[0, :ne].astype(jnp.int32)
    off, te, tv, nv, pos = moe_plan(cnt, tok, tg, n_tiles)
    xs = moe_dispatch(pos, off, cnt, x, g, tg, n_tiles)
    ys = moe_grouped_ffn(te, tv, nv, xs, w_in, w_out, layer, tg)
    return moe_combine(pos, gates, x, ys, final_g)


def _gla_gate_body(x_ref, g_ref, wl_ref, w2_ref, b_ref, o_ref):
    h = _rms(x_ref[...], g_ref[...]).astype(BF16)
    g_low = _dot(h, wl_ref[...]).astype(BF16)
    o_ref[...] = _log_sigmoid(_dot(g_low, w2_ref[...]) + b_ref[...]) / B_TAU


def gla_gate(x, g, w_low, w_gate2, bias, tm=1024):
    n, d = x.shape
    dk = w_gate2.shape[1]
    tm = min(tm, n)
    wl = jnp.pad(w_low, ((0, 0), (0, V7X_LANES - B_GATE_RANK))).astype(BF16)
    w2 = jnp.pad(w_gate2, ((0, V7X_LANES - B_GATE_RANK), (0, 0))).astype(BF16)
    return pl.pallas_call(
        _gla_gate_body,
        grid=(n // tm,),
        in_specs=[
            pl.BlockSpec((tm, d), lambda i: (i, 0)),
            pl.BlockSpec((1, d), lambda i: (0, 0)),
            pl.BlockSpec((d, V7X_LANES), lambda i: (0, 0)),
            pl.BlockSpec((V7X_LANES, dk), lambda i: (0, 0)),
            pl.BlockSpec((1, dk), lambda i: (0, 0)),
        ],
        out_specs=pl.BlockSpec((tm, dk), lambda i: (i, 0)),
        out_shape=jax.ShapeDtypeStruct((n, dk), F32),
        compiler_params=_params("parallel"),
        name="gla_gate",
    )(x, g, wl, w2, bias)


def _gla_body(q_ref, k_ref, v_ref, r_ref, la_ref, og_ref, o_ref, st_ref, *, tc, hk, hv):
    @pl.when(pl.program_id(1) == 0)
    def _():
        st_ref[...] = jnp.zeros_like(st_ref)

    c = B_CHUNK
    tril = _tril_mask(c)
    ones_tril = tril.astype(BF16)
    scale = hk**-0.5
    for ci in range(tc // c):
        rs = slice(ci * c, (ci + 1) * c)
        hi, mid, lo = _split3(la_ref[rs, :])
        bcum = _dot(ones_tril, hi) + _dot(ones_tril, mid) + _dot(ones_tril, lo)
        b_last = bcum[c - 1 : c, :]
        e_pos = jnp.exp(bcum)
        e_neg = jnp.exp(-bcum)
        e_end = jnp.exp(b_last - bcum)
        dec = jnp.exp(b_last)
        for h in range(B_HEADS):
            ks = slice(h * hk, (h + 1) * hk)
            vs = slice(h * hv, (h + 1) * hv)
            q = q_ref[rs, ks].astype(F32) * scale
            k = k_ref[rs, ks].astype(F32)
            v = v_ref[rs, vs]
            q_d = (q * e_pos[:, ks]).astype(BF16)
            k_d = (k * e_neg[:, ks]).astype(BF16)
            k_end = (k * e_end[:, ks]).astype(BF16)
            att = jnp.where(tril, _dot_nt(q_d, k_d), 0.0)
            st = st_ref[h]
            o = _dot(att.astype(BF16), v) + _dot_nt(q_d, st.astype(BF16))
            st_ref[h] = dec[:, ks] * st + _dot_tn(v, k_end)
            y = _rms(o, og_ref[...]).astype(BF16).astype(F32)
            r = r_ref[rs, vs].astype(F32)
            o_ref[rs, vs] = (y * (r * jax.nn.sigmoid(r))).astype(BF16)


def gla_chunks(proj, la, o_g, bsz, tc=256):
    n = proj.shape[0]
    dk = la.shape[1]
    dv = (proj.shape[1] - 2 * dk) // 2
    s = n // bsz
    tc = min(tc, s)
    nt = s // tc
    hk, hv = dk // B_HEADS, dv // B_HEADS
    assert (2 * dk) % dv == 0
    v_blk = 2 * dk // dv
    row = lambda b, t: b * nt + t
    return pl.pallas_call(
        functools.partial(_gla_body, tc=tc, hk=hk, hv=hv),
        grid=(bsz, nt),
        in_specs=[
            pl.BlockSpec((tc, dk), lambda b, t: (row(b, t), 0)),
            pl.BlockSpec((tc, dk), lambda b, t: (row(b, t), 1)),
            pl.BlockSpec((tc, dv), lambda b, t: (row(b, t), v_blk)),
            pl.BlockSpec((tc, dv), lambda b, t: (row(b, t), v_blk + 1)),
            pl.BlockSpec((tc, dk), lambda b, t: (row(b, t), 0)),
            pl.BlockSpec((1, hv), lambda b, t: (0, 0)),
        ],
        out_specs=pl.BlockSpec((tc, dv), lambda b, t: (row(b, t), 0)),
        out_shape=jax.ShapeDtypeStruct((n, dv), BF16),
        scratch_shapes=[pltpu.VMEM((B_HEADS, hv, hk), F32)],
        compiler_params=_params("parallel", "arbitrary"),
        name="gla_chunks",
    )(proj, proj, proj, proj, la, o_g)


def _fox_gate_body(x_ref, g_ref, wf_ref, b_ref, crow_ref, carry_ref, *, tb, nh):
    @pl.when(pl.program_id(1) == 0)
    def _():
        carry_ref[...] = jnp.zeros_like(carry_ref)

    h = _rms(x_ref[...], g_ref[...]).astype(BF16)
    lane = lax.broadcasted_iota(jnp.int32, (tb, V7X_LANES), 1)
    log_f = jnp.where(lane < nh, _log_sigmoid(_dot(h, wf_ref[...]) + b_ref[...]), 0.0)
    ones_tril = _tril_mask(tb).astype(BF16)
    hi, mid, lo = _split3(log_f)
    c = carry_ref[...] + (_dot(ones_tril, hi) + _dot(ones_tril, mid) + _dot(ones_tril, lo))
    carry_ref[...] = c[tb - 1 : tb, :]
    sel = (lax.broadcasted_iota(jnp.int32, (nh, V7X_LANES), 0) == lax.broadcasted_iota(jnp.int32, (nh, V7X_LANES), 1)).astype(BF16)
    hi, mid, lo = _split3(c)
    crow_ref[...] = _dot_nt(sel, hi) + _dot_nt(sel, mid) + _dot_nt(sel, lo)


def fox_gate(x, g, w_f, f_bias, bsz, tb=512):
    n, d = x.shape
    nh = w_f.shape[1]
    s = n // bsz
    tb = min(tb, s)
    nt = s // tb
    wf = jnp.pad(w_f, ((0, 0), (0, V7X_LANES - nh))).astype(BF16)
    bias = jnp.pad(f_bias.reshape(1, nh), ((0, 0), (0, V7X_LANES - nh)))
    return pl.pallas_call(
        functools.partial(_fox_gate_body, tb=tb, nh=nh),
        grid=(bsz, nt),
        in_specs=[
            pl.BlockSpec((tb, d), lambda b, t: (b * nt + t, 0)),
            pl.BlockSpec((1, d), lambda b, t: (0, 0)),
            pl.BlockSpec((d, V7X_LANES), lambda b, t: (0, 0)),
            pl.BlockSpec((1, V7X_LANES), lambda b, t: (0, 0)),
        ],
        out_specs=pl.BlockSpec((None, nh, tb), lambda b, t: (b, 0, t)),
        out_shape=jax.ShapeDtypeStruct((bsz, nh, s), F32),
        scratch_shapes=[pltpu.VMEM((1, V7X_LANES), F32)],
        compiler_params=_params("parallel", "arbitrary"),
        name="fox_gate",
    )(x, g, wf, bias)


FOX_KEY_BLOCK = 512
FOX_SUBTILES = 8
FOX_ROW_CHUNK = 32


def _fox_flash_body(q_ref, k_ref, v_ref, crow_ref, o_ref, qs_ref, va_ref, s_ref, p_ref, al_ref, m_ref, acc_ref, *, t, nsub, scale):
    i = pl.program_id(2)
    dh = q_ref.shape[1]
    nl = t // V7X_LANES

    @pl.when(i == 0)
    def _():
        va_ref[:, :dh] = v_ref[...]
        va_ref[:, dh:] = jnp.ones((va_ref.shape[0], V7X_LANES), BF16)

    qs_ref[...] = (q_ref[...].astype(F32) * (scale * LOG2_E)).astype(BF16)
    m_ref[...] = jnp.full_like(m_ref, -jnp.inf)
    acc_ref[...] = jnp.zeros_like(acc_ref)

    def block(sub, kb, masked):
        rows = slice(sub * t, (sub + 1) * t)
        off = pl.multiple_of(kb * t, t)
        s_ref[sub] = _dot_nt(qs_ref[rows, :], k_ref[pl.ds(off, t), :])
        bias = crow_ref[pl.ds(kb, 1), :] * LOG2_E

        def chunk(c, carry):
            r = pl.multiple_of(c * FOX_ROW_CHUNK, FOX_ROW_CHUNK)
            sc = s_ref[sub, pl.ds(r, FOX_ROW_CHUNK), :] - bias
            if masked:
                row = r + lax.broadcasted_iota(jnp.int32, sc.shape, 0)
                col = lax.broadcasted_iota(jnp.int32, sc.shape, 1)
                sc = jnp.where(col <= row, sc, -jnp.inf)
            tiles = [sc[:, a * V7X_LANES : (a + 1) * V7X_LANES] for a in range(nl)]
            mx = functools.reduce(jnp.maximum, tiles)
            m_old = m_ref[pl.ds(sub * t + r, FOX_ROW_CHUNK), :]
            m_new = jnp.maximum(m_old, jnp.max(mx, axis=-1, keepdims=True))
            al_ref[sub, pl.ds(r, FOX_ROW_CHUNK), :] = jnp.exp2(m_old - m_new)
            m_ref[pl.ds(sub * t + r, FOX_ROW_CHUNK), :] = m_new
            for a in range(nl):
                p_ref[sub, pl.ds(r, FOX_ROW_CHUNK), a * V7X_LANES : (a + 1) * V7X_LANES] = jnp.exp2(tiles[a] - m_new).astype(BF16)
            return carry

        lax.fori_loop(0, t // FOX_ROW_CHUNK, chunk, 0, unroll=True)
        pv = _dot(p_ref[sub], va_ref[pl.ds(off, t), :])
        alpha = al_ref[sub]
        acc_ref[rows, :dh] = alpha * acc_ref[rows, :dh] + pv[:, :dh]
        acc_ref[rows, dh:] = alpha * acc_ref[rows, dh:] + pv[:, dh:]

    def below_diagonal(kb, carry):
        for sub in range(nsub):
            block(sub, kb, False)
        return carry

    first = i * nsub
    lax.fori_loop(0, first, below_diagonal, 0)
    for kb in range(nsub):
        for sub in range(kb, nsub):
            block(sub, first + kb, masked=(sub == kb))
    o_ref[...] = (acc_ref[:, :dh] / acc_ref[:, dh:]).astype(o_ref.dtype)


def fox_flash(proj, crow, bsz, nh):
    n = proj.shape[0]
    dh = proj.shape[1] // (3 * nh)
    assert dh == V7X_LANES
    s = n // bsz
    t = min(FOX_KEY_BLOCK, s)
    nsub = min(FOX_SUBTILES, s // t)
    tq = t * nsub
    nq = s // tq
    crow = crow.reshape(bsz, nh, s // t, t)
    return pl.pallas_call(
        functools.partial(_fox_flash_body, t=t, nsub=nsub, scale=dh**-0.5),
        grid=(bsz, nh, nq),
        in_specs=[
            pl.BlockSpec((tq, dh), lambda b, h, i: (b * nq + i, h)),
            pl.BlockSpec((s, dh), lambda b, h, i: (b, nh + h)),
            pl.BlockSpec((s, dh), lambda b, h, i: (b, 2 * nh + h)),
            pl.BlockSpec((None, None, s // t, t), lambda b, h, i: (b, h, 0, 0)),
        ],
        out_specs=pl.BlockSpec((tq, dh), lambda b, h, i: (b * nq + i, h)),
        out_shape=jax.ShapeDtypeStruct((n, nh * dh), BF16),
        scratch_shapes=[
            pltpu.VMEM((tq, dh), BF16),
            pltpu.VMEM((s, dh + V7X_LANES), BF16),
            pltpu.VMEM((nsub, t, t), F32),
            pltpu.VMEM((nsub, t, t), BF16),
            pltpu.VMEM((nsub, t, V7X_LANES), F32),
            pltpu.VMEM((tq, V7X_LANES), F32),
            pltpu.VMEM((tq, dh + V7X_LANES), F32),
        ],
        compiler_params=_params("parallel", "parallel", "arbitrary"),
        name="fox_flash",
    )(proj, proj, proj, crow)


def kernel(x, norm1_g, norm2_g, a_w_in, a_vnorm_g, a_w_s, a_b_s, a_w_out, b_w_in, b_w_gate2, b_gate_bias, b_onorm_g, b_w_out, c_w_in, c_f_bias, c_w_out, ffn_w_in, ffn_w_out, moe_router, moe_w_in, moe_w_out, final_g):
    bsz, s, d = x.shape
    depth = norm1_g.shape[0]
    xf = x.reshape(bsz * s, d)
    for i in range(depth):
        g1 = norm1_g[i].reshape(1, d)
        m, j = i % N_MIXERS, i // N_MIXERS
        dense = i % 2 == 0
        pending = None
        if m == 0:
            xf = gmlp_mixer(xf, g1, a_w_in[j].astype(BF16), a_vnorm_g[j].reshape(1, -1), a_w_s[j], a_b_s[j].T, a_w_out[j].astype(BF16))
        elif m == 1:
            dk = b_w_gate2.shape[2]
            n_main = b_w_in.shape[2] - B_GATE_RANK
            proj = norm_matmul(xf, g1, b_w_in[j, :, :n_main].astype(BF16))
            la = gla_gate(xf, g1, b_w_in[j, :, n_main:], b_w_gate2[j], b_gate_bias[j].reshape(1, dk))
            o = gla_chunks(proj, la, b_onorm_g[j].reshape(1, -1), bsz)
            pending = (o, b_w_out[j].astype(BF16))
        else:
            proj = norm_matmul(xf, g1, c_w_in[j, :, : 3 * d].astype(BF16))
            crow = fox_gate(xf, g1, c_w_in[j, :, 3 * d :], c_f_bias[j], bsz)
            o = fox_flash(proj, crow, bsz, C_HEADS)
            pending = (o, c_w_out[j].astype(BF16))
        if pending is not None and not dense:
            xf = matmul_residual(*pending, xf)
            pending = None
        g2 = norm2_g[i].reshape(1, d)
        fg = final_g.reshape(1, d) if i == depth - 1 else None
        if dense:
            mo, mw = pending if pending is not None else (None, None)
            xf = ffn_resident(xf, g2, ffn_w_in[i // 2].astype(BF16), ffn_w_out[i // 2].astype(BF16), final_g=fg, mixer_out=mo, mixer_w=mw)
        else:
            xf = moe_ffn(xf, g2, moe_router[i // 2], moe_w_in, moe_w_out, i // 2, final_g=fg)
    return xf.reshape(bsz, s, d)
```
